```python
import math
import jax, jax.numpy as jnp
from jax import lax
import numpy as np

D_MODEL = 1024
BATCH = 16
SEQ = 256
DEPTH = 2
DEC_BATCH = 2
DEC_SEQ = 4096
PAST_LEN = 256

GRID_W = 64
HEAD_DIM = 64
SSM_WIDTH = D_MODEL // 4
SSM_GROUP_CH = 16
SSM_GROUPS = SSM_WIDTH // SSM_GROUP_CH
SSM_STATE = 64
NA_HEADS = (D_MODEL // 4) // HEAD_DIM
NA_WIDTH = NA_HEADS * HEAD_DIM
NA_WIN_R = 8
NA_WIN_C = 16
NA_QCB = 16
NA_KCB = NA_QCB + NA_WIN_C
NA_NCB = GRID_W // NA_QCB
GQA_HEADS = (D_MODEL // 2) // HEAD_DIM
GQA_KV_HEADS = 2
GQA_REP = GQA_HEADS // GQA_KV_HEADS
GQA_WIDTH = GQA_HEADS * HEAD_DIM
GQA_KV_WIDTH = GQA_KV_HEADS * HEAD_DIM
MIX_WIDTH = SSM_WIDTH + NA_WIDTH + GQA_WIDTH
IN_WIDTH = SSM_WIDTH + 3 * NA_WIDTH + GQA_WIDTH + 2 * GQA_KV_WIDTH
D_FF = ((8 * D_MODEL + 3 * 256 - 1) // (3 * 256)) * 256
ROPE_THETA = 10000.0
Q_BLOCK = 128
LN_EPS = 1e-6
RMS_EPS = 1e-6
DEEPNORM_ALPHA = (2 * DEPTH) ** 0.25
DEEPNORM_BETA = (8 * DEPTH) ** -0.25

kernel_name = 'hybrid_s5_natten_gqa_diffusion_step'


def layer_norm(x, g=None, b=None):
    xf = x.astype(jnp.float32)
    mu = jnp.mean(xf, axis=-1, keepdims=True)
    var = jnp.mean(jnp.square(xf - mu), axis=-1, keepdims=True)
    y = (xf - mu) * lax.rsqrt(var + LN_EPS)
    if g is not None:
        y = y * g.astype(jnp.float32) + b.astype(jnp.float32)
    return y.astype(x.dtype)


def rms_norm_heads(x, g):
    xf = x.astype(jnp.float32)
    y = xf * lax.rsqrt(jnp.mean(jnp.square(xf), axis=-1, keepdims=True) + RMS_EPS) * g.astype(jnp.float32)
    return y.astype(x.dtype)


def rope_2d(x):
    n = x.shape[1]
    nf = HEAD_DIM // 4
    t = jnp.arange(n)
    inv = ROPE_THETA ** (-jnp.arange(nf, dtype=jnp.float32) / nf)
    rows = (t // GRID_W).astype(jnp.float32)
    cols = (t % GRID_W).astype(jnp.float32)
    ang = jnp.stack([rows[:, None] * inv, cols[:, None] * inv], axis=1)
    ang = ang.reshape((1, n) + (1,) * (x.ndim - 3) + (2, nf))
    cos, sin = jnp.cos(ang), jnp.sin(ang)
    xr = x.astype(jnp.float32).reshape(x.shape[:-1] + (2, 2, nf))
    x1, x2 = xr[..., 0, :], xr[..., 1, :]
    out = jnp.stack([x1 * cos - x2 * sin, x2 * cos + x1 * sin], axis=-2)
    return out.reshape(x.shape).astype(x.dtype)


def attend_dense(q, k, v):
    b, lq = q.shape[0], q.shape[1]
    qb = min(Q_BLOCK, lq)
    nb = lq // qb
    qs = jnp.moveaxis(q.reshape((b, nb, qb) + q.shape[2:]), 1, 0)
    scale = HEAD_DIM ** -0.5

    def block(qi):
        s = jnp.einsum('bqgrd,bkgd->bgrqk', qi, k).astype(jnp.float32) * scale
        p = jax.nn.softmax(s, axis=-1).astype(v.dtype)
        return jnp.einsum('bgrqk,bkgd->bqgrd', p, v)

    o = lax.map(block, qs)
    return jnp.moveaxis(o, 0, 1).reshape(q.shape)


def na_indices(rows):
    wr = min(NA_WIN_R, rows)
    r = np.arange(rows)
    rs = np.clip(r - wr // 2, 0, rows - wr)
    key_rows = rs[:, None] + np.arange(wr)[None, :]
    cb = np.arange(NA_NCB)
    kcs = np.clip(cb * NA_QCB - NA_WIN_C // 2, 0, GRID_W - NA_KCB)
    key_cols = kcs[:, None] + np.arange(NA_KCB)[None, :]
    q_cols = cb[:, None] * NA_QCB + np.arange(NA_QCB)[None, :]
    cs = np.clip(q_cols - NA_WIN_C // 2, 0, GRID_W - NA_WIN_C)
    col_valid = (key_cols[:, None, :] >= cs[:, :, None]) & (key_cols[:, None, :] < cs[:, :, None] + NA_WIN_C)
    kidx = key_rows[:, None, :, None] * GRID_W + key_cols[None, :, None, :]
    dr_idx = key_rows - r[:, None] + (NA_WIN_R - 1)
    dc_idx = np.clip(key_cols[:, None, :] - q_cols[:, :, None], -(NA_WIN_C - 1), NA_WIN_C - 1) + (NA_WIN_C - 1)
    return wr, kidx, col_valid, dr_idx, dc_idx


def na_latent(q, k, v, ck, cv, bias_tab):
    b, n, h, dh = q.shape
    rows = n // GRID_W
    wr, kidx, col_valid, dr_idx, dc_idx = na_indices(rows)
    scale = HEAD_DIM ** -0.5
    kg = k[:, kidx]
    vg = v[:, kidx]
    qg = q.reshape(b, rows, NA_NCB, NA_QCB, h, dh)
    s_loc = jnp.einsum('brnqhd,brnwkhd->bhrnqwk', qg, kg).astype(jnp.float32) * scale
    bias = bias_tab.astype(jnp.float32)[:, dr_idx[:, None, None, :, None], dc_idx[None, :, :, None, :]]
    s_loc = jnp.where(col_valid[None, None, None, :, :, None, :], s_loc + bias[None], -jnp.inf)
    n_loc = wr * NA_KCB
    s_loc = s_loc.reshape(b, h, rows, NA_NCB, NA_QCB, n_loc)
    s_ctx = jnp.einsum('brnqhd,bchd->bhrnqc', qg, ck).astype(jnp.float32) * scale
    p = jax.nn.softmax(jnp.concatenate([s_loc, s_ctx], axis=-1), axis=-1).astype(v.dtype)
    p_loc = p[..., :n_loc].reshape(b, h, rows, NA_NCB, NA_QCB, wr, NA_KCB)
    p_ctx = p[..., n_loc:]
    o = jnp.einsum('bhrnqwk,brnwkhd->brnqhd', p_loc, vg) + jnp.einsum('bhrnqc,bchd->brnqhd', p_ctx, cv)
    return o.reshape(b, n, h * dh)


def ssm_discretize(lam_re, lam_im, log_dt, b_re, b_im):
    lam = lax.complex(jnp.minimum(lam_re.astype(jnp.float32), -1e-4), lam_im.astype(jnp.float32))
    dt = jnp.exp(log_dt.astype(jnp.float32))[:, None]
    lam_bar = jnp.exp(lam * dt)
    b_bar = ((lam_bar - 1.0) / lam)[..., None] * lax.complex(b_re.astype(jnp.float32), b_im.astype(jnp.float32))
    return lam_bar, b_bar


def _ssm_combine(e1, e2):
    a1, b1 = e1
    a2, b2 = e2
    return a1 * a2, a2 * b1 + b2


def ssm_scan(u, lam_bar, b_bar, h0, reverse):
    bu = jnp.einsum('blgc,gpc->blgp', u.astype(jnp.complex64), b_bar)
    first = -1 if reverse else 0
    bu = bu.at[:, first].add(lam_bar * h0)
    a = jnp.broadcast_to(lam_bar, bu.shape)
    _, hs = lax.associative_scan(_ssm_combine, (a, bu), reverse=reverse, axis=1)
    return hs


def ssm_mixer(u, lp, h0_f, h0_b):
    b, n, _ = u.shape
    uf = u.astype(jnp.float32).reshape(b, n, SSM_GROUPS, SSM_GROUP_CH)
    y = uf * lp['ssm_d'].astype(jnp.float32).reshape(SSM_GROUPS, SSM_GROUP_CH)
    finals = []
    for d, h0, rev in ((0, h0_f, False), (1, h0_b, True)):
        lam_bar, b_bar = ssm_discretize(lp['ssm_lam_re'][d], lp['ssm_lam_im'][d], lp['ssm_log_dt'][d],
                                        lp['ssm_b_re'][d], lp['ssm_b_im'][d])
        hs = ssm_scan(uf, lam_bar, b_bar, h0, rev)
        c_mat = lax.complex(lp['ssm_c_re'][d].astype(jnp.float32), lp['ssm_c_im'][d].astype(jnp.float32))
        y = y + jnp.einsum('blgp,gcp->blgc', hs, c_mat).real
        finals.append(hs[:, 0] if rev else hs[:, -1])
    g = jax.nn.gelu(y.reshape(b, n, SSM_WIDTH))
    out = g * jax.nn.sigmoid(g @ lp['w_ssm_glu'].astype(jnp.float32))
    return out.astype(u.dtype), jnp.stack(finals, axis=1)


def mixing(h, lp, ctx):
    b, n, _ = h.shape
    sizes = [SSM_WIDTH, NA_WIDTH, NA_WIDTH, NA_WIDTH, GQA_WIDTH, GQA_KV_WIDTH, GQA_KV_WIDTH]
    cuts = [int(s) for s in np.cumsum(sizes)[:-1]]
    z = h @ lp['w_in']
    u, q_na, k_na, v_na, q_g, k_g, v_g = jnp.split(z, cuts, axis=-1)
    q_na = q_na.reshape(b, n, NA_HEADS, HEAD_DIM)
    k_na = k_na.reshape(b, n, NA_HEADS, HEAD_DIM)
    v_na = v_na.reshape(b, n, NA_HEADS, HEAD_DIM)
    q_g = rms_norm_heads(q_g.reshape(b, n, GQA_KV_HEADS, GQA_REP, HEAD_DIM), lp['q_norm_g'])
    k_g = rms_norm_heads(k_g.reshape(b, n, GQA_KV_HEADS, HEAD_DIM), lp['k_norm_g'])
    v_g = v_g.reshape(b, n, GQA_KV_HEADS, HEAD_DIM)
    if ctx is None:
        h0 = jnp.zeros((b, SSM_GROUPS, SSM_STATE), jnp.complex64)
        ssm_out, st = ssm_mixer(u, lp, h0, h0)
        na_out = attend_dense(q_na[:, :, :, None, :], k_na, v_na).reshape(b, n, NA_WIDTH)
        gqa_out = attend_dense(q_g, k_g, v_g).reshape(b, n, GQA_WIDTH)
        new_ctx = (k_na, v_na, k_g, v_g, st.real, st.imag)
    else:
        ck_na, cv_na, ck_g, cv_g, st_re, st_im = ctx
        st = lax.complex(st_re.astype(jnp.float32), st_im.astype(jnp.float32))
        ssm_out, _ = ssm_mixer(u, lp, st[:, 0], st[:, 1])
        na_out = na_latent(q_na, k_na, v_na, ck_na, cv_na, lp['na_bias'])
        q_g = rope_2d(q_g)
        k_g = rope_2d(k_g)
        k_all = jnp.concatenate([ck_g.astype(k_g.dtype), k_g], axis=1)
        v_all = jnp.concatenate([cv_g.astype(v_g.dtype), v_g], axis=1)
        gqa_out = attend_dense(q_g, k_all, v_all).reshape(b, n, GQA_WIDTH)
        new_ctx = None
    o = jnp.concatenate([ssm_out, na_out.astype(h.dtype), gqa_out.astype(h.dtype)], axis=-1) @ lp['w_out']
    return o, new_ctx


def trunk_layer(x, cond, lp, ctx):
    m = jax.nn.silu(cond.astype(jnp.float32)) @ lp['w_ada'].astype(jnp.float32) + lp['b_ada'].astype(jnp.float32)
    m = m.astype(x.dtype)[:, None, :]
    sh1, sc1, g1, sh2, sc2, g2 = jnp.split(m, 6, axis=-1)
    h = layer_norm(x) * (1.0 + sc1) + sh1
    o, new_ctx = mixing(h, lp, ctx)
    x = layer_norm(DEEPNORM_ALPHA * x + g1 * o, lp['ln1_g'], lp['ln1_b'])
    h = layer_norm(x) * (1.0 + sc2) + sh2
    a, gate = jnp.split(h @ lp['w_ffn_in'], 2, axis=-1)
    f = (jax.nn.silu(a) * gate) @ lp['w_ffn_out']
    x = layer_norm(DEEPNORM_ALPHA * x + g2 * f, lp['ln2_g'], lp['ln2_b'])
    return x, new_ctx


def setup_inputs(seed: int = 0) -> dict:
    key = jax.random.key(seed)
    ks = jax.random.split(key, 40)
    f32 = jnp.float32
    nrm = lambda k, s, sc: jax.random.normal(k, s, f32) * sc
    lam_im = jnp.pi * jnp.arange(SSM_STATE, dtype=f32)
    return {
        'x_prompt': nrm(ks[0], (BATCH, SEQ, D_MODEL), 1.0),
        'x_sample': nrm(ks[1], (DEC_BATCH, DEC_SEQ, D_MODEL), 1.0),
        'c': nrm(ks[2], (DEC_BATCH, D_MODEL), 1.0),
        'cache_na_k': nrm(ks[3], (DEC_BATCH, DEPTH, PAST_LEN, NA_HEADS, HEAD_DIM), 1.0),
        'cache_na_v': nrm(ks[4], (DEC_BATCH, DEPTH, PAST_LEN, NA_HEADS, HEAD_DIM), 1.0),
        'cache_gqa_k': nrm(ks[5], (DEC_BATCH, DEPTH, PAST_LEN, GQA_KV_HEADS, HEAD_DIM), 1.0),
        'cache_gqa_v': nrm(ks[6], (DEC_BATCH, DEPTH, PAST_LEN, GQA_KV_HEADS, HEAD_DIM), 1.0),
        'state_ssm_re': nrm(ks[7], (DEC_BATCH, DEPTH, 2, SSM_GROUPS, SSM_STATE), 0.1),
        'state_ssm_im': nrm(ks[8], (DEC_BATCH, DEPTH, 2, SSM_GROUPS, SSM_STATE), 0.1),
        'c_ctx': nrm(ks[9], (D_MODEL,), 1.0),
        'w_ada': nrm(ks[10], (DEPTH, D_MODEL, 6 * D_MODEL), 0.5 * D_MODEL ** -0.5),
        'b_ada': nrm(ks[11], (DEPTH, 6 * D_MODEL), 0.01),
        'w_in': nrm(ks[12], (DEPTH, D_MODEL, IN_WIDTH), D_MODEL ** -0.5),
        'w_out': nrm(ks[13], (DEPTH, MIX_WIDTH, D_MODEL), DEEPNORM_BETA * MIX_WIDTH ** -0.5),
        'q_norm_g': 1.0 + nrm(ks[14], (DEPTH, HEAD_DIM), 0.02),
        'k_norm_g': 1.0 + nrm(ks[15], (DEPTH, HEAD_DIM), 0.02),
        'na_bias': nrm(ks[16], (DEPTH, NA_HEADS, 2 * NA_WIN_R - 1, 2 * NA_WIN_C - 1), 0.1),
        'ssm_lam_re': -0.5 + nrm(ks[17], (DEPTH, 2, SSM_GROUPS, SSM_STATE), 0.01),
        'ssm_lam_im': lam_im + nrm(ks[18], (DEPTH, 2, SSM_GROUPS, SSM_STATE), 0.01),
        'ssm_log_dt': jax.random.uniform(ks[19], (DEPTH, 2, SSM_GROUPS), f32, math.log(1e-3), math.log(1e-1)),
        'ssm_b_re': nrm(ks[20], (DEPTH, 2, SSM_GROUPS, SSM_STATE, SSM_GROUP_CH), (2 * SSM_GROUP_CH) ** -0.5),
        'ssm_b_im': nrm(ks[21], (DEPTH, 2, SSM_GROUPS, SSM_STATE, SSM_GROUP_CH), (2 * SSM_GROUP_CH) ** -0.5),
        'ssm_c_re': nrm(ks[22], (DEPTH, 2, SSM_GROUPS, SSM_GROUP_CH, SSM_STATE), SSM_STATE ** -0.5),
        'ssm_c_im': nrm(ks[23], (DEPTH, 2, SSM_GROUPS, SSM_GROUP_CH, SSM_STATE), SSM_STATE ** -0.5),
        'ssm_d': nrm(ks[24], (DEPTH, SSM_WIDTH), 1.0),
        'w_ssm_glu': nrm(ks[25], (DEPTH, SSM_WIDTH, SSM_WIDTH), SSM_WIDTH ** -0.5),
        'ln1_g': 1.0 + nrm(ks[26], (DEPTH, D_MODEL), 0.02),
        'ln1_b': nrm(ks[27], (DEPTH, D_MODEL), 0.02),
        'ln2_g': 1.0 + nrm(ks[28], (DEPTH, D_MODEL), 0.02),
        'ln2_b': nrm(ks[29], (DEPTH, D_MODEL), 0.02),
        'w_ffn_in': nrm(ks[30], (DEPTH, D_MODEL, 2 * D_FF), D_MODEL ** -0.5),
        'w_ffn_out': nrm(ks[31], (DEPTH, D_FF, D_MODEL), DEEPNORM_BETA * D_FF ** -0.5),
    }


def reference(x_prompt, x_sample, c, cache_na_k, cache_na_v, cache_gqa_k, cache_gqa_v, state_ssm_re, state_ssm_im,
              c_ctx, w_ada, b_ada, w_in, w_out, q_norm_g, k_norm_g, na_bias, ssm_lam_re, ssm_lam_im, ssm_log_dt,
              ssm_b_re, ssm_b_im, ssm_c_re, ssm_c_im, ssm_d, w_ssm_glu, ln1_g, ln1_b, ln2_g, ln2_b,
              w_ffn_in, w_ffn_out):
    y_prompt = x_prompt
    y_sample = x_sample
    na_k, na_v, g_k, g_v, st_re, st_im = [], [], [], [], [], []
    for l in range(DEPTH):
        lp = dict(w_ada=w_ada[l], b_ada=b_ada[l], w_in=w_in[l], w_out=w_out[l], q_norm_g=q_norm_g[l],
                  k_norm_g=k_norm_g[l], na_bias=na_bias[l], ssm_lam_re=ssm_lam_re[l], ssm_lam_im=ssm_lam_im[l],
                  ssm_log_dt=ssm_log_dt[l], ssm_b_re=ssm_b_re[l], ssm_b_im=ssm_b_im[l], ssm_c_re=ssm_c_re[l],
                  ssm_c_im=ssm_c_im[l], ssm_d=ssm_d[l], w_ssm_glu=w_ssm_glu[l], ln1_g=ln1_g[l], ln1_b=ln1_b[l],
                  ln2_g=ln2_g[l], ln2_b=ln2_b[l], w_ffn_in=w_ffn_in[l], w_ffn_out=w_ffn_out[l])
        y_prompt, ctx_out = trunk_layer(y_prompt, c_ctx[None, :], lp, None)
        na_k.append(ctx_out[0]); na_v.append(ctx_out[1]); g_k.append(ctx_out[2]); g_v.append(ctx_out[3])
        st_re.append(ctx_out[4]); st_im.append(ctx_out[5])
        ctx_l = (cache_na_k[:, l], cache_na_v[:, l], cache_gqa_k[:, l], cache_gqa_v[:, l],
                 state_ssm_re[:, l], state_ssm_im[:, l])
        y_sample, _ = trunk_layer(y_sample, c, lp, ctx_l)
    new_na_k = jnp.stack(na_k, axis=1)
    new_na_v = jnp.stack(na_v, axis=1)
    new_gqa_k = jnp.stack(g_k, axis=1)
    new_gqa_v = jnp.stack(g_v, axis=1)
    new_state_re = jnp.stack(st_re, axis=1)
    new_state_im = jnp.stack(st_im, axis=1)
    return (y_prompt, y_sample, new_na_k, new_na_v, new_gqa_k, new_gqa_v, new_state_re, new_state_im)
```

```python
import functools
import math

import numpy as np
import jax
import jax.numpy as jnp
from jax import lax
from jax.experimental import pallas as pl
from jax.experimental.pallas import tpu as pltpu

F32 = jnp.float32
BF16 = jnp.bfloat16

D_MODEL = 1024
BATCH = 16
SEQ = 256
DEPTH = 2
DEC_BATCH = 2
DEC_SEQ = 4096
PAST_LEN = 256
GRID_W = 64
GRID_H = DEC_SEQ // GRID_W
HEAD_DIM = 64
SSM_WIDTH = 256
SSM_GROUP_CH = 16
SSM_GROUPS = 16
SSM_STATE = 64
NA_HEADS = 4
NA_WIDTH = 256
NA_WIN_R = 8
NA_WIN_C = 16
GQA_HEADS = 8
GQA_KV_HEADS = 2
GQA_REP = 4
GQA_WIDTH = 512
GQA_KV_WIDTH = 128
IN_WIDTH = 1792
D_FF = 2816
ROPE_THETA = 10000.0
LN_EPS = 1e-6
RMS_EPS = 1e-6
DEEPNORM_ALPHA = (2 * DEPTH) ** 0.25
QK_SCALE = HEAD_DIM ** -0.5

TOKENS = 4096
SSM_CHUNK = 16
N_CHUNK = TOKENS // SSM_CHUNK
Z_WIDTH = IN_WIDTH - SSM_WIDTH
Z_QNA, Z_KNA, Z_VNA, Z_QG, Z_KG, Z_VG = 0, 256, 512, 768, 1280, 1408

VMEM_LIMIT = 56 * 1024 * 1024
LANES = 128


def _sigmoid(x):
    return 1.0 / (1.0 + jnp.exp(-x))


def _ln(x):
    mu = jnp.mean(x, axis=-1, keepdims=True)
    xc = x - mu
    var = jnp.mean(xc * xc, axis=-1, keepdims=True)
    return xc * lax.rsqrt(var + LN_EPS)


def _dot(a, b):
    return jnp.dot(a, b, preferred_element_type=F32)


def _dot_nt(a, b):
    return lax.dot_general(a, b, (((1,), (1,)), ((), ())), preferred_element_type=F32)


def _mod_kernel(c_ref, w_ref, b_ref, o_ref):
    c = c_ref[...]
    s = c * _sigmoid(c)
    o_ref[0] = jnp.dot(s, w_ref[0], preferred_element_type=F32, precision=lax.Precision.HIGHEST) + b_ref[0]


def _modulation(cond8, w_ada, b_ada):
    nb = 1536
    return pl.pallas_call(
        _mod_kernel,
        out_shape=jax.ShapeDtypeStruct((DEPTH, 8, 6 * D_MODEL), F32),
        grid=(DEPTH, 6 * D_MODEL // nb),
        in_specs=[
            pl.BlockSpec((8, D_MODEL), lambda l, j: (0, 0)),
            pl.BlockSpec((1, D_MODEL, nb), lambda l, j: (l, 0, j)),
            pl.BlockSpec((1, 1, nb), lambda l, j: (l, 0, j)),
        ],
        out_specs=pl.BlockSpec((1, 8, nb), lambda l, j: (l, 0, j)),
        compiler_params=pltpu.CompilerParams(dimension_semantics=("parallel", "parallel"),
                                             vmem_limit_bytes=VMEM_LIMIT),
        name="mod",
    )(cond8, w_ada, b_ada.reshape(DEPTH, 1, 6 * D_MODEL))


def _swap16(x):
    w = x.shape[-1]
    lane = lax.broadcasted_iota(jnp.int32, x.shape, x.ndim - 1)
    return jnp.where((lane & 16) != 0, pltpu.roll(x, 16, x.ndim - 1), pltpu.roll(x, w - 16, x.ndim - 1))


def _win_kernel(x_ref, mod_ref, w_ref, qg_ref, kg_ref, hm_ref, cos_ref, sin_ref, *out_refs, rope, emit_kv):
    u_ref, z_ref = out_refs[0], out_refs[1]
    x = x_ref[0]
    h = _ln(x) * (1.0 + mod_ref[0, 1:2, :]) + mod_ref[0, 0:1, :]
    z = _dot(h.astype(BF16), w_ref[...])
    u_ref[0, 0] = z[:, 0:LANES]
    u_ref[0, 1] = z[:, LANES:SSM_WIDTH]
    q_na = z[:, 256:512] * QK_SCALE
    kv_na = z[:, 512:1024]
    q_g = z[:, 1024:1536]
    k_g = z[:, 1536:1664]
    v_g = z[:, 1664:1792]
    q_ms = _dot((q_g * q_g).astype(BF16), hm_ref[...])
    k_ms = _dot((k_g * k_g).astype(BF16), hm_ref[0:GQA_KV_WIDTH, 0:GQA_KV_WIDTH])
    q_n = q_g * lax.rsqrt(q_ms + RMS_EPS) * qg_ref[...]
    k_n = k_g * lax.rsqrt(k_ms + RMS_EPS) * kg_ref[...]
    if emit_kv:
        out_refs[2][0] = kv_na
        out_refs[3][0] = jnp.concatenate([k_n, v_g], axis=1)
    if rope:
        cos = cos_ref[...]
        sin = sin_ref[...]
        cos4 = jnp.concatenate([cos] * 4, axis=1)
        sin4 = jnp.concatenate([sin] * 4, axis=1)
        q_n = q_n * cos4 + _swap16(q_n) * sin4
        k_n = k_n * cos + _swap16(k_n) * sin
    zb = jnp.concatenate([q_na, kv_na, q_n * QK_SCALE, k_n, v_g], axis=1)
    z_ref[0] = zb.astype(BF16)


def _in_projection(x, mod, w_in, qg, kg, hm, cos, sin, *, rope, emit_kv):
    bg = x.shape[0]
    tm = 512
    out_shape = [jax.ShapeDtypeStruct((bg, 2, TOKENS, LANES), F32),
                 jax.ShapeDtypeStruct((bg, TOKENS, Z_WIDTH), BF16)]
    out_specs = [pl.BlockSpec((1, 2, tm, LANES), lambda b, i: (b, 0, i, 0)),
                 pl.BlockSpec((1, tm, Z_WIDTH), lambda b, i: (b, i, 0))]
    if emit_kv:
        out_shape += [jax.ShapeDtypeStruct((bg, TOKENS, 2 * NA_WIDTH), F32),
                      jax.ShapeDtypeStruct((bg, TOKENS, 2 * GQA_KV_WIDTH), F32)]
        out_specs += [pl.BlockSpec((1, tm, 2 * NA_WIDTH), lambda b, i: (b, i, 0)),
                      pl.BlockSpec((1, tm, 2 * GQA_KV_WIDTH), lambda b, i: (b, i, 0))]
    return pl.pallas_call(
        functools.partial(_win_kernel, rope=rope, emit_kv=emit_kv),
        out_shape=out_shape,
        grid=(bg, TOKENS // tm),
        in_specs=[
            pl.BlockSpec((1, tm, D_MODEL), lambda b, i: (b, i, 0)),
            pl.BlockSpec((1, 6, D_MODEL), lambda b, i: (b, 0, 0)),
            pl.BlockSpec((D_MODEL, IN_WIDTH), lambda b, i: (0, 0)),
            pl.BlockSpec((1, GQA_WIDTH), lambda b, i: (0, 0)),
            pl.BlockSpec((1, GQA_KV_WIDTH), lambda b, i: (0, 0)),
            pl.BlockSpec((GQA_WIDTH, GQA_WIDTH), lambda b, i: (0, 0)),
            pl.BlockSpec((tm, 128), lambda b, i: (i, 0)),
            pl.BlockSpec((tm, 128), lambda b, i: (i, 0)),
        ],
        out_specs=out_specs,
        compiler_params=pltpu.CompilerParams(dimension_semantics=("parallel", "parallel"),
                                             vmem_limit_bytes=VMEM_LIMIT),
        name="win_lat" if rope else "win_ctx",
    )(x, mod, w_in, qg, kg, hm, cos, sin)


def _ssm_kernel(u_ref, m_ref, bend_ref, cpow_ref, a1_ref, a2_ref, h0_ref, d_ref, wglu_ref, *refs,
                n_seq, emit_final):
    if emit_final:
        out_ref, fin_ref, y_s, yt_s, u_s, inj_s, injp_s, hp_s, he_s = refs
    else:
        out_ref, y_s, yt_s, u_s, inj_s, injp_s, hp_s = refs
        fin_ref = he_s = None
    d = pl.program_id(1)
    chunks_per_seq = N_CHUNK // n_seq

    def chunk_rows(t):
        sl = pl.ds(t, N_CHUNK, stride=SSM_CHUNK)
        return jnp.concatenate([u_ref[0, 0, sl, :], u_ref[0, 1, sl, :]], axis=1)

    @pl.when(d == 0)
    def _build():
        for t in range(SSM_CHUNK):
            a_t = chunk_rows(t)
            yt_s[:, t * SSM_GROUP_CH:(t + 1) * SSM_GROUP_CH, :] = a_t.T.reshape(SSM_GROUPS, SSM_GROUP_CH, N_CHUNK)

        def tr(g, carry):
            u_s[g] = yt_s[g].T.astype(BF16)
            return carry
        lax.fori_loop(0, SSM_GROUPS, tr, 0)

    def proj(g, carry):
        ug = u_s[g]
        y = _dot(ug, m_ref[0, g])

        @pl.when(d == 0)
        def _():
            y_s[g] = y

        @pl.when(d != 0)
        def _():
            y_s[g] = y_s[g] + y
        inj = _dot(ug, bend_ref[0, g])
        inj_s[pl.ds(g, N_CHUNK, stride=SSM_GROUPS), :] = inj[:, :LANES]
        injp_s[pl.ds(g, N_CHUNK, stride=SSM_GROUPS), :] = inj[:, LANES:]
        return carry
    lax.fori_loop(0, SSM_GROUPS, proj, 0)

    a1 = a1_ref[0]
    a2 = a2_ref[0]
    h0 = h0_ref[0, 0]

    def step(j, carry):
        s, sp = carry
        n = j + d * (N_CHUNK - 1 - 2 * j)
        if n_seq > 1:
            keep = (j % chunks_per_seq != 0).astype(F32)
            s = s * keep
            sp = sp * keep
        row = pl.multiple_of(n * SSM_GROUPS, SSM_GROUPS)
        hp_s[pl.ds(row, SSM_GROUPS), :] = s
        s_new = a1 * s + a2 * sp + inj_s[pl.ds(row, SSM_GROUPS), :]
        sp_new = a1 * sp - a2 * s + injp_s[pl.ds(row, SSM_GROUPS), :]
        if emit_final:
            he_s[pl.ds(row, SSM_GROUPS), :] = s_new
        return s_new, sp_new
    lax.fori_loop(0, N_CHUNK, step, (h0[:, :128], h0[:, 128:]))

    def carry_in(g, carry):
        hp = hp_s[pl.ds(g, N_CHUNK, stride=SSM_GROUPS), :]
        y_s[g] = y_s[g] + _dot(hp.astype(BF16), cpow_ref[0, g])
        return carry
    lax.fori_loop(0, SSM_GROUPS, carry_in, 0)

    if emit_final:
        last = (chunks_per_seq - 1) * (1 - d)
        for sq in range(n_seq):
            row = pl.multiple_of((sq * chunks_per_seq + last) * SSM_GROUPS, SSM_GROUPS)
            fin_ref[0, 0, sq * SSM_GROUPS:(sq + 1) * SSM_GROUPS, :] = he_s[pl.ds(row, SSM_GROUPS), :]

    @pl.when(d == 1)
    def _finish():
        def tr(g, carry):
            yt_s[g] = y_s[g].T
            return carry
        lax.fori_loop(0, SSM_GROUPS, tr, 0)
        dskip = d_ref[...]
        wglu = wglu_ref[...]
        for t in range(SSM_CHUNK):
            blk = yt_s[:, t * SSM_GROUP_CH:(t + 1) * SSM_GROUP_CH, :].reshape(SSM_WIDTH, N_CHUNK)
            y = blk.T + chunk_rows(t) * dskip
            gl = jax.nn.gelu(y, approximate=True)
            gate = _dot(gl.astype(BF16), wglu)
            o = gl * _sigmoid(gate)
            out_ref[0, 0, pl.ds(t, N_CHUNK, stride=SSM_CHUNK), :] = o[:, :LANES]
            out_ref[0, 1, pl.ds(t, N_CHUNK, stride=SSM_CHUNK), :] = o[:, LANES:]


def _ssm(u, ops, h0, dskip, wglu, *, n_seq, emit_final):
    bg = u.shape[0]
    m, bend, cpow, a1, a2 = ops
    out_shape = [jax.ShapeDtypeStruct((bg, 2, TOKENS, LANES), F32)]
    out_specs = [pl.BlockSpec((1, 2, TOKENS, LANES), lambda b, d: (b, 0, 0, 0))]
    scratch = [pltpu.VMEM((SSM_GROUPS, N_CHUNK, 256), F32),
               pltpu.VMEM((SSM_GROUPS, 256, N_CHUNK), F32),
               pltpu.VMEM((SSM_GROUPS, N_CHUNK, 256), BF16),
               pltpu.VMEM((N_CHUNK * SSM_GROUPS, LANES), F32),
               pltpu.VMEM((N_CHUNK * SSM_GROUPS, LANES), F32),
               pltpu.VMEM((N_CHUNK * SSM_GROUPS, LANES), F32)]
    if emit_final:
        out_shape.append(jax.ShapeDtypeStruct((bg, 2, n_seq * SSM_GROUPS, 128), F32))
        out_specs.append(pl.BlockSpec((1, 1, n_seq * SSM_GROUPS, 128), lambda b, d: (b, d, 0, 0)))
        scratch.append(pltpu.VMEM((N_CHUNK * SSM_GROUPS, 128), F32))
    return pl.pallas_call(
        functools.partial(_ssm_kernel, n_seq=n_seq, emit_final=emit_final),
        out_shape=out_shape,
        grid=(bg, 2),
        in_specs=[
            pl.BlockSpec((1, 2, TOKENS, LANES), lambda b, d: (b, 0, 0, 0)),
            pl.BlockSpec((1, SSM_GROUPS, 256, 256), lambda b, d: (d, 0, 0, 0)),
            pl.BlockSpec((1, SSM_GROUPS, 256, 256), lambda b, d: (d, 0, 0, 0)),
            pl.BlockSpec((1, SSM_GROUPS, 128, 256), lambda b, d: (d, 0, 0, 0)),
            pl.BlockSpec((1, SSM_GROUPS, 128), lambda b, d: (d, 0, 0)),
            pl.BlockSpec((1, SSM_GROUPS, 128), lambda b, d: (d, 0, 0)),
            pl.BlockSpec((1, 1, SSM_GROUPS, 256), lambda b, d: (b, d, 0, 0)),
            pl.BlockSpec((1, SSM_WIDTH), lambda b, d: (0, 0)),
            pl.BlockSpec((SSM_WIDTH, SSM_WIDTH), lambda b, d: (0, 0)),
        ],
        out_specs=out_specs,
        scratch_shapes=scratch,
        compiler_params=pltpu.CompilerParams(dimension_semantics=("parallel", "arbitrary"),
                                             vmem_limit_bytes=VMEM_LIMIT),
        name="ssm_ctx" if emit_final else "ssm_lat",
    )(u, m, bend, cpow, a1, a2, h0, dskip, wglu)


def _ssm_operators(lam_re, lam_im, log_dt, b_re, b_im, c_re, c_im):
    hp = lax.Precision.HIGHEST
    t_n = SSM_CHUNK
    lr = jnp.minimum(lam_re.astype(F32), -1e-4)
    li = lam_im.astype(F32)
    dt = jnp.exp(log_dt.astype(F32))[..., None]
    k = jnp.arange(t_n + 1, dtype=F32)[:, None, None, None]
    mag = jnp.exp(lr * dt * k)
    ang = li * dt * k
    p_r, p_i = mag * jnp.cos(ang), mag * jnp.sin(ang)
    n_r, n_i = p_r[1] - 1.0, p_i[1]
    den = lr * lr + li * li
    q_r, q_i = (n_r * lr + n_i * li) / den, (n_i * lr - n_r * li) / den
    bb_r = q_r[..., None] * b_re - q_i[..., None] * b_im
    bb_i = q_r[..., None] * b_im + q_i[..., None] * b_re
    cp_r = c_re[None] * p_r[:, :, :, None, :] - c_im[None] * p_i[:, :, :, None, :]
    cp_i = c_re[None] * p_i[:, :, :, None, :] + c_im[None] * p_r[:, :, :, None, :]
    kern = (jnp.einsum('kdgcp,dgpe->kdgce', cp_r, bb_r, precision=hp)
            - jnp.einsum('kdgcp,dgpe->kdgce', cp_i, bb_i, precision=hp))
    t = np.arange(t_n)
    mats, bends, cpows = [], [], []
    for d in range(2):
        lag = (t[None, :] - t[:, None]) if d == 0 else (t[:, None] - t[None, :])
        mask = jnp.asarray(lag >= 0, F32)
        kd = kern[np.clip(lag, 0, t_n), d]
        kd = kd * mask[:, :, None, None, None]
        mats.append(kd.transpose(2, 0, 4, 1, 3).reshape(SSM_GROUPS, 256, 256))
        pw = (t_n - 1 - t) if d == 0 else t
        e_r = p_r[pw, d][..., None] * bb_r[d][None] - p_i[pw, d][..., None] * bb_i[d][None]
        e_i = p_r[pw, d][..., None] * bb_i[d][None] + p_i[pw, d][..., None] * bb_r[d][None]
        e_r = e_r.transpose(1, 0, 3, 2).reshape(SSM_GROUPS, 256, SSM_STATE)
        e_i = e_i.transpose(1, 0, 3, 2).reshape(SSM_GROUPS, 256, SSM_STATE)
        bends.append(jnp.concatenate([e_r, e_i, e_i, e_r], axis=-1))
        pc = (t + 1) if d == 0 else (t_n - t)
        o_r = cp_r[pc, d].transpose(1, 3, 0, 2).reshape(SSM_GROUPS, SSM_STATE, 256)
        o_i = cp_i[pc, d].transpose(1, 3, 0, 2).reshape(SSM_GROUPS, SSM_STATE, 256)
        cpows.append(jnp.concatenate([o_r, -o_i], axis=1))
    a_r, a_i = p_r[t_n], p_i[t_n]
    a1 = jnp.concatenate([a_r, a_r], axis=-1)
    a2 = jnp.concatenate([-a_i, a_i], axis=-1)
    return (jnp.stack(mats).astype(BF16), jnp.stack(bends).astype(BF16), jnp.stack(cpows).astype(BF16), a1, a2)


def _softmax_pv(s, v):
    m = jnp.max(s, axis=-1, keepdims=True)
    p = jnp.exp(s - m)
    l = jnp.sum(p, axis=-1, keepdims=True)
    return _dot(p.astype(BF16), v) / l


def _ctx_attn_kernel(z_ref, na_ref, gqa_ref):
    z = z_ref[0]
    outs = []
    for h in range(NA_HEADS):
        sl = slice(h * HEAD_DIM, (h + 1) * HEAD_DIM)
        q = z[:, Z_QNA:Z_QNA + NA_WIDTH][:, sl]
        k = z[:, Z_KNA:Z_KNA + NA_WIDTH][:, sl]
        v = z[:, Z_VNA:Z_VNA + NA_WIDTH][:, sl]
        outs.append(_softmax_pv(_dot_nt(q, k), v))
    na_ref[0] = jnp.concatenate(outs, axis=1).astype(BF16)
    outs = []
    for g in range(GQA_KV_HEADS):
        k = z[:, Z_KG + g * HEAD_DIM:Z_KG + (g + 1) * HEAD_DIM]
        v = z[:, Z_VG + g * HEAD_DIM:Z_VG + (g + 1) * HEAD_DIM]
        for r in range(GQA_REP):
            c0 = Z_QG + (g * GQA_REP + r) * HEAD_DIM
            outs.append(_softmax_pv(_dot_nt(z[:, c0:c0 + HEAD_DIM], k), v))
    gqa_ref[0] = jnp.concatenate(outs, axis=1).astype(BF16)


def _ctx_attention(zb):
    z3 = zb.reshape(BATCH, SEQ, Z_WIDTH)
    na, gqa = pl.pallas_call(
        _ctx_attn_kernel,
        out_shape=[jax.ShapeDtypeStruct((BATCH, SEQ, NA_WIDTH), BF16),
                   jax.ShapeDtypeStruct((BATCH, SEQ, GQA_WIDTH), BF16)],
        grid=(BATCH,),
        in_specs=[pl.BlockSpec((1, SEQ, Z_WIDTH), lambda b: (b, 0, 0))],
        out_specs=[pl.BlockSpec((1, SEQ, NA_WIDTH), lambda b: (b, 0, 0)),
                   pl.BlockSpec((1, SEQ, GQA_WIDTH), lambda b: (b, 0, 0))],
        compiler_params=pltpu.CompilerParams(dimension_semantics=("parallel",), vmem_limit_bytes=VMEM_LIMIT),
        name="attn_ctx",
    )(z3)
    return na.reshape(1, TOKENS, NA_WIDTH), gqa.reshape(1, TOKENS, GQA_WIDTH)


NA_KEYS = NA_WIN_R * GRID_W


def _na_lat_kernel(q_ref, k_ref, v_ref, ck_ref, cv_ref, bias_ref, o_ref):
    r = pl.program_id(1)
    rs = jnp.clip(r - NA_WIN_R // 2, 0, GRID_H - NA_WIN_R)
    start = pl.multiple_of(rs * GRID_W, GRID_W)
    q = q_ref[0]
    kw = k_ref[0, pl.ds(start, NA_KEYS), :]
    vw = v_ref[0, pl.ds(start, NA_KEYS), :]
    ck = ck_ref[0]
    cv = cv_ref[0]
    outs = []
    for h in range(NA_HEADS):
        sl = slice(h * HEAD_DIM, (h + 1) * HEAD_DIM)
        qh = q[:, sl]
        s_loc = _dot_nt(qh, kw[:, sl]) + bias_ref[0, h]
        s_ctx = _dot_nt(qh, ck[:, sl])
        m = jnp.maximum(jnp.max(s_loc, axis=-1, keepdims=True), jnp.max(s_ctx, axis=-1, keepdims=True))
        p_loc = jnp.exp(s_loc - m)
        p_ctx = jnp.exp(s_ctx - m)
        l = jnp.sum(p_loc, axis=-1, keepdims=True) + jnp.sum(p_ctx, axis=-1, keepdims=True)
        o = _dot(p_loc.astype(BF16), vw[:, sl]) + _dot(p_ctx.astype(BF16), cv[:, sl])
        outs.append(o / l)
    o_ref[0] = jnp.concatenate(outs, axis=1).astype(BF16)


def _na_row_class(r):
    return r - jnp.clip(r - NA_WIN_R // 2, 0, GRID_H - NA_WIN_R)


def _na_latent(zb, ck, cv, bias):
    return pl.pallas_call(
        _na_lat_kernel,
        out_shape=jax.ShapeDtypeStruct((DEC_BATCH, TOKENS, NA_WIDTH), BF16),
        grid=(DEC_BATCH, GRID_H),
        in_specs=[
            pl.BlockSpec((1, GRID_W, NA_WIDTH), lambda b, r: (b, r, Z_QNA // NA_WIDTH)),
            pl.BlockSpec((1, TOKENS, NA_WIDTH), lambda b, r: (b, 0, Z_KNA // NA_WIDTH)),
            pl.BlockSpec((1, TOKENS, NA_WIDTH), lambda b, r: (b, 0, Z_VNA // NA_WIDTH)),
            pl.BlockSpec((1, PAST_LEN, NA_WIDTH), lambda b, r: (b, 0, 0)),
            pl.BlockSpec((1, PAST_LEN, NA_WIDTH), lambda b, r: (b, 0, 0)),
            pl.BlockSpec((1, NA_HEADS, GRID_W, NA_KEYS), lambda b, r: (_na_row_class(r), 0, 0, 0)),
        ],
        out_specs=pl.BlockSpec((1, GRID_W, NA_WIDTH), lambda b, r: (b, r, 0)),
        compiler_params=pltpu.CompilerParams(dimension_semantics=("parallel", "arbitrary"),
                                             vmem_limit_bytes=VMEM_LIMIT),
        name="na_lat",
    )(zb, zb, zb, ck, cv, bias)


def _na_bias_table(tab):
    cls = np.arange(NA_WIN_R)[:, None]
    j = np.arange(NA_WIN_R)[None, :]
    dr = j - cls + (NA_WIN_R - 1)
    q = np.arange(GRID_W)
    kc = np.arange(GRID_W)
    cs = np.clip(q - NA_WIN_C // 2, 0, GRID_W - NA_WIN_C)
    valid = (kc[None, :] >= cs[:, None]) & (kc[None, :] < cs[:, None] + NA_WIN_C)
    dc = np.clip(kc[None, :] - q[:, None], -(NA_WIN_C - 1), NA_WIN_C - 1) + (NA_WIN_C - 1)
    b = tab.astype(F32)[:, dr[:, :, None, None], dc[None, None, :, :]]
    b = jnp.where(valid[None, None, None], b, -jnp.inf)
    return b.transpose(1, 0, 3, 2, 4).reshape(NA_WIN_R, NA_HEADS, GRID_W, NA_KEYS)


GQA_KEYS = PAST_LEN + DEC_SEQ


def _gqa_kernel(q_ref, kt_ref, v_ref, o_ref, *, tq, tk):
    q = q_ref[0]
    qs = jnp.concatenate([q[:, r * HEAD_DIM:(r + 1) * HEAD_DIM] for r in range(GQA_REP)], axis=0)
    rows = GQA_REP * tq

    def body(j, carry):
        m, l, acc = carry
        off = pl.multiple_of(j * tk, tk)
        kt = kt_ref[0, 0, :, pl.ds(off, tk)]
        v = v_ref[0, 0, pl.ds(off, tk), :]
        s = _dot(qs, kt)
        m_new = jnp.maximum(m, jnp.max(s, axis=-1, keepdims=True))
        alpha = jnp.exp(m - m_new)
        p = jnp.exp(s - m_new)
        l = alpha * l + jnp.sum(p, axis=-1, keepdims=True)
        acc = alpha * acc + _dot(p.astype(BF16), v)
        return m_new, l, acc

    init = (jnp.full((rows, 1), -jnp.inf, F32), jnp.zeros((rows, 1), F32), jnp.zeros((rows, HEAD_DIM), F32))
    m, l, acc = lax.fori_loop(0, GQA_KEYS // tk, body, init)
    out = acc / l
    o_ref[0] = jnp.concatenate([out[r * tq:(r + 1) * tq] for r in range(GQA_REP)], axis=1).astype(BF16)


def _gqa_latent(zb, kt, v):
    tq, tk = 256, 256
    return pl.pallas_call(
        functools.partial(_gqa_kernel, tq=tq, tk=tk),
        out_shape=jax.ShapeDtypeStruct((DEC_BATCH, TOKENS, GQA_WIDTH), BF16),
        grid=(DEC_BATCH, GQA_KV_HEADS, TOKENS // tq),
        in_specs=[
            pl.BlockSpec((1, tq, 256), lambda b, g, i: (b, i, Z_QG // 256 + g)),
            pl.BlockSpec((1, 1, HEAD_DIM, GQA_KEYS), lambda b, g, i: (b, g, 0, 0)),
            pl.BlockSpec((1, 1, GQA_KEYS, HEAD_DIM), lambda b, g, i: (b, g, 0, 0)),
        ],
        out_specs=pl.BlockSpec((1, tq, 256), lambda b, g, i: (b, i, g)),
        compiler_params=pltpu.CompilerParams(dimension_semantics=("parallel", "parallel", "arbitrary"),
                                             vmem_limit_bytes=VMEM_LIMIT),
        name="gqa_lat",
    )(zb, kt, v)


FF_CHUNK = 256


def _ffn_kernel(x_ref, ssm_ref, na_ref, gqa_ref, mod_ref, wout_ref, ln1g_ref, ln1b_ref, wa_ref, wg_ref, wo_ref,
                ln2g_ref, ln2b_ref, o_ref, x1_s, h2_s, acc_s):
    k = pl.program_id(2)

    @pl.when(k == 0)
    def _mix():
        ssm = jnp.concatenate([ssm_ref[0, 0], ssm_ref[0, 1]], axis=1)
        o = (_dot(ssm.astype(BF16), wout_ref[0:SSM_WIDTH, :])
             + _dot(na_ref[0], wout_ref[SSM_WIDTH:SSM_WIDTH + NA_WIDTH, :])
             + _dot(gqa_ref[0], wout_ref[SSM_WIDTH + NA_WIDTH:, :]))
        y = DEEPNORM_ALPHA * x_ref[0] + mod_ref[0, 2:3, :] * o
        x1 = _ln(y) * ln1g_ref[...] + ln1b_ref[...]
        x1_s[...] = x1
        h2 = _ln(x1) * (1.0 + mod_ref[0, 4:5, :]) + mod_ref[0, 3:4, :]
        h2_s[...] = h2.astype(BF16)

    h2 = h2_s[...]
    a = _dot(h2, wa_ref[...])
    gt = _dot(h2, wg_ref[...])
    f = (a * _sigmoid(a) * gt).astype(BF16)
    contrib = _dot(f, wo_ref[...])

    @pl.when(k == 0)
    def _():
        acc_s[...] = contrib

    @pl.when(k != 0)
    def _():
        acc_s[...] = acc_s[...] + contrib

    @pl.when(k == pl.num_programs(2) - 1)
    def _out():
        y = DEEPNORM_ALPHA * x1_s[...] + mod_ref[0, 5:6, :] * acc_s[...]
        o_ref[0] = _ln(y) * ln2g_ref[...] + ln2b_ref[...]


def _out_ffn(x, ssm, na, gqa, mod, w_out, ln1g, ln1b, w_ffn_in, w_ffn_out, ln2g, ln2b):
    bg = x.shape[0]
    tm = 1024
    nk = D_FF // FF_CHUNK
    row = lambda b, i, k: (0, 0)
    return pl.pallas_call(
        _ffn_kernel,
        out_shape=jax.ShapeDtypeStruct((bg, TOKENS, D_MODEL), F32),
        grid=(bg, TOKENS // tm, nk),
        in_specs=[
            pl.BlockSpec((1, tm, D_MODEL), lambda b, i, k: (b, i, 0)),
            pl.BlockSpec((1, 2, tm, LANES), lambda b, i, k: (b, 0, i, 0)),
            pl.BlockSpec((1, tm, NA_WIDTH), lambda b, i, k: (b, i, 0)),
            pl.BlockSpec((1, tm, GQA_WIDTH), lambda b, i, k: (b, i, 0)),
            pl.BlockSpec((1, 6, D_MODEL), lambda b, i, k: (b, 0, 0)),
            pl.BlockSpec((D_MODEL, D_MODEL), row),
            pl.BlockSpec((1, D_MODEL), row),
            pl.BlockSpec((1, D_MODEL), row),
            pl.BlockSpec((D_MODEL, FF_CHUNK), lambda b, i, k: (0, k)),
            pl.BlockSpec((D_MODEL, FF_CHUNK), lambda b, i, k: (0, k + nk)),
            pl.BlockSpec((FF_CHUNK, D_MODEL), lambda b, i, k: (k, 0)),
            pl.BlockSpec((1, D_MODEL), row),
            pl.BlockSpec((1, D_MODEL), row),
        ],
        out_specs=pl.BlockSpec((1, tm, D_MODEL), lambda b, i, k: (b, i, 0)),
        scratch_shapes=[pltpu.VMEM((tm, D_MODEL), F32), pltpu.VMEM((tm, D_MODEL), BF16),
                        pltpu.VMEM((tm, D_MODEL), F32)],
        compiler_params=pltpu.CompilerParams(dimension_semantics=("parallel", "parallel", "arbitrary"),
                                             vmem_limit_bytes=VMEM_LIMIT),
        name="ffn",
    )(x, ssm, na, gqa, mod, w_out, ln1g, ln1b, w_ffn_in, w_ffn_in, w_ffn_out, ln2g, ln2b)


def _rope_tables():
    nf = HEAD_DIM // 4
    t = jnp.arange(DEC_SEQ)
    inv = ROPE_THETA ** (-jnp.arange(nf, dtype=F32) / nf)
    rows = (t // GRID_W).astype(F32)
    cols = (t % GRID_W).astype(F32)
    ang = jnp.stack([rows[:, None] * inv, cols[:, None] * inv], axis=1)
    cos = jnp.repeat(jnp.cos(ang)[:, :, None, :], 2, axis=2).reshape(DEC_SEQ, HEAD_DIM)
    sin = jnp.sin(ang)[:, :, None, :] * jnp.asarray([-1.0, 1.0], F32)[None, None, :, None]
    sin = sin.reshape(DEC_SEQ, HEAD_DIM)
    return jnp.tile(cos, (1, 2)), jnp.tile(sin, (1, 2))


def _head_mean_matrix():
    h = np.arange(GQA_WIDTH) // HEAD_DIM
    return jnp.asarray((h[:, None] == h[None, :]).astype(np.float32) / HEAD_DIM, BF16)


def kernel(x_prompt, x_sample, c, cache_na_k, cache_na_v, cache_gqa_k, cache_gqa_v, state_ssm_re, state_ssm_im,
           c_ctx, w_ada, b_ada, w_in, w_out, q_norm_g, k_norm_g, na_bias, ssm_lam_re, ssm_lam_im, ssm_log_dt,
           ssm_b_re, ssm_b_im, ssm_c_re, ssm_c_im, ssm_d, w_ssm_glu, ln1_g, ln1_b, ln2_g, ln2_b,
           w_ffn_in, w_ffn_out):
    cond8 = jnp.concatenate([c_ctx[None, :], c, jnp.zeros((8 - 1 - DEC_BATCH, D_MODEL), F32)], axis=0)
    mod = _modulation(cond8, w_ada, b_ada).reshape(DEPTH, 8, 6, D_MODEL)
    cos, sin = _rope_tables()
    hm = _head_mean_matrix()
    w_in_b = w_in.astype(BF16)
    w_out_b = w_out.astype(BF16)
    w_ffn_in_b = w_ffn_in.astype(BF16)
    w_ffn_out_b = w_ffn_out.astype(BF16)
    w_glu_b = w_ssm_glu.astype(BF16)

    y_ctx = x_prompt.reshape(1, TOKENS, D_MODEL)
    y_lat = x_sample
    zeros_h0 = jnp.zeros((1, 2, SSM_GROUPS, 256), F32)
    na_k, na_v, g_k, g_v, st_re, st_im = [], [], [], [], [], []
    for l in range(DEPTH):
        qg = jnp.tile(q_norm_g[l], GQA_HEADS)[None, :]
        kg = jnp.tile(k_norm_g[l], GQA_KV_HEADS)[None, :]
        ops = _ssm_operators(ssm_lam_re[l], ssm_lam_im[l], ssm_log_dt[l], ssm_b_re[l], ssm_b_im[l],
                             ssm_c_re[l], ssm_c_im[l])
        dskip = ssm_d[l][None, :]
        ln = (ln1_g[l][None, :], ln1_b[l][None, :], ln2_g[l][None, :], ln2_b[l][None, :])

        mod_c = mod[l, 0:1]
        u, zb, kv_na, kv_g = _in_projection(y_ctx, mod_c, w_in_b[l], qg, kg, hm, cos, sin, rope=False, emit_kv=True)
        ssm_o, fin = _ssm(u, ops, zeros_h0, dskip, w_glu_b[l], n_seq=BATCH, emit_final=True)
        na_o, gqa_o = _ctx_attention(zb)
        y_ctx = _out_ffn(y_ctx, ssm_o, na_o, gqa_o, mod_c, w_out_b[l], ln[0], ln[1], w_ffn_in_b[l], w_ffn_out_b[l],
                         ln[2], ln[3])
        kv_na = kv_na.reshape(BATCH, SEQ, 2, NA_HEADS, HEAD_DIM)
        kv_g = kv_g.reshape(BATCH, SEQ, 2, GQA_KV_HEADS, HEAD_DIM)
        na_k.append(kv_na[:, :, 0]); na_v.append(kv_na[:, :, 1]); g_k.append(kv_g[:, :, 0]); g_v.append(kv_g[:, :, 1])
        fin = fin.reshape(2, BATCH, SSM_GROUPS, 2, SSM_STATE).transpose(1, 0, 2, 3, 4)
        st_re.append(fin[:, :, :, 0]); st_im.append(fin[:, :, :, 1])

        mod_s = mod[l, 1:1 + DEC_BATCH]
        u, zb = _in_projection(y_lat, mod_s, w_in_b[l], qg, kg, hm, cos, sin, rope=True, emit_kv=False)
        s_re, s_im = state_ssm_re[:, l].astype(F32), state_ssm_im[:, l].astype(F32)
        h0 = jnp.concatenate([s_re, s_im, s_im, s_re], axis=-1)
        ssm_o = _ssm(u, ops, h0, dskip, w_glu_b[l], n_seq=1, emit_final=False)[0]
        ck = cache_na_k[:, l].reshape(DEC_BATCH, PAST_LEN, NA_WIDTH).astype(BF16)
        cv = cache_na_v[:, l].reshape(DEC_BATCH, PAST_LEN, NA_WIDTH).astype(BF16)
        na_o = _na_latent(zb, ck, cv, _na_bias_table(na_bias[l]))
        k_new = zb[:, :, Z_KG:Z_KG + GQA_KV_WIDTH].reshape(DEC_BATCH, DEC_SEQ, GQA_KV_HEADS, HEAD_DIM)
        v_new = zb[:, :, Z_VG:Z_VG + GQA_KV_WIDTH].reshape(DEC_BATCH, DEC_SEQ, GQA_KV_HEADS, HEAD_DIM)
        k_all = jnp.concatenate([cache_gqa_k[:, l].astype(BF16), k_new], axis=1)
        v_all = jnp.concatenate([cache_gqa_v[:, l].astype(BF16), v_new], axis=1)
        gqa_o = _gqa_latent(zb, k_all.transpose(0, 2, 3, 1), v_all.transpose(0, 2, 1, 3))
        y_lat = _out_ffn(y_lat, ssm_o, na_o, gqa_o, mod_s, w_out_b[l], ln[0], ln[1], w_ffn_in_b[l], w_ffn_out_b[l],
                         ln[2], ln[3])

    stack = lambda xs: jnp.stack(xs, axis=1)
    return (y_ctx.reshape(BATCH, SEQ, D_MODEL), y_lat, stack(na_k), stack(na_v), stack(g_k), stack(g_v),
            stack(st_re), stack(st_im))
```

```python
import functools
import math

import numpy as np
import jax
import jax.numpy as jnp
from jax import lax
from jax.experimental import pallas as pl
from jax.experimental.pallas import tpu as pltpu

F32 = jnp.float32
BF16 = jnp.bfloat16

D_MODEL = 1024
BATCH = 16
SEQ = 256
DEPTH = 2
DEC_BATCH = 2
DEC_SEQ = 4096
PAST_LEN = 256
GRID_W = 64
GRID_H = DEC_SEQ // GRID_W
HEAD_DIM = 64
SSM_WIDTH = 256
SSM_GROUP_CH = 16
SSM_GROUPS = 16
SSM_STATE = 64
NA_HEADS = 4
NA_WIDTH = 256
NA_WIN_R = 8
NA_WIN_C = 16
GQA_HEADS = 8
GQA_KV_HEADS = 2
GQA_REP = 4
GQA_WIDTH = 512
GQA_KV_WIDTH = 128
IN_WIDTH = 1792
D_FF = 2816
ROPE_THETA = 10000.0
LN_EPS = 1e-6
RMS_EPS = 1e-6
DEEPNORM_ALPHA = (2 * DEPTH) ** 0.25
QK_SCALE = HEAD_DIM ** -0.5
LOG2E = math.log2(math.e)

TOKENS = 4096
SSM_CHUNK = 16
N_CHUNK = TOKENS // SSM_CHUNK
Z_WIDTH = IN_WIDTH - SSM_WIDTH
Z_QNA, Z_KNA, Z_VNA, Z_QG, Z_KG, Z_VG = 0, 256, 512, 768, 1280, 1408

VMEM_LIMIT = 56 * 1024 * 1024
LANES = 128


def _sigmoid(x):
    return 1.0 / (1.0 + jnp.exp(-x))


def _ln(x):
    mu = jnp.mean(x, axis=-1, keepdims=True)
    xc = x - mu
    var = jnp.mean(xc * xc, axis=-1, keepdims=True)
    return xc * lax.rsqrt(var + LN_EPS)


def _dot(a, b):
    return jnp.dot(a, b, preferred_element_type=F32)


def _dot_nt(a, b):
    return lax.dot_general(a, b, (((1,), (1,)), ((), ())), preferred_element_type=F32)


def _mod_kernel(c_ref, w_ref, b_ref, o_ref):
    c = c_ref[...]
    s = c * _sigmoid(c)
    o_ref[0] = jnp.dot(s, w_ref[0], preferred_element_type=F32, precision=lax.Precision.HIGHEST) + b_ref[0]


def _modulation(cond8, w_ada, b_ada):
    nb = 1536
    return pl.pallas_call(
        _mod_kernel,
        out_shape=jax.ShapeDtypeStruct((DEPTH, 8, 6 * D_MODEL), F32),
        grid=(DEPTH, 6 * D_MODEL // nb),
        in_specs=[
            pl.BlockSpec((8, D_MODEL), lambda l, j: (0, 0)),
            pl.BlockSpec((1, D_MODEL, nb), lambda l, j: (l, 0, j)),
            pl.BlockSpec((1, 1, nb), lambda l, j: (l, 0, j)),
        ],
        out_specs=pl.BlockSpec((1, 8, nb), lambda l, j: (l, 0, j)),
        compiler_params=pltpu.CompilerParams(dimension_semantics=("parallel", "parallel"),
                                             vmem_limit_bytes=VMEM_LIMIT),
        name="mod",
    )(cond8, w_ada, b_ada.reshape(DEPTH, 1, 6 * D_MODEL))


def _swap16(x):
    w = x.shape[-1]
    lane = lax.broadcasted_iota(jnp.int32, x.shape, x.ndim - 1)
    return jnp.where((lane & 16) != 0, pltpu.roll(x, 16, x.ndim - 1), pltpu.roll(x, w - 16, x.ndim - 1))


def _win_kernel(x_ref, mod_ref, w_ref, qg_ref, kg_ref, hm_ref, cos_ref, sin_ref, *out_refs, latent):
    u_ref, z_ref = out_refs[0], out_refs[1]
    x = x_ref[0]
    h = _ln(x) * (1.0 + mod_ref[0, 1:2, :]) + mod_ref[0, 0:1, :]
    z = _dot(h.astype(BF16), w_ref[0])
    u_ref[0, 0] = z[:, 0:LANES]
    u_ref[0, 1] = z[:, LANES:SSM_WIDTH]
    q_na = z[:, 256:512] * QK_SCALE
    kv_na = z[:, 512:1024]
    q_g = z[:, 1024:1536]
    k_g = z[:, 1536:1664]
    v_g = z[:, 1664:1792]
    q_ms = _dot((q_g * q_g).astype(BF16), hm_ref[...])
    k_ms = _dot((k_g * k_g).astype(BF16), hm_ref[0:GQA_KV_WIDTH, 0:GQA_KV_WIDTH])
    q_n = q_g * lax.rsqrt(q_ms + RMS_EPS) * qg_ref[0]
    k_n = k_g * lax.rsqrt(k_ms + RMS_EPS) * kg_ref[0]
    if latent:
        kt_ref, v1_ref = out_refs[2], out_refs[3]
        cos = cos_ref[...]
        sin = sin_ref[...]
        cos4 = jnp.concatenate([cos] * 4, axis=1)
        sin4 = jnp.concatenate([sin] * 4, axis=1)
        q_n = (q_n * cos4 + _swap16(q_n) * sin4) * (QK_SCALE * LOG2E)
        k_n = k_n * cos + _swap16(k_n) * sin
        kt_ref[0] = k_n.T.astype(BF16)
        lane = lax.broadcasted_iota(jnp.int32, v_g.shape, 1)
        ones_col = jnp.where(lane == HEAD_DIM, 1.0, 0.0)
        v1_ref[0, 0] = jnp.where(lane < HEAD_DIM, v_g, ones_col).astype(BF16)
        v1_ref[0, 1] = jnp.where(lane < HEAD_DIM, pltpu.roll(v_g, HEAD_DIM, 1), ones_col).astype(BF16)
    else:
        out_refs[2][0] = kv_na
        out_refs[3][0] = jnp.concatenate([k_n, v_g], axis=1)
        q_n = q_n * QK_SCALE
    zb = jnp.concatenate([q_na, kv_na, q_n, k_n, v_g], axis=1)
    z_ref[0] = zb.astype(BF16)


def _in_projection(l, x, mod, w_in, qg, kg, hm, cos, sin, *, latent):
    bg = x.shape[0]
    tm = 512
    out_shape = [jax.ShapeDtypeStruct((bg, 2, TOKENS, LANES), F32),
                 jax.ShapeDtypeStruct((bg, TOKENS, Z_WIDTH), BF16)]
    out_specs = [pl.BlockSpec((1, 2, tm, LANES), lambda b, i: (b, 0, i, 0)),
                 pl.BlockSpec((1, tm, Z_WIDTH), lambda b, i: (b, i, 0))]
    if latent:
        out_shape += [jax.ShapeDtypeStruct((bg, GQA_KV_WIDTH, TOKENS), BF16),
                      jax.ShapeDtypeStruct((bg, GQA_KV_HEADS, TOKENS, LANES), BF16)]
        out_specs += [pl.BlockSpec((1, GQA_KV_WIDTH, tm), lambda b, i: (b, 0, i)),
                      pl.BlockSpec((1, GQA_KV_HEADS, tm, LANES), lambda b, i: (b, 0, i, 0))]
    else:
        out_shape += [jax.ShapeDtypeStruct((bg, TOKENS, 2 * NA_WIDTH), F32),
                      jax.ShapeDtypeStruct((bg, TOKENS, 2 * GQA_KV_WIDTH), F32)]
        out_specs += [pl.BlockSpec((1, tm, 2 * NA_WIDTH), lambda b, i: (b, i, 0)),
                      pl.BlockSpec((1, tm, 2 * GQA_KV_WIDTH), lambda b, i: (b, i, 0))]
    return pl.pallas_call(
        functools.partial(_win_kernel, latent=latent),
        out_shape=out_shape,
        grid=(bg, TOKENS // tm),
        in_specs=[
            pl.BlockSpec((1, tm, D_MODEL), lambda b, i: (b, i, 0)),
            pl.BlockSpec((1, 6, D_MODEL), lambda b, i: (b, 0, 0)),
            pl.BlockSpec((1, D_MODEL, IN_WIDTH), lambda b, i: (l, 0, 0)),
            pl.BlockSpec((1, 1, GQA_WIDTH), lambda b, i: (l, 0, 0)),
            pl.BlockSpec((1, 1, GQA_KV_WIDTH), lambda b, i: (l, 0, 0)),
            pl.BlockSpec((GQA_WIDTH, GQA_WIDTH), lambda b, i: (0, 0)),
            pl.BlockSpec((tm, LANES), lambda b, i: (i, 0)),
            pl.BlockSpec((tm, LANES), lambda b, i: (i, 0)),
        ],
        out_specs=out_specs,
        compiler_params=pltpu.CompilerParams(dimension_semantics=("parallel", "parallel"),
                                             vmem_limit_bytes=VMEM_LIMIT),
        name="win_lat" if latent else "win_ctx",
    )(x, mod, w_in, qg, kg, hm, cos, sin)


def _ssm_kernel(u_ref, m_ref, bend_ref, cpow_ref, a1_ref, a2_ref, h0_ref, d_ref, wglu_ref, *refs,
                n_seq, emit_final):
    if emit_final:
        out_ref, fin_ref, y_s, yt_s, u_s, inj_s, injp_s, hp_s, he_s = refs
    else:
        out_ref, y_s, yt_s, u_s, inj_s, injp_s, hp_s = refs
        fin_ref = he_s = None
    d = pl.program_id(1)
    chunks_per_seq = N_CHUNK // n_seq

    def chunk_rows(t):
        sl = pl.ds(t, N_CHUNK, stride=SSM_CHUNK)
        return jnp.concatenate([u_ref[0, 0, sl, :], u_ref[0, 1, sl, :]], axis=1)

    @pl.when(d == 0)
    def _build():
        for t in range(SSM_CHUNK):
            a_t = chunk_rows(t)
            yt_s[:, t * SSM_GROUP_CH:(t + 1) * SSM_GROUP_CH, :] = a_t.T.reshape(SSM_GROUPS, SSM_GROUP_CH, N_CHUNK)

        def tr(g, carry):
            u_s[g] = yt_s[g].T.astype(BF16)
            return carry
        lax.fori_loop(0, SSM_GROUPS, tr, 0)

    def proj(g, carry):
        ug = u_s[g]
        y = _dot(ug, m_ref[0, 0, g])

        @pl.when(d == 0)
        def _():
            y_s[g] = y

        @pl.when(d != 0)
        def _():
            y_s[g] = y_s[g] + y
        inj = _dot(ug, bend_ref[0, 0, g])
        inj_s[pl.ds(g, N_CHUNK, stride=SSM_GROUPS), :] = inj[:, :LANES]
        injp_s[pl.ds(g, N_CHUNK, stride=SSM_GROUPS), :] = inj[:, LANES:]
        return carry
    lax.fori_loop(0, SSM_GROUPS, proj, 0)

    a1 = a1_ref[0, 0]
    a2 = a2_ref[0, 0]
    h0 = h0_ref[0, 0]

    def step(j, carry):
        s, sp = carry
        n = j + d * (N_CHUNK - 1 - 2 * j)
        if n_seq > 1:
            keep = (j % chunks_per_seq != 0).astype(F32)
            s = s * keep
            sp = sp * keep
        row = pl.multiple_of(n * SSM_GROUPS, SSM_GROUPS)
        hp_s[pl.ds(row, SSM_GROUPS), :] = s
        s_new = a1 * s + a2 * sp + inj_s[pl.ds(row, SSM_GROUPS), :]
        sp_new = a1 * sp - a2 * s + injp_s[pl.ds(row, SSM_GROUPS), :]
        if emit_final:
            he_s[pl.ds(row, SSM_GROUPS), :] = s_new
        return s_new, sp_new
    lax.fori_loop(0, N_CHUNK, step, (h0[:, :LANES], h0[:, LANES:]))

    def carry_in(g, carry):
        hp = hp_s[pl.ds(g, N_CHUNK, stride=SSM_GROUPS), :]
        y_s[g] = y_s[g] + _dot(hp.astype(BF16), cpow_ref[0, 0, g])
        return carry
    lax.fori_loop(0, SSM_GROUPS, carry_in, 0)

    if emit_final:
        last = (chunks_per_seq - 1) * (1 - d)
        for sq in range(n_seq):
            row = pl.multiple_of((sq * chunks_per_seq + last) * SSM_GROUPS, SSM_GROUPS)
            fin_ref[0, 0, sq * SSM_GROUPS:(sq + 1) * SSM_GROUPS, :] = he_s[pl.ds(row, SSM_GROUPS), :]

    @pl.when(d == 1)
    def _finish():
        def tr(g, carry):
            yt_s[g] = y_s[g].T
            return carry
        lax.fori_loop(0, SSM_GROUPS, tr, 0)
        dskip = d_ref[0]
        wglu = wglu_ref[0]
        for t in range(SSM_CHUNK):
            blk = yt_s[:, t * SSM_GROUP_CH:(t + 1) * SSM_GROUP_CH, :].reshape(SSM_WIDTH, N_CHUNK)
            y = blk.T + chunk_rows(t) * dskip
            gl = jax.nn.gelu(y, approximate=True)
            gate = _dot(gl.astype(BF16), wglu)
            o = gl * _sigmoid(gate)
            out_ref[0, 0, pl.ds(t, N_CHUNK, stride=SSM_CHUNK), :] = o[:, :LANES]
            out_ref[0, 1, pl.ds(t, N_CHUNK, stride=SSM_CHUNK), :] = o[:, LANES:]


def _ssm(l, u, ops, h0, dskip, wglu, *, n_seq, emit_final):
    bg = u.shape[0]
    m, bend, cpow, a1, a2 = ops
    out_shape = [jax.ShapeDtypeStruct((bg, 2, TOKENS, LANES), F32)]
    out_specs = [pl.BlockSpec((1, 2, TOKENS, LANES), lambda b, d: (b, 0, 0, 0))]
    scratch = [pltpu.VMEM((SSM_GROUPS, N_CHUNK, 256), F32),
               pltpu.VMEM((SSM_GROUPS, 256, N_CHUNK), F32),
               pltpu.VMEM((SSM_GROUPS, N_CHUNK, 256), BF16),
               pltpu.VMEM((N_CHUNK * SSM_GROUPS, LANES), F32),
               pltpu.VMEM((N_CHUNK * SSM_GROUPS, LANES), F32),
               pltpu.VMEM((N_CHUNK * SSM_GROUPS, LANES), F32)]
    if emit_final:
        out_shape.append(jax.ShapeDtypeStruct((bg, 2, n_seq * SSM_GROUPS, LANES), F32))
        out_specs.append(pl.BlockSpec((1, 1, n_seq * SSM_GROUPS, LANES), lambda b, d: (b, d, 0, 0)))
        scratch.append(pltpu.VMEM((N_CHUNK * SSM_GROUPS, LANES), F32))
    return pl.pallas_call(
        functools.partial(_ssm_kernel, n_seq=n_seq, emit_final=emit_final),
        out_shape=out_shape,
        grid=(bg, 2),
        in_specs=[
            pl.BlockSpec((1, 2, TOKENS, LANES), lambda b, d: (b, 0, 0, 0)),
            pl.BlockSpec((1, 1, SSM_GROUPS, 256, 256), lambda b, d: (l, d, 0, 0, 0)),
            pl.BlockSpec((1, 1, SSM_GROUPS, 256, 256), lambda b, d: (l, d, 0, 0, 0)),
            pl.BlockSpec((1, 1, SSM_GROUPS, 128, 256), lambda b, d: (l, d, 0, 0, 0)),
            pl.BlockSpec((1, 1, SSM_GROUPS, LANES), lambda b, d: (l, d, 0, 0)),
            pl.BlockSpec((1, 1, SSM_GROUPS, LANES), lambda b, d: (l, d, 0, 0)),
            pl.BlockSpec((1, 1, SSM_GROUPS, 256), lambda b, d: (b, d, 0, 0)),
            pl.BlockSpec((1, 1, SSM_WIDTH), lambda b, d: (l, 0, 0)),
            pl.BlockSpec((1, SSM_WIDTH, SSM_WIDTH), lambda b, d: (l, 0, 0)),
        ],
        out_specs=out_specs,
        scratch_shapes=scratch,
        compiler_params=pltpu.CompilerParams(dimension_semantics=("parallel", "arbitrary"),
                                             vmem_limit_bytes=VMEM_LIMIT),
        name="ssm_ctx" if emit_final else "ssm_lat",
    )(u, m, bend, cpow, a1, a2, h0, dskip, wglu)


def _ssm_operators(lam_re, lam_im, log_dt, b_re, b_im, c_re, c_im):
    hp = lax.Precision.HIGHEST
    t_n = SSM_CHUNK
    lr = jnp.minimum(lam_re.astype(F32), -1e-4)
    li = lam_im.astype(F32)
    dt = jnp.exp(log_dt.astype(F32))[..., None]
    k = jnp.arange(t_n + 1, dtype=F32)[:, None, None, None, None]
    mag = jnp.exp(lr * dt * k)
    ang = li * dt * k
    p_r, p_i = mag * jnp.cos(ang), mag * jnp.sin(ang)
    n_r, n_i = p_r[1] - 1.0, p_i[1]
    den = lr * lr + li * li
    q_r, q_i = (n_r * lr + n_i * li) / den, (n_i * lr - n_r * li) / den
    bb_r = q_r[..., None] * b_re - q_i[..., None] * b_im
    bb_i = q_r[..., None] * b_im + q_i[..., None] * b_re
    cp_r = c_re[None] * p_r[..., None, :] - c_im[None] * p_i[..., None, :]
    cp_i = c_re[None] * p_i[..., None, :] + c_im[None] * p_r[..., None, :]
    kern = (jnp.einsum('kldgcp,ldgpe->kldgce', cp_r, bb_r, precision=hp)
            - jnp.einsum('kldgcp,ldgpe->kldgce', cp_i, bb_i, precision=hp))
    pad = jnp.zeros((t_n - 1,) + kern.shape[1:2] + kern.shape[3:], F32)
    mats, bends, cpows = [], [], []
    for d in range(2):
        kd = kern[:t_n, :, d]
        band = jnp.concatenate([pad, kd], axis=0) if d == 0 else jnp.concatenate([kd[::-1], pad], axis=0)
        rows = jnp.stack([band[t_n - 1 - t:2 * t_n - 1 - t] for t in range(t_n)])
        mats.append(rows.transpose(2, 3, 0, 5, 1, 4).reshape(DEPTH, SSM_GROUPS, 256, 256))
        e_pr = p_r[:t_n, :, d][::-1] if d == 0 else p_r[:t_n, :, d]
        e_pi = p_i[:t_n, :, d][::-1] if d == 0 else p_i[:t_n, :, d]
        e_r = e_pr[..., None] * bb_r[None, :, d] - e_pi[..., None] * bb_i[None, :, d]
        e_i = e_pr[..., None] * bb_i[None, :, d] + e_pi[..., None] * bb_r[None, :, d]
        e_r = e_r.transpose(1, 2, 0, 4, 3).reshape(DEPTH, SSM_GROUPS, 256, SSM_STATE)
        e_i = e_i.transpose(1, 2, 0, 4, 3).reshape(DEPTH, SSM_GROUPS, 256, SSM_STATE)
        bends.append(jnp.concatenate([e_r, e_i, e_i, e_r], axis=-1))
        o_r = cp_r[1:, :, d] if d == 0 else cp_r[1:, :, d][::-1]
        o_i = cp_i[1:, :, d] if d == 0 else cp_i[1:, :, d][::-1]
        o_r = o_r.transpose(1, 2, 4, 0, 3).reshape(DEPTH, SSM_GROUPS, SSM_STATE, 256)
        o_i = o_i.transpose(1, 2, 4, 0, 3).reshape(DEPTH, SSM_GROUPS, SSM_STATE, 256)
        cpows.append(jnp.concatenate([o_r, -o_i], axis=2))
    a_r, a_i = p_r[t_n], p_i[t_n]
    a1 = jnp.concatenate([a_r, a_r], axis=-1)
    a2 = jnp.concatenate([-a_i, a_i], axis=-1)
    stack = lambda xs: jnp.stack(xs, axis=1).astype(BF16)
    return stack(mats), stack(bends), stack(cpows), a1, a2


def _softmax_pv(s, v):
    m = jnp.max(s, axis=-1, keepdims=True)
    p = jnp.exp(s - m)
    l = jnp.sum(p, axis=-1, keepdims=True)
    return _dot(p.astype(BF16), v) / l


def _ctx_attn_kernel(z_ref, na_ref, gqa_ref):
    z = z_ref[0]
    outs = []
    for h in range(NA_HEADS):
        sl = slice(h * HEAD_DIM, (h + 1) * HEAD_DIM)
        q = z[:, Z_QNA:Z_QNA + NA_WIDTH][:, sl]
        k = z[:, Z_KNA:Z_KNA + NA_WIDTH][:, sl]
        v = z[:, Z_VNA:Z_VNA + NA_WIDTH][:, sl]
        outs.append(_softmax_pv(_dot_nt(q, k), v))
    na_ref[0] = jnp.concatenate(outs, axis=1).astype(BF16)
    outs = []
    for g in range(GQA_KV_HEADS):
        k = z[:, Z_KG + g * HEAD_DIM:Z_KG + (g + 1) * HEAD_DIM]
        v = z[:, Z_VG + g * HEAD_DIM:Z_VG + (g + 1) * HEAD_DIM]
        for r in range(GQA_REP):
            c0 = Z_QG + (g * GQA_REP + r) * HEAD_DIM
            outs.append(_softmax_pv(_dot_nt(z[:, c0:c0 + HEAD_DIM], k), v))
    gqa_ref[0] = jnp.concatenate(outs, axis=1).astype(BF16)


def _ctx_attention(zb):
    z3 = zb.reshape(BATCH, SEQ, Z_WIDTH)
    na, gqa = pl.pallas_call(
        _ctx_attn_kernel,
        out_shape=[jax.ShapeDtypeStruct((BATCH, SEQ, NA_WIDTH), BF16),
                   jax.ShapeDtypeStruct((BATCH, SEQ, GQA_WIDTH), BF16)],
        grid=(BATCH,),
        in_specs=[pl.BlockSpec((1, SEQ, Z_WIDTH), lambda b: (b, 0, 0))],
        out_specs=[pl.BlockSpec((1, SEQ, NA_WIDTH), lambda b: (b, 0, 0)),
                   pl.BlockSpec((1, SEQ, GQA_WIDTH), lambda b: (b, 0, 0))],
        compiler_params=pltpu.CompilerParams(dimension_semantics=("parallel",), vmem_limit_bytes=VMEM_LIMIT),
        name="attn_ctx",
    )(z3)
    return na.reshape(1, TOKENS, NA_WIDTH), gqa.reshape(1, TOKENS, GQA_WIDTH)


NA_KEYS = NA_WIN_R * GRID_W


def _na_lat_kernel(q_ref, k_ref, v_ref, ck_ref, cv_ref, bias_ref, o_ref):
    r = pl.program_id(1)
    rs = jnp.clip(r - NA_WIN_R // 2, 0, GRID_H - NA_WIN_R)
    start = pl.multiple_of(rs * GRID_W, GRID_W)
    q = q_ref[0]
    kw = k_ref[0, pl.ds(start, NA_KEYS), :]
    vw = v_ref[0, pl.ds(start, NA_KEYS), :]
    ck = ck_ref[0, 0]
    cv = cv_ref[0, 0]
    outs = []
    for h in range(NA_HEADS):
        sl = slice(h * HEAD_DIM, (h + 1) * HEAD_DIM)
        qh = q[:, sl]
        s_loc = _dot_nt(qh, kw[:, sl]) + bias_ref[0, 0, h]
        s_ctx = _dot_nt(qh, ck[:, sl])
        m = jnp.maximum(jnp.max(s_loc, axis=-1, keepdims=True), jnp.max(s_ctx, axis=-1, keepdims=True))
        p_loc = jnp.exp(s_loc - m)
        p_ctx = jnp.exp(s_ctx - m)
        l = jnp.sum(p_loc, axis=-1, keepdims=True) + jnp.sum(p_ctx, axis=-1, keepdims=True)
        o = _dot(p_loc.astype(BF16), vw[:, sl]) + _dot(p_ctx.astype(BF16), cv[:, sl])
        outs.append(o / l)
    o_ref[0] = jnp.concatenate(outs, axis=1).astype(BF16)


def _na_row_class(r):
    return r - jnp.clip(r - NA_WIN_R // 2, 0, GRID_H - NA_WIN_R)


def _na_latent(l, zb, ck, cv, bias):
    return pl.pallas_call(
        _na_lat_kernel,
        out_shape=jax.ShapeDtypeStruct((DEC_BATCH, TOKENS, NA_WIDTH), BF16),
        grid=(DEC_BATCH, GRID_H),
        in_specs=[
            pl.BlockSpec((1, GRID_W, NA_WIDTH), lambda b, r: (b, r, Z_QNA // NA_WIDTH)),
            pl.BlockSpec((1, TOKENS, NA_WIDTH), lambda b, r: (b, 0, Z_KNA // NA_WIDTH)),
            pl.BlockSpec((1, TOKENS, NA_WIDTH), lambda b, r: (b, 0, Z_VNA // NA_WIDTH)),
            pl.BlockSpec((1, 1, PAST_LEN, NA_WIDTH), lambda b, r: (b, l, 0, 0)),
            pl.BlockSpec((1, 1, PAST_LEN, NA_WIDTH), lambda b, r: (b, l, 0, 0)),
            pl.BlockSpec((1, 1, NA_HEADS, GRID_W, NA_KEYS), lambda b, r: (l, _na_row_class(r), 0, 0, 0)),
        ],
        out_specs=pl.BlockSpec((1, GRID_W, NA_WIDTH), lambda b, r: (b, r, 0)),
        compiler_params=pltpu.CompilerParams(dimension_semantics=("parallel", "arbitrary"),
                                             vmem_limit_bytes=VMEM_LIMIT),
        name="na_lat",
    )(zb, zb, zb, ck, cv, bias)


def _na_bias_table(tab):
    q = np.arange(GRID_W)
    kc = np.arange(GRID_W)
    cs = np.clip(q - NA_WIN_C // 2, 0, GRID_W - NA_WIN_C)
    valid = (kc[None, :] >= cs[:, None]) & (kc[None, :] < cs[:, None] + NA_WIN_C)
    w = GRID_W - 1
    padded = jnp.pad(tab.astype(F32), ((0, 0), (0, 0), (0, 0), (w, w)))
    off = w + NA_WIN_C - 1
    cols = jnp.stack([padded[..., off - i:off - i + GRID_W] for i in range(GRID_W)], axis=-2)
    cols = jnp.where(valid, cols, -jnp.inf)
    b = jnp.stack([cols[:, :, NA_WIN_R - 1 - c:2 * NA_WIN_R - 1 - c] for c in range(NA_WIN_R)], axis=1)
    return b.transpose(0, 1, 2, 4, 3, 5).reshape(DEPTH, NA_WIN_R, NA_HEADS, GRID_W, NA_KEYS)


GQA_TK = 256
GQA_KEYS = DEC_SEQ + PAST_LEN


def _gqa_kernel(q_ref, ktn_ref, ktc_ref, vn_ref, vc_ref, o_ref, s_s, m_s, *, tq):
    q = q_ref[0]
    qs = jnp.concatenate([q[:, r * HEAD_DIM:(r + 1) * HEAD_DIM] for r in range(GQA_REP)], axis=0)
    n_new = DEC_SEQ // GQA_TK

    def fold(s):
        return jnp.maximum(s[:, :LANES], s[:, LANES:])

    s = _dot(qs, ktc_ref[0])
    s_s[:, DEC_SEQ:GQA_KEYS] = s
    m_s[...] = fold(s)
    for c in range(0, n_new, 2):
        s0 = _dot(qs, ktn_ref[0, :, c * GQA_TK:(c + 1) * GQA_TK])
        s1 = _dot(qs, ktn_ref[0, :, (c + 1) * GQA_TK:(c + 2) * GQA_TK])
        s_s[:, c * GQA_TK:(c + 1) * GQA_TK] = s0
        s_s[:, (c + 1) * GQA_TK:(c + 2) * GQA_TK] = s1
        m_s[...] = jnp.maximum(m_s[...], jnp.maximum(fold(s0), fold(s1)))

    m = jnp.max(m_s[...], axis=-1, keepdims=True)
    m_s[...] = jnp.broadcast_to(m, m_s.shape)

    def probs(s):
        mb = m_s[...]
        return jnp.exp2(s - jnp.concatenate([mb, mb], axis=1)).astype(BF16)

    acc = _dot(probs(s_s[:, DEC_SEQ:GQA_KEYS]), vc_ref[0, 0])
    for c in range(n_new):
        acc = acc + _dot(probs(s_s[:, c * GQA_TK:(c + 1) * GQA_TK]), vn_ref[0, 0, c * GQA_TK:(c + 1) * GQA_TK, :])
    out = acc[:, :HEAD_DIM] / acc[:, HEAD_DIM:HEAD_DIM + 1]
    o_ref[0] = jnp.concatenate([out[r * tq:(r + 1) * tq] for r in range(GQA_REP)], axis=1).astype(BF16)


def _gqa_latent(zb, kt_new, kt_cache, v1_new, v1_cache):
    tq = 128
    rows = GQA_REP * tq
    return pl.pallas_call(
        functools.partial(_gqa_kernel, tq=tq),
        out_shape=jax.ShapeDtypeStruct((DEC_BATCH, TOKENS, GQA_WIDTH), BF16),
        grid=(DEC_BATCH, GQA_KV_HEADS, TOKENS // tq),
        in_specs=[
            pl.BlockSpec((1, tq, 256), lambda b, g, i: (b, i, Z_QG // 256 + g)),
            pl.BlockSpec((1, HEAD_DIM, DEC_SEQ), lambda b, g, i: (b, g, 0)),
            pl.BlockSpec((1, HEAD_DIM, PAST_LEN), lambda b, g, i: (b, g, 0)),
            pl.BlockSpec((1, 1, DEC_SEQ, LANES), lambda b, g, i: (b, g, 0, 0)),
            pl.BlockSpec((1, 1, PAST_LEN, LANES), lambda b, g, i: (b, g, 0, 0)),
        ],
        out_specs=pl.BlockSpec((1, tq, 256), lambda b, g, i: (b, i, g)),
        scratch_shapes=[pltpu.VMEM((rows, GQA_KEYS), F32), pltpu.VMEM((rows, LANES), F32)],
        compiler_params=pltpu.CompilerParams(dimension_semantics=("parallel", "parallel", "arbitrary"),
                                             vmem_limit_bytes=VMEM_LIMIT),
        name="gqa_lat",
    )(zb, kt_new, kt_cache, v1_new, v1_cache)


FF_CHUNK = 256


def _ffn_kernel(x_ref, ssm_ref, na_ref, gqa_ref, mod_ref, wout_ref, ln1g_ref, ln1b_ref, wa_ref, wg_ref, wo_ref,
                ln2g_ref, ln2b_ref, o_ref, x1_s, h2_s, acc_s):
    k = pl.program_id(2)

    @pl.when(k == 0)
    def _mix():
        ssm = jnp.concatenate([ssm_ref[0, 0], ssm_ref[0, 1]], axis=1)
        o = (_dot(ssm.astype(BF16), wout_ref[0, 0:SSM_WIDTH, :])
             + _dot(na_ref[0], wout_ref[0, SSM_WIDTH:SSM_WIDTH + NA_WIDTH, :])
             + _dot(gqa_ref[0], wout_ref[0, SSM_WIDTH + NA_WIDTH:, :]))
        y = DEEPNORM_ALPHA * x_ref[0] + mod_ref[0, 2:3, :] * o
        x1 = _ln(y) * ln1g_ref[0] + ln1b_ref[0]
        x1_s[...] = x1
        h2 = _ln(x1) * (1.0 + mod_ref[0, 4:5, :]) + mod_ref[0, 3:4, :]
        h2_s[...] = h2.astype(BF16)

    h2 = h2_s[...]
    a = _dot(h2, wa_ref[0])
    gt = _dot(h2, wg_ref[0])
    f = (a * _sigmoid(a) * gt).astype(BF16)
    contrib = _dot(f, wo_ref[0])

    @pl.when(k == 0)
    def _():
        acc_s[...] = contrib

    @pl.when(k != 0)
    def _():
        acc_s[...] = acc_s[...] + contrib

    @pl.when(k == pl.num_programs(2) - 1)
    def _out():
        y = DEEPNORM_ALPHA * x1_s[...] + mod_ref[0, 5:6, :] * acc_s[...]
        o_ref[0] = _ln(y) * ln2g_ref[0] + ln2b_ref[0]


def _out_ffn(l, x, ssm, na, gqa, mod, w_out, ln1g, ln1b, w_ffn_in, w_ffn_out, ln2g, ln2b):
    bg = x.shape[0]
    tm = 1024
    nk = D_FF // FF_CHUNK
    vec = pl.BlockSpec((1, 1, D_MODEL), lambda b, i, k: (l, 0, 0))
    return pl.pallas_call(
        _ffn_kernel,
        out_shape=jax.ShapeDtypeStruct((bg, TOKENS, D_MODEL), F32),
        grid=(bg, TOKENS // tm, nk),
        in_specs=[
            pl.BlockSpec((1, tm, D_MODEL), lambda b, i, k: (b, i, 0)),
            pl.BlockSpec((1, 2, tm, LANES), lambda b, i, k: (b, 0, i, 0)),
            pl.BlockSpec((1, tm, NA_WIDTH), lambda b, i, k: (b, i, 0)),
            pl.BlockSpec((1, tm, GQA_WIDTH), lambda b, i, k: (b, i, 0)),
            pl.BlockSpec((1, 6, D_MODEL), lambda b, i, k: (b, 0, 0)),
            pl.BlockSpec((1, D_MODEL, D_MODEL), lambda b, i, k: (l, 0, 0)),
            vec,
            vec,
            pl.BlockSpec((1, D_MODEL, FF_CHUNK), lambda b, i, k: (l, 0, k)),
            pl.BlockSpec((1, D_MODEL, FF_CHUNK), lambda b, i, k: (l, 0, k + nk)),
            pl.BlockSpec((1, FF_CHUNK, D_MODEL), lambda b, i, k: (l, k, 0)),
            vec,
            vec,
        ],
        out_specs=pl.BlockSpec((1, tm, D_MODEL), lambda b, i, k: (b, i, 0)),
        scratch_shapes=[pltpu.VMEM((tm, D_MODEL), F32), pltpu.VMEM((tm, D_MODEL), BF16),
                        pltpu.VMEM((tm, D_MODEL), F32)],
        compiler_params=pltpu.CompilerParams(dimension_semantics=("parallel", "parallel", "arbitrary"),
                                             vmem_limit_bytes=VMEM_LIMIT),
        name="ffn",
    )(x, ssm, na, gqa, mod, w_out, ln1g, ln1b, w_ffn_in, w_ffn_in, w_ffn_out, ln2g, ln2b)


def _rope_tables():
    nf = HEAD_DIM // 4
    t = jnp.arange(DEC_SEQ)
    inv = ROPE_THETA ** (-jnp.arange(nf, dtype=F32) / nf)
    rows = (t // GRID_W).astype(F32)
    cols = (t % GRID_W).astype(F32)
    ang = jnp.stack([rows[:, None] * inv, cols[:, None] * inv], axis=1)
    cos = jnp.repeat(jnp.cos(ang)[:, :, None, :], 2, axis=2).reshape(DEC_SEQ, HEAD_DIM)
    sin = jnp.sin(ang)[:, :, None, :] * jnp.asarray([-1.0, 1.0], F32)[None, None, :, None]
    sin = sin.reshape(DEC_SEQ, HEAD_DIM)
    return jnp.tile(cos, (1, 2)), jnp.tile(sin, (1, 2))


def _head_mean_matrix():
    h = np.arange(GQA_WIDTH) // HEAD_DIM
    return jnp.asarray((h[:, None] == h[None, :]).astype(np.float32) / HEAD_DIM, BF16)


def kernel(x_prompt, x_sample, c, cache_na_k, cache_na_v, cache_gqa_k, cache_gqa_v, state_ssm_re, state_ssm_im,
           c_ctx, w_ada, b_ada, w_in, w_out, q_norm_g, k_norm_g, na_bias, ssm_lam_re, ssm_lam_im, ssm_log_dt,
           ssm_b_re, ssm_b_im, ssm_c_re, ssm_c_im, ssm_d, w_ssm_glu, ln1_g, ln1_b, ln2_g, ln2_b,
           w_ffn_in, w_ffn_out):
    cond8 = jnp.concatenate([c_ctx[None, :], c, jnp.zeros((8 - 1 - DEC_BATCH, D_MODEL), F32)], axis=0)
    mod = _modulation(cond8, w_ada, b_ada).reshape(DEPTH, 8, 6, D_MODEL)
    cos, sin = _rope_tables()
    hm = _head_mean_matrix()
    w_in_b = w_in.astype(BF16)
    w_out_b = w_out.astype(BF16)
    w_ffn_in_b = w_ffn_in.astype(BF16)
    w_ffn_out_b = w_ffn_out.astype(BF16)
    w_glu_b = w_ssm_glu.astype(BF16)
    qg = jnp.tile(q_norm_g, (1, GQA_HEADS))[:, None, :]
    kg = jnp.tile(k_norm_g, (1, GQA_KV_HEADS))[:, None, :]
    dskip = ssm_d[:, None, :]
    ln = (ln1_g[:, None, :], ln1_b[:, None, :], ln2_g[:, None, :], ln2_b[:, None, :])
    ops = _ssm_operators(ssm_lam_re, ssm_lam_im, ssm_log_dt, ssm_b_re, ssm_b_im, ssm_c_re, ssm_c_im)
    na_tab = _na_bias_table(na_bias)
    ck_na = cache_na_k.reshape(DEC_BATCH, DEPTH, PAST_LEN, NA_WIDTH).astype(BF16)
    cv_na = cache_na_v.reshape(DEC_BATCH, DEPTH, PAST_LEN, NA_WIDTH).astype(BF16)
    kt_cache = cache_gqa_k.transpose(1, 0, 3, 4, 2).reshape(DEPTH, DEC_BATCH, GQA_KV_WIDTH, PAST_LEN).astype(BF16)
    cv_g = cache_gqa_v.transpose(1, 0, 3, 2, 4)
    v1_cache = jnp.concatenate([cv_g, jnp.ones(cv_g.shape[:-1] + (1,), F32),
                                jnp.zeros(cv_g.shape[:-1] + (LANES - HEAD_DIM - 1,), F32)], axis=-1).astype(BF16)
    s_re = state_ssm_re.astype(F32).transpose(1, 0, 2, 3, 4)
    s_im = state_ssm_im.astype(F32).transpose(1, 0, 2, 3, 4)
    h0_lat = jnp.concatenate([s_re, s_im, s_im, s_re], axis=-1)
    h0_ctx = jnp.zeros((1, 2, SSM_GROUPS, 256), F32)

    y_ctx = x_prompt.reshape(1, TOKENS, D_MODEL)
    y_lat = x_sample
    kv_na_l, kv_g_l, fin_l = [], [], []
    for l in range(DEPTH):
        mod_c = mod[l, 0:1]
        u, zb, kv_na, kv_g = _in_projection(l, y_ctx, mod_c, w_in_b, qg, kg, hm, cos, sin, latent=False)
        ssm_o, fin = _ssm(l, u, ops, h0_ctx, dskip, w_glu_b, n_seq=BATCH, emit_final=True)
        na_o, gqa_o = _ctx_attention(zb)
        y_ctx = _out_ffn(l, y_ctx, ssm_o, na_o, gqa_o, mod_c, w_out_b, ln[0], ln[1], w_ffn_in_b, w_ffn_out_b,
                         ln[2], ln[3])
        kv_na_l.append(kv_na); kv_g_l.append(kv_g); fin_l.append(fin)

        mod_s = mod[l, 1:1 + DEC_BATCH]
        u, zb, kt_new, v1_new = _in_projection(l, y_lat, mod_s, w_in_b, qg, kg, hm, cos, sin, latent=True)
        ssm_o = _ssm(l, u, ops, h0_lat[l], dskip, w_glu_b, n_seq=1, emit_final=False)[0]
        na_o = _na_latent(l, zb, ck_na, cv_na, na_tab)
        gqa_o = _gqa_latent(zb, kt_new, kt_cache[l], v1_new, v1_cache[l])
        y_lat = _out_ffn(l, y_lat, ssm_o, na_o, gqa_o, mod_s, w_out_b, ln[0], ln[1], w_ffn_in_b, w_ffn_out_b,
                         ln[2], ln[3])

    kv_na = jnp.stack(kv_na_l, axis=0).reshape(DEPTH, BATCH, SEQ, 2, NA_HEADS, HEAD_DIM)
    kv_g = jnp.stack(kv_g_l, axis=0).reshape(DEPTH, BATCH, SEQ, 2, GQA_KV_HEADS, HEAD_DIM)
    kv_na = kv_na.transpose(3, 1, 0, 2, 4, 5)
    kv_g = kv_g.transpose(3, 1, 0, 2, 4, 5)
    fin = jnp.stack(fin_l, axis=0).reshape(DEPTH, 2, BATCH, SSM_GROUPS, 2, SSM_STATE)
    fin = fin.transpose(4, 2, 0, 1, 3, 5)
    return (y_ctx.reshape(BATCH, SEQ, D_MODEL), y_lat, kv_na[0], kv_na[1], kv_g[0], kv_g[1], fin[0], fin[1])
```

```python
import functools
import math

import numpy as np
import jax
import jax.numpy as jnp
from jax import lax
from jax.experimental import pallas as pl
from jax.experimental.pallas import tpu as pltpu

F32 = jnp.float32
BF16 = jnp.bfloat16

D_MODEL = 1024
BATCH = 16
SEQ = 256
DEPTH = 2
DEC_BATCH = 2
DEC_SEQ = 4096
PAST_LEN = 256
GRID_W = 64
GRID_H = DEC_SEQ // GRID_W
HEAD_DIM = 64
SSM_WIDTH = 256
SSM_GROUP_CH = 16
SSM_GROUPS = 16
SSM_STATE = 64
NA_HEADS = 4
NA_WIDTH = 256
NA_WIN_R = 8
NA_WIN_C = 16
GQA_HEADS = 8
GQA_KV_HEADS = 2
GQA_REP = 4
GQA_WIDTH = 512
GQA_KV_WIDTH = 128
IN_WIDTH = 1792
D_FF = 2816
ROPE_THETA = 10000.0
LN_EPS = 1e-6
RMS_EPS = 1e-6
DEEPNORM_ALPHA = (2 * DEPTH) ** 0.25
QK_SCALE = HEAD_DIM ** -0.5
LOG2E = math.log2(math.e)

TOKENS = 4096
SSM_CHUNK = 16
N_CHUNK = TOKENS // SSM_CHUNK
Z_WIDTH = IN_WIDTH - SSM_WIDTH
Z_QNA, Z_KNA, Z_VNA, Z_QG, Z_KG, Z_VG = 0, 256, 512, 768, 1280, 1408

VMEM_LIMIT = 56 * 1024 * 1024
LANES = 128


def _sigmoid(x):
    return 1.0 / (1.0 + jnp.exp(-x))


def _ln(x):
    mu = jnp.mean(x, axis=-1, keepdims=True)
    xc = x - mu
    var = jnp.mean(xc * xc, axis=-1, keepdims=True)
    return xc * lax.rsqrt(var + LN_EPS)


def _dot(a, b):
    return jnp.dot(a, b, preferred_element_type=F32)


def _dot_nt(a, b):
    return lax.dot_general(a, b, (((1,), (1,)), ((), ())), preferred_element_type=F32)


def _mod_kernel(c_ref, w_ref, b_ref, o_ref):
    c = c_ref[...]
    s = c * _sigmoid(c)
    o_ref[0] = jnp.dot(s, w_ref[0], preferred_element_type=F32, precision=lax.Precision.HIGHEST) + b_ref[0]


def _modulation(cond8, w_ada, b_ada):
    nb = 1536
    return pl.pallas_call(
        _mod_kernel,
        out_shape=jax.ShapeDtypeStruct((DEPTH, 8, 6 * D_MODEL), F32),
        grid=(DEPTH, 6 * D_MODEL // nb),
        in_specs=[
            pl.BlockSpec((8, D_MODEL), lambda l, j: (0, 0)),
            pl.BlockSpec((1, D_MODEL, nb), lambda l, j: (l, 0, j)),
            pl.BlockSpec((1, 1, nb), lambda l, j: (l, 0, j)),
        ],
        out_specs=pl.BlockSpec((1, 8, nb), lambda l, j: (l, 0, j)),
        compiler_params=pltpu.CompilerParams(dimension_semantics=("parallel", "parallel"),
                                             vmem_limit_bytes=VMEM_LIMIT),
        name="mod",
    )(cond8, w_ada, b_ada.reshape(DEPTH, 1, 6 * D_MODEL))


def _swap16(x):
    w = x.shape[-1]
    lane = lax.broadcasted_iota(jnp.int32, x.shape, x.ndim - 1)
    return jnp.where((lane & 16) != 0, pltpu.roll(x, 16, x.ndim - 1), pltpu.roll(x, w - 16, x.ndim - 1))


def _win_kernel(x_ref, mod_ref, w_ref, qg_ref, kg_ref, hm_ref, cos_ref, sin_ref, *out_refs, latent):
    u_ref, z_ref = out_refs[0], out_refs[1]
    x = x_ref[0]
    h = _ln(x) * (1.0 + mod_ref[0, 1:2, :]) + mod_ref[0, 0:1, :]
    z = _dot(h.astype(BF16), w_ref[0])
    u_ref[0, 0] = z[:, 0:LANES]
    u_ref[0, 1] = z[:, LANES:SSM_WIDTH]
    q_na = z[:, 256:512] * QK_SCALE
    kv_na = z[:, 512:1024]
    q_g = z[:, 1024:1536]
    k_g = z[:, 1536:1664]
    v_g = z[:, 1664:1792]
    q_ms = _dot((q_g * q_g).astype(BF16), hm_ref[...])
    k_ms = _dot((k_g * k_g).astype(BF16), hm_ref[0:GQA_KV_WIDTH, 0:GQA_KV_WIDTH])
    q_n = q_g * lax.rsqrt(q_ms + RMS_EPS) * qg_ref[0]
    k_n = k_g * lax.rsqrt(k_ms + RMS_EPS) * kg_ref[0]
    if latent:
        kt_ref, v1_ref = out_refs[2], out_refs[3]
        cos = cos_ref[...]
        sin = sin_ref[...]
        cos4 = jnp.concatenate([cos] * 4, axis=1)
        sin4 = jnp.concatenate([sin] * 4, axis=1)
        q_n = (q_n * cos4 + _swap16(q_n) * sin4) * (QK_SCALE * LOG2E)
        k_n = k_n * cos + _swap16(k_n) * sin
        kt_ref[0] = k_n.T.astype(BF16)
        lane = lax.broadcasted_iota(jnp.int32, v_g.shape, 1)
        ones_col = jnp.where(lane == HEAD_DIM, 1.0, 0.0)
        v1_ref[0, 0] = jnp.where(lane < HEAD_DIM, v_g, ones_col).astype(BF16)
        v1_ref[0, 1] = jnp.where(lane < HEAD_DIM, pltpu.roll(v_g, HEAD_DIM, 1), ones_col).astype(BF16)
    else:
        out_refs[2][0] = kv_na
        out_refs[3][0] = jnp.concatenate([k_n, v_g], axis=1)
        q_n = q_n * QK_SCALE
    zb = jnp.concatenate([q_na, kv_na, q_n, k_n, v_g], axis=1)
    z_ref[0] = zb.astype(BF16)


def _in_projection(l, x, mod, w_in, qg, kg, hm, cos, sin, *, latent):
    bg = x.shape[0]
    tm = 512
    out_shape = [jax.ShapeDtypeStruct((bg, 2, TOKENS, LANES), F32),
                 jax.ShapeDtypeStruct((bg, TOKENS, Z_WIDTH), BF16)]
    out_specs = [pl.BlockSpec((1, 2, tm, LANES), lambda b, i: (b, 0, i, 0)),
                 pl.BlockSpec((1, tm, Z_WIDTH), lambda b, i: (b, i, 0))]
    if latent:
        out_shape += [jax.ShapeDtypeStruct((bg, GQA_KV_WIDTH, TOKENS), BF16),
                      jax.ShapeDtypeStruct((bg, GQA_KV_HEADS, TOKENS, LANES), BF16)]
        out_specs += [pl.BlockSpec((1, GQA_KV_WIDTH, tm), lambda b, i: (b, 0, i)),
                      pl.BlockSpec((1, GQA_KV_HEADS, tm, LANES), lambda b, i: (b, 0, i, 0))]
    else:
        out_shape += [jax.ShapeDtypeStruct((bg, TOKENS, 2 * NA_WIDTH), F32),
                      jax.ShapeDtypeStruct((bg, TOKENS, 2 * GQA_KV_WIDTH), F32)]
        out_specs += [pl.BlockSpec((1, tm, 2 * NA_WIDTH), lambda b, i: (b, i, 0)),
                      pl.BlockSpec((1, tm, 2 * GQA_KV_WIDTH), lambda b, i: (b, i, 0))]
    return pl.pallas_call(
        functools.partial(_win_kernel, latent=latent),
        out_shape=out_shape,
        grid=(bg, TOKENS // tm),
        in_specs=[
            pl.BlockSpec((1, tm, D_MODEL), lambda b, i: (b, i, 0)),
            pl.BlockSpec((1, 6, D_MODEL), lambda b, i: (b, 0, 0)),
            pl.BlockSpec((1, D_MODEL, IN_WIDTH), lambda b, i: (l, 0, 0)),
            pl.BlockSpec((1, 1, GQA_WIDTH), lambda b, i: (l, 0, 0)),
            pl.BlockSpec((1, 1, GQA_KV_WIDTH), lambda b, i: (l, 0, 0)),
            pl.BlockSpec((GQA_WIDTH, GQA_WIDTH), lambda b, i: (0, 0)),
            pl.BlockSpec((tm, LANES), lambda b, i: (i, 0)),
            pl.BlockSpec((tm, LANES), lambda b, i: (i, 0)),
        ],
        out_specs=out_specs,
        compiler_params=pltpu.CompilerParams(dimension_semantics=("parallel", "parallel"),
                                             vmem_limit_bytes=VMEM_LIMIT),
        name="win_lat" if latent else "win_ctx",
    )(x, mod, w_in, qg, kg, hm, cos, sin)


def _ssm_kernel(u_ref, m_ref, bend_ref, cpow_ref, a1_ref, a2_ref, h0_ref, d_ref, wglu_ref, *refs,
                n_seq, emit_final):
    if emit_final:
        out_ref, fin_ref, y_s, yt_s, u_s, inj_s, injp_s, hp_s, he_s = refs
    else:
        out_ref, y_s, yt_s, u_s, inj_s, injp_s, hp_s = refs
        fin_ref = he_s = None
    d = pl.program_id(1)
    chunks_per_seq = N_CHUNK // n_seq

    def chunk_rows(t):
        sl = pl.ds(t, N_CHUNK, stride=SSM_CHUNK)
        return jnp.concatenate([u_ref[0, 0, sl, :], u_ref[0, 1, sl, :]], axis=1)

    @pl.when(d == 0)
    def _build():
        for t in range(SSM_CHUNK):
            a_t = chunk_rows(t)
            yt_s[:, t * SSM_GROUP_CH:(t + 1) * SSM_GROUP_CH, :] = a_t.T.reshape(SSM_GROUPS, SSM_GROUP_CH, N_CHUNK)

        def tr(g, carry):
            u_s[g] = yt_s[g].T.astype(BF16)
            return carry
        lax.fori_loop(0, SSM_GROUPS, tr, 0)

    def proj(g, carry):
        ug = u_s[g]
        y = _dot(ug, m_ref[0, 0, g])

        @pl.when(d == 0)
        def _():
            y_s[g] = y

        @pl.when(d != 0)
        def _():
            y_s[g] = y_s[g] + y
        inj = _dot(ug, bend_ref[0, 0, g])
        inj_s[pl.ds(g, N_CHUNK, stride=SSM_GROUPS), :] = inj[:, :LANES]
        injp_s[pl.ds(g, N_CHUNK, stride=SSM_GROUPS), :] = inj[:, LANES:]
        return carry
    lax.fori_loop(0, SSM_GROUPS, proj, 0)

    a1 = a1_ref[0, 0]
    a2 = a2_ref[0, 0]
    h0 = h0_ref[0, 0]

    def step(j, carry):
        s, sp = carry
        n = j + d * (N_CHUNK - 1 - 2 * j)
        if n_seq > 1:
            keep = jnp.where(j % chunks_per_seq != 0, 1.0, 0.0).astype(F32)
            s = s * keep
            sp = sp * keep
        row = pl.multiple_of(n * SSM_GROUPS, SSM_GROUPS)
        hp_s[pl.ds(row, SSM_GROUPS), :] = s
        s_new = a1 * s + a2 * sp + inj_s[pl.ds(row, SSM_GROUPS), :]
        sp_new = a1 * sp - a2 * s + injp_s[pl.ds(row, SSM_GROUPS), :]
        if emit_final:
            he_s[pl.ds(row, SSM_GROUPS), :] = s_new
        return s_new, sp_new
    lax.fori_loop(0, N_CHUNK, step, (h0[:, :LANES], h0[:, LANES:]))

    def carry_in(g, carry):
        hp = hp_s[pl.ds(g, N_CHUNK, stride=SSM_GROUPS), :]
        y_s[g] = y_s[g] + _dot(hp.astype(BF16), cpow_ref[0, 0, g])
        return carry
    lax.fori_loop(0, SSM_GROUPS, carry_in, 0)

    if emit_final:
        last = (chunks_per_seq - 1) * (1 - d)
        for sq in range(n_seq):
            row = pl.multiple_of((sq * chunks_per_seq + last) * SSM_GROUPS, SSM_GROUPS)
            fin_ref[0, 0, sq * SSM_GROUPS:(sq + 1) * SSM_GROUPS, :] = he_s[pl.ds(row, SSM_GROUPS), :]

    @pl.when(d == 1)
    def _finish():
        def tr(g, carry):
            yt_s[g] = y_s[g].T
            return carry
        lax.fori_loop(0, SSM_GROUPS, tr, 0)
        dskip = d_ref[0]
        wglu = wglu_ref[0]
        for t in range(SSM_CHUNK):
            blk = yt_s[:, t * SSM_GROUP_CH:(t + 1) * SSM_GROUP_CH, :].reshape(SSM_WIDTH, N_CHUNK)
            y = blk.T + chunk_rows(t) * dskip
            gl = jax.nn.gelu(y, approximate=True)
            gate = _dot(gl.astype(BF16), wglu)
            o = gl * _sigmoid(gate)
            out_ref[0, 0, pl.ds(t, N_CHUNK, stride=SSM_CHUNK), :] = o[:, :LANES]
            out_ref[0, 1, pl.ds(t, N_CHUNK, stride=SSM_CHUNK), :] = o[:, LANES:]


def _ssm(l, u, ops, h0, dskip, wglu, *, n_seq, emit_final):
    bg = u.shape[0]
    m, bend, cpow, a1, a2 = ops
    out_shape = [jax.ShapeDtypeStruct((bg, 2, TOKENS, LANES), F32)]
    out_specs = [pl.BlockSpec((1, 2, TOKENS, LANES), lambda b, d: (b, 0, 0, 0))]
    scratch = [pltpu.VMEM((SSM_GROUPS, N_CHUNK, 256), F32),
               pltpu.VMEM((SSM_GROUPS, 256, N_CHUNK), F32),
               pltpu.VMEM((SSM_GROUPS, N_CHUNK, 256), BF16),
               pltpu.VMEM((N_CHUNK * SSM_GROUPS, LANES), F32),
               pltpu.VMEM((N_CHUNK * SSM_GROUPS, LANES), F32),
               pltpu.VMEM((N_CHUNK * SSM_GROUPS, LANES), F32)]
    if emit_final:
        out_shape.append(jax.ShapeDtypeStruct((bg, 2, n_seq * SSM_GROUPS, LANES), F32))
        out_specs.append(pl.BlockSpec((1, 1, n_seq * SSM_GROUPS, LANES), lambda b, d: (b, d, 0, 0)))
        scratch.append(pltpu.VMEM((N_CHUNK * SSM_GROUPS, LANES), F32))
    return pl.pallas_call(
        functools.partial(_ssm_kernel, n_seq=n_seq, emit_final=emit_final),
        out_shape=out_shape,
        grid=(bg, 2),
        in_specs=[
            pl.BlockSpec((1, 2, TOKENS, LANES), lambda b, d: (b, 0, 0, 0)),
            pl.BlockSpec((1, 1, SSM_GROUPS, 256, 256), lambda b, d: (l, d, 0, 0, 0)),
            pl.BlockSpec((1, 1, SSM_GROUPS, 256, 256), lambda b, d: (l, d, 0, 0, 0)),
            pl.BlockSpec((1, 1, SSM_GROUPS, 128, 256), lambda b, d: (l, d, 0, 0, 0)),
            pl.BlockSpec((1, 1, SSM_GROUPS, LANES), lambda b, d: (l, d, 0, 0)),
            pl.BlockSpec((1, 1, SSM_GROUPS, LANES), lambda b, d: (l, d, 0, 0)),
            pl.BlockSpec((1, 1, SSM_GROUPS, 256), lambda b, d: (b, d, 0, 0)),
            pl.BlockSpec((1, 1, SSM_WIDTH), lambda b, d: (l, 0, 0)),
            pl.BlockSpec((1, SSM_WIDTH, SSM_WIDTH), lambda b, d: (l, 0, 0)),
        ],
        out_specs=out_specs,
        scratch_shapes=scratch,
        compiler_params=pltpu.CompilerParams(dimension_semantics=("parallel", "arbitrary"),
                                             vmem_limit_bytes=VMEM_LIMIT),
        name="ssm_ctx" if emit_final else "ssm_lat",
    )(u, m, bend, cpow, a1, a2, h0, dskip, wglu)


def _ssm_operators(lam_re, lam_im, log_dt, b_re, b_im, c_re, c_im):
    hp = lax.Precision.HIGHEST
    t_n = SSM_CHUNK
    lr = jnp.minimum(lam_re.astype(F32), -1e-4)
    li = lam_im.astype(F32)
    dt = jnp.exp(log_dt.astype(F32))[..., None]
    k = jnp.arange(t_n + 1, dtype=F32)[:, None, None, None, None]
    mag = jnp.exp(lr * dt * k)
    ang = li * dt * k
    p_r, p_i = mag * jnp.cos(ang), mag * jnp.sin(ang)
    n_r, n_i = p_r[1] - 1.0, p_i[1]
    den = lr * lr + li * li
    q_r, q_i = (n_r * lr + n_i * li) / den, (n_i * lr - n_r * li) / den
    bb_r = q_r[..., None] * b_re - q_i[..., None] * b_im
    bb_i = q_r[..., None] * b_im + q_i[..., None] * b_re
    cp_r = c_re[None] * p_r[..., None, :] - c_im[None] * p_i[..., None, :]
    cp_i = c_re[None] * p_i[..., None, :] + c_im[None] * p_r[..., None, :]
    kern = (jnp.einsum('kldgcp,ldgpe->kldgce', cp_r, bb_r, precision=hp)
            - jnp.einsum('kldgcp,ldgpe->kldgce', cp_i, bb_i, precision=hp))
    pad = jnp.zeros((t_n - 1,) + kern.shape[1:2] + kern.shape[3:], F32)
    mats, bends, cpows = [], [], []
    for d in range(2):
        kd = kern[:t_n, :, d]
        band = jnp.concatenate([pad, kd], axis=0) if d == 0 else jnp.concatenate([kd[::-1], pad], axis=0)
        rows = jnp.stack([band[t_n - 1 - t:2 * t_n - 1 - t] for t in range(t_n)])
        mats.append(rows.transpose(2, 3, 0, 5, 1, 4).reshape(DEPTH, SSM_GROUPS, 256, 256))
        e_pr = p_r[:t_n, :, d][::-1] if d == 0 else p_r[:t_n, :, d]
        e_pi = p_i[:t_n, :, d][::-1] if d == 0 else p_i[:t_n, :, d]
        e_r = e_pr[..., None] * bb_r[None, :, d] - e_pi[..., None] * bb_i[None, :, d]
        e_i = e_pr[..., None] * bb_i[None, :, d] + e_pi[..., None] * bb_r[None, :, d]
        e_r = e_r.transpose(1, 2, 0, 4, 3).reshape(DEPTH, SSM_GROUPS, 256, SSM_STATE)
        e_i = e_i.transpose(1, 2, 0, 4, 3).reshape(DEPTH, SSM_GROUPS, 256, SSM_STATE)
        bends.append(jnp.concatenate([e_r, e_i, e_i, e_r], axis=-1))
        o_r = cp_r[1:, :, d] if d == 0 else cp_r[1:, :, d][::-1]
        o_i = cp_i[1:, :, d] if d == 0 else cp_i[1:, :, d][::-1]
        o_r = o_r.transpose(1, 2, 4, 0, 3).reshape(DEPTH, SSM_GROUPS, SSM_STATE, 256)
        o_i = o_i.transpose(1, 2, 4, 0, 3).reshape(DEPTH, SSM_GROUPS, SSM_STATE, 256)
        cpows.append(jnp.concatenate([o_r, -o_i], axis=2))
    a_r, a_i = p_r[t_n], p_i[t_n]
    a1 = jnp.concatenate([a_r, a_r], axis=-1)
    a2 = jnp.concatenate([-a_i, a_i], axis=-1)
    stack = lambda xs: jnp.stack(xs, axis=1).astype(BF16)
    return stack(mats), stack(bends), stack(cpows), a1, a2


def _softmax_pv(s, v):
    m = jnp.max(s, axis=-1, keepdims=True)
    p = jnp.exp(s - m)
    l = jnp.sum(p, axis=-1, keepdims=True)
    return _dot(p.astype(BF16), v) / l


def _ctx_attn_kernel(z_ref, na_ref, gqa_ref):
    z = z_ref[0]
    outs = []
    for h in range(NA_HEADS):
        sl = slice(h * HEAD_DIM, (h + 1) * HEAD_DIM)
        q = z[:, Z_QNA:Z_QNA + NA_WIDTH][:, sl]
        k = z[:, Z_KNA:Z_KNA + NA_WIDTH][:, sl]
        v = z[:, Z_VNA:Z_VNA + NA_WIDTH][:, sl]
        outs.append(_softmax_pv(_dot_nt(q, k), v))
    na_ref[0] = jnp.concatenate(outs, axis=1).astype(BF16)
    outs = []
    for g in range(GQA_KV_HEADS):
        k = z[:, Z_KG + g * HEAD_DIM:Z_KG + (g + 1) * HEAD_DIM]
        v = z[:, Z_VG + g * HEAD_DIM:Z_VG + (g + 1) * HEAD_DIM]
        for r in range(GQA_REP):
            c0 = Z_QG + (g * GQA_REP + r) * HEAD_DIM
            outs.append(_softmax_pv(_dot_nt(z[:, c0:c0 + HEAD_DIM], k), v))
    gqa_ref[0] = jnp.concatenate(outs, axis=1).astype(BF16)


def _ctx_attention(zb):
    z3 = zb.reshape(BATCH, SEQ, Z_WIDTH)
    na, gqa = pl.pallas_call(
        _ctx_attn_kernel,
        out_shape=[jax.ShapeDtypeStruct((BATCH, SEQ, NA_WIDTH), BF16),
                   jax.ShapeDtypeStruct((BATCH, SEQ, GQA_WIDTH), BF16)],
        grid=(BATCH,),
        in_specs=[pl.BlockSpec((1, SEQ, Z_WIDTH), lambda b: (b, 0, 0))],
        out_specs=[pl.BlockSpec((1, SEQ, NA_WIDTH), lambda b: (b, 0, 0)),
                   pl.BlockSpec((1, SEQ, GQA_WIDTH), lambda b: (b, 0, 0))],
        compiler_params=pltpu.CompilerParams(dimension_semantics=("parallel",), vmem_limit_bytes=VMEM_LIMIT),
        name="attn_ctx",
    )(z3)
    return na.reshape(1, TOKENS, NA_WIDTH), gqa.reshape(1, TOKENS, GQA_WIDTH)


NA_KEYS = NA_WIN_R * GRID_W


def _na_lat_kernel(q_ref, k_ref, v_ref, ck_ref, cv_ref, bias_ref, o_ref):
    r = pl.program_id(1)
    rs = jnp.clip(r - NA_WIN_R // 2, 0, GRID_H - NA_WIN_R)
    start = pl.multiple_of(rs * GRID_W, GRID_W)
    q = q_ref[0]
    kw = k_ref[0, pl.ds(start, NA_KEYS), :]
    vw = v_ref[0, pl.ds(start, NA_KEYS), :]
    ck = ck_ref[0, 0]
    cv = cv_ref[0, 0]
    outs = []
    for h in range(NA_HEADS):
        sl = slice(h * HEAD_DIM, (h + 1) * HEAD_DIM)
        qh = q[:, sl]
        s_loc = _dot_nt(qh, kw[:, sl]) + bias_ref[0, 0, h]
        s_ctx = _dot_nt(qh, ck[:, sl])
        m = jnp.maximum(jnp.max(s_loc, axis=-1, keepdims=True), jnp.max(s_ctx, axis=-1, keepdims=True))
        p_loc = jnp.exp(s_loc - m)
        p_ctx = jnp.exp(s_ctx - m)
        l = jnp.sum(p_loc, axis=-1, keepdims=True) + jnp.sum(p_ctx, axis=-1, keepdims=True)
        o = _dot(p_loc.astype(BF16), vw[:, sl]) + _dot(p_ctx.astype(BF16), cv[:, sl])
        outs.append(o / l)
    o_ref[0] = jnp.concatenate(outs, axis=1).astype(BF16)


def _na_row_class(r):
    return r - jnp.clip(r - NA_WIN_R // 2, 0, GRID_H - NA_WIN_R)


def _na_latent(l, zb, ck, cv, bias):
    return pl.pallas_call(
        _na_lat_kernel,
        out_shape=jax.ShapeDtypeStruct((DEC_BATCH, TOKENS, NA_WIDTH), BF16),
        grid=(DEC_BATCH, GRID_H),
        in_specs=[
            pl.BlockSpec((1, GRID_W, NA_WIDTH), lambda b, r: (b, r, Z_QNA // NA_WIDTH)),
            pl.BlockSpec((1, TOKENS, NA_WIDTH), lambda b, r: (b, 0, Z_KNA // NA_WIDTH)),
            pl.BlockSpec((1, TOKENS, NA_WIDTH), lambda b, r: (b, 0, Z_VNA // NA_WIDTH)),
            pl.BlockSpec((1, 1, PAST_LEN, NA_WIDTH), lambda b, r: (b, l, 0, 0)),
            pl.BlockSpec((1, 1, PAST_LEN, NA_WIDTH), lambda b, r: (b, l, 0, 0)),
            pl.BlockSpec((1, 1, NA_HEADS, GRID_W, NA_KEYS), lambda b, r: (l, _na_row_class(r), 0, 0, 0)),
        ],
        out_specs=pl.BlockSpec((1, GRID_W, NA_WIDTH), lambda b, r: (b, r, 0)),
        compiler_params=pltpu.CompilerParams(dimension_semantics=("parallel", "arbitrary"),
                                             vmem_limit_bytes=VMEM_LIMIT),
        name="na_lat",
    )(zb, zb, zb, ck, cv, bias)


def _na_bias_table(tab):
    q = np.arange(GRID_W)
    kc = np.arange(GRID_W)
    cs = np.clip(q - NA_WIN_C // 2, 0, GRID_W - NA_WIN_C)
    valid = (kc[None, :] >= cs[:, None]) & (kc[None, :] < cs[:, None] + NA_WIN_C)
    w = GRID_W - 1
    padded = jnp.pad(tab.astype(F32), ((0, 0), (0, 0), (0, 0), (w, w)))
    off = w + NA_WIN_C - 1
    cols = jnp.stack([padded[..., off - i:off - i + GRID_W] for i in range(GRID_W)], axis=-2)
    cols = jnp.where(valid, cols, -jnp.inf)
    b = jnp.stack([cols[:, :, NA_WIN_R - 1 - c:2 * NA_WIN_R - 1 - c] for c in range(NA_WIN_R)], axis=1)
    return b.transpose(0, 1, 2, 4, 3, 5).reshape(DEPTH, NA_WIN_R, NA_HEADS, GRID_W, NA_KEYS)


GQA_TK = 256
GQA_KEYS = DEC_SEQ + PAST_LEN


def _gqa_kernel(q_ref, ktn_ref, ktc_ref, vn_ref, vc_ref, o_ref, s_s, m_s, *, tq):
    q = q_ref[0]
    n_new = DEC_SEQ // GQA_TK
    half = GQA_REP * tq // 2
    qh = [jnp.concatenate([q[:, r * HEAD_DIM:(r + 1) * HEAD_DIM] for r in (2 * i, 2 * i + 1)], axis=0)
          for i in range(2)]
    rows = [slice(i * half, (i + 1) * half) for i in range(2)]

    def fold(s):
        return jnp.maximum(s[:, :LANES], s[:, LANES:])

    for i in range(2):
        s = _dot(qh[i], ktc_ref[0])
        s_s[rows[i], DEC_SEQ:GQA_KEYS] = s
        m_s[rows[i], :] = fold(s)
    for c in range(0, n_new, 2):
        for i in range(2):
            s0 = _dot(qh[i], ktn_ref[0, :, c * GQA_TK:(c + 1) * GQA_TK])
            s1 = _dot(qh[i], ktn_ref[0, :, (c + 1) * GQA_TK:(c + 2) * GQA_TK])
            s_s[rows[i], c * GQA_TK:(c + 1) * GQA_TK] = s0
            s_s[rows[i], (c + 1) * GQA_TK:(c + 2) * GQA_TK] = s1
            m_s[rows[i], :] = jnp.maximum(m_s[rows[i], :], jnp.maximum(fold(s0), fold(s1)))

    m = jnp.max(m_s[...], axis=-1, keepdims=True)
    m_s[...] = jnp.broadcast_to(m, m_s.shape)

    def probs(i, c0):
        mb = m_s[rows[i], :]
        return jnp.exp2(s_s[rows[i], c0:c0 + GQA_TK] - jnp.concatenate([mb, mb], axis=1)).astype(BF16)

    acc = [_dot(probs(i, DEC_SEQ), vc_ref[0, 0]) for i in range(2)]
    for c in range(n_new):
        for i in range(2):
            acc[i] = acc[i] + _dot(probs(i, c * GQA_TK), vn_ref[0, 0, c * GQA_TK:(c + 1) * GQA_TK, :])
    outs = []
    for i in range(2):
        out = acc[i][:, :HEAD_DIM] / acc[i][:, HEAD_DIM:HEAD_DIM + 1]
        outs += [out[:tq], out[tq:]]
    o_ref[0] = jnp.concatenate(outs, axis=1).astype(BF16)


def _gqa_latent(zb, kt_new, kt_cache, v1_new, v1_cache):
    tq = 128
    rows = GQA_REP * tq
    return pl.pallas_call(
        functools.partial(_gqa_kernel, tq=tq),
        out_shape=jax.ShapeDtypeStruct((DEC_BATCH, TOKENS, GQA_WIDTH), BF16),
        grid=(DEC_BATCH, GQA_KV_HEADS, TOKENS // tq),
        in_specs=[
            pl.BlockSpec((1, tq, 256), lambda b, g, i: (b, i, Z_QG // 256 + g)),
            pl.BlockSpec((1, HEAD_DIM, DEC_SEQ), lambda b, g, i: (b, g, 0)),
            pl.BlockSpec((1, HEAD_DIM, PAST_LEN), lambda b, g, i: (b, g, 0)),
            pl.BlockSpec((1, 1, DEC_SEQ, LANES), lambda b, g, i: (b, g, 0, 0)),
            pl.BlockSpec((1, 1, PAST_LEN, LANES), lambda b, g, i: (b, g, 0, 0)),
        ],
        out_specs=pl.BlockSpec((1, tq, 256), lambda b, g, i: (b, i, g)),
        scratch_shapes=[pltpu.VMEM((rows, GQA_KEYS), F32), pltpu.VMEM((rows, LANES), F32)],
        compiler_params=pltpu.CompilerParams(dimension_semantics=("parallel", "parallel", "arbitrary"),
                                             vmem_limit_bytes=VMEM_LIMIT),
        name="gqa_lat",
    )(zb, kt_new, kt_cache, v1_new, v1_cache)


FF_CHUNK = 256


def _ffn_kernel(x_ref, ssm_ref, na_ref, gqa_ref, mod_ref, wout_ref, ln1g_ref, ln1b_ref, wa_ref, wg_ref, wo_ref,
                ln2g_ref, ln2b_ref, o_ref, x1_s, h2_s, acc_s):
    k = pl.program_id(2)

    @pl.when(k == 0)
    def _mix():
        ssm = jnp.concatenate([ssm_ref[0, 0], ssm_ref[0, 1]], axis=1)
        o = (_dot(ssm.astype(BF16), wout_ref[0, 0:SSM_WIDTH, :])
             + _dot(na_ref[0], wout_ref[0, SSM_WIDTH:SSM_WIDTH + NA_WIDTH, :])
             + _dot(gqa_ref[0], wout_ref[0, SSM_WIDTH + NA_WIDTH:, :]))
        y = DEEPNORM_ALPHA * x_ref[0] + mod_ref[0, 2:3, :] * o
        x1 = _ln(y) * ln1g_ref[0] + ln1b_ref[0]
        x1_s[...] = x1
        h2 = _ln(x1) * (1.0 + mod_ref[0, 4:5, :]) + mod_ref[0, 3:4, :]
        h2_s[...] = h2.astype(BF16)

    h2 = h2_s[...]
    a = _dot(h2, wa_ref[0])
    gt = _dot(h2, wg_ref[0])
    f = (a * _sigmoid(a) * gt).astype(BF16)
    contrib = _dot(f, wo_ref[0])

    @pl.when(k == 0)
    def _():
        acc_s[...] = contrib

    @pl.when(k != 0)
    def _():
        acc_s[...] = acc_s[...] + contrib

    @pl.when(k == pl.num_programs(2) - 1)
    def _out():
        y = DEEPNORM_ALPHA * x1_s[...] + mod_ref[0, 5:6, :] * acc_s[...]
        o_ref[0] = _ln(y) * ln2g_ref[0] + ln2b_ref[0]


def _out_ffn(l, x, ssm, na, gqa, mod, w_out, ln1g, ln1b, w_ffn_in, w_ffn_out, ln2g, ln2b):
    bg = x.shape[0]
    tm = 1024
    nk = D_FF // FF_CHUNK
    vec = pl.BlockSpec((1, 1, D_MODEL), lambda b, i, k: (l, 0, 0))
    return pl.pallas_call(
        _ffn_kernel,
        out_shape=jax.ShapeDtypeStruct((bg, TOKENS, D_MODEL), F32),
        grid=(bg, TOKENS // tm, nk),
        in_specs=[
            pl.BlockSpec((1, tm, D_MODEL), lambda b, i, k: (b, i, 0)),
            pl.BlockSpec((1, 2, tm, LANES), lambda b, i, k: (b, 0, i, 0)),
            pl.BlockSpec((1, tm, NA_WIDTH), lambda b, i, k: (b, i, 0)),
            pl.BlockSpec((1, tm, GQA_WIDTH), lambda b, i, k: (b, i, 0)),
            pl.BlockSpec((1, 6, D_MODEL), lambda b, i, k: (b, 0, 0)),
            pl.BlockSpec((1, D_MODEL, D_MODEL), lambda b, i, k: (l, 0, 0)),
            vec,
            vec,
            pl.BlockSpec((1, D_MODEL, FF_CHUNK), lambda b, i, k: (l, 0, k)),
            pl.BlockSpec((1, D_MODEL, FF_CHUNK), lambda b, i, k: (l, 0, k + nk)),
            pl.BlockSpec((1, FF_CHUNK, D_MODEL), lambda b, i, k: (l, k, 0)),
            vec,
            vec,
        ],
        out_specs=pl.BlockSpec((1, tm, D_MODEL), lambda b, i, k: (b, i, 0)),
        scratch_shapes=[pltpu.VMEM((tm, D_MODEL), F32), pltpu.VMEM((tm, D_MODEL), BF16),
                        pltpu.VMEM((tm, D_MODEL), F32)],
        compiler_params=pltpu.CompilerParams(dimension_semantics=("parallel", "parallel", "arbitrary"),
                                             vmem_limit_bytes=VMEM_LIMIT),
        name="ffn",
    )(x, ssm, na, gqa, mod, w_out, ln1g, ln1b, w_ffn_in, w_ffn_in, w_ffn_out, ln2g, ln2b)


def _rope_tables():
    nf = HEAD_DIM // 4
    t = jnp.arange(DEC_SEQ)
    inv = ROPE_THETA ** (-jnp.arange(nf, dtype=F32) / nf)
    rows = (t // GRID_W).astype(F32)
    cols = (t % GRID_W).astype(F32)
    ang = jnp.stack([rows[:, None] * inv, cols[:, None] * inv], axis=1)
    cos = jnp.repeat(jnp.cos(ang)[:, :, None, :], 2, axis=2).reshape(DEC_SEQ, HEAD_DIM)
    sin = jnp.sin(ang)[:, :, None, :] * jnp.asarray([-1.0, 1.0], F32)[None, None, :, None]
    sin = sin.reshape(DEC_SEQ, HEAD_DIM)
    return jnp.tile(cos, (1, 2)), jnp.tile(sin, (1, 2))


def _head_mean_matrix():
    h = np.arange(GQA_WIDTH) // HEAD_DIM
    return jnp.asarray((h[:, None] == h[None, :]).astype(np.float32) / HEAD_DIM, BF16)


def kernel(x_prompt, x_sample, c, cache_na_k, cache_na_v, cache_gqa_k, cache_gqa_v, state_ssm_re, state_ssm_im,
           c_ctx, w_ada, b_ada, w_in, w_out, q_norm_g, k_norm_g, na_bias, ssm_lam_re, ssm_lam_im, ssm_log_dt,
           ssm_b_re, ssm_b_im, ssm_c_re, ssm_c_im, ssm_d, w_ssm_glu, ln1_g, ln1_b, ln2_g, ln2_b,
           w_ffn_in, w_ffn_out):
    cond8 = jnp.concatenate([c_ctx[None, :], c, jnp.zeros((8 - 1 - DEC_BATCH, D_MODEL), F32)], axis=0)
    mod = _modulation(cond8, w_ada, b_ada).reshape(DEPTH, 8, 6, D_MODEL)
    cos, sin = _rope_tables()
    hm = _head_mean_matrix()
    w_in_b = w_in.astype(BF16)
    w_out_b = w_out.astype(BF16)
    w_ffn_in_b = w_ffn_in.astype(BF16)
    w_ffn_out_b = w_ffn_out.astype(BF16)
    w_glu_b = w_ssm_glu.astype(BF16)
    qg = jnp.tile(q_norm_g, (1, GQA_HEADS))[:, None, :]
    kg = jnp.tile(k_norm_g, (1, GQA_KV_HEADS))[:, None, :]
    dskip = ssm_d[:, None, :]
    ln = (ln1_g[:, None, :], ln1_b[:, None, :], ln2_g[:, None, :], ln2_b[:, None, :])
    ops = _ssm_operators(ssm_lam_re, ssm_lam_im, ssm_log_dt, ssm_b_re, ssm_b_im, ssm_c_re, ssm_c_im)
    na_tab = _na_bias_table(na_bias)
    ck_na = cache_na_k.reshape(DEC_BATCH, DEPTH, PAST_LEN, NA_WIDTH).astype(BF16)
    cv_na = cache_na_v.reshape(DEC_BATCH, DEPTH, PAST_LEN, NA_WIDTH).astype(BF16)
    kt_cache = cache_gqa_k.transpose(1, 0, 3, 4, 2).reshape(DEPTH, DEC_BATCH, GQA_KV_WIDTH, PAST_LEN).astype(BF16)
    cv_g = cache_gqa_v.transpose(1, 0, 3, 2, 4)
    v1_cache = jnp.concatenate([cv_g, jnp.ones(cv_g.shape[:-1] + (1,), F32),
                                jnp.zeros(cv_g.shape[:-1] + (LANES - HEAD_DIM - 1,), F32)], axis=-1).astype(BF16)
    s_re = state_ssm_re.astype(F32).transpose(1, 0, 2, 3, 4)
    s_im = state_ssm_im.astype(F32).transpose(1, 0, 2, 3, 4)
    h0_lat = jnp.concatenate([s_re, s_im, s_im, s_re], axis=-1)
    h0_ctx = jnp.zeros((1, 2, SSM_GROUPS, 256), F32)

    y_ctx = x_prompt.reshape(1, TOKENS, D_MODEL)
    y_lat = x_sample
    kv_na_l, kv_g_l, fin_l = [], [], []
    for l in range(DEPTH):
        mod_c = mod[l, 0:1]
        u, zb, kv_na, kv_g = _in_projection(l, y_ctx, mod_c, w_in_b, qg, kg, hm, cos, sin, latent=False)
        ssm_o, fin = _ssm(l, u, ops, h0_ctx, dskip, w_glu_b, n_seq=BATCH, emit_final=True)
        na_o, gqa_o = _ctx_attention(zb)
        y_ctx = _out_ffn(l, y_ctx, ssm_o, na_o, gqa_o, mod_c, w_out_b, ln[0], ln[1], w_ffn_in_b, w_ffn_out_b,
                         ln[2], ln[3])
        kv_na_l.append(kv_na); kv_g_l.append(kv_g); fin_l.append(fin)

        mod_s = mod[l, 1:1 + DEC_BATCH]
        u, zb, kt_new, v1_new = _in_projection(l, y_lat, mod_s, w_in_b, qg, kg, hm, cos, sin, latent=True)
        ssm_o = _ssm(l, u, ops, h0_lat[l], dskip, w_glu_b, n_seq=1, emit_final=False)[0]
        na_o = _na_latent(l, zb, ck_na, cv_na, na_tab)
        gqa_o = _gqa_latent(zb, kt_new, kt_cache[l], v1_new, v1_cache[l])
        y_lat = _out_ffn(l, y_lat, ssm_o, na_o, gqa_o, mod_s, w_out_b, ln[0], ln[1], w_ffn_in_b, w_ffn_out_b,
                         ln[2], ln[3])

    kv_na = jnp.stack(kv_na_l, axis=0).reshape(DEPTH, BATCH, SEQ, 2, NA_HEADS, HEAD_DIM)
    kv_g = jnp.stack(kv_g_l, axis=0).reshape(DEPTH, BATCH, SEQ, 2, GQA_KV_HEADS, HEAD_DIM)
    kv_na = kv_na.transpose(3, 1, 0, 2, 4, 5)
    kv_g = kv_g.transpose(3, 1, 0, 2, 4, 5)
    fin = jnp.stack(fin_l, axis=0).reshape(DEPTH, 2, BATCH, SSM_GROUPS, 2, SSM_STATE)
    fin = fin.transpose(4, 2, 0, 1, 3, 5)
    return (y_ctx.reshape(BATCH, SEQ, D_MODEL), y_lat, kv_na[0], kv_na[1], kv_g[0], kv_g[1], fin[0], fin[1])
```

```python
import functools
import math

import numpy as np
import jax
import jax.numpy as jnp
from jax import lax
from jax.experimental import pallas as pl
from jax.experimental.pallas import tpu as pltpu

F32 = jnp.float32
BF16 = jnp.bfloat16

D_MODEL = 1024
BATCH = 16
SEQ = 256
DEPTH = 2
DEC_BATCH = 2
DEC_SEQ = 4096
PAST_LEN = 256
GRID_W = 64
GRID_H = DEC_SEQ // GRID_W
HEAD_DIM = 64
SSM_WIDTH = 256
SSM_GROUP_CH = 16
SSM_GROUPS = 16
SSM_STATE = 64
NA_HEADS = 4
NA_WIDTH = 256
NA_WIN_R = 8
NA_WIN_C = 16
GQA_HEADS = 8
GQA_KV_HEADS = 2
GQA_REP = 4
GQA_WIDTH = 512
GQA_KV_WIDTH = 128
IN_WIDTH = 1792
D_FF = 2816
ROPE_THETA = 10000.0
LN_EPS = 1e-6
RMS_EPS = 1e-6
DEEPNORM_ALPHA = (2 * DEPTH) ** 0.25
QK_SCALE = HEAD_DIM ** -0.5
LOG2E = math.log2(math.e)

TOKENS = 4096
SSM_CHUNK = 16
N_CHUNK = TOKENS // SSM_CHUNK
Z_WIDTH = IN_WIDTH - SSM_WIDTH
Z_QNA, Z_KNA, Z_VNA, Z_QG, Z_KG, Z_VG = 0, 256, 512, 768, 1280, 1408

VMEM_LIMIT = 56 * 1024 * 1024
LANES = 128


def _sigmoid(x):
    return 1.0 / (1.0 + jnp.exp(-x))


def _ln(x):
    mu = jnp.mean(x, axis=-1, keepdims=True)
    xc = x - mu
    var = jnp.mean(xc * xc, axis=-1, keepdims=True)
    return xc * lax.rsqrt(var + LN_EPS)


def _dot(a, b):
    return jnp.dot(a, b, preferred_element_type=F32)


def _dot_nt(a, b):
    return lax.dot_general(a, b, (((1,), (1,)), ((), ())), preferred_element_type=F32)


def _mod_kernel(c_ref, w_ref, b_ref, o_ref):
    c = c_ref[...]
    s = c * _sigmoid(c)
    o_ref[0] = jnp.dot(s, w_ref[0], preferred_element_type=F32, precision=lax.Precision.HIGHEST) + b_ref[0]


def _modulation(cond8, w_ada, b_ada):
    nb = 1536
    return pl.pallas_call(
        _mod_kernel,
        out_shape=jax.ShapeDtypeStruct((DEPTH, 8, 6 * D_MODEL), F32),
        grid=(DEPTH, 6 * D_MODEL // nb),
        in_specs=[
            pl.BlockSpec((8, D_MODEL), lambda l, j: (0, 0)),
            pl.BlockSpec((1, D_MODEL, nb), lambda l, j: (l, 0, j)),
            pl.BlockSpec((1, 1, nb), lambda l, j: (l, 0, j)),
        ],
        out_specs=pl.BlockSpec((1, 8, nb), lambda l, j: (l, 0, j)),
        compiler_params=pltpu.CompilerParams(dimension_semantics=("parallel", "parallel"),
                                             vmem_limit_bytes=VMEM_LIMIT),
        name="mod",
    )(cond8, w_ada, b_ada.reshape(DEPTH, 1, 6 * D_MODEL))


def _swap16(x):
    w = x.shape[-1]
    lane = lax.broadcasted_iota(jnp.int32, x.shape, x.ndim - 1)
    return jnp.where((lane & 16) != 0, pltpu.roll(x, 16, x.ndim - 1), pltpu.roll(x, w - 16, x.ndim - 1))


def _win_kernel(x_ref, mod_ref, w_ref, qg_ref, kg_ref, hm_ref, cos_ref, sin_ref, *out_refs, latent):
    u_ref, z_ref = out_refs[0], out_refs[1]
    x = x_ref[0]
    h = _ln(x) * (1.0 + mod_ref[0, 1:2, :]) + mod_ref[0, 0:1, :]
    z = _dot(h.astype(BF16), w_ref[0])
    u_ref[0, 0] = z[:, 0:LANES]
    u_ref[0, 1] = z[:, LANES:SSM_WIDTH]
    q_na = z[:, 256:512] * QK_SCALE
    kv_na = z[:, 512:1024]
    q_g = z[:, 1024:1536]
    k_g = z[:, 1536:1664]
    v_g = z[:, 1664:1792]
    q_ms = _dot((q_g * q_g).astype(BF16), hm_ref[...])
    k_ms = _dot((k_g * k_g).astype(BF16), hm_ref[0:GQA_KV_WIDTH, 0:GQA_KV_WIDTH])
    q_n = q_g * lax.rsqrt(q_ms + RMS_EPS) * qg_ref[0]
    k_n = k_g * lax.rsqrt(k_ms + RMS_EPS) * kg_ref[0]
    if latent:
        kt_ref, v1_ref = out_refs[2], out_refs[3]
        cos = cos_ref[...]
        sin = sin_ref[...]
        cos4 = jnp.concatenate([cos] * 4, axis=1)
        sin4 = jnp.concatenate([sin] * 4, axis=1)
        q_n = (q_n * cos4 + _swap16(q_n) * sin4) * (QK_SCALE * LOG2E)
        k_n = k_n * cos + _swap16(k_n) * sin
        kt_ref[0] = k_n.T.astype(BF16)
        lane = lax.broadcasted_iota(jnp.int32, v_g.shape, 1)
        ones_col = jnp.where(lane == HEAD_DIM, 1.0, 0.0)
        v1_ref[0, 0] = jnp.where(lane < HEAD_DIM, v_g, ones_col).astype(BF16)
        v1_ref[0, 1] = jnp.where(lane < HEAD_DIM, pltpu.roll(v_g, HEAD_DIM, 1), ones_col).astype(BF16)
    else:
        out_refs[2][0] = kv_na
        out_refs[3][0] = jnp.concatenate([k_n, v_g], axis=1)
        q_n = q_n * QK_SCALE
    zb = jnp.concatenate([q_na, kv_na, q_n, k_n, v_g], axis=1)
    z_ref[0] = zb.astype(BF16)


def _in_projection(l, x, mod, w_in, qg, kg, hm, cos, sin, *, latent):
    bg = x.shape[0]
    tm = 512
    out_shape = [jax.ShapeDtypeStruct((bg, 2, TOKENS, LANES), F32),
                 jax.ShapeDtypeStruct((bg, TOKENS, Z_WIDTH), BF16)]
    out_specs = [pl.BlockSpec((1, 2, tm, LANES), lambda b, i: (b, 0, i, 0)),
                 pl.BlockSpec((1, tm, Z_WIDTH), lambda b, i: (b, i, 0))]
    if latent:
        out_shape += [jax.ShapeDtypeStruct((bg, GQA_KV_WIDTH, TOKENS), BF16),
                      jax.ShapeDtypeStruct((bg, GQA_KV_HEADS, TOKENS, LANES), BF16)]
        out_specs += [pl.BlockSpec((1, GQA_KV_WIDTH, tm), lambda b, i: (b, 0, i)),
                      pl.BlockSpec((1, GQA_KV_HEADS, tm, LANES), lambda b, i: (b, 0, i, 0))]
    else:
        out_shape += [jax.ShapeDtypeStruct((bg, TOKENS, 2 * NA_WIDTH), F32),
                      jax.ShapeDtypeStruct((bg, TOKENS, 2 * GQA_KV_WIDTH), F32)]
        out_specs += [pl.BlockSpec((1, tm, 2 * NA_WIDTH), lambda b, i: (b, i, 0)),
                      pl.BlockSpec((1, tm, 2 * GQA_KV_WIDTH), lambda b, i: (b, i, 0))]
    return pl.pallas_call(
        functools.partial(_win_kernel, latent=latent),
        out_shape=out_shape,
        grid=(bg, TOKENS // tm),
        in_specs=[
            pl.BlockSpec((1, tm, D_MODEL), lambda b, i: (b, i, 0)),
            pl.BlockSpec((1, 6, D_MODEL), lambda b, i: (b, 0, 0)),
            pl.BlockSpec((1, D_MODEL, IN_WIDTH), lambda b, i: (l, 0, 0)),
            pl.BlockSpec((1, 1, GQA_WIDTH), lambda b, i: (l, 0, 0)),
            pl.BlockSpec((1, 1, GQA_KV_WIDTH), lambda b, i: (l, 0, 0)),
            pl.BlockSpec((GQA_WIDTH, GQA_WIDTH), lambda b, i: (0, 0)),
            pl.BlockSpec((tm, LANES), lambda b, i: (i, 0)),
            pl.BlockSpec((tm, LANES), lambda b, i: (i, 0)),
        ],
        out_specs=out_specs,
        compiler_params=pltpu.CompilerParams(dimension_semantics=("parallel", "parallel"),
                                             vmem_limit_bytes=VMEM_LIMIT),
        name="win_lat" if latent else "win_ctx",
    )(x, mod, w_in, qg, kg, hm, cos, sin)


def _ssm_kernel(u_ref, m_ref, bend_ref, cpow_ref, a1_ref, a2_ref, h0_ref, d_ref, wglu_ref, *refs,
                n_seq, emit_final):
    if emit_final:
        out_ref, fin_ref, y_s, yt_s, u_s, inj_s, injp_s, hp_s, he_s = refs
    else:
        out_ref, y_s, yt_s, u_s, inj_s, injp_s, hp_s = refs
        fin_ref = he_s = None
    d = pl.program_id(1)
    chunks_per_seq = N_CHUNK // n_seq

    def chunk_rows(t):
        sl = pl.ds(t, N_CHUNK, stride=SSM_CHUNK)
        return jnp.concatenate([u_ref[0, 0, sl, :], u_ref[0, 1, sl, :]], axis=1)

    @pl.when(d == 0)
    def _build():
        for t in range(SSM_CHUNK):
            a_t = chunk_rows(t)
            yt_s[:, t * SSM_GROUP_CH:(t + 1) * SSM_GROUP_CH, :] = a_t.T.reshape(SSM_GROUPS, SSM_GROUP_CH, N_CHUNK)

        def tr(g, carry):
            u_s[g] = yt_s[g].T.astype(BF16)
            return carry
        lax.fori_loop(0, SSM_GROUPS, tr, 0)

    def proj(g, carry):
        ug = u_s[g]
        y = _dot(ug, m_ref[0, 0, g])

        @pl.when(d == 0)
        def _():
            y_s[g] = y

        @pl.when(d != 0)
        def _():
            y_s[g] = y_s[g] + y
        inj = _dot(ug, bend_ref[0, 0, g])
        inj_s[pl.ds(g, N_CHUNK, stride=SSM_GROUPS), :] = inj[:, :LANES]
        injp_s[pl.ds(g, N_CHUNK, stride=SSM_GROUPS), :] = inj[:, LANES:]
        return carry
    lax.fori_loop(0, SSM_GROUPS, proj, 0)

    a1 = a1_ref[0, 0]
    a2 = a2_ref[0, 0]
    h0 = h0_ref[0, 0]

    def step(j, carry):
        s, sp = carry
        n = j + d * (N_CHUNK - 1 - 2 * j)
        if n_seq > 1:
            keep = jnp.where(j % chunks_per_seq != 0, 1.0, 0.0).astype(F32)
            s = s * keep
            sp = sp * keep
        row = pl.multiple_of(n * SSM_GROUPS, SSM_GROUPS)
        hp_s[pl.ds(row, SSM_GROUPS), :] = s
        s_new = a1 * s + a2 * sp + inj_s[pl.ds(row, SSM_GROUPS), :]
        sp_new = a1 * sp - a2 * s + injp_s[pl.ds(row, SSM_GROUPS), :]
        if emit_final:
            he_s[pl.ds(row, SSM_GROUPS), :] = s_new
        return s_new, sp_new
    lax.fori_loop(0, N_CHUNK, step, (h0[:, :LANES], h0[:, LANES:]))

    def carry_in(g, carry):
        hp = hp_s[pl.ds(g, N_CHUNK, stride=SSM_GROUPS), :]
        y_s[g] = y_s[g] + _dot(hp.astype(BF16), cpow_ref[0, 0, g])
        return carry
    lax.fori_loop(0, SSM_GROUPS, carry_in, 0)

    if emit_final:
        last = (chunks_per_seq - 1) * (1 - d)
        for sq in range(n_seq):
            row = pl.multiple_of((sq * chunks_per_seq + last) * SSM_GROUPS, SSM_GROUPS)
            fin_ref[0, 0, sq * SSM_GROUPS:(sq + 1) * SSM_GROUPS, :] = he_s[pl.ds(row, SSM_GROUPS), :]

    @pl.when(d == 1)
    def _finish():
        def tr(g, carry):
            yt_s[g] = y_s[g].T
            return carry
        lax.fori_loop(0, SSM_GROUPS, tr, 0)
        dskip = d_ref[0]
        wglu = wglu_ref[0]
        for t in range(SSM_CHUNK):
            blk = yt_s[:, t * SSM_GROUP_CH:(t + 1) * SSM_GROUP_CH, :].reshape(SSM_WIDTH, N_CHUNK)
            y = blk.T + chunk_rows(t) * dskip
            gl = jax.nn.gelu(y, approximate=True)
            gate = _dot(gl.astype(BF16), wglu)
            o = gl * _sigmoid(gate)
            out_ref[0, 0, pl.ds(t, N_CHUNK, stride=SSM_CHUNK), :] = o[:, :LANES]
            out_ref[0, 1, pl.ds(t, N_CHUNK, stride=SSM_CHUNK), :] = o[:, LANES:]


def _ssm(l, u, ops, h0, dskip, wglu, *, n_seq, emit_final):
    bg = u.shape[0]
    m, bend, cpow, a1, a2 = ops
    out_shape = [jax.ShapeDtypeStruct((bg, 2, TOKENS, LANES), F32)]
    out_specs = [pl.BlockSpec((1, 2, TOKENS, LANES), lambda b, d: (b, 0, 0, 0))]
    scratch = [pltpu.VMEM((SSM_GROUPS, N_CHUNK, 256), F32),
               pltpu.VMEM((SSM_GROUPS, 256, N_CHUNK), F32),
               pltpu.VMEM((SSM_GROUPS, N_CHUNK, 256), BF16),
               pltpu.VMEM((N_CHUNK * SSM_GROUPS, LANES), F32),
               pltpu.VMEM((N_CHUNK * SSM_GROUPS, LANES), F32),
               pltpu.VMEM((N_CHUNK * SSM_GROUPS, LANES), F32)]
    if emit_final:
        out_shape.append(jax.ShapeDtypeStruct((bg, 2, n_seq * SSM_GROUPS, LANES), F32))
        out_specs.append(pl.BlockSpec((1, 1, n_seq * SSM_GROUPS, LANES), lambda b, d: (b, d, 0, 0)))
        scratch.append(pltpu.VMEM((N_CHUNK * SSM_GROUPS, LANES), F32))
    return pl.pallas_call(
        functools.partial(_ssm_kernel, n_seq=n_seq, emit_final=emit_final),
        out_shape=out_shape,
        grid=(bg, 2),
        in_specs=[
            pl.BlockSpec((1, 2, TOKENS, LANES), lambda b, d: (b, 0, 0, 0)),
            pl.BlockSpec((1, 1, SSM_GROUPS, 256, 256), lambda b, d: (l, d, 0, 0, 0)),
            pl.BlockSpec((1, 1, SSM_GROUPS, 256, 256), lambda b, d: (l, d, 0, 0, 0)),
            pl.BlockSpec((1, 1, SSM_GROUPS, 128, 256), lambda b, d: (l, d, 0, 0, 0)),
            pl.BlockSpec((1, 1, SSM_GROUPS, LANES), lambda b, d: (l, d, 0, 0)),
            pl.BlockSpec((1, 1, SSM_GROUPS, LANES), lambda b, d: (l, d, 0, 0)),
            pl.BlockSpec((1, 1, SSM_GROUPS, 256), lambda b, d: (b, d, 0, 0)),
            pl.BlockSpec((1, 1, SSM_WIDTH), lambda b, d: (l, 0, 0)),
            pl.BlockSpec((1, SSM_WIDTH, SSM_WIDTH), lambda b, d: (l, 0, 0)),
        ],
        out_specs=out_specs,
        scratch_shapes=scratch,
        compiler_params=pltpu.CompilerParams(dimension_semantics=("parallel", "arbitrary"),
                                             vmem_limit_bytes=VMEM_LIMIT),
        name="ssm_ctx" if emit_final else "ssm_lat",
    )(u, m, bend, cpow, a1, a2, h0, dskip, wglu)


def _ssm_operators(lam_re, lam_im, log_dt, b_re, b_im, c_re, c_im):
    hp = lax.Precision.HIGHEST
    t_n = SSM_CHUNK
    lr = jnp.minimum(lam_re.astype(F32), -1e-4)
    li = lam_im.astype(F32)
    dt = jnp.exp(log_dt.astype(F32))[..., None]
    k = jnp.arange(t_n + 1, dtype=F32)[:, None, None, None, None]
    mag = jnp.exp(lr * dt * k)
    ang = li * dt * k
    p_r, p_i = mag * jnp.cos(ang), mag * jnp.sin(ang)
    n_r, n_i = p_r[1] - 1.0, p_i[1]
    den = lr * lr + li * li
    q_r, q_i = (n_r * lr + n_i * li) / den, (n_i * lr - n_r * li) / den
    bb_r = q_r[..., None] * b_re - q_i[..., None] * b_im
    bb_i = q_r[..., None] * b_im + q_i[..., None] * b_re
    cp_r = c_re[None] * p_r[..., None, :] - c_im[None] * p_i[..., None, :]
    cp_i = c_re[None] * p_i[..., None, :] + c_im[None] * p_r[..., None, :]
    kern = (jnp.einsum('kldgcp,ldgpe->kldgce', cp_r, bb_r, precision=hp)
            - jnp.einsum('kldgcp,ldgpe->kldgce', cp_i, bb_i, precision=hp))
    kern, cp_r, cp_i = kern.astype(BF16), cp_r.astype(BF16), cp_i.astype(BF16)
    pad = jnp.zeros((t_n - 1,) + kern.shape[1:2] + kern.shape[3:], BF16)
    mats, bends, cpows = [], [], []
    for d in range(2):
        kd = kern[:t_n, :, d]
        band = jnp.concatenate([pad, kd], axis=0) if d == 0 else jnp.concatenate([kd[::-1], pad], axis=0)
        rows = jnp.stack([band[t_n - 1 - t:2 * t_n - 1 - t] for t in range(t_n)])
        mats.append(rows.transpose(2, 3, 0, 5, 1, 4).reshape(DEPTH, SSM_GROUPS, 256, 256))
        e_pr = p_r[:t_n, :, d][::-1] if d == 0 else p_r[:t_n, :, d]
        e_pi = p_i[:t_n, :, d][::-1] if d == 0 else p_i[:t_n, :, d]
        e_r = (e_pr[..., None] * bb_r[None, :, d] - e_pi[..., None] * bb_i[None, :, d]).astype(BF16)
        e_i = (e_pr[..., None] * bb_i[None, :, d] + e_pi[..., None] * bb_r[None, :, d]).astype(BF16)
        e_r = e_r.transpose(1, 2, 0, 4, 3).reshape(DEPTH, SSM_GROUPS, 256, SSM_STATE)
        e_i = e_i.transpose(1, 2, 0, 4, 3).reshape(DEPTH, SSM_GROUPS, 256, SSM_STATE)
        bends.append(jnp.concatenate([e_r, e_i, e_i, e_r], axis=-1))
        o_r = cp_r[1:, :, d] if d == 0 else cp_r[1:, :, d][::-1]
        o_i = cp_i[1:, :, d] if d == 0 else cp_i[1:, :, d][::-1]
        o_r = o_r.transpose(1, 2, 4, 0, 3).reshape(DEPTH, SSM_GROUPS, SSM_STATE, 256)
        o_i = o_i.transpose(1, 2, 4, 0, 3).reshape(DEPTH, SSM_GROUPS, SSM_STATE, 256)
        cpows.append(jnp.concatenate([o_r, -o_i], axis=2))
    a_r, a_i = p_r[t_n], p_i[t_n]
    a1 = jnp.concatenate([a_r, a_r], axis=-1)
    a2 = jnp.concatenate([-a_i, a_i], axis=-1)
    stack = lambda xs: jnp.stack(xs, axis=1).astype(BF16)
    return stack(mats), stack(bends), stack(cpows), a1, a2


def _softmax_pv(s, v):
    m = jnp.max(s, axis=-1, keepdims=True)
    p = jnp.exp(s - m)
    l = jnp.sum(p, axis=-1, keepdims=True)
    return _dot(p.astype(BF16), v) / l


def _ctx_attn_kernel(z_ref, na_ref, gqa_ref):
    z = z_ref[0]
    outs = []
    for h in range(NA_HEADS):
        sl = slice(h * HEAD_DIM, (h + 1) * HEAD_DIM)
        q = z[:, Z_QNA:Z_QNA + NA_WIDTH][:, sl]
        k = z[:, Z_KNA:Z_KNA + NA_WIDTH][:, sl]
        v = z[:, Z_VNA:Z_VNA + NA_WIDTH][:, sl]
        outs.append(_softmax_pv(_dot_nt(q, k), v))
    na_ref[0] = jnp.concatenate(outs, axis=1).astype(BF16)
    outs = []
    for g in range(GQA_KV_HEADS):
        k = z[:, Z_KG + g * HEAD_DIM:Z_KG + (g + 1) * HEAD_DIM]
        v = z[:, Z_VG + g * HEAD_DIM:Z_VG + (g + 1) * HEAD_DIM]
        for r in range(GQA_REP):
            c0 = Z_QG + (g * GQA_REP + r) * HEAD_DIM
            outs.append(_softmax_pv(_dot_nt(z[:, c0:c0 + HEAD_DIM], k), v))
    gqa_ref[0] = jnp.concatenate(outs, axis=1).astype(BF16)


def _ctx_attention(zb):
    z3 = zb.reshape(BATCH, SEQ, Z_WIDTH)
    na, gqa = pl.pallas_call(
        _ctx_attn_kernel,
        out_shape=[jax.ShapeDtypeStruct((BATCH, SEQ, NA_WIDTH), BF16),
                   jax.ShapeDtypeStruct((BATCH, SEQ, GQA_WIDTH), BF16)],
        grid=(BATCH,),
        in_specs=[pl.BlockSpec((1, SEQ, Z_WIDTH), lambda b: (b, 0, 0))],
        out_specs=[pl.BlockSpec((1, SEQ, NA_WIDTH), lambda b: (b, 0, 0)),
                   pl.BlockSpec((1, SEQ, GQA_WIDTH), lambda b: (b, 0, 0))],
        compiler_params=pltpu.CompilerParams(dimension_semantics=("parallel",), vmem_limit_bytes=VMEM_LIMIT),
        name="attn_ctx",
    )(z3)
    return na.reshape(1, TOKENS, NA_WIDTH), gqa.reshape(1, TOKENS, GQA_WIDTH)


NA_KEYS = NA_WIN_R * GRID_W


NA_ROWS = 2


def _na_lat_kernel(q_ref, k_ref, v_ref, ck_ref, cv_ref, *refs):
    bias_refs, o_ref = refs[:NA_ROWS], refs[NA_ROWS]
    i = pl.program_id(1)
    ck = ck_ref[0, 0]
    cv = cv_ref[0, 0]
    qt = q_ref[0].astype(F32).T
    hd = lax.broadcasted_iota(jnp.int32, (NA_WIDTH, NA_WIDTH), 0) // HEAD_DIM
    hq = lax.broadcasted_iota(jnp.int32, (NA_WIDTH, NA_WIDTH), 1) // HEAD_DIM
    for j in range(NA_ROWS):
        r = NA_ROWS * i + j
        rs = jnp.clip(r - NA_WIN_R // 2, 0, GRID_H - NA_WIN_R)
        start = pl.multiple_of(rs * GRID_W, GRID_W)
        kw = k_ref[0, pl.ds(start, NA_KEYS), :]
        vw = v_ref[0, pl.ds(start, NA_KEYS), :]
        qj = qt[:, j * GRID_W:(j + 1) * GRID_W]
        bd = jnp.where(hd == hq, jnp.concatenate([qj] * NA_HEADS, axis=1), 0.0).astype(BF16)
        s_loc = _dot(kw, bd) + bias_refs[j][0, 0]
        s_ctx = _dot(ck, bd)
        m = jnp.maximum(jnp.max(s_loc, axis=0, keepdims=True), jnp.max(s_ctx, axis=0, keepdims=True))
        p_loc = jnp.exp(s_loc - m)
        p_ctx = jnp.exp(s_ctx - m)
        inv = 1.0 / (jnp.sum(p_loc, axis=0, keepdims=True) + jnp.sum(p_ctx, axis=0, keepdims=True))
        full = (_dot((p_loc * inv).T.astype(BF16), vw)
                + _dot((p_ctx * inv).T.astype(BF16), cv))
        out = jnp.concatenate([full[h * HEAD_DIM:(h + 1) * HEAD_DIM, h * HEAD_DIM:(h + 1) * HEAD_DIM]
                               for h in range(NA_HEADS)], axis=1)
        o_ref[0, j * GRID_W:(j + 1) * GRID_W, :] = out.astype(BF16)


def _na_row_class(r):
    return r - jnp.clip(r - NA_WIN_R // 2, 0, GRID_H - NA_WIN_R)


def _na_latent(l, zb, ck, cv, bias):
    tq = NA_ROWS * GRID_W
    bias_specs = [pl.BlockSpec((1, 1, NA_KEYS, NA_WIDTH),
                               functools.partial(lambda b, i, j: (l, _na_row_class(NA_ROWS * i + j), 0, 0), j=j))
                  for j in range(NA_ROWS)]
    return pl.pallas_call(
        _na_lat_kernel,
        out_shape=jax.ShapeDtypeStruct((DEC_BATCH, TOKENS, NA_WIDTH), BF16),
        grid=(DEC_BATCH, GRID_H // NA_ROWS),
        in_specs=[
            pl.BlockSpec((1, tq, NA_WIDTH), lambda b, i: (b, i, Z_QNA // NA_WIDTH)),
            pl.BlockSpec((1, TOKENS, NA_WIDTH), lambda b, i: (b, 0, Z_KNA // NA_WIDTH)),
            pl.BlockSpec((1, TOKENS, NA_WIDTH), lambda b, i: (b, 0, Z_VNA // NA_WIDTH)),
            pl.BlockSpec((1, 1, PAST_LEN, NA_WIDTH), lambda b, i: (b, l, 0, 0)),
            pl.BlockSpec((1, 1, PAST_LEN, NA_WIDTH), lambda b, i: (b, l, 0, 0)),
        ] + bias_specs,
        out_specs=pl.BlockSpec((1, tq, NA_WIDTH), lambda b, i: (b, i, 0)),
        compiler_params=pltpu.CompilerParams(dimension_semantics=("parallel", "arbitrary"),
                                             vmem_limit_bytes=VMEM_LIMIT),
        name="na_lat",
    )(zb, zb, zb, ck, cv, *([bias] * NA_ROWS))


def _na_bias_table(tab):
    q = np.arange(GRID_W)
    kc = np.arange(GRID_W)
    cs = np.clip(q - NA_WIN_C // 2, 0, GRID_W - NA_WIN_C)
    valid = (kc[None, :] >= cs[:, None]) & (kc[None, :] < cs[:, None] + NA_WIN_C)
    w = GRID_W - 1
    padded = jnp.pad(tab.astype(F32), ((0, 0), (0, 0), (0, 0), (w, w)))
    off = w + NA_WIN_C - 1
    cols = jnp.stack([padded[..., off - i:off - i + GRID_W] for i in range(GRID_W)], axis=-2)
    cols = jnp.where(valid, cols, -jnp.inf)
    b = jnp.stack([cols[:, :, NA_WIN_R - 1 - c:2 * NA_WIN_R - 1 - c] for c in range(NA_WIN_R)], axis=1)
    return b.transpose(0, 1, 3, 5, 2, 4).reshape(DEPTH, NA_WIN_R, NA_KEYS, NA_WIDTH)


GQA_TK = 256
GQA_KEYS = DEC_SEQ + PAST_LEN


def _gqa_kernel(q_ref, ktn_ref, ktc_ref, vn_ref, vc_ref, o_ref, s_s, m_s, *, tq):
    q = q_ref[0]
    n_new = DEC_SEQ // GQA_TK
    half = GQA_REP * tq // 2
    qh = [jnp.concatenate([q[:, r * HEAD_DIM:(r + 1) * HEAD_DIM] for r in (2 * i, 2 * i + 1)], axis=0)
          for i in range(2)]
    rows = [slice(i * half, (i + 1) * half) for i in range(2)]

    def fold(s):
        return jnp.maximum(s[:, :LANES], s[:, LANES:])

    for i in range(2):
        s = _dot(qh[i], ktc_ref[0])
        s_s[rows[i], DEC_SEQ:GQA_KEYS] = s
        m_s[rows[i], :] = fold(s)
    for c in range(0, n_new, 2):
        for i in range(2):
            s0 = _dot(qh[i], ktn_ref[0, :, c * GQA_TK:(c + 1) * GQA_TK])
            s1 = _dot(qh[i], ktn_ref[0, :, (c + 1) * GQA_TK:(c + 2) * GQA_TK])
            s_s[rows[i], c * GQA_TK:(c + 1) * GQA_TK] = s0
            s_s[rows[i], (c + 1) * GQA_TK:(c + 2) * GQA_TK] = s1
            m_s[rows[i], :] = jnp.maximum(m_s[rows[i], :], jnp.maximum(fold(s0), fold(s1)))

    m = jnp.max(m_s[...], axis=-1, keepdims=True)
    m_s[...] = jnp.broadcast_to(m, m_s.shape)

    def probs(i, c0):
        mb = m_s[rows[i], :]
        return jnp.exp2(s_s[rows[i], c0:c0 + GQA_TK] - jnp.concatenate([mb, mb], axis=1)).astype(BF16)

    acc = [_dot(probs(i, DEC_SEQ), vc_ref[0, 0]) for i in range(2)]
    for c in range(n_new):
        for i in range(2):
            acc[i] = acc[i] + _dot(probs(i, c * GQA_TK), vn_ref[0, 0, c * GQA_TK:(c + 1) * GQA_TK, :])
    outs = []
    for i in range(2):
        out = acc[i][:, :HEAD_DIM] / acc[i][:, HEAD_DIM:HEAD_DIM + 1]
        outs += [out[:tq], out[tq:]]
    o_ref[0] = jnp.concatenate(outs, axis=1).astype(BF16)


def _gqa_latent(zb, kt_new, kt_cache, v1_new, v1_cache):
    tq = 128
    rows = GQA_REP * tq
    return pl.pallas_call(
        functools.partial(_gqa_kernel, tq=tq),
        out_shape=jax.ShapeDtypeStruct((DEC_BATCH, TOKENS, GQA_WIDTH), BF16),
        grid=(DEC_BATCH, GQA_KV_HEADS, TOKENS // tq),
        in_specs=[
            pl.BlockSpec((1, tq, 256), lambda b, g, i: (b, i, Z_QG // 256 + g)),
            pl.BlockSpec((1, HEAD_DIM, DEC_SEQ), lambda b, g, i: (b, g, 0)),
            pl.BlockSpec((1, HEAD_DIM, PAST_LEN), lambda b, g, i: (b, g, 0)),
            pl.BlockSpec((1, 1, DEC_SEQ, LANES), lambda b, g, i: (b, g, 0, 0)),
            pl.BlockSpec((1, 1, PAST_LEN, LANES), lambda b, g, i: (b, g, 0, 0)),
        ],
        out_specs=pl.BlockSpec((1, tq, 256), lambda b, g, i: (b, i, g)),
        scratch_shapes=[pltpu.VMEM((rows, GQA_KEYS), F32), pltpu.VMEM((rows, LANES), F32)],
        compiler_params=pltpu.CompilerParams(dimension_semantics=("parallel", "parallel", "arbitrary"),
                                             vmem_limit_bytes=VMEM_LIMIT),
        name="gqa_lat",
    )(zb, kt_new, kt_cache, v1_new, v1_cache)


FF_CHUNK = 256


def _ffn_kernel(x_ref, ssm_ref, na_ref, gqa_ref, mod_ref, wout_ref, ln1g_ref, ln1b_ref, win_ref, wo_ref,
                ln2g_ref, ln2b_ref, o_ref, h2_s):
    ssm = jnp.concatenate([ssm_ref[0, 0], ssm_ref[0, 1]], axis=1)
    o = (_dot(ssm.astype(BF16), wout_ref[0, 0:SSM_WIDTH, :])
         + _dot(na_ref[0], wout_ref[0, SSM_WIDTH:SSM_WIDTH + NA_WIDTH, :])
         + _dot(gqa_ref[0], wout_ref[0, SSM_WIDTH + NA_WIDTH:, :]))
    y = DEEPNORM_ALPHA * x_ref[0] + mod_ref[0, 2:3, :] * o
    x1 = _ln(y) * ln1g_ref[0] + ln1b_ref[0]
    h2 = _ln(x1) * (1.0 + mod_ref[0, 4:5, :]) + mod_ref[0, 3:4, :]
    h2_s[...] = h2.astype(BF16)
    acc = None
    for j in range(D_FF // FF_CHUNK):
        h2b = h2_s[...]
        a = _dot(h2b, win_ref[0, :, j * FF_CHUNK:(j + 1) * FF_CHUNK])
        gt = _dot(h2b, win_ref[0, :, D_FF + j * FF_CHUNK:D_FF + (j + 1) * FF_CHUNK])
        f = (a * _sigmoid(a) * gt).astype(BF16)
        c = _dot(f, wo_ref[0, j * FF_CHUNK:(j + 1) * FF_CHUNK, :])
        acc = c if acc is None else acc + c
    y = DEEPNORM_ALPHA * x1 + mod_ref[0, 5:6, :] * acc
    o_ref[0] = _ln(y) * ln2g_ref[0] + ln2b_ref[0]


def _out_ffn(l, x, ssm, na, gqa, mod, w_out, ln1g, ln1b, w_ffn_in, w_ffn_out, ln2g, ln2b):
    bg = x.shape[0]
    tm = 512
    vec = pl.BlockSpec((1, 1, D_MODEL), lambda b, i: (l, 0, 0))
    resident = lambda shape: pl.BlockSpec((1,) + shape, lambda b, i: (l, 0, 0), pipeline_mode=pl.Buffered(1))
    return pl.pallas_call(
        _ffn_kernel,
        out_shape=jax.ShapeDtypeStruct((bg, TOKENS, D_MODEL), F32),
        grid=(bg, TOKENS // tm),
        in_specs=[
            pl.BlockSpec((1, tm, D_MODEL), lambda b, i: (b, i, 0)),
            pl.BlockSpec((1, 2, tm, LANES), lambda b, i: (b, 0, i, 0)),
            pl.BlockSpec((1, tm, NA_WIDTH), lambda b, i: (b, i, 0)),
            pl.BlockSpec((1, tm, GQA_WIDTH), lambda b, i: (b, i, 0)),
            pl.BlockSpec((1, 6, D_MODEL), lambda b, i: (b, 0, 0)),
            resident((D_MODEL, D_MODEL)),
            vec,
            vec,
            resident((D_MODEL, 2 * D_FF)),
            resident((D_FF, D_MODEL)),
            vec,
            vec,
        ],
        out_specs=pl.BlockSpec((1, tm, D_MODEL), lambda b, i: (b, i, 0)),
        scratch_shapes=[pltpu.VMEM((tm, D_MODEL), BF16)],
        compiler_params=pltpu.CompilerParams(dimension_semantics=("parallel", "parallel"),
                                             vmem_limit_bytes=VMEM_LIMIT),
        name="ffn",
    )(x, ssm, na, gqa, mod, w_out, ln1g, ln1b, w_ffn_in, w_ffn_out, ln2g, ln2b)


def _rope_tables():
    nf = HEAD_DIM // 4
    t = jnp.arange(DEC_SEQ)
    inv = ROPE_THETA ** (-jnp.arange(nf, dtype=F32) / nf)
    rows = (t // GRID_W).astype(F32)
    cols = (t % GRID_W).astype(F32)
    ang = jnp.stack([rows[:, None] * inv, cols[:, None] * inv], axis=1)
    cos = jnp.repeat(jnp.cos(ang)[:, :, None, :], 2, axis=2).reshape(DEC_SEQ, HEAD_DIM)
    sin = jnp.sin(ang)[:, :, None, :] * jnp.asarray([-1.0, 1.0], F32)[None, None, :, None]
    sin = sin.reshape(DEC_SEQ, HEAD_DIM)
    return jnp.tile(cos, (1, 2)), jnp.tile(sin, (1, 2))


def _head_mean_matrix():
    h = np.arange(GQA_WIDTH) // HEAD_DIM
    return jnp.asarray((h[:, None] == h[None, :]).astype(np.float32) / HEAD_DIM, BF16)


def kernel(x_prompt, x_sample, c, cache_na_k, cache_na_v, cache_gqa_k, cache_gqa_v, state_ssm_re, state_ssm_im,
           c_ctx, w_ada, b_ada, w_in, w_out, q_norm_g, k_norm_g, na_bias, ssm_lam_re, ssm_lam_im, ssm_log_dt,
           ssm_b_re, ssm_b_im, ssm_c_re, ssm_c_im, ssm_d, w_ssm_glu, ln1_g, ln1_b, ln2_g, ln2_b,
           w_ffn_in, w_ffn_out):
    cond8 = jnp.concatenate([c_ctx[None, :], c, jnp.zeros((8 - 1 - DEC_BATCH, D_MODEL), F32)], axis=0)
    mod = _modulation(cond8, w_ada, b_ada).reshape(DEPTH, 8, 6, D_MODEL)
    cos, sin = _rope_tables()
    hm = _head_mean_matrix()
    w_in_b = w_in.astype(BF16)
    w_out_b = w_out.astype(BF16)
    w_ffn_in_b = w_ffn_in.astype(BF16)
    w_ffn_out_b = w_ffn_out.astype(BF16)
    w_glu_b = w_ssm_glu.astype(BF16)
    qg = jnp.tile(q_norm_g, (1, GQA_HEADS))[:, None, :]
    kg = jnp.tile(k_norm_g, (1, GQA_KV_HEADS))[:, None, :]
    dskip = ssm_d[:, None, :]
    ln = (ln1_g[:, None, :], ln1_b[:, None, :], ln2_g[:, None, :], ln2_b[:, None, :])
    ops = _ssm_operators(ssm_lam_re, ssm_lam_im, ssm_log_dt, ssm_b_re, ssm_b_im, ssm_c_re, ssm_c_im)
    na_tab = _na_bias_table(na_bias)
    ck_na = cache_na_k.reshape(DEC_BATCH, DEPTH, PAST_LEN, NA_WIDTH).astype(BF16)
    cv_na = cache_na_v.reshape(DEC_BATCH, DEPTH, PAST_LEN, NA_WIDTH).astype(BF16)
    kt_cache = cache_gqa_k.transpose(1, 0, 3, 4, 2).reshape(DEPTH, DEC_BATCH, GQA_KV_WIDTH, PAST_LEN).astype(BF16)
    cv_g = cache_gqa_v.transpose(1, 0, 3, 2, 4)
    v1_cache = jnp.concatenate([cv_g, jnp.ones(cv_g.shape[:-1] + (1,), F32),
                                jnp.zeros(cv_g.shape[:-1] + (LANES - HEAD_DIM - 1,), F32)], axis=-1).astype(BF16)
    s_re = state_ssm_re.astype(F32).transpose(1, 0, 2, 3, 4)
    s_im = state_ssm_im.astype(F32).transpose(1, 0, 2, 3, 4)
    h0_lat = jnp.concatenate([s_re, s_im, s_im, s_re], axis=-1)
    h0_ctx = jnp.zeros((1, 2, SSM_GROUPS, 256), F32)

    y_ctx = x_prompt.reshape(1, TOKENS, D_MODEL)
    y_lat = x_sample
    kv_na_l, kv_g_l, fin_l = [], [], []
    for l in range(DEPTH):
        mod_c = mod[l, 0:1]
        u, zb, kv_na, kv_g = _in_projection(l, y_ctx, mod_c, w_in_b, qg, kg, hm, cos, sin, latent=False)
        ssm_o, fin = _ssm(l, u, ops, h0_ctx, dskip, w_glu_b, n_seq=BATCH, emit_final=True)
        na_o, gqa_o = _ctx_attention(zb)
        y_ctx = _out_ffn(l, y_ctx, ssm_o, na_o, gqa_o, mod_c, w_out_b, ln[0], ln[1], w_ffn_in_b, w_ffn_out_b,
                         ln[2], ln[3])
        kv_na_l.append(kv_na); kv_g_l.append(kv_g); fin_l.append(fin)

        mod_s = mod[l, 1:1 + DEC_BATCH]
        u, zb, kt_new, v1_new = _in_projection(l, y_lat, mod_s, w_in_b, qg, kg, hm, cos, sin, latent=True)
        ssm_o = _ssm(l, u, ops, h0_lat[l], dskip, w_glu_b, n_seq=1, emit_final=False)[0]
        na_o = _na_latent(l, zb, ck_na, cv_na, na_tab)
        gqa_o = _gqa_latent(zb, kt_new, kt_cache[l], v1_new, v1_cache[l])
        y_lat = _out_ffn(l, y_lat, ssm_o, na_o, gqa_o, mod_s, w_out_b, ln[0], ln[1], w_ffn_in_b, w_ffn_out_b,
                         ln[2], ln[3])

    kv_na = jnp.stack(kv_na_l, axis=0).reshape(DEPTH, BATCH, SEQ, 2, NA_HEADS, HEAD_DIM)
    kv_g = jnp.stack(kv_g_l, axis=0).reshape(DEPTH, BATCH, SEQ, 2, GQA_KV_HEADS, HEAD_DIM)
    kv_na = kv_na.transpose(3, 1, 0, 2, 4, 5)
    kv_g = kv_g.transpose(3, 1, 0, 2, 4, 5)
    fin = jnp.stack(fin_l, axis=0).reshape(DEPTH, 2, BATCH, SSM_GROUPS, 2, SSM_STATE)
    fin = fin.transpose(4, 2, 0, 1, 3, 5)
    return (y_ctx.reshape(BATCH, SEQ, D_MODEL), y_lat, kv_na[0], kv_na[1], kv_g[0], kv_g[1], fin[0], fin[1])
```

```python
import functools
import math

import numpy as np
import jax
import jax.numpy as jnp
from jax import lax
from jax.experimental import pallas as pl
from jax.experimental.pallas import tpu as pltpu

F32 = jnp.float32
BF16 = jnp.bfloat16

D_MODEL = 1024
BATCH = 16
SEQ = 256
DEPTH = 2
DEC_BATCH = 2
DEC_SEQ = 4096
PAST_LEN = 256
GRID_W = 64
GRID_H = DEC_SEQ // GRID_W
HEAD_DIM = 64
SSM_WIDTH = 256
SSM_GROUP_CH = 16
SSM_GROUPS = 16
SSM_STATE = 64
NA_HEADS = 4
NA_WIDTH = 256
NA_WIN_R = 8
NA_WIN_C = 16
GQA_HEADS = 8
GQA_KV_HEADS = 2
GQA_REP = 4
GQA_WIDTH = 512
GQA_KV_WIDTH = 128
IN_WIDTH = 1792
D_FF = 2816
ROPE_THETA = 10000.0
LN_EPS = 1e-6
RMS_EPS = 1e-6
DEEPNORM_ALPHA = (2 * DEPTH) ** 0.25
QK_SCALE = HEAD_DIM ** -0.5
LOG2E = math.log2(math.e)

TOKENS = 4096
SSM_CHUNK = 16
N_CHUNK = TOKENS // SSM_CHUNK
Z_WIDTH = IN_WIDTH - SSM_WIDTH
Z_QNA, Z_KNA, Z_VNA, Z_QG, Z_KG, Z_VG = 0, 256, 512, 768, 1280, 1408

VMEM_LIMIT = 56 * 1024 * 1024
LANES = 128


def _sigmoid(x):
    return 1.0 / (1.0 + jnp.exp(-x))


def _ln(x):
    mu = jnp.mean(x, axis=-1, keepdims=True)
    xc = x - mu
    var = jnp.mean(xc * xc, axis=-1, keepdims=True)
    return xc * lax.rsqrt(var + LN_EPS)


def _dot(a, b):
    return jnp.dot(a, b, preferred_element_type=F32)


def _dot_nt(a, b):
    return lax.dot_general(a, b, (((1,), (1,)), ((), ())), preferred_element_type=F32)


def _mod_kernel(c_ref, w_ref, b_ref, o_ref):
    c = c_ref[...]
    s = c * _sigmoid(c)
    o_ref[0] = jnp.dot(s, w_ref[0], preferred_element_type=F32, precision=lax.Precision.HIGHEST) + b_ref[0]


def _modulation(cond8, w_ada, b_ada):
    nb = 1536
    return pl.pallas_call(
        _mod_kernel,
        out_shape=jax.ShapeDtypeStruct((DEPTH, 8, 6 * D_MODEL), F32),
        grid=(DEPTH, 6 * D_MODEL // nb),
        in_specs=[
            pl.BlockSpec((8, D_MODEL), lambda l, j: (0, 0)),
            pl.BlockSpec((1, D_MODEL, nb), lambda l, j: (l, 0, j)),
            pl.BlockSpec((1, 1, nb), lambda l, j: (l, 0, j)),
        ],
        out_specs=pl.BlockSpec((1, 8, nb), lambda l, j: (l, 0, j)),
        compiler_params=pltpu.CompilerParams(dimension_semantics=("parallel", "parallel"),
                                             vmem_limit_bytes=VMEM_LIMIT),
        name="mod",
    )(cond8, w_ada, b_ada.reshape(DEPTH, 1, 6 * D_MODEL))


def _swap16(x):
    w = x.shape[-1]
    lane = lax.broadcasted_iota(jnp.int32, x.shape, x.ndim - 1)
    return jnp.where((lane & 16) != 0, pltpu.roll(x, 16, x.ndim - 1), pltpu.roll(x, w - 16, x.ndim - 1))


def _win_kernel(x_ref, mod_ref, w_ref, qg_ref, kg_ref, hm_ref, cos_ref, sin_ref, *out_refs, latent):
    u_ref, z_ref = out_refs[0], out_refs[1]
    x = x_ref[0]
    h = _ln(x) * (1.0 + mod_ref[0, 1:2, :]) + mod_ref[0, 0:1, :]
    z = _dot(h.astype(BF16), w_ref[0])
    u_ref[0, 0] = z[:, 0:LANES]
    u_ref[0, 1] = z[:, LANES:SSM_WIDTH]
    q_na = z[:, 256:512] * QK_SCALE
    kv_na = z[:, 512:1024]
    q_g = z[:, 1024:1536]
    k_g = z[:, 1536:1664]
    v_g = z[:, 1664:1792]
    q_ms = _dot((q_g * q_g).astype(BF16), hm_ref[...])
    k_ms = _dot((k_g * k_g).astype(BF16), hm_ref[0:GQA_KV_WIDTH, 0:GQA_KV_WIDTH])
    q_n = q_g * lax.rsqrt(q_ms + RMS_EPS) * qg_ref[0]
    k_n = k_g * lax.rsqrt(k_ms + RMS_EPS) * kg_ref[0]
    if latent:
        kt_ref, v1_ref = out_refs[2], out_refs[3]
        cos = cos_ref[...]
        sin = sin_ref[...]
        cos4 = jnp.concatenate([cos] * 4, axis=1)
        sin4 = jnp.concatenate([sin] * 4, axis=1)
        q_n = (q_n * cos4 + _swap16(q_n) * sin4) * (QK_SCALE * LOG2E)
        k_n = k_n * cos + _swap16(k_n) * sin
        kt_ref[0] = k_n.T.astype(BF16)
        lane = lax.broadcasted_iota(jnp.int32, v_g.shape, 1)
        ones_col = jnp.where(lane == HEAD_DIM, 1.0, 0.0)
        v1_ref[0, 0] = jnp.where(lane < HEAD_DIM, v_g, ones_col).astype(BF16)
        v1_ref[0, 1] = jnp.where(lane < HEAD_DIM, pltpu.roll(v_g, HEAD_DIM, 1), ones_col).astype(BF16)
    else:
        out_refs[2][0] = kv_na
        out_refs[3][0] = jnp.concatenate([k_n, v_g], axis=1)
        q_n = q_n * QK_SCALE
    zb = jnp.concatenate([q_na, kv_na, q_n, k_n, v_g], axis=1)
    z_ref[0] = zb.astype(BF16)


def _in_projection(l, x, mod, w_in, qg, kg, hm, cos, sin, *, latent):
    bg = x.shape[0]
    tm = 512
    out_shape = [jax.ShapeDtypeStruct((bg, 2, TOKENS, LANES), F32),
                 jax.ShapeDtypeStruct((bg, TOKENS, Z_WIDTH), BF16)]
    out_specs = [pl.BlockSpec((1, 2, tm, LANES), lambda b, i: (b, 0, i, 0)),
                 pl.BlockSpec((1, tm, Z_WIDTH), lambda b, i: (b, i, 0))]
    if latent:
        out_shape += [jax.ShapeDtypeStruct((bg, GQA_KV_WIDTH, TOKENS), BF16),
                      jax.ShapeDtypeStruct((bg, GQA_KV_HEADS, TOKENS, LANES), BF16)]
        out_specs += [pl.BlockSpec((1, GQA_KV_WIDTH, tm), lambda b, i: (b, 0, i)),
                      pl.BlockSpec((1, GQA_KV_HEADS, tm, LANES), lambda b, i: (b, 0, i, 0))]
    else:
        out_shape += [jax.ShapeDtypeStruct((bg, TOKENS, 2 * NA_WIDTH), F32),
                      jax.ShapeDtypeStruct((bg, TOKENS, 2 * GQA_KV_WIDTH), F32)]
        out_specs += [pl.BlockSpec((1, tm, 2 * NA_WIDTH), lambda b, i: (b, i, 0)),
                      pl.BlockSpec((1, tm, 2 * GQA_KV_WIDTH), lambda b, i: (b, i, 0))]
    return pl.pallas_call(
        functools.partial(_win_kernel, latent=latent),
        out_shape=out_shape,
        grid=(bg, TOKENS // tm),
        in_specs=[
            pl.BlockSpec((1, tm, D_MODEL), lambda b, i: (b, i, 0)),
            pl.BlockSpec((1, 6, D_MODEL), lambda b, i: (b, 0, 0)),
            pl.BlockSpec((1, D_MODEL, IN_WIDTH), lambda b, i: (l, 0, 0)),
            pl.BlockSpec((1, 1, GQA_WIDTH), lambda b, i: (l, 0, 0)),
            pl.BlockSpec((1, 1, GQA_KV_WIDTH), lambda b, i: (l, 0, 0)),
            pl.BlockSpec((GQA_WIDTH, GQA_WIDTH), lambda b, i: (0, 0)),
            pl.BlockSpec((tm, LANES), lambda b, i: (i, 0)),
            pl.BlockSpec((tm, LANES), lambda b, i: (i, 0)),
        ],
        out_specs=out_specs,
        compiler_params=pltpu.CompilerParams(dimension_semantics=("parallel", "parallel"),
                                             vmem_limit_bytes=VMEM_LIMIT),
        name="win_lat" if latent else "win_ctx",
    )(x, mod, w_in, qg, kg, hm, cos, sin)


def _ssm_kernel(u_ref, crt_ref, cit_ref, pkr_ref, pki_ref, pcr_ref, pci_ref, bb1_ref, bb2_ref, pwr_ref, pwi_ref,
                a1_ref, a2_ref, h0_ref, d_ref, wglu_ref, *refs, n_seq, emit_final):
    if emit_final:
        out_ref, fin_ref, y_s, yt_s, u_s, inj_s, injp_s, hp_s, m_s, bend_s, cpow_s, he_s = refs
    else:
        out_ref, y_s, yt_s, u_s, inj_s, injp_s, hp_s, m_s, bend_s, cpow_s = refs
        fin_ref = he_s = None
    d = pl.program_id(1)
    chunks_per_seq = N_CHUNK // n_seq

    def build_ops(g, carry):
        cr = crt_ref[0, 0, g]
        ci = cit_ref[0, 0, g]

        def re_proj(pr, pi):
            return jnp.concatenate([cr * pr - ci * pi, -(cr * pi + ci * pr)], axis=0)
        cpow_s[g] = re_proj(pcr_ref[0, 0, g], pci_ref[0, 0, g]).astype(BF16)
        bb1 = bb1_ref[0, 0, g]
        bb2 = bb2_ref[0, 0, g]
        kcat = jnp.dot(bb1[:, :LANES], re_proj(pkr_ref[0, 0, g], pki_ref[0, 0, g]),
                       preferred_element_type=F32, precision=lax.Precision.HIGHEST)
        lane = lax.broadcasted_iota(jnp.int32, kcat.shape, 1)
        for t in range(SSM_CHUNK):
            rows = slice(t * SSM_GROUP_CH, (t + 1) * SSM_GROUP_CH)
            lo, hi = t * SSM_GROUP_CH, (t + 1) * SSM_GROUP_CH
            fwd = kcat if t == 0 else jnp.where(lane >= lo, pltpu.roll(kcat, lo, 1), 0.0)
            bwd = kcat if hi == 256 else jnp.where(lane < hi, pltpu.roll(kcat, hi, 1), 0.0)
            m_s[g, rows, :] = jnp.where(d == 0, fwd, bwd).astype(BF16)
            bend_s[g, rows, :] = (pwr_ref[0, 0, g, t:t + 1, :] * bb1 + pwi_ref[0, 0, g, t:t + 1, :] * bb2).astype(BF16)
        return carry
    lax.fori_loop(0, SSM_GROUPS, build_ops, 0)

    def chunk_rows(t):
        sl = pl.ds(t, N_CHUNK, stride=SSM_CHUNK)
        return jnp.concatenate([u_ref[0, 0, sl, :], u_ref[0, 1, sl, :]], axis=1)

    @pl.when(d == 0)
    def _build():
        for t in range(SSM_CHUNK):
            a_t = chunk_rows(t)
            yt_s[:, t * SSM_GROUP_CH:(t + 1) * SSM_GROUP_CH, :] = a_t.T.reshape(SSM_GROUPS, SSM_GROUP_CH, N_CHUNK)

        def tr(g, carry):
            u_s[g] = yt_s[g].T.astype(BF16)
            return carry
        lax.fori_loop(0, SSM_GROUPS, tr, 0)

    def proj(g, carry):
        ug = u_s[g]
        y = _dot(ug, m_s[g])

        @pl.when(d == 0)
        def _():
            y_s[g] = y

        @pl.when(d != 0)
        def _():
            y_s[g] = y_s[g] + y
        inj = _dot(ug, bend_s[g])
        inj_s[pl.ds(g, N_CHUNK, stride=SSM_GROUPS), :] = inj[:, :LANES]
        injp_s[pl.ds(g, N_CHUNK, stride=SSM_GROUPS), :] = inj[:, LANES:]
        return carry
    lax.fori_loop(0, SSM_GROUPS, proj, 0)

    a1 = a1_ref[0, 0]
    a2 = a2_ref[0, 0]
    h0 = h0_ref[0, 0]

    def step(j, carry):
        s, sp = carry
        n = j + d * (N_CHUNK - 1 - 2 * j)
        if n_seq > 1:
            keep = jnp.where(j % chunks_per_seq != 0, 1.0, 0.0).astype(F32)
            s = s * keep
            sp = sp * keep
        row = pl.multiple_of(n * SSM_GROUPS, SSM_GROUPS)
        hp_s[pl.ds(row, SSM_GROUPS), :] = s
        s_new = a1 * s + a2 * sp + inj_s[pl.ds(row, SSM_GROUPS), :]
        sp_new = a1 * sp - a2 * s + injp_s[pl.ds(row, SSM_GROUPS), :]
        if emit_final:
            he_s[pl.ds(row, SSM_GROUPS), :] = s_new
        return s_new, sp_new
    lax.fori_loop(0, N_CHUNK, step, (h0[:, :LANES], h0[:, LANES:]))

    def carry_in(g, carry):
        hp = hp_s[pl.ds(g, N_CHUNK, stride=SSM_GROUPS), :]
        y_s[g] = y_s[g] + _dot(hp.astype(BF16), cpow_s[g])
        return carry
    lax.fori_loop(0, SSM_GROUPS, carry_in, 0)

    if emit_final:
        last = (chunks_per_seq - 1) * (1 - d)
        for sq in range(n_seq):
            row = pl.multiple_of((sq * chunks_per_seq + last) * SSM_GROUPS, SSM_GROUPS)
            fin_ref[0, 0, sq * SSM_GROUPS:(sq + 1) * SSM_GROUPS, :] = he_s[pl.ds(row, SSM_GROUPS), :]

    @pl.when(d == 1)
    def _finish():
        def tr(g, carry):
            yt_s[g] = y_s[g].T
            return carry
        lax.fori_loop(0, SSM_GROUPS, tr, 0)
        dskip = d_ref[0]
        wglu = wglu_ref[0]
        for t in range(SSM_CHUNK):
            blk = yt_s[:, t * SSM_GROUP_CH:(t + 1) * SSM_GROUP_CH, :].reshape(SSM_WIDTH, N_CHUNK)
            y = blk.T + chunk_rows(t) * dskip
            gl = jax.nn.gelu(y, approximate=True)
            gate = _dot(gl.astype(BF16), wglu)
            o = gl * _sigmoid(gate)
            out_ref[0, 0, pl.ds(t, N_CHUNK, stride=SSM_CHUNK), :] = o[:, :LANES]
            out_ref[0, 1, pl.ds(t, N_CHUNK, stride=SSM_CHUNK), :] = o[:, LANES:]


def _ssm(l, u, tables, h0, dskip, wglu, *, n_seq, emit_final):
    bg = u.shape[0]
    out_shape = [jax.ShapeDtypeStruct((bg, 2, TOKENS, LANES), F32)]
    out_specs = [pl.BlockSpec((1, 2, TOKENS, LANES), lambda b, d: (b, 0, 0, 0))]
    scratch = [pltpu.VMEM((SSM_GROUPS, N_CHUNK, 256), F32),
               pltpu.VMEM((SSM_GROUPS, 256, N_CHUNK), F32),
               pltpu.VMEM((SSM_GROUPS, N_CHUNK, 256), BF16),
               pltpu.VMEM((N_CHUNK * SSM_GROUPS, LANES), F32),
               pltpu.VMEM((N_CHUNK * SSM_GROUPS, LANES), F32),
               pltpu.VMEM((N_CHUNK * SSM_GROUPS, LANES), F32),
               pltpu.VMEM((SSM_GROUPS, 256, 256), BF16),
               pltpu.VMEM((SSM_GROUPS, 256, 256), BF16),
               pltpu.VMEM((SSM_GROUPS, LANES, 256), BF16)]
    if emit_final:
        out_shape.append(jax.ShapeDtypeStruct((bg, 2, n_seq * SSM_GROUPS, LANES), F32))
        out_specs.append(pl.BlockSpec((1, 1, n_seq * SSM_GROUPS, LANES), lambda b, d: (b, d, 0, 0)))
        scratch.append(pltpu.VMEM((N_CHUNK * SSM_GROUPS, LANES), F32))
    per_dir = lambda rows: pl.BlockSpec((1, 1, SSM_GROUPS, rows, 256), lambda b, d: (l, d, 0, 0, 0))
    *tabs, a1, a2 = tables
    return pl.pallas_call(
        functools.partial(_ssm_kernel, n_seq=n_seq, emit_final=emit_final),
        out_shape=out_shape,
        grid=(bg, 2),
        in_specs=[pl.BlockSpec((1, 2, TOKENS, LANES), lambda b, d: (b, 0, 0, 0))]
        + [per_dir(SSM_STATE)] * 6 + [per_dir(SSM_GROUP_CH)] * 4
        + [
            pl.BlockSpec((1, 1, SSM_GROUPS, LANES), lambda b, d: (l, d, 0, 0)),
            pl.BlockSpec((1, 1, SSM_GROUPS, LANES), lambda b, d: (l, d, 0, 0)),
            pl.BlockSpec((1, 1, SSM_GROUPS, 256), lambda b, d: (b, d, 0, 0)),
            pl.BlockSpec((1, 1, SSM_WIDTH), lambda b, d: (l, 0, 0)),
            pl.BlockSpec((1, SSM_WIDTH, SSM_WIDTH), lambda b, d: (l, 0, 0)),
        ],
        out_specs=out_specs,
        scratch_shapes=scratch,
        compiler_params=pltpu.CompilerParams(dimension_semantics=("parallel", "arbitrary"),
                                             vmem_limit_bytes=VMEM_LIMIT),
        name="ssm_ctx" if emit_final else "ssm_lat",
    )(u, *tabs, a1, a2, h0, dskip, wglu)


def _ssm_tables(lam_re, lam_im, log_dt, b_re, b_im, c_re, c_im):
    t_n = SSM_CHUNK
    lr = jnp.minimum(lam_re.astype(F32), -1e-4)
    li = lam_im.astype(F32)
    dt = jnp.exp(log_dt.astype(F32))[..., None]
    k = jnp.arange(t_n + 1, dtype=F32)[:, None, None, None, None]
    mag = jnp.exp(lr * dt * k)
    ang = li * dt * k
    p_r, p_i = mag * jnp.cos(ang), mag * jnp.sin(ang)
    n_r, n_i = p_r[1] - 1.0, p_i[1]
    den = lr * lr + li * li
    q_r, q_i = (n_r * lr + n_i * li) / den, (n_i * lr - n_r * li) / den
    bb_r = q_r[..., None] * b_re - q_i[..., None] * b_im
    bb_i = q_r[..., None] * b_im + q_i[..., None] * b_re

    def powers(x, lo, reversed_dir):
        x = x[lo:lo + t_n]
        per_d = [x[::-1, :, d] if d == reversed_dir else x[:, :, d] for d in range(2)]
        return jnp.stack(per_d, axis=2).transpose(1, 2, 3, 4, 0)
    spread = lambda x: jnp.repeat(x, SSM_GROUP_CH, axis=-1)
    pk_r, pk_i = spread(powers(p_r, 0, 1)), spread(powers(p_i, 0, 1))
    pc_r, pc_i = spread(powers(p_r, 1, 1)), spread(powers(p_i, 1, 1))
    pw_r = jnp.tile(jnp.swapaxes(powers(p_r, 0, 0), -1, -2), (1, 1, 1, 1, 4))
    pw_i = jnp.tile(jnp.swapaxes(powers(p_i, 0, 0), -1, -2), (1, 1, 1, 1, 4))
    cr_t = jnp.tile(jnp.swapaxes(c_re.astype(F32), -1, -2), (1, 1, 1, 1, t_n))
    ci_t = jnp.tile(jnp.swapaxes(c_im.astype(F32), -1, -2), (1, 1, 1, 1, t_n))
    br_t, bi_t = jnp.swapaxes(bb_r, -1, -2), jnp.swapaxes(bb_i, -1, -2)
    bb1 = jnp.concatenate([br_t, bi_t, bi_t, br_t], axis=-1)
    bb2 = jnp.concatenate([-bi_t, br_t, br_t, -bi_t], axis=-1)
    a_r, a_i = p_r[t_n], p_i[t_n]
    a1 = jnp.concatenate([a_r, a_r], axis=-1)
    a2 = jnp.concatenate([-a_i, a_i], axis=-1)
    return cr_t, ci_t, pk_r, pk_i, pc_r, pc_i, bb1, bb2, pw_r, pw_i, a1, a2


def _softmax_pv(s, v):
    m = jnp.max(s, axis=-1, keepdims=True)
    p = jnp.exp(s - m)
    l = jnp.sum(p, axis=-1, keepdims=True)
    return _dot(p.astype(BF16), v) / l


def _ctx_attn_kernel(z_ref, na_ref, gqa_ref):
    z = z_ref[0]
    outs = []
    for h in range(NA_HEADS):
        sl = slice(h * HEAD_DIM, (h + 1) * HEAD_DIM)
        q = z[:, Z_QNA:Z_QNA + NA_WIDTH][:, sl]
        k = z[:, Z_KNA:Z_KNA + NA_WIDTH][:, sl]
        v = z[:, Z_VNA:Z_VNA + NA_WIDTH][:, sl]
        outs.append(_softmax_pv(_dot_nt(q, k), v))
    na_ref[0] = jnp.concatenate(outs, axis=1).astype(BF16)
    outs = []
    for g in range(GQA_KV_HEADS):
        k = z[:, Z_KG + g * HEAD_DIM:Z_KG + (g + 1) * HEAD_DIM]
        v = z[:, Z_VG + g * HEAD_DIM:Z_VG + (g + 1) * HEAD_DIM]
        for r in range(GQA_REP):
            c0 = Z_QG + (g * GQA_REP + r) * HEAD_DIM
            outs.append(_softmax_pv(_dot_nt(z[:, c0:c0 + HEAD_DIM], k), v))
    gqa_ref[0] = jnp.concatenate(outs, axis=1).astype(BF16)


def _ctx_attention(zb):
    z3 = zb.reshape(BATCH, SEQ, Z_WIDTH)
    na, gqa = pl.pallas_call(
        _ctx_attn_kernel,
        out_shape=[jax.ShapeDtypeStruct((BATCH, SEQ, NA_WIDTH), BF16),
                   jax.ShapeDtypeStruct((BATCH, SEQ, GQA_WIDTH), BF16)],
        grid=(BATCH,),
        in_specs=[pl.BlockSpec((1, SEQ, Z_WIDTH), lambda b: (b, 0, 0))],
        out_specs=[pl.BlockSpec((1, SEQ, NA_WIDTH), lambda b: (b, 0, 0)),
                   pl.BlockSpec((1, SEQ, GQA_WIDTH), lambda b: (b, 0, 0))],
        compiler_params=pltpu.CompilerParams(dimension_semantics=("parallel",), vmem_limit_bytes=VMEM_LIMIT),
        name="attn_ctx",
    )(z3)
    return na.reshape(1, TOKENS, NA_WIDTH), gqa.reshape(1, TOKENS, GQA_WIDTH)


NA_KEYS = NA_WIN_R * GRID_W


NA_ROWS = 2


def _na_lat_kernel(q_ref, k_ref, v_ref, ck_ref, cv_ref, *refs):
    bias_refs, o_ref = refs[:NA_ROWS], refs[NA_ROWS]
    i = pl.program_id(1)
    ck = ck_ref[0, 0]
    cv = cv_ref[0, 0]
    qt = q_ref[0].astype(F32).T
    hd = lax.broadcasted_iota(jnp.int32, (NA_WIDTH, NA_WIDTH), 0) // HEAD_DIM
    hq = lax.broadcasted_iota(jnp.int32, (NA_WIDTH, NA_WIDTH), 1) // HEAD_DIM
    for j in range(NA_ROWS):
        r = NA_ROWS * i + j
        rs = jnp.clip(r - NA_WIN_R // 2, 0, GRID_H - NA_WIN_R)
        start = pl.multiple_of(rs * GRID_W, GRID_W)
        kw = k_ref[0, pl.ds(start, NA_KEYS), :]
        vw = v_ref[0, pl.ds(start, NA_KEYS), :]
        qj = qt[:, j * GRID_W:(j + 1) * GRID_W]
        bd = jnp.where(hd == hq, jnp.concatenate([qj] * NA_HEADS, axis=1), 0.0).astype(BF16)
        s_loc = _dot(kw, bd) + bias_refs[j][0, 0]
        s_ctx = _dot(ck, bd)
        m = jnp.maximum(jnp.max(s_loc, axis=0, keepdims=True), jnp.max(s_ctx, axis=0, keepdims=True))
        p_loc = jnp.exp(s_loc - m)
        p_ctx = jnp.exp(s_ctx - m)
        inv = 1.0 / (jnp.sum(p_loc, axis=0, keepdims=True) + jnp.sum(p_ctx, axis=0, keepdims=True))
        full = (_dot((p_loc * inv).T.astype(BF16), vw)
                + _dot((p_ctx * inv).T.astype(BF16), cv))
        out = jnp.concatenate([full[h * HEAD_DIM:(h + 1) * HEAD_DIM, h * HEAD_DIM:(h + 1) * HEAD_DIM]
                               for h in range(NA_HEADS)], axis=1)
        o_ref[0, j * GRID_W:(j + 1) * GRID_W, :] = out.astype(BF16)


def _na_row_class(r):
    return r - jnp.clip(r - NA_WIN_R // 2, 0, GRID_H - NA_WIN_R)


def _na_latent(l, zb, ck, cv, bias):
    tq = NA_ROWS * GRID_W
    bias_specs = [pl.BlockSpec((1, 1, NA_KEYS, NA_WIDTH),
                               functools.partial(lambda b, i, j: (l, _na_row_class(NA_ROWS * i + j), 0, 0), j=j))
                  for j in range(NA_ROWS)]
    return pl.pallas_call(
        _na_lat_kernel,
        out_shape=jax.ShapeDtypeStruct((DEC_BATCH, TOKENS, NA_WIDTH), BF16),
        grid=(DEC_BATCH, GRID_H // NA_ROWS),
        in_specs=[
            pl.BlockSpec((1, tq, NA_WIDTH), lambda b, i: (b, i, Z_QNA // NA_WIDTH)),
            pl.BlockSpec((1, TOKENS, NA_WIDTH), lambda b, i: (b, 0, Z_KNA // NA_WIDTH)),
            pl.BlockSpec((1, TOKENS, NA_WIDTH), lambda b, i: (b, 0, Z_VNA // NA_WIDTH)),
            pl.BlockSpec((1, 1, PAST_LEN, NA_WIDTH), lambda b, i: (b, l, 0, 0)),
            pl.BlockSpec((1, 1, PAST_LEN, NA_WIDTH), lambda b, i: (b, l, 0, 0)),
        ] + bias_specs,
        out_specs=pl.BlockSpec((1, tq, NA_WIDTH), lambda b, i: (b, i, 0)),
        compiler_params=pltpu.CompilerParams(dimension_semantics=("parallel", "arbitrary"),
                                             vmem_limit_bytes=VMEM_LIMIT),
        name="na_lat",
    )(zb, zb, zb, ck, cv, *([bias] * NA_ROWS))


def _na_bias_table(tab):
    q = np.arange(GRID_W)
    kc = np.arange(GRID_W)
    cs = np.clip(q - NA_WIN_C // 2, 0, GRID_W - NA_WIN_C)
    valid = (kc[None, :] >= cs[:, None]) & (kc[None, :] < cs[:, None] + NA_WIN_C)
    w = GRID_W - 1
    padded = jnp.pad(tab.astype(F32), ((0, 0), (0, 0), (0, 0), (w, w)))
    off = w + NA_WIN_C - 1
    cols = jnp.stack([padded[..., off - i:off - i + GRID_W] for i in range(GRID_W)], axis=-2)
    cols = jnp.where(valid, cols, -jnp.inf)
    b = jnp.stack([cols[:, :, NA_WIN_R - 1 - c:2 * NA_WIN_R - 1 - c] for c in range(NA_WIN_R)], axis=1)
    return b.transpose(0, 1, 3, 5, 2, 4).reshape(DEPTH, NA_WIN_R, NA_KEYS, NA_WIDTH)


GQA_TK = 256
GQA_KEYS = DEC_SEQ + PAST_LEN


def _gqa_kernel(q_ref, ktn_ref, ktc_ref, vn_ref, vc_ref, o_ref, s_s, m_s, *, tq):
    q = q_ref[0]
    n_new = DEC_SEQ // GQA_TK
    half = GQA_REP * tq // 2
    qh = [jnp.concatenate([q[:, r * HEAD_DIM:(r + 1) * HEAD_DIM] for r in (2 * i, 2 * i + 1)], axis=0)
          for i in range(2)]
    rows = [slice(i * half, (i + 1) * half) for i in range(2)]

    def fold(s):
        return jnp.maximum(s[:, :LANES], s[:, LANES:])

    for i in range(2):
        s = _dot(qh[i], ktc_ref[0])
        s_s[rows[i], DEC_SEQ:GQA_KEYS] = s
        m_s[rows[i], :] = fold(s)
    for c in range(0, n_new, 2):
        for i in range(2):
            s0 = _dot(qh[i], ktn_ref[0, :, c * GQA_TK:(c + 1) * GQA_TK])
            s1 = _dot(qh[i], ktn_ref[0, :, (c + 1) * GQA_TK:(c + 2) * GQA_TK])
            s_s[rows[i], c * GQA_TK:(c + 1) * GQA_TK] = s0
            s_s[rows[i], (c + 1) * GQA_TK:(c + 2) * GQA_TK] = s1
            m_s[rows[i], :] = jnp.maximum(m_s[rows[i], :], jnp.maximum(fold(s0), fold(s1)))

    m = jnp.max(m_s[...], axis=-1, keepdims=True)
    m_s[...] = jnp.broadcast_to(m, m_s.shape)

    def probs(i, c0):
        mb = m_s[rows[i], :]
        return jnp.exp2(s_s[rows[i], c0:c0 + GQA_TK] - jnp.concatenate([mb, mb], axis=1)).astype(BF16)

    acc = [_dot(probs(i, DEC_SEQ), vc_ref[0, 0]) for i in range(2)]
    for c in range(n_new):
        for i in range(2):
            acc[i] = acc[i] + _dot(probs(i, c * GQA_TK), vn_ref[0, 0, c * GQA_TK:(c + 1) * GQA_TK, :])
    outs = []
    for i in range(2):
        out = acc[i][:, :HEAD_DIM] / acc[i][:, HEAD_DIM:HEAD_DIM + 1]
        outs += [out[:tq], out[tq:]]
    o_ref[0] = jnp.concatenate(outs, axis=1).astype(BF16)


def _gqa_latent(zb, kt_new, kt_cache, v1_new, v1_cache):
    tq = 256
    rows = GQA_REP * tq
    return pl.pallas_call(
        functools.partial(_gqa_kernel, tq=tq),
        out_shape=jax.ShapeDtypeStruct((DEC_BATCH, TOKENS, GQA_WIDTH), BF16),
        grid=(DEC_BATCH, GQA_KV_HEADS, TOKENS // tq),
        in_specs=[
            pl.BlockSpec((1, tq, 256), lambda b, g, i: (b, i, Z_QG // 256 + g)),
            pl.BlockSpec((1, HEAD_DIM, DEC_SEQ), lambda b, g, i: (b, g, 0)),
            pl.BlockSpec((1, HEAD_DIM, PAST_LEN), lambda b, g, i: (b, g, 0)),
            pl.BlockSpec((1, 1, DEC_SEQ, LANES), lambda b, g, i: (b, g, 0, 0)),
            pl.BlockSpec((1, 1, PAST_LEN, LANES), lambda b, g, i: (b, g, 0, 0)),
        ],
        out_specs=pl.BlockSpec((1, tq, 256), lambda b, g, i: (b, i, g)),
        scratch_shapes=[pltpu.VMEM((rows, GQA_KEYS), F32), pltpu.VMEM((rows, LANES), F32)],
        compiler_params=pltpu.CompilerParams(dimension_semantics=("parallel", "parallel", "arbitrary"),
                                             vmem_limit_bytes=VMEM_LIMIT),
        name="gqa_lat",
    )(zb, kt_new, kt_cache, v1_new, v1_cache)


FF_CHUNK = 256


def _ffn_kernel(x_ref, ssm_ref, na_ref, gqa_ref, mod_ref, wout_ref, ln1g_ref, ln1b_ref, win_ref, wo_ref,
                ln2g_ref, ln2b_ref, o_ref, h2_s):
    ssm = jnp.concatenate([ssm_ref[0, 0], ssm_ref[0, 1]], axis=1)
    o = (_dot(ssm.astype(BF16), wout_ref[0, 0:SSM_WIDTH, :])
         + _dot(na_ref[0], wout_ref[0, SSM_WIDTH:SSM_WIDTH + NA_WIDTH, :])
         + _dot(gqa_ref[0], wout_ref[0, SSM_WIDTH + NA_WIDTH:, :]))
    y = DEEPNORM_ALPHA * x_ref[0] + mod_ref[0, 2:3, :] * o
    x1 = _ln(y) * ln1g_ref[0] + ln1b_ref[0]
    h2 = _ln(x1) * (1.0 + mod_ref[0, 4:5, :]) + mod_ref[0, 3:4, :]
    h2_s[...] = h2.astype(BF16)
    acc = None
    for j in range(D_FF // FF_CHUNK):
        h2b = h2_s[...]
        a = _dot(h2b, win_ref[0, :, j * FF_CHUNK:(j + 1) * FF_CHUNK])
        gt = _dot(h2b, win_ref[0, :, D_FF + j * FF_CHUNK:D_FF + (j + 1) * FF_CHUNK])
        f = (a * _sigmoid(a) * gt).astype(BF16)
        c = _dot(f, wo_ref[0, j * FF_CHUNK:(j + 1) * FF_CHUNK, :])
        acc = c if acc is None else acc + c
    y = DEEPNORM_ALPHA * x1 + mod_ref[0, 5:6, :] * acc
    o_ref[0] = _ln(y) * ln2g_ref[0] + ln2b_ref[0]


def _out_ffn(l, x, ssm, na, gqa, mod, w_out, ln1g, ln1b, w_ffn_in, w_ffn_out, ln2g, ln2b):
    bg = x.shape[0]
    tm = 512
    vec = pl.BlockSpec((1, 1, D_MODEL), lambda b, i: (l, 0, 0))
    resident = lambda shape: pl.BlockSpec((1,) + shape, lambda b, i: (l, 0, 0), pipeline_mode=pl.Buffered(1))
    return pl.pallas_call(
        _ffn_kernel,
        out_shape=jax.ShapeDtypeStruct((bg, TOKENS, D_MODEL), F32),
        grid=(bg, TOKENS // tm),
        in_specs=[
            pl.BlockSpec((1, tm, D_MODEL), lambda b, i: (b, i, 0)),
            pl.BlockSpec((1, 2, tm, LANES), lambda b, i: (b, 0, i, 0)),
            pl.BlockSpec((1, tm, NA_WIDTH), lambda b, i: (b, i, 0)),
            pl.BlockSpec((1, tm, GQA_WIDTH), lambda b, i: (b, i, 0)),
            pl.BlockSpec((1, 6, D_MODEL), lambda b, i: (b, 0, 0)),
            resident((D_MODEL, D_MODEL)),
            vec,
            vec,
            resident((D_MODEL, 2 * D_FF)),
            resident((D_FF, D_MODEL)),
            vec,
            vec,
        ],
        out_specs=pl.BlockSpec((1, tm, D_MODEL), lambda b, i: (b, i, 0)),
        scratch_shapes=[pltpu.VMEM((tm, D_MODEL), BF16)],
        compiler_params=pltpu.CompilerParams(dimension_semantics=("parallel", "parallel"),
                                             vmem_limit_bytes=VMEM_LIMIT),
        name="ffn",
    )(x, ssm, na, gqa, mod, w_out, ln1g, ln1b, w_ffn_in, w_ffn_out, ln2g, ln2b)


def _rope_tables():
    nf = HEAD_DIM // 4
    t = jnp.arange(DEC_SEQ)
    inv = ROPE_THETA ** (-jnp.arange(nf, dtype=F32) / nf)
    rows = (t // GRID_W).astype(F32)
    cols = (t % GRID_W).astype(F32)
    ang = jnp.stack([rows[:, None] * inv, cols[:, None] * inv], axis=1)
    cos = jnp.repeat(jnp.cos(ang)[:, :, None, :], 2, axis=2).reshape(DEC_SEQ, HEAD_DIM)
    sin = jnp.sin(ang)[:, :, None, :] * jnp.asarray([-1.0, 1.0], F32)[None, None, :, None]
    sin = sin.reshape(DEC_SEQ, HEAD_DIM)
    return jnp.tile(cos, (1, 2)), jnp.tile(sin, (1, 2))


def _head_mean_matrix():
    h = np.arange(GQA_WIDTH) // HEAD_DIM
    return jnp.asarray((h[:, None] == h[None, :]).astype(np.float32) / HEAD_DIM, BF16)


def kernel(x_prompt, x_sample, c, cache_na_k, cache_na_v, cache_gqa_k, cache_gqa_v, state_ssm_re, state_ssm_im,
           c_ctx, w_ada, b_ada, w_in, w_out, q_norm_g, k_norm_g, na_bias, ssm_lam_re, ssm_lam_im, ssm_log_dt,
           ssm_b_re, ssm_b_im, ssm_c_re, ssm_c_im, ssm_d, w_ssm_glu, ln1_g, ln1_b, ln2_g, ln2_b,
           w_ffn_in, w_ffn_out):
    cond8 = jnp.concatenate([c_ctx[None, :], c, jnp.zeros((8 - 1 - DEC_BATCH, D_MODEL), F32)], axis=0)
    mod = _modulation(cond8, w_ada, b_ada).reshape(DEPTH, 8, 6, D_MODEL)
    cos, sin = _rope_tables()
    hm = _head_mean_matrix()
    w_in_b = w_in.astype(BF16)
    w_out_b = w_out.astype(BF16)
    w_ffn_in_b = w_ffn_in.astype(BF16)
    w_ffn_out_b = w_ffn_out.astype(BF16)
    w_glu_b = w_ssm_glu.astype(BF16)
    qg = jnp.tile(q_norm_g, (1, GQA_HEADS))[:, None, :]
    kg = jnp.tile(k_norm_g, (1, GQA_KV_HEADS))[:, None, :]
    dskip = ssm_d[:, None, :]
    ln = (ln1_g[:, None, :], ln1_b[:, None, :], ln2_g[:, None, :], ln2_b[:, None, :])
    ops = _ssm_tables(ssm_lam_re, ssm_lam_im, ssm_log_dt, ssm_b_re, ssm_b_im, ssm_c_re, ssm_c_im)
    na_tab = _na_bias_table(na_bias)
    ck_na = cache_na_k.reshape(DEC_BATCH, DEPTH, PAST_LEN, NA_WIDTH).astype(BF16)
    cv_na = cache_na_v.reshape(DEC_BATCH, DEPTH, PAST_LEN, NA_WIDTH).astype(BF16)
    kt_cache = cache_gqa_k.transpose(1, 0, 3, 4, 2).reshape(DEPTH, DEC_BATCH, GQA_KV_WIDTH, PAST_LEN).astype(BF16)
    cv_g = cache_gqa_v.transpose(1, 0, 3, 2, 4)
    v1_cache = jnp.concatenate([cv_g, jnp.ones(cv_g.shape[:-1] + (1,), F32),
                                jnp.zeros(cv_g.shape[:-1] + (LANES - HEAD_DIM - 1,), F32)], axis=-1).astype(BF16)
    s_re = state_ssm_re.astype(F32).transpose(1, 0, 2, 3, 4)
    s_im = state_ssm_im.astype(F32).transpose(1, 0, 2, 3, 4)
    h0_lat = jnp.concatenate([s_re, s_im, s_im, s_re], axis=-1)
    h0_ctx = jnp.zeros((1, 2, SSM_GROUPS, 256), F32)

    y_ctx = x_prompt.reshape(1, TOKENS, D_MODEL)
    y_lat = x_sample
    kv_na_l, kv_g_l, fin_l = [], [], []
    for l in range(DEPTH):
        mod_c = mod[l, 0:1]
        u, zb, kv_na, kv_g = _in_projection(l, y_ctx, mod_c, w_in_b, qg, kg, hm, cos, sin, latent=False)
        ssm_o, fin = _ssm(l, u, ops, h0_ctx, dskip, w_glu_b, n_seq=BATCH, emit_final=True)
        na_o, gqa_o = _ctx_attention(zb)
        y_ctx = _out_ffn(l, y_ctx, ssm_o, na_o, gqa_o, mod_c, w_out_b, ln[0], ln[1], w_ffn_in_b, w_ffn_out_b,
                         ln[2], ln[3])
        kv_na_l.append(kv_na); kv_g_l.append(kv_g); fin_l.append(fin)

        mod_s = mod[l, 1:1 + DEC_BATCH]
        u, zb, kt_new, v1_new = _in_projection(l, y_lat, mod_s, w_in_b, qg, kg, hm, cos, sin, latent=True)
        ssm_o = _ssm(l, u, ops, h0_lat[l], dskip, w_glu_b, n_seq=1, emit_final=False)[0]
        na_o = _na_latent(l, zb, ck_na, cv_na, na_tab)
        gqa_o = _gqa_latent(zb, kt_new, kt_cache[l], v1_new, v1_cache[l])
        y_lat = _out_ffn(l, y_lat, ssm_o, na_o, gqa_o, mod_s, w_out_b, ln[0], ln[1], w_ffn_in_b, w_ffn_out_b,
                         ln[2], ln[3])

    kv_na = jnp.stack(kv_na_l, axis=0).reshape(DEPTH, BATCH, SEQ, 2, NA_HEADS, HEAD_DIM)
    kv_g = jnp.stack(kv_g_l, axis=0).reshape(DEPTH, BATCH, SEQ, 2, GQA_KV_HEADS, HEAD_DIM)
    kv_na = kv_na.transpose(3, 1, 0, 2, 4, 5)
    kv_g = kv_g.transpose(3, 1, 0, 2, 4, 5)
    fin = jnp.stack(fin_l, axis=0).reshape(DEPTH, 2, BATCH, SSM_GROUPS, 2, SSM_STATE)
    fin = fin.transpose(4, 2, 0, 1, 3, 5)
    return (y_ctx.reshape(BATCH, SEQ, D_MODEL), y_lat, kv_na[0], kv_na[1], kv_g[0], kv_g[1], fin[0], fin[1])
```

```python
import functools
import math

import numpy as np
import jax
import jax.numpy as jnp
from jax import lax
from jax.experimental import pallas as pl
from jax.experimental.pallas import tpu as pltpu

F32 = jnp.float32
BF16 = jnp.bfloat16

D_MODEL = 1024
BATCH = 16
SEQ = 256
DEPTH = 2
DEC_BATCH = 2
DEC_SEQ = 4096
PAST_LEN = 256
GRID_W = 64
GRID_H = DEC_SEQ // GRID_W
HEAD_DIM = 64
SSM_WIDTH = 256
SSM_GROUP_CH = 16
SSM_GROUPS = 16
SSM_STATE = 64
NA_HEADS = 4
NA_WIDTH = 256
NA_WIN_R = 8
NA_WIN_C = 16
GQA_HEADS = 8
GQA_KV_HEADS = 2
GQA_REP = 4
GQA_WIDTH = 512
GQA_KV_WIDTH = 128
IN_WIDTH = 1792
D_FF = 2816
ROPE_THETA = 10000.0
LN_EPS = 1e-6
RMS_EPS = 1e-6
DEEPNORM_ALPHA = (2 * DEPTH) ** 0.25
QK_SCALE = HEAD_DIM ** -0.5
LOG2E = math.log2(math.e)

TOKENS = 4096
SSM_CHUNK = 16
N_CHUNK = TOKENS // SSM_CHUNK
Z_WIDTH = IN_WIDTH - SSM_WIDTH
Z_QNA, Z_KNA, Z_VNA, Z_QG, Z_KG, Z_VG = 0, 256, 512, 768, 1280, 1408

VMEM_LIMIT = 56 * 1024 * 1024
LANES = 128


def _sigmoid(x):
    return 1.0 / (1.0 + jnp.exp(-x))


def _ln(x):
    mu = jnp.mean(x, axis=-1, keepdims=True)
    xc = x - mu
    var = jnp.mean(xc * xc, axis=-1, keepdims=True)
    return xc * lax.rsqrt(var + LN_EPS)


def _dot(a, b):
    return jnp.dot(a, b, preferred_element_type=F32)


def _dot_nt(a, b):
    return lax.dot_general(a, b, (((1,), (1,)), ((), ())), preferred_element_type=F32)


def _mod_kernel(c_ref, w_ref, b_ref, o_ref):
    c = c_ref[...]
    s = c * _sigmoid(c)
    o_ref[0] = jnp.dot(s, w_ref[0], preferred_element_type=F32, precision=lax.Precision.HIGHEST) + b_ref[0]


def _modulation(cond8, w_ada, b_ada):
    nb = 1536
    return pl.pallas_call(
        _mod_kernel,
        out_shape=jax.ShapeDtypeStruct((DEPTH, 8, 6 * D_MODEL), F32),
        grid=(DEPTH, 6 * D_MODEL // nb),
        in_specs=[
            pl.BlockSpec((8, D_MODEL), lambda l, j: (0, 0)),
            pl.BlockSpec((1, D_MODEL, nb), lambda l, j: (l, 0, j)),
            pl.BlockSpec((1, 1, nb), lambda l, j: (l, 0, j)),
        ],
        out_specs=pl.BlockSpec((1, 8, nb), lambda l, j: (l, 0, j)),
        compiler_params=pltpu.CompilerParams(dimension_semantics=("parallel", "parallel"),
                                             vmem_limit_bytes=VMEM_LIMIT),
        name="mod",
    )(cond8, w_ada, b_ada.reshape(DEPTH, 1, 6 * D_MODEL))


def _swap16(x):
    w = x.shape[-1]
    lane = lax.broadcasted_iota(jnp.int32, x.shape, x.ndim - 1)
    return jnp.where((lane & 16) != 0, pltpu.roll(x, 16, x.ndim - 1), pltpu.roll(x, w - 16, x.ndim - 1))


def _win_kernel(x_ref, mod_ref, w_ref, qg_ref, kg_ref, hm_ref, cos_ref, sin_ref, *out_refs, latent):
    u_ref, z_ref = out_refs[0], out_refs[1]
    x = x_ref[0]
    h = _ln(x) * (1.0 + mod_ref[0, 1:2, :]) + mod_ref[0, 0:1, :]
    z = _dot(h.astype(BF16), w_ref[0])
    u_ref[0, 0] = z[:, 0:LANES]
    u_ref[0, 1] = z[:, LANES:SSM_WIDTH]
    q_na = z[:, 256:512] * QK_SCALE
    kv_na = z[:, 512:1024]
    q_g = z[:, 1024:1536]
    k_g = z[:, 1536:1664]
    v_g = z[:, 1664:1792]
    q_ms = _dot((q_g * q_g).astype(BF16), hm_ref[...])
    k_ms = _dot((k_g * k_g).astype(BF16), hm_ref[0:GQA_KV_WIDTH, 0:GQA_KV_WIDTH])
    q_n = q_g * lax.rsqrt(q_ms + RMS_EPS) * qg_ref[0]
    k_n = k_g * lax.rsqrt(k_ms + RMS_EPS) * kg_ref[0]
    if latent:
        kt_ref, v1_ref = out_refs[2], out_refs[3]
        cos = cos_ref[...]
        sin = sin_ref[...]
        cos4 = jnp.concatenate([cos] * 4, axis=1)
        sin4 = jnp.concatenate([sin] * 4, axis=1)
        q_n = (q_n * cos4 + _swap16(q_n) * sin4) * (QK_SCALE * LOG2E)
        k_n = k_n * cos + _swap16(k_n) * sin
        kt_ref[0] = k_n.T.astype(BF16)
        lane = lax.broadcasted_iota(jnp.int32, v_g.shape, 1)
        ones_col = jnp.where(lane == HEAD_DIM, 1.0, 0.0)
        v1_ref[0, 0] = jnp.where(lane < HEAD_DIM, v_g, ones_col).astype(BF16)
        v1_ref[0, 1] = jnp.where(lane < HEAD_DIM, pltpu.roll(v_g, HEAD_DIM, 1), ones_col).astype(BF16)
    else:
        out_refs[2][0] = kv_na
        out_refs[3][0] = jnp.concatenate([k_n, v_g], axis=1)
        q_n = q_n * QK_SCALE
    zb = jnp.concatenate([q_na, kv_na, q_n, k_n, v_g], axis=1)
    z_ref[0] = zb.astype(BF16)


def _in_projection(l, x, mod, w_in, qg, kg, hm, cos, sin, *, latent):
    bg = x.shape[0]
    tm = 512
    out_shape = [jax.ShapeDtypeStruct((bg, 2, TOKENS, LANES), F32),
                 jax.ShapeDtypeStruct((bg, TOKENS, Z_WIDTH), BF16)]
    out_specs = [pl.BlockSpec((1, 2, tm, LANES), lambda b, i: (b, 0, i, 0)),
                 pl.BlockSpec((1, tm, Z_WIDTH), lambda b, i: (b, i, 0))]
    if latent:
        out_shape += [jax.ShapeDtypeStruct((bg, GQA_KV_WIDTH, TOKENS), BF16),
                      jax.ShapeDtypeStruct((bg, GQA_KV_HEADS, TOKENS, LANES), BF16)]
        out_specs += [pl.BlockSpec((1, GQA_KV_WIDTH, tm), lambda b, i: (b, 0, i)),
                      pl.BlockSpec((1, GQA_KV_HEADS, tm, LANES), lambda b, i: (b, 0, i, 0))]
    else:
        out_shape += [jax.ShapeDtypeStruct((bg, TOKENS, 2 * NA_WIDTH), F32),
                      jax.ShapeDtypeStruct((bg, TOKENS, 2 * GQA_KV_WIDTH), F32)]
        out_specs += [pl.BlockSpec((1, tm, 2 * NA_WIDTH), lambda b, i: (b, i, 0)),
                      pl.BlockSpec((1, tm, 2 * GQA_KV_WIDTH), lambda b, i: (b, i, 0))]
    return pl.pallas_call(
        functools.partial(_win_kernel, latent=latent),
        out_shape=out_shape,
        grid=(bg, TOKENS // tm),
        in_specs=[
            pl.BlockSpec((1, tm, D_MODEL), lambda b, i: (b, i, 0)),
            pl.BlockSpec((1, 6, D_MODEL), lambda b, i: (b, 0, 0)),
            pl.BlockSpec((1, D_MODEL, IN_WIDTH), lambda b, i: (l, 0, 0)),
            pl.BlockSpec((1, 1, GQA_WIDTH), lambda b, i: (l, 0, 0)),
            pl.BlockSpec((1, 1, GQA_KV_WIDTH), lambda b, i: (l, 0, 0)),
            pl.BlockSpec((GQA_WIDTH, GQA_WIDTH), lambda b, i: (0, 0)),
            pl.BlockSpec((tm, LANES), lambda b, i: (i, 0)),
            pl.BlockSpec((tm, LANES), lambda b, i: (i, 0)),
        ],
        out_specs=out_specs,
        compiler_params=pltpu.CompilerParams(dimension_semantics=("parallel", "parallel"),
                                             vmem_limit_bytes=VMEM_LIMIT),
        name="win_lat" if latent else "win_ctx",
    )(x, mod, w_in, qg, kg, hm, cos, sin)


STATE_LANES = SSM_GROUPS * LANES


def _ssm_ops_kernel(crt_ref, cit_ref, pkr_ref, pki_ref, pcr_ref, pci_ref, bb1_ref, bb2_ref, pwr_ref, pwi_ref,
                    m_ref, bend_ref, cpow_ref):
    d = pl.program_id(1)

    def build_ops(g, carry):
        cr = crt_ref[0, 0, g]
        ci = cit_ref[0, 0, g]

        def re_proj(pr, pi):
            return jnp.concatenate([cr * pr - ci * pi, -(cr * pi + ci * pr)], axis=0)
        cpow_ref[0, 0, g] = re_proj(pcr_ref[0, 0, g], pci_ref[0, 0, g]).astype(BF16)
        bb1 = bb1_ref[0, 0, g]
        bb2 = bb2_ref[0, 0, g]
        kcat = jnp.dot(bb1[:, :LANES], re_proj(pkr_ref[0, 0, g], pki_ref[0, 0, g]),
                       preferred_element_type=F32, precision=lax.Precision.HIGHEST)
        lane = lax.broadcasted_iota(jnp.int32, kcat.shape, 1)
        for t in range(SSM_CHUNK):
            rows = slice(t * SSM_GROUP_CH, (t + 1) * SSM_GROUP_CH)
            lo, hi = t * SSM_GROUP_CH, (t + 1) * SSM_GROUP_CH
            fwd = kcat if t == 0 else jnp.where(lane >= lo, pltpu.roll(kcat, lo, 1), 0.0)
            bwd = kcat if hi == 256 else jnp.where(lane < hi, pltpu.roll(kcat, hi, 1), 0.0)
            m_ref[0, 0, g, rows, :] = jnp.where(d == 0, fwd, bwd).astype(BF16)
            bend_ref[0, 0, g, rows, :] = (pwr_ref[0, 0, g, t:t + 1, :] * bb1
                                          + pwi_ref[0, 0, g, t:t + 1, :] * bb2).astype(BF16)
        return carry
    lax.fori_loop(0, SSM_GROUPS, build_ops, 0)


def _ssm_operators(tables):
    per_dir = lambda rows: pl.BlockSpec((1, 1, SSM_GROUPS, rows, 256), lambda l, d: (l, d, 0, 0, 0))
    shape = lambda rows: jax.ShapeDtypeStruct((DEPTH, 2, SSM_GROUPS, rows, 256), BF16)
    return pl.pallas_call(
        _ssm_ops_kernel,
        out_shape=[shape(256), shape(256), shape(LANES)],
        grid=(DEPTH, 2),
        in_specs=[per_dir(SSM_STATE)] * 6 + [per_dir(SSM_GROUP_CH)] * 4,
        out_specs=[per_dir(256), per_dir(256), per_dir(LANES)],
        compiler_params=pltpu.CompilerParams(dimension_semantics=("parallel", "parallel"),
                                             vmem_limit_bytes=VMEM_LIMIT),
        name="ssm_ops",
    )(*tables)


def _ssm_kernel(u_ref, m_ref, bend_ref, cpow_ref, a1_ref, a2_ref, h0s_ref, h0p_ref, d_ref, wglu_ref, *refs,
                n_seq, emit_final):
    if emit_final:
        out_ref, fin_ref, y_s, yt_s, u_s, inja_s, injb_s, hp_s = refs
    else:
        out_ref, y_s, yt_s, u_s, inja_s, injb_s, hp_s = refs
        fin_ref = None
    chunks_per_seq = N_CHUNK // n_seq

    def chunk_rows(t):
        sl = pl.ds(t, N_CHUNK, stride=SSM_CHUNK)
        return jnp.concatenate([u_ref[0, 0, sl, :], u_ref[0, 1, sl, :]], axis=1)

    for t in range(SSM_CHUNK):
        a_t = chunk_rows(t)
        yt_s[:, t * SSM_GROUP_CH:(t + 1) * SSM_GROUP_CH, :] = a_t.T.reshape(SSM_GROUPS, SSM_GROUP_CH, N_CHUNK)

    def tr_in(g, carry):
        u_s[g] = yt_s[g].T.astype(BF16)
        return carry
    lax.fori_loop(0, SSM_GROUPS, tr_in, 0)

    for d in range(2):
        def proj(g, carry, d=d):
            ug = u_s[g]
            y = _dot(ug, m_ref[0, d, g])
            y_s[g] = y if d == 0 else y_s[g] + y
            inj = _dot(ug, bend_ref[0, d, g])
            lanes = pl.ds(pl.multiple_of(g * LANES, LANES), LANES)
            inja_s[d, :, lanes] = inj[:, :LANES]
            injb_s[d, :, lanes] = inj[:, LANES:]
            return carry
        lax.fori_loop(0, SSM_GROUPS, proj, 0)

    a1 = a1_ref[0]
    a2 = a2_ref[0]

    def step(j, carry):
        s, sp = carry
        nf = j
        nb = N_CHUNK - 1 - j
        if n_seq > 1:
            keep = jnp.where(j % chunks_per_seq != 0, 1.0, 0.0).astype(F32)
            s = s * keep
            sp = sp * keep
        hp_s[0, pl.ds(nf, 1), :] = s[0:1]
        hp_s[1, pl.ds(nb, 1), :] = s[1:2]
        ia = jnp.concatenate([inja_s[0, pl.ds(nf, 1), :], inja_s[1, pl.ds(nb, 1), :]], axis=0)
        ib = jnp.concatenate([injb_s[0, pl.ds(nf, 1), :], injb_s[1, pl.ds(nb, 1), :]], axis=0)
        s_new = a1 * s + a2 * sp + ia
        sp_new = a1 * sp - a2 * s + ib
        if emit_final:
            @pl.when(j % chunks_per_seq == chunks_per_seq - 1)
            def _():
                sq = j // chunks_per_seq
                fin_ref[0, 0, pl.ds(sq, 1), :] = s_new[0:1]
                fin_ref[0, 1, pl.ds(n_seq - 1 - sq, 1), :] = s_new[1:2]
        return s_new, sp_new
    lax.fori_loop(0, N_CHUNK, step, (h0s_ref[0], h0p_ref[0]))

    for d in range(2):
        def carry_in(g, carry, d=d):
            hp = hp_s[d, :, pl.ds(pl.multiple_of(g * LANES, LANES), LANES)]
            y_s[g] = y_s[g] + _dot(hp.astype(BF16), cpow_ref[0, d, g])
            return carry
        lax.fori_loop(0, SSM_GROUPS, carry_in, 0)

    def tr_out(g, carry):
        yt_s[g] = y_s[g].T
        return carry
    lax.fori_loop(0, SSM_GROUPS, tr_out, 0)
    dskip = d_ref[0]
    wglu = wglu_ref[0]
    for t in range(SSM_CHUNK):
        blk = yt_s[:, t * SSM_GROUP_CH:(t + 1) * SSM_GROUP_CH, :].reshape(SSM_WIDTH, N_CHUNK)
        y = blk.T + chunk_rows(t) * dskip
        gl = jax.nn.gelu(y, approximate=True)
        gate = _dot(gl.astype(BF16), wglu)
        o = gl * _sigmoid(gate)
        out_ref[0, 0, pl.ds(t, N_CHUNK, stride=SSM_CHUNK), :] = o[:, :LANES]
        out_ref[0, 1, pl.ds(t, N_CHUNK, stride=SSM_CHUNK), :] = o[:, LANES:]


def _ssm(l, u, ops, a1, a2, h0s, h0p, dskip, wglu, *, n_seq, emit_final):
    bg = u.shape[0]
    m, bend, cpow = ops
    out_shape = [jax.ShapeDtypeStruct((bg, 2, TOKENS, LANES), F32)]
    out_specs = [pl.BlockSpec((1, 2, TOKENS, LANES), lambda b: (b, 0, 0, 0))]
    if emit_final:
        out_shape.append(jax.ShapeDtypeStruct((bg, 2, n_seq, STATE_LANES), F32))
        out_specs.append(pl.BlockSpec((1, 2, n_seq, STATE_LANES), lambda b: (b, 0, 0, 0)))
    resident = lambda rows: pl.BlockSpec((1, 2, SSM_GROUPS, rows, 256), lambda b: (l, 0, 0, 0, 0),
                                         pipeline_mode=pl.Buffered(1))
    return pl.pallas_call(
        functools.partial(_ssm_kernel, n_seq=n_seq, emit_final=emit_final),
        out_shape=out_shape,
        grid=(bg,),
        in_specs=[
            pl.BlockSpec((1, 2, TOKENS, LANES), lambda b: (b, 0, 0, 0)),
            resident(256), resident(256), resident(LANES),
            pl.BlockSpec((1, 2, STATE_LANES), lambda b: (l, 0, 0)),
            pl.BlockSpec((1, 2, STATE_LANES), lambda b: (l, 0, 0)),
            pl.BlockSpec((1, 2, STATE_LANES), lambda b: (b, 0, 0)),
            pl.BlockSpec((1, 2, STATE_LANES), lambda b: (b, 0, 0)),
            pl.BlockSpec((1, 1, SSM_WIDTH), lambda b: (l, 0, 0)),
            pl.BlockSpec((1, SSM_WIDTH, SSM_WIDTH), lambda b: (l, 0, 0)),
        ],
        out_specs=out_specs,
        scratch_shapes=[pltpu.VMEM((SSM_GROUPS, N_CHUNK, 256), F32),
                        pltpu.VMEM((SSM_GROUPS, 256, N_CHUNK), F32),
                        pltpu.VMEM((SSM_GROUPS, N_CHUNK, 256), BF16),
                        pltpu.VMEM((2, N_CHUNK, STATE_LANES), F32),
                        pltpu.VMEM((2, N_CHUNK, STATE_LANES), F32),
                        pltpu.VMEM((2, N_CHUNK, STATE_LANES), F32)],
        compiler_params=pltpu.CompilerParams(dimension_semantics=("parallel",), vmem_limit_bytes=VMEM_LIMIT),
        name="ssm_ctx" if emit_final else "ssm_lat",
    )(u, m, bend, cpow, a1, a2, h0s, h0p, dskip, wglu)


def _ssm_tables(lam_re, lam_im, log_dt, b_re, b_im, c_re, c_im):
    t_n = SSM_CHUNK
    lr = jnp.minimum(lam_re.astype(F32), -1e-4)
    li = lam_im.astype(F32)
    dt = jnp.exp(log_dt.astype(F32))[..., None]
    k = jnp.arange(t_n + 1, dtype=F32)[:, None, None, None, None]
    mag = jnp.exp(lr * dt * k)
    ang = li * dt * k
    p_r, p_i = mag * jnp.cos(ang), mag * jnp.sin(ang)
    n_r, n_i = p_r[1] - 1.0, p_i[1]
    den = lr * lr + li * li
    q_r, q_i = (n_r * lr + n_i * li) / den, (n_i * lr - n_r * li) / den
    bb_r = q_r[..., None] * b_re - q_i[..., None] * b_im
    bb_i = q_r[..., None] * b_im + q_i[..., None] * b_re

    def powers(x, lo, reversed_dir):
        x = x[lo:lo + t_n]
        per_d = [x[::-1, :, d] if d == reversed_dir else x[:, :, d] for d in range(2)]
        return jnp.stack(per_d, axis=2).transpose(1, 2, 3, 4, 0)
    spread = lambda x: jnp.repeat(x, SSM_GROUP_CH, axis=-1)
    pk_r, pk_i = spread(powers(p_r, 0, 1)), spread(powers(p_i, 0, 1))
    pc_r, pc_i = spread(powers(p_r, 1, 1)), spread(powers(p_i, 1, 1))
    pw_r = jnp.tile(jnp.swapaxes(powers(p_r, 0, 0), -1, -2), (1, 1, 1, 1, 4))
    pw_i = jnp.tile(jnp.swapaxes(powers(p_i, 0, 0), -1, -2), (1, 1, 1, 1, 4))
    cr_t = jnp.tile(jnp.swapaxes(c_re.astype(F32), -1, -2), (1, 1, 1, 1, t_n))
    ci_t = jnp.tile(jnp.swapaxes(c_im.astype(F32), -1, -2), (1, 1, 1, 1, t_n))
    br_t, bi_t = jnp.swapaxes(bb_r, -1, -2), jnp.swapaxes(bb_i, -1, -2)
    bb1 = jnp.concatenate([br_t, bi_t, bi_t, br_t], axis=-1)
    bb2 = jnp.concatenate([-bi_t, br_t, br_t, -bi_t], axis=-1)
    a_r, a_i = p_r[t_n], p_i[t_n]
    a1 = jnp.concatenate([a_r, a_r], axis=-1)
    a2 = jnp.concatenate([-a_i, a_i], axis=-1)
    return cr_t, ci_t, pk_r, pk_i, pc_r, pc_i, bb1, bb2, pw_r, pw_i, a1, a2


def _softmax_pv(s, v):
    m = jnp.max(s, axis=-1, keepdims=True)
    p = jnp.exp(s - m)
    l = jnp.sum(p, axis=-1, keepdims=True)
    return _dot(p.astype(BF16), v) / l


def _ctx_attn_kernel(z_ref, na_ref, gqa_ref):
    z = z_ref[0]
    outs = []
    for h in range(NA_HEADS):
        sl = slice(h * HEAD_DIM, (h + 1) * HEAD_DIM)
        q = z[:, Z_QNA:Z_QNA + NA_WIDTH][:, sl]
        k = z[:, Z_KNA:Z_KNA + NA_WIDTH][:, sl]
        v = z[:, Z_VNA:Z_VNA + NA_WIDTH][:, sl]
        outs.append(_softmax_pv(_dot_nt(q, k), v))
    na_ref[0] = jnp.concatenate(outs, axis=1).astype(BF16)
    outs = []
    for g in range(GQA_KV_HEADS):
        k = z[:, Z_KG + g * HEAD_DIM:Z_KG + (g + 1) * HEAD_DIM]
        v = z[:, Z_VG + g * HEAD_DIM:Z_VG + (g + 1) * HEAD_DIM]
        for r in range(GQA_REP):
            c0 = Z_QG + (g * GQA_REP + r) * HEAD_DIM
            outs.append(_softmax_pv(_dot_nt(z[:, c0:c0 + HEAD_DIM], k), v))
    gqa_ref[0] = jnp.concatenate(outs, axis=1).astype(BF16)


def _ctx_attention(zb):
    z3 = zb.reshape(BATCH, SEQ, Z_WIDTH)
    na, gqa = pl.pallas_call(
        _ctx_attn_kernel,
        out_shape=[jax.ShapeDtypeStruct((BATCH, SEQ, NA_WIDTH), BF16),
                   jax.ShapeDtypeStruct((BATCH, SEQ, GQA_WIDTH), BF16)],
        grid=(BATCH,),
        in_specs=[pl.BlockSpec((1, SEQ, Z_WIDTH), lambda b: (b, 0, 0))],
        out_specs=[pl.BlockSpec((1, SEQ, NA_WIDTH), lambda b: (b, 0, 0)),
                   pl.BlockSpec((1, SEQ, GQA_WIDTH), lambda b: (b, 0, 0))],
        compiler_params=pltpu.CompilerParams(dimension_semantics=("parallel",), vmem_limit_bytes=VMEM_LIMIT),
        name="attn_ctx",
    )(z3)
    return na.reshape(1, TOKENS, NA_WIDTH), gqa.reshape(1, TOKENS, GQA_WIDTH)


NA_KEYS = NA_WIN_R * GRID_W


NA_ROWS = 4


def _na_lat_kernel(q_ref, k_ref, v_ref, ck_ref, cv_ref, *refs):
    bias_refs, o_ref = refs[:NA_ROWS], refs[NA_ROWS]
    i = pl.program_id(1)
    ck = ck_ref[0, 0]
    cv = cv_ref[0, 0]
    qt = q_ref[0].astype(F32).T
    hd = lax.broadcasted_iota(jnp.int32, (NA_WIDTH, NA_WIDTH), 0) // HEAD_DIM
    hq = lax.broadcasted_iota(jnp.int32, (NA_WIDTH, NA_WIDTH), 1) // HEAD_DIM
    for j in range(NA_ROWS):
        r = NA_ROWS * i + j
        rs = jnp.clip(r - NA_WIN_R // 2, 0, GRID_H - NA_WIN_R)
        start = pl.multiple_of(rs * GRID_W, GRID_W)
        kw = k_ref[0, pl.ds(start, NA_KEYS), :]
        vw = v_ref[0, pl.ds(start, NA_KEYS), :]
        qj = qt[:, j * GRID_W:(j + 1) * GRID_W]
        bd = jnp.where(hd == hq, jnp.concatenate([qj] * NA_HEADS, axis=1), 0.0).astype(BF16)
        s_loc = _dot(kw, bd) + bias_refs[j][0, 0]
        s_ctx = _dot(ck, bd)
        m = jnp.maximum(jnp.max(s_loc, axis=0, keepdims=True), jnp.max(s_ctx, axis=0, keepdims=True))
        p_loc = jnp.exp(s_loc - m)
        p_ctx = jnp.exp(s_ctx - m)
        inv = 1.0 / (jnp.sum(p_loc, axis=0, keepdims=True) + jnp.sum(p_ctx, axis=0, keepdims=True))
        full = (_dot((p_loc * inv).T.astype(BF16), vw)
                + _dot((p_ctx * inv).T.astype(BF16), cv))
        out = jnp.concatenate([full[h * HEAD_DIM:(h + 1) * HEAD_DIM, h * HEAD_DIM:(h + 1) * HEAD_DIM]
                               for h in range(NA_HEADS)], axis=1)
        o_ref[0, j * GRID_W:(j + 1) * GRID_W, :] = out.astype(BF16)


def _na_row_class(r):
    return r - jnp.clip(r - NA_WIN_R // 2, 0, GRID_H - NA_WIN_R)


def _na_latent(l, zb, ck, cv, bias):
    tq = NA_ROWS * GRID_W
    bias_specs = [pl.BlockSpec((1, 1, NA_KEYS, NA_WIDTH),
                               functools.partial(lambda b, i, j: (l, _na_row_class(NA_ROWS * i + j), 0, 0), j=j))
                  for j in range(NA_ROWS)]
    return pl.pallas_call(
        _na_lat_kernel,
        out_shape=jax.ShapeDtypeStruct((DEC_BATCH, TOKENS, NA_WIDTH), BF16),
        grid=(DEC_BATCH, GRID_H // NA_ROWS),
        in_specs=[
            pl.BlockSpec((1, tq, NA_WIDTH), lambda b, i: (b, i, Z_QNA // NA_WIDTH)),
            pl.BlockSpec((1, TOKENS, NA_WIDTH), lambda b, i: (b, 0, Z_KNA // NA_WIDTH)),
            pl.BlockSpec((1, TOKENS, NA_WIDTH), lambda b, i: (b, 0, Z_VNA // NA_WIDTH)),
            pl.BlockSpec((1, 1, PAST_LEN, NA_WIDTH), lambda b, i: (b, l, 0, 0)),
            pl.BlockSpec((1, 1, PAST_LEN, NA_WIDTH), lambda b, i: (b, l, 0, 0)),
        ] + bias_specs,
        out_specs=pl.BlockSpec((1, tq, NA_WIDTH), lambda b, i: (b, i, 0)),
        compiler_params=pltpu.CompilerParams(dimension_semantics=("parallel", "arbitrary"),
                                             vmem_limit_bytes=VMEM_LIMIT),
        name="na_lat",
    )(zb, zb, zb, ck, cv, *([bias] * NA_ROWS))


def _na_bias_table(tab):
    q = np.arange(GRID_W)
    kc = np.arange(GRID_W)
    cs = np.clip(q - NA_WIN_C // 2, 0, GRID_W - NA_WIN_C)
    valid = (kc[None, :] >= cs[:, None]) & (kc[None, :] < cs[:, None] + NA_WIN_C)
    w = GRID_W - 1
    padded = jnp.pad(tab.astype(F32), ((0, 0), (0, 0), (0, 0), (w, w)))
    off = w + NA_WIN_C - 1
    cols = jnp.stack([padded[..., off - i:off - i + GRID_W] for i in range(GRID_W)], axis=-2)
    cols = jnp.where(valid, cols, -jnp.inf)
    b = jnp.stack([cols[:, :, NA_WIN_R - 1 - c:2 * NA_WIN_R - 1 - c] for c in range(NA_WIN_R)], axis=1)
    return b.transpose(0, 1, 3, 5, 2, 4).reshape(DEPTH, NA_WIN_R, NA_KEYS, NA_WIDTH)


GQA_TK = 256
GQA_KEYS = DEC_SEQ + PAST_LEN


def _gqa_kernel(q_ref, ktn_ref, ktc_ref, vn_ref, vc_ref, o_ref, s_s, m_s, *, tq):
    q = q_ref[0]
    n_new = DEC_SEQ // GQA_TK
    half = GQA_REP * tq // 2
    qh = [jnp.concatenate([q[:, r * HEAD_DIM:(r + 1) * HEAD_DIM] for r in (2 * i, 2 * i + 1)], axis=0)
          for i in range(2)]
    rows = [slice(i * half, (i + 1) * half) for i in range(2)]

    def fold(s):
        return jnp.maximum(s[:, :LANES], s[:, LANES:])

    for i in range(2):
        s = _dot(qh[i], ktc_ref[0])
        s_s[rows[i], DEC_SEQ:GQA_KEYS] = s
        m_s[rows[i], :] = fold(s)
    for c in range(0, n_new, 2):
        for i in range(2):
            s0 = _dot(qh[i], ktn_ref[0, :, c * GQA_TK:(c + 1) * GQA_TK])
            s1 = _dot(qh[i], ktn_ref[0, :, (c + 1) * GQA_TK:(c + 2) * GQA_TK])
            s_s[rows[i], c * GQA_TK:(c + 1) * GQA_TK] = s0
            s_s[rows[i], (c + 1) * GQA_TK:(c + 2) * GQA_TK] = s1
            m_s[rows[i], :] = jnp.maximum(m_s[rows[i], :], jnp.maximum(fold(s0), fold(s1)))

    m = jnp.max(m_s[...], axis=-1, keepdims=True)
    m_s[...] = jnp.broadcast_to(m, m_s.shape)

    def probs(i, c0):
        mb = m_s[rows[i], :]
        return jnp.exp2(s_s[rows[i], c0:c0 + GQA_TK] - jnp.concatenate([mb, mb], axis=1)).astype(BF16)

    acc = [_dot(probs(i, DEC_SEQ), vc_ref[0, 0]) for i in range(2)]
    for c in range(n_new):
        for i in range(2):
            acc[i] = acc[i] + _dot(probs(i, c * GQA_TK), vn_ref[0, 0, c * GQA_TK:(c + 1) * GQA_TK, :])
    outs = []
    for i in range(2):
        out = acc[i][:, :HEAD_DIM] / acc[i][:, HEAD_DIM:HEAD_DIM + 1]
        outs += [out[:tq], out[tq:]]
    o_ref[0] = jnp.concatenate(outs, axis=1).astype(BF16)


def _gqa_latent(zb, kt_new, kt_cache, v1_new, v1_cache):
    tq = 256
    rows = GQA_REP * tq
    return pl.pallas_call(
        functools.partial(_gqa_kernel, tq=tq),
        out_shape=jax.ShapeDtypeStruct((DEC_BATCH, TOKENS, GQA_WIDTH), BF16),
        grid=(DEC_BATCH, GQA_KV_HEADS, TOKENS // tq),
        in_specs=[
            pl.BlockSpec((1, tq, 256), lambda b, g, i: (b, i, Z_QG // 256 + g)),
            pl.BlockSpec((1, HEAD_DIM, DEC_SEQ), lambda b, g, i: (b, g, 0)),
            pl.BlockSpec((1, HEAD_DIM, PAST_LEN), lambda b, g, i: (b, g, 0)),
            pl.BlockSpec((1, 1, DEC_SEQ, LANES), lambda b, g, i: (b, g, 0, 0)),
            pl.BlockSpec((1, 1, PAST_LEN, LANES), lambda b, g, i: (b, g, 0, 0)),
        ],
        out_specs=pl.BlockSpec((1, tq, 256), lambda b, g, i: (b, i, g)),
        scratch_shapes=[pltpu.VMEM((rows, GQA_KEYS), F32), pltpu.VMEM((rows, LANES), F32)],
        compiler_params=pltpu.CompilerParams(dimension_semantics=("parallel", "parallel", "arbitrary"),
                                             vmem_limit_bytes=VMEM_LIMIT),
        name="gqa_lat",
    )(zb, kt_new, kt_cache, v1_new, v1_cache)


FF_CHUNK = 256


def _ffn_kernel(x_ref, ssm_ref, na_ref, gqa_ref, mod_ref, wout_ref, ln1g_ref, ln1b_ref, win_ref, wo_ref,
                ln2g_ref, ln2b_ref, o_ref, h2_s):
    ssm = jnp.concatenate([ssm_ref[0, 0], ssm_ref[0, 1]], axis=1)
    o = (_dot(ssm.astype(BF16), wout_ref[0, 0:SSM_WIDTH, :])
         + _dot(na_ref[0], wout_ref[0, SSM_WIDTH:SSM_WIDTH + NA_WIDTH, :])
         + _dot(gqa_ref[0], wout_ref[0, SSM_WIDTH + NA_WIDTH:, :]))
    y = DEEPNORM_ALPHA * x_ref[0] + mod_ref[0, 2:3, :] * o
    x1 = _ln(y) * ln1g_ref[0] + ln1b_ref[0]
    h2 = _ln(x1) * (1.0 + mod_ref[0, 4:5, :]) + mod_ref[0, 3:4, :]
    h2_s[...] = h2.astype(BF16)
    acc = None
    for j in range(D_FF // FF_CHUNK):
        h2b = h2_s[...]
        a = _dot(h2b, win_ref[0, :, j * FF_CHUNK:(j + 1) * FF_CHUNK])
        gt = _dot(h2b, win_ref[0, :, D_FF + j * FF_CHUNK:D_FF + (j + 1) * FF_CHUNK])
        f = (a * _sigmoid(a) * gt).astype(BF16)
        c = _dot(f, wo_ref[0, j * FF_CHUNK:(j + 1) * FF_CHUNK, :])
        acc = c if acc is None else acc + c
    y = DEEPNORM_ALPHA * x1 + mod_ref[0, 5:6, :] * acc
    o_ref[0] = _ln(y) * ln2g_ref[0] + ln2b_ref[0]


def _out_ffn(l, x, ssm, na, gqa, mod, w_out, ln1g, ln1b, w_ffn_in, w_ffn_out, ln2g, ln2b):
    bg = x.shape[0]
    tm = 512
    vec = pl.BlockSpec((1, 1, D_MODEL), lambda b, i: (l, 0, 0))
    resident = lambda shape: pl.BlockSpec((1,) + shape, lambda b, i: (l, 0, 0), pipeline_mode=pl.Buffered(1))
    return pl.pallas_call(
        _ffn_kernel,
        out_shape=jax.ShapeDtypeStruct((bg, TOKENS, D_MODEL), F32),
        grid=(bg, TOKENS // tm),
        in_specs=[
            pl.BlockSpec((1, tm, D_MODEL), lambda b, i: (b, i, 0)),
            pl.BlockSpec((1, 2, tm, LANES), lambda b, i: (b, 0, i, 0)),
            pl.BlockSpec((1, tm, NA_WIDTH), lambda b, i: (b, i, 0)),
            pl.BlockSpec((1, tm, GQA_WIDTH), lambda b, i: (b, i, 0)),
            pl.BlockSpec((1, 6, D_MODEL), lambda b, i: (b, 0, 0)),
            resident((D_MODEL, D_MODEL)),
            vec,
            vec,
            resident((D_MODEL, 2 * D_FF)),
            resident((D_FF, D_MODEL)),
            vec,
            vec,
        ],
        out_specs=pl.BlockSpec((1, tm, D_MODEL), lambda b, i: (b, i, 0)),
        scratch_shapes=[pltpu.VMEM((tm, D_MODEL), BF16)],
        compiler_params=pltpu.CompilerParams(dimension_semantics=("parallel", "parallel"),
                                             vmem_limit_bytes=VMEM_LIMIT),
        name="ffn",
    )(x, ssm, na, gqa, mod, w_out, ln1g, ln1b, w_ffn_in, w_ffn_out, ln2g, ln2b)


def _rope_tables():
    nf = HEAD_DIM // 4
    t = jnp.arange(DEC_SEQ)
    inv = ROPE_THETA ** (-jnp.arange(nf, dtype=F32) / nf)
    rows = (t // GRID_W).astype(F32)
    cols = (t % GRID_W).astype(F32)
    ang = jnp.stack([rows[:, None] * inv, cols[:, None] * inv], axis=1)
    cos = jnp.repeat(jnp.cos(ang)[:, :, None, :], 2, axis=2).reshape(DEC_SEQ, HEAD_DIM)
    sin = jnp.sin(ang)[:, :, None, :] * jnp.asarray([-1.0, 1.0], F32)[None, None, :, None]
    sin = sin.reshape(DEC_SEQ, HEAD_DIM)
    return jnp.tile(cos, (1, 2)), jnp.tile(sin, (1, 2))


def _head_mean_matrix():
    h = np.arange(GQA_WIDTH) // HEAD_DIM
    return jnp.asarray((h[:, None] == h[None, :]).astype(np.float32) / HEAD_DIM, BF16)


def kernel(x_prompt, x_sample, c, cache_na_k, cache_na_v, cache_gqa_k, cache_gqa_v, state_ssm_re, state_ssm_im,
           c_ctx, w_ada, b_ada, w_in, w_out, q_norm_g, k_norm_g, na_bias, ssm_lam_re, ssm_lam_im, ssm_log_dt,
           ssm_b_re, ssm_b_im, ssm_c_re, ssm_c_im, ssm_d, w_ssm_glu, ln1_g, ln1_b, ln2_g, ln2_b,
           w_ffn_in, w_ffn_out):
    cond8 = jnp.concatenate([c_ctx[None, :], c, jnp.zeros((8 - 1 - DEC_BATCH, D_MODEL), F32)], axis=0)
    mod = _modulation(cond8, w_ada, b_ada).reshape(DEPTH, 8, 6, D_MODEL)
    cos, sin = _rope_tables()
    hm = _head_mean_matrix()
    w_in_b = w_in.astype(BF16)
    w_out_b = w_out.astype(BF16)
    w_ffn_in_b = w_ffn_in.astype(BF16)
    w_ffn_out_b = w_ffn_out.astype(BF16)
    w_glu_b = w_ssm_glu.astype(BF16)
    qg = jnp.tile(q_norm_g, (1, GQA_HEADS))[:, None, :]
    kg = jnp.tile(k_norm_g, (1, GQA_KV_HEADS))[:, None, :]
    dskip = ssm_d[:, None, :]
    ln = (ln1_g[:, None, :], ln1_b[:, None, :], ln2_g[:, None, :], ln2_b[:, None, :])
    *ssm_tabs, a1, a2 = _ssm_tables(ssm_lam_re, ssm_lam_im, ssm_log_dt, ssm_b_re, ssm_b_im, ssm_c_re, ssm_c_im)
    ops = _ssm_operators(ssm_tabs)
    a1 = a1.reshape(DEPTH, 2, STATE_LANES)
    a2 = a2.reshape(DEPTH, 2, STATE_LANES)
    na_tab = _na_bias_table(na_bias)
    ck_na = cache_na_k.reshape(DEC_BATCH, DEPTH, PAST_LEN, NA_WIDTH).astype(BF16)
    cv_na = cache_na_v.reshape(DEC_BATCH, DEPTH, PAST_LEN, NA_WIDTH).astype(BF16)
    kt_cache = cache_gqa_k.transpose(1, 0, 3, 4, 2).reshape(DEPTH, DEC_BATCH, GQA_KV_WIDTH, PAST_LEN).astype(BF16)
    cv_g = cache_gqa_v.transpose(1, 0, 3, 2, 4)
    v1_cache = jnp.concatenate([cv_g, jnp.ones(cv_g.shape[:-1] + (1,), F32),
                                jnp.zeros(cv_g.shape[:-1] + (LANES - HEAD_DIM - 1,), F32)], axis=-1).astype(BF16)
    s_re = state_ssm_re.astype(F32).transpose(1, 0, 2, 3, 4)
    s_im = state_ssm_im.astype(F32).transpose(1, 0, 2, 3, 4)
    h0s_lat = jnp.concatenate([s_re, s_im], axis=-1).reshape(DEPTH, DEC_BATCH, 2, STATE_LANES)
    h0p_lat = jnp.concatenate([s_im, s_re], axis=-1).reshape(DEPTH, DEC_BATCH, 2, STATE_LANES)
    h0_ctx = jnp.zeros((1, 2, STATE_LANES), F32)

    y_ctx = x_prompt.reshape(1, TOKENS, D_MODEL)
    y_lat = x_sample
    kv_na_l, kv_g_l, fin_l = [], [], []
    for l in range(DEPTH):
        mod_c = mod[l, 0:1]
        u, zb, kv_na, kv_g = _in_projection(l, y_ctx, mod_c, w_in_b, qg, kg, hm, cos, sin, latent=False)
        ssm_o, fin = _ssm(l, u, ops, a1, a2, h0_ctx, h0_ctx, dskip, w_glu_b, n_seq=BATCH, emit_final=True)
        na_o, gqa_o = _ctx_attention(zb)
        y_ctx = _out_ffn(l, y_ctx, ssm_o, na_o, gqa_o, mod_c, w_out_b, ln[0], ln[1], w_ffn_in_b, w_ffn_out_b,
                         ln[2], ln[3])
        kv_na_l.append(kv_na); kv_g_l.append(kv_g); fin_l.append(fin)

        mod_s = mod[l, 1:1 + DEC_BATCH]
        u, zb, kt_new, v1_new = _in_projection(l, y_lat, mod_s, w_in_b, qg, kg, hm, cos, sin, latent=True)
        ssm_o = _ssm(l, u, ops, a1, a2, h0s_lat[l], h0p_lat[l], dskip, w_glu_b, n_seq=1,
                     emit_final=False)[0]
        na_o = _na_latent(l, zb, ck_na, cv_na, na_tab)
        gqa_o = _gqa_latent(zb, kt_new, kt_cache[l], v1_new, v1_cache[l])
        y_lat = _out_ffn(l, y_lat, ssm_o, na_o, gqa_o, mod_s, w_out_b, ln[0], ln[1], w_ffn_in_b, w_ffn_out_b,
                         ln[2], ln[3])

    kv_na = jnp.stack(kv_na_l, axis=0).reshape(DEPTH, BATCH, SEQ, 2, NA_HEADS, HEAD_DIM)
    kv_g = jnp.stack(kv_g_l, axis=0).reshape(DEPTH, BATCH, SEQ, 2, GQA_KV_HEADS, HEAD_DIM)
    kv_na = kv_na.transpose(3, 1, 0, 2, 4, 5)
    kv_g = kv_g.transpose(3, 1, 0, 2, 4, 5)
    fin = jnp.stack(fin_l, axis=0).reshape(DEPTH, 2, BATCH, SSM_GROUPS, 2, SSM_STATE)
    fin = fin.transpose(4, 2, 0, 1, 3, 5)
    return (y_ctx.reshape(BATCH, SEQ, D_MODEL), y_lat, kv_na[0], kv_na[1], kv_g[0], kv_g[1], fin[0], fin[1])
```

```python
import functools
import math

import numpy as np
import jax
import jax.numpy as jnp
from jax import lax
from jax.experimental import pallas as pl
from jax.experimental.pallas import tpu as pltpu

F32 = jnp.float32
BF16 = jnp.bfloat16

D_MODEL = 1024
BATCH = 16
SEQ = 256
DEPTH = 2
DEC_BATCH = 2
DEC_SEQ = 4096
PAST_LEN = 256
GRID_W = 64
GRID_H = DEC_SEQ // GRID_W
HEAD_DIM = 64
SSM_WIDTH = 256
SSM_GROUP_CH = 16
SSM_GROUPS = 16
SSM_STATE = 64
NA_HEADS = 4
NA_WIDTH = 256
NA_WIN_R = 8
NA_WIN_C = 16
GQA_HEADS = 8
GQA_KV_HEADS = 2
GQA_REP = 4
GQA_WIDTH = 512
GQA_KV_WIDTH = 128
IN_WIDTH = 1792
D_FF = 2816
ROPE_THETA = 10000.0
LN_EPS = 1e-6
RMS_EPS = 1e-6
DEEPNORM_ALPHA = (2 * DEPTH) ** 0.25
QK_SCALE = HEAD_DIM ** -0.5
LOG2E = math.log2(math.e)

TOKENS = 4096
SSM_CHUNK = 16
N_CHUNK = TOKENS // SSM_CHUNK
Z_WIDTH = IN_WIDTH - SSM_WIDTH
Z_QNA, Z_KNA, Z_VNA, Z_QG, Z_KG, Z_VG = 0, 256, 512, 768, 1280, 1408

VMEM_LIMIT = 56 * 1024 * 1024
LANES = 128


def _sigmoid(x):
    return 1.0 / (1.0 + jnp.exp(-x))


def _ln(x):
    mu = jnp.mean(x, axis=-1, keepdims=True)
    xc = x - mu
    var = jnp.mean(xc * xc, axis=-1, keepdims=True)
    return xc * lax.rsqrt(var + LN_EPS)


def _dot(a, b):
    return jnp.dot(a, b, preferred_element_type=F32)


def _dot_nt(a, b):
    return lax.dot_general(a, b, (((1,), (1,)), ((), ())), preferred_element_type=F32)


def _mod_kernel(c_ref, w_ref, b_ref, o_ref):
    c = c_ref[...]
    s = c * _sigmoid(c)
    o_ref[0] = jnp.dot(s, w_ref[0], preferred_element_type=F32, precision=lax.Precision.HIGHEST) + b_ref[0]


def _modulation(cond8, w_ada, b_ada):
    nb = 1536
    return pl.pallas_call(
        _mod_kernel,
        out_shape=jax.ShapeDtypeStruct((DEPTH, 8, 6 * D_MODEL), F32),
        grid=(DEPTH, 6 * D_MODEL // nb),
        in_specs=[
            pl.BlockSpec((8, D_MODEL), lambda l, j: (0, 0)),
            pl.BlockSpec((1, D_MODEL, nb), lambda l, j: (l, 0, j)),
            pl.BlockSpec((1, 1, nb), lambda l, j: (l, 0, j)),
        ],
        out_specs=pl.BlockSpec((1, 8, nb), lambda l, j: (l, 0, j)),
        compiler_params=pltpu.CompilerParams(dimension_semantics=("parallel", "parallel"),
                                             vmem_limit_bytes=VMEM_LIMIT),
        name="mod",
    )(cond8, w_ada, b_ada.reshape(DEPTH, 1, 6 * D_MODEL))


def _swap16(x):
    w = x.shape[-1]
    lane = lax.broadcasted_iota(jnp.int32, x.shape, x.ndim - 1)
    return jnp.where((lane & 16) != 0, pltpu.roll(x, 16, x.ndim - 1), pltpu.roll(x, w - 16, x.ndim - 1))


def _win_kernel(x_ref, mod_ref, w_ref, qg_ref, kg_ref, hm_ref, cos_ref, sin_ref, *out_refs, latent):
    u_ref, z_ref = out_refs[0], out_refs[1]
    x = x_ref[0]
    h = _ln(x) * (1.0 + mod_ref[0, 1:2, :]) + mod_ref[0, 0:1, :]
    z = _dot(h.astype(BF16), w_ref[0])
    u_ref[0, 0] = z[:, 0:LANES]
    u_ref[0, 1] = z[:, LANES:SSM_WIDTH]
    q_na = z[:, 256:512] * QK_SCALE
    kv_na = z[:, 512:1024]
    q_g = z[:, 1024:1536]
    k_g = z[:, 1536:1664]
    v_g = z[:, 1664:1792]
    q_ms = _dot((q_g * q_g).astype(BF16), hm_ref[...])
    k_ms = _dot((k_g * k_g).astype(BF16), hm_ref[0:GQA_KV_WIDTH, 0:GQA_KV_WIDTH])
    q_n = q_g * lax.rsqrt(q_ms + RMS_EPS) * qg_ref[0]
    k_n = k_g * lax.rsqrt(k_ms + RMS_EPS) * kg_ref[0]
    if latent:
        kt_ref, v1_ref = out_refs[2], out_refs[3]
        cos = cos_ref[...]
        sin = sin_ref[...]
        cos4 = jnp.concatenate([cos] * 4, axis=1)
        sin4 = jnp.concatenate([sin] * 4, axis=1)
        q_n = (q_n * cos4 + _swap16(q_n) * sin4) * (QK_SCALE * LOG2E)
        k_n = k_n * cos + _swap16(k_n) * sin
        kt_ref[0] = k_n.T.astype(BF16)
        lane = lax.broadcasted_iota(jnp.int32, v_g.shape, 1)
        ones_col = jnp.where(lane == HEAD_DIM, 1.0, 0.0)
        v1_ref[0, 0] = jnp.where(lane < HEAD_DIM, v_g, ones_col).astype(BF16)
        v1_ref[0, 1] = jnp.where(lane < HEAD_DIM, pltpu.roll(v_g, HEAD_DIM, 1), ones_col).astype(BF16)
    else:
        out_refs[2][0] = kv_na
        out_refs[3][0] = jnp.concatenate([k_n, v_g], axis=1)
        q_n = q_n * QK_SCALE
    zb = jnp.concatenate([q_na, kv_na, q_n, k_n, v_g], axis=1)
    z_ref[0] = zb.astype(BF16)


def _in_projection(l, x, mod, w_in, qg, kg, hm, cos, sin, *, latent):
    bg = x.shape[0]
    tm = 512
    out_shape = [jax.ShapeDtypeStruct((bg, 2, TOKENS, LANES), F32),
                 jax.ShapeDtypeStruct((bg, TOKENS, Z_WIDTH), BF16)]
    out_specs = [pl.BlockSpec((1, 2, tm, LANES), lambda b, i: (b, 0, i, 0)),
                 pl.BlockSpec((1, tm, Z_WIDTH), lambda b, i: (b, i, 0))]
    if latent:
        out_shape += [jax.ShapeDtypeStruct((bg, GQA_KV_WIDTH, TOKENS), BF16),
                      jax.ShapeDtypeStruct((bg, GQA_KV_HEADS, TOKENS, LANES), BF16)]
        out_specs += [pl.BlockSpec((1, GQA_KV_WIDTH, tm), lambda b, i: (b, 0, i)),
                      pl.BlockSpec((1, GQA_KV_HEADS, tm, LANES), lambda b, i: (b, 0, i, 0))]
    else:
        out_shape += [jax.ShapeDtypeStruct((bg, TOKENS, 2 * NA_WIDTH), F32),
                      jax.ShapeDtypeStruct((bg, TOKENS, 2 * GQA_KV_WIDTH), F32)]
        out_specs += [pl.BlockSpec((1, tm, 2 * NA_WIDTH), lambda b, i: (b, i, 0)),
                      pl.BlockSpec((1, tm, 2 * GQA_KV_WIDTH), lambda b, i: (b, i, 0))]
    return pl.pallas_call(
        functools.partial(_win_kernel, latent=latent),
        out_shape=out_shape,
        grid=(bg, TOKENS // tm),
        in_specs=[
            pl.BlockSpec((1, tm, D_MODEL), lambda b, i: (b, i, 0)),
            pl.BlockSpec((1, 6, D_MODEL), lambda b, i: (b, 0, 0)),
            pl.BlockSpec((1, D_MODEL, IN_WIDTH), lambda b, i: (l, 0, 0)),
            pl.BlockSpec((1, 1, GQA_WIDTH), lambda b, i: (l, 0, 0)),
            pl.BlockSpec((1, 1, GQA_KV_WIDTH), lambda b, i: (l, 0, 0)),
            pl.BlockSpec((GQA_WIDTH, GQA_WIDTH), lambda b, i: (0, 0)),
            pl.BlockSpec((tm, LANES), lambda b, i: (i, 0)),
            pl.BlockSpec((tm, LANES), lambda b, i: (i, 0)),
        ],
        out_specs=out_specs,
        compiler_params=pltpu.CompilerParams(dimension_semantics=("parallel", "parallel"),
                                             vmem_limit_bytes=VMEM_LIMIT),
        name="win_lat" if latent else "win_ctx",
    )(x, mod, w_in, qg, kg, hm, cos, sin)


STATE_LANES = SSM_GROUPS * LANES


def _ssm_ops_kernel(crt_ref, cit_ref, pkr_ref, pki_ref, pcr_ref, pci_ref, bb1_ref, bb2_ref, pwr_ref, pwi_ref,
                    m_ref, bend_ref, cpow_ref):
    d = pl.program_id(1)

    def build_ops(g, carry):
        cr = crt_ref[0, 0, g]
        ci = cit_ref[0, 0, g]

        def re_proj(pr, pi):
            return jnp.concatenate([cr * pr - ci * pi, -(cr * pi + ci * pr)], axis=0)
        cpow_ref[0, 0, g] = re_proj(pcr_ref[0, 0, g], pci_ref[0, 0, g]).astype(BF16)
        bb1 = bb1_ref[0, 0, g]
        bb2 = bb2_ref[0, 0, g]
        kcat = jnp.dot(bb1[:, :LANES], re_proj(pkr_ref[0, 0, g], pki_ref[0, 0, g]),
                       preferred_element_type=F32, precision=lax.Precision.HIGHEST)
        lane = lax.broadcasted_iota(jnp.int32, kcat.shape, 1)
        for t in range(SSM_CHUNK):
            rows = slice(t * SSM_GROUP_CH, (t + 1) * SSM_GROUP_CH)
            lo, hi = t * SSM_GROUP_CH, (t + 1) * SSM_GROUP_CH
            fwd = kcat if t == 0 else jnp.where(lane >= lo, pltpu.roll(kcat, lo, 1), 0.0)
            bwd = kcat if hi == 256 else jnp.where(lane < hi, pltpu.roll(kcat, hi, 1), 0.0)
            m_ref[0, 0, g, rows, :] = jnp.where(d == 0, fwd, bwd).astype(BF16)
            bend_ref[0, 0, g, rows, :] = (pwr_ref[0, 0, g, t:t + 1, :] * bb1
                                          + pwi_ref[0, 0, g, t:t + 1, :] * bb2).astype(BF16)
        return carry
    lax.fori_loop(0, SSM_GROUPS, build_ops, 0)


def _ssm_operators(tables):
    per_dir = lambda rows: pl.BlockSpec((1, 1, SSM_GROUPS, rows, 256), lambda l, d: (l, d, 0, 0, 0))
    shape = lambda rows: jax.ShapeDtypeStruct((DEPTH, 2, SSM_GROUPS, rows, 256), BF16)
    return pl.pallas_call(
        _ssm_ops_kernel,
        out_shape=[shape(256), shape(256), shape(LANES)],
        grid=(DEPTH, 2),
        in_specs=[per_dir(SSM_STATE)] * 6 + [per_dir(SSM_GROUP_CH)] * 4,
        out_specs=[per_dir(256), per_dir(256), per_dir(LANES)],
        compiler_params=pltpu.CompilerParams(dimension_semantics=("parallel", "parallel"),
                                             vmem_limit_bytes=VMEM_LIMIT),
        name="ssm_ops",
    )(*tables)


def _ssm_kernel(u_ref, m_ref, bend_ref, cpow_ref, a1_ref, a2_ref, h0s_ref, h0p_ref, d_ref, wglu_ref, *refs,
                n_seq, emit_final):
    if emit_final:
        out_ref, fin_ref, y_s, yt_s, u_s, inja_s, injb_s, hp_s = refs
    else:
        out_ref, y_s, yt_s, u_s, inja_s, injb_s, hp_s = refs
        fin_ref = None
    chunks_per_seq = N_CHUNK // n_seq

    def chunk_rows(t):
        sl = pl.ds(t, N_CHUNK, stride=SSM_CHUNK)
        return jnp.concatenate([u_ref[0, 0, sl, :], u_ref[0, 1, sl, :]], axis=1)

    for t in range(SSM_CHUNK):
        a_t = chunk_rows(t)
        yt_s[:, t * SSM_GROUP_CH:(t + 1) * SSM_GROUP_CH, :] = a_t.T.reshape(SSM_GROUPS, SSM_GROUP_CH, N_CHUNK)

    def tr_in(g, carry):
        u_s[g] = yt_s[g].T.astype(BF16)
        return carry
    lax.fori_loop(0, SSM_GROUPS, tr_in, 0)

    for d in range(2):
        def proj(g, carry, d=d):
            ug = u_s[g]
            y = _dot(ug, m_ref[0, d, g])
            y_s[g] = y if d == 0 else y_s[g] + y
            inj = _dot(ug, bend_ref[0, d, g])
            lanes = pl.ds(pl.multiple_of(g * LANES, LANES), LANES)
            inja_s[d, :, lanes] = inj[:, :LANES]
            injb_s[d, :, lanes] = inj[:, LANES:]
            return carry
        lax.fori_loop(0, SSM_GROUPS, proj, 0)

    a1 = a1_ref[0]
    a2 = a2_ref[0]

    def step(j, carry):
        s, sp = carry
        nf = j
        nb = N_CHUNK - 1 - j
        if n_seq > 1:
            keep = jnp.where(j % chunks_per_seq != 0, 1.0, 0.0).astype(F32)
            s = s * keep
            sp = sp * keep
        hp_s[0, pl.ds(nf, 1), :] = s[0:1]
        hp_s[1, pl.ds(nb, 1), :] = s[1:2]
        ia = jnp.concatenate([inja_s[0, pl.ds(nf, 1), :], inja_s[1, pl.ds(nb, 1), :]], axis=0)
        ib = jnp.concatenate([injb_s[0, pl.ds(nf, 1), :], injb_s[1, pl.ds(nb, 1), :]], axis=0)
        s_new = a1 * s + a2 * sp + ia
        sp_new = a1 * sp - a2 * s + ib
        if emit_final:
            @pl.when(j % chunks_per_seq == chunks_per_seq - 1)
            def _():
                sq = j // chunks_per_seq
                fin_ref[0, 0, pl.ds(sq, 1), :] = s_new[0:1]
                fin_ref[0, 1, pl.ds(n_seq - 1 - sq, 1), :] = s_new[1:2]
        return s_new, sp_new
    lax.fori_loop(0, N_CHUNK, step, (h0s_ref[0], h0p_ref[0]))

    for d in range(2):
        def carry_in(g, carry, d=d):
            hp = hp_s[d, :, pl.ds(pl.multiple_of(g * LANES, LANES), LANES)]
            y_s[g] = y_s[g] + _dot(hp.astype(BF16), cpow_ref[0, d, g])
            return carry
        lax.fori_loop(0, SSM_GROUPS, carry_in, 0)

    def tr_out(g, carry):
        yt_s[g] = y_s[g].T
        return carry
    lax.fori_loop(0, SSM_GROUPS, tr_out, 0)
    dskip = d_ref[0]
    wglu = wglu_ref[0]
    for t in range(SSM_CHUNK):
        blk = yt_s[:, t * SSM_GROUP_CH:(t + 1) * SSM_GROUP_CH, :].reshape(SSM_WIDTH, N_CHUNK)
        y = blk.T + chunk_rows(t) * dskip
        gl = jax.nn.gelu(y, approximate=True)
        gate = _dot(gl.astype(BF16), wglu)
        o = gl * _sigmoid(gate)
        out_ref[0, 0, pl.ds(t, N_CHUNK, stride=SSM_CHUNK), :] = o[:, :LANES]
        out_ref[0, 1, pl.ds(t, N_CHUNK, stride=SSM_CHUNK), :] = o[:, LANES:]


def _ssm(l, u, ops, a1, a2, h0s, h0p, dskip, wglu, *, n_seq, emit_final):
    bg = u.shape[0]
    m, bend, cpow = ops
    out_shape = [jax.ShapeDtypeStruct((bg, 2, TOKENS, LANES), F32)]
    out_specs = [pl.BlockSpec((1, 2, TOKENS, LANES), lambda b: (b, 0, 0, 0))]
    if emit_final:
        out_shape.append(jax.ShapeDtypeStruct((bg, 2, n_seq, STATE_LANES), F32))
        out_specs.append(pl.BlockSpec((1, 2, n_seq, STATE_LANES), lambda b: (b, 0, 0, 0)))
    resident = lambda rows: pl.BlockSpec((1, 2, SSM_GROUPS, rows, 256), lambda b: (l, 0, 0, 0, 0),
                                         pipeline_mode=pl.Buffered(1))
    return pl.pallas_call(
        functools.partial(_ssm_kernel, n_seq=n_seq, emit_final=emit_final),
        out_shape=out_shape,
        grid=(bg,),
        in_specs=[
            pl.BlockSpec((1, 2, TOKENS, LANES), lambda b: (b, 0, 0, 0)),
            resident(256), resident(256), resident(LANES),
            pl.BlockSpec((1, 2, STATE_LANES), lambda b: (l, 0, 0)),
            pl.BlockSpec((1, 2, STATE_LANES), lambda b: (l, 0, 0)),
            pl.BlockSpec((1, 2, STATE_LANES), lambda b: (b, 0, 0)),
            pl.BlockSpec((1, 2, STATE_LANES), lambda b: (b, 0, 0)),
            pl.BlockSpec((1, 1, SSM_WIDTH), lambda b: (l, 0, 0)),
            pl.BlockSpec((1, SSM_WIDTH, SSM_WIDTH), lambda b: (l, 0, 0)),
        ],
        out_specs=out_specs,
        scratch_shapes=[pltpu.VMEM((SSM_GROUPS, N_CHUNK, 256), F32),
                        pltpu.VMEM((SSM_GROUPS, 256, N_CHUNK), F32),
                        pltpu.VMEM((SSM_GROUPS, N_CHUNK, 256), BF16),
                        pltpu.VMEM((2, N_CHUNK, STATE_LANES), F32),
                        pltpu.VMEM((2, N_CHUNK, STATE_LANES), F32),
                        pltpu.VMEM((2, N_CHUNK, STATE_LANES), F32)],
        compiler_params=pltpu.CompilerParams(dimension_semantics=("parallel",), vmem_limit_bytes=VMEM_LIMIT),
        name="ssm_ctx" if emit_final else "ssm_lat",
    )(u, m, bend, cpow, a1, a2, h0s, h0p, dskip, wglu)


def _ssm_tables(lam_re, lam_im, log_dt, b_re, b_im, c_re, c_im):
    t_n = SSM_CHUNK
    lr = jnp.minimum(lam_re.astype(F32), -1e-4)
    li = lam_im.astype(F32)
    dt = jnp.exp(log_dt.astype(F32))[..., None]
    lr_dt, li_dt = lr * dt, li * dt

    def power(k, expand=lambda x: x):
        mag = jnp.exp(expand(lr_dt) * k)
        return mag * jnp.cos(expand(li_dt) * k), mag * jnp.sin(expand(li_dt) * k)
    l_r, l_i = power(1.0)
    a_r, a_i = power(float(t_n))
    n_r, n_i = l_r - 1.0, l_i
    den = lr * lr + li * li
    q_r, q_i = (n_r * lr + n_i * li) / den, (n_i * lr - n_r * li) / den
    bb_r = q_r[..., None] * b_re - q_i[..., None] * b_im
    bb_i = q_r[..., None] * b_im + q_i[..., None] * b_re

    up = np.arange(t_n, dtype=np.float32)
    down = up[::-1].copy()
    per_dir = lambda fwd, bwd: jnp.asarray(np.stack([fwd, bwd]))[None, :, None, None, :]
    on_lanes = lambda x: x[..., None]
    spread = lambda k: jnp.repeat(k, SSM_GROUP_CH, axis=-1)
    pk_r, pk_i = power(spread(per_dir(up, down)), on_lanes)
    pc_r, pc_i = power(spread(per_dir(up + 1.0, down + 1.0)), on_lanes)
    pw_r, pw_i = power(jnp.swapaxes(per_dir(down, up), -1, -2), lambda x: x[..., None, :])
    pw_r, pw_i = jnp.tile(pw_r, (1, 1, 1, 1, 4)), jnp.tile(pw_i, (1, 1, 1, 1, 4))
    cr_t = jnp.tile(jnp.swapaxes(c_re.astype(F32), -1, -2), (1, 1, 1, 1, t_n))
    ci_t = jnp.tile(jnp.swapaxes(c_im.astype(F32), -1, -2), (1, 1, 1, 1, t_n))
    br_t, bi_t = jnp.swapaxes(bb_r, -1, -2), jnp.swapaxes(bb_i, -1, -2)
    bb1 = jnp.concatenate([br_t, bi_t, bi_t, br_t], axis=-1)
    bb2 = jnp.concatenate([-bi_t, br_t, br_t, -bi_t], axis=-1)
    a1 = jnp.concatenate([a_r, a_r], axis=-1)
    a2 = jnp.concatenate([-a_i, a_i], axis=-1)
    return cr_t, ci_t, pk_r, pk_i, pc_r, pc_i, bb1, bb2, pw_r, pw_i, a1, a2


def _attend_transposed(k, v, q, n_kv, rep):
    nq = q.shape[0]
    qt = q.astype(F32).T
    zero = jnp.zeros((HEAD_DIM, nq), F32)
    bd = jnp.concatenate(
        [jnp.concatenate([qt[(g * rep + r) * HEAD_DIM:(g * rep + r + 1) * HEAD_DIM] if g == gg else zero
                          for gg in range(n_kv) for r in range(rep)], axis=1)
         for g in range(n_kv)], axis=0).astype(BF16)
    s = _dot(k, bd)
    m = jnp.max(s, axis=0, keepdims=True)
    p = jnp.exp(s - m)
    p = p * (1.0 / jnp.sum(p, axis=0, keepdims=True))
    ot = _dot(v.astype(F32).T.astype(BF16), p.astype(BF16))
    heads = [ot[g * HEAD_DIM:(g + 1) * HEAD_DIM, (g * rep + r) * nq:(g * rep + r + 1) * nq]
             for g in range(n_kv) for r in range(rep)]
    return jnp.concatenate(heads, axis=0).T


def _ctx_attn_kernel(z_ref, na_ref, gqa_ref):
    z = z_ref[0]
    na_ref[0] = _attend_transposed(z[:, Z_KNA:Z_KNA + NA_WIDTH], z[:, Z_VNA:Z_VNA + NA_WIDTH],
                                   z[:, Z_QNA:Z_QNA + NA_WIDTH], NA_HEADS, 1).astype(BF16)
    gqa_ref[0] = _attend_transposed(z[:, Z_KG:Z_KG + GQA_KV_WIDTH], z[:, Z_VG:Z_VG + GQA_KV_WIDTH],
                                    z[:, Z_QG:Z_QG + GQA_WIDTH], GQA_KV_HEADS, GQA_REP).astype(BF16)


def _ctx_attention(zb):
    z3 = zb.reshape(BATCH, SEQ, Z_WIDTH)
    na, gqa = pl.pallas_call(
        _ctx_attn_kernel,
        out_shape=[jax.ShapeDtypeStruct((BATCH, SEQ, NA_WIDTH), BF16),
                   jax.ShapeDtypeStruct((BATCH, SEQ, GQA_WIDTH), BF16)],
        grid=(BATCH,),
        in_specs=[pl.BlockSpec((1, SEQ, Z_WIDTH), lambda b: (b, 0, 0))],
        out_specs=[pl.BlockSpec((1, SEQ, NA_WIDTH), lambda b: (b, 0, 0)),
                   pl.BlockSpec((1, SEQ, GQA_WIDTH), lambda b: (b, 0, 0))],
        compiler_params=pltpu.CompilerParams(dimension_semantics=("parallel",), vmem_limit_bytes=VMEM_LIMIT),
        name="attn_ctx",
    )(z3)
    return na.reshape(1, TOKENS, NA_WIDTH), gqa.reshape(1, TOKENS, GQA_WIDTH)


NA_KEYS = NA_WIN_R * GRID_W


NA_ROWS = 4


def _na_lat_kernel(q_ref, k_ref, v_ref, ck_ref, cv_ref, *refs):
    bias_refs, o_ref = refs[:NA_ROWS], refs[NA_ROWS]
    i = pl.program_id(1)
    ck = ck_ref[0, 0]
    cv = cv_ref[0, 0]
    qt = q_ref[0].astype(F32).T
    hd = lax.broadcasted_iota(jnp.int32, (NA_WIDTH, NA_WIDTH), 0) // HEAD_DIM
    hq = lax.broadcasted_iota(jnp.int32, (NA_WIDTH, NA_WIDTH), 1) // HEAD_DIM
    for j in range(NA_ROWS):
        r = NA_ROWS * i + j
        rs = jnp.clip(r - NA_WIN_R // 2, 0, GRID_H - NA_WIN_R)
        start = pl.multiple_of(rs * GRID_W, GRID_W)
        kw = k_ref[0, pl.ds(start, NA_KEYS), :]
        vw = v_ref[0, pl.ds(start, NA_KEYS), :]
        qj = qt[:, j * GRID_W:(j + 1) * GRID_W]
        bd = jnp.where(hd == hq, jnp.concatenate([qj] * NA_HEADS, axis=1), 0.0).astype(BF16)
        s_loc = _dot(kw, bd) + bias_refs[j][0, 0]
        s_ctx = _dot(ck, bd)
        m = jnp.maximum(jnp.max(s_loc, axis=0, keepdims=True), jnp.max(s_ctx, axis=0, keepdims=True))
        p_loc = jnp.exp(s_loc - m)
        p_ctx = jnp.exp(s_ctx - m)
        inv = 1.0 / (jnp.sum(p_loc, axis=0, keepdims=True) + jnp.sum(p_ctx, axis=0, keepdims=True))
        full = (_dot((p_loc * inv).T.astype(BF16), vw)
                + _dot((p_ctx * inv).T.astype(BF16), cv))
        out = jnp.concatenate([full[h * HEAD_DIM:(h + 1) * HEAD_DIM, h * HEAD_DIM:(h + 1) * HEAD_DIM]
                               for h in range(NA_HEADS)], axis=1)
        o_ref[0, j * GRID_W:(j + 1) * GRID_W, :] = out.astype(BF16)


def _na_row_class(r):
    return r - jnp.clip(r - NA_WIN_R // 2, 0, GRID_H - NA_WIN_R)


def _na_latent(l, zb, ck, cv, bias):
    tq = NA_ROWS * GRID_W
    bias_specs = [pl.BlockSpec((1, 1, NA_KEYS, NA_WIDTH),
                               functools.partial(lambda b, i, j: (l, _na_row_class(NA_ROWS * i + j), 0, 0), j=j))
                  for j in range(NA_ROWS)]
    return pl.pallas_call(
        _na_lat_kernel,
        out_shape=jax.ShapeDtypeStruct((DEC_BATCH, TOKENS, NA_WIDTH), BF16),
        grid=(DEC_BATCH, GRID_H // NA_ROWS),
        in_specs=[
            pl.BlockSpec((1, tq, NA_WIDTH), lambda b, i: (b, i, Z_QNA // NA_WIDTH)),
            pl.BlockSpec((1, TOKENS, NA_WIDTH), lambda b, i: (b, 0, Z_KNA // NA_WIDTH)),
            pl.BlockSpec((1, TOKENS, NA_WIDTH), lambda b, i: (b, 0, Z_VNA // NA_WIDTH)),
            pl.BlockSpec((1, 1, PAST_LEN, NA_WIDTH), lambda b, i: (b, l, 0, 0)),
            pl.BlockSpec((1, 1, PAST_LEN, NA_WIDTH), lambda b, i: (b, l, 0, 0)),
        ] + bias_specs,
        out_specs=pl.BlockSpec((1, tq, NA_WIDTH), lambda b, i: (b, i, 0)),
        compiler_params=pltpu.CompilerParams(dimension_semantics=("parallel", "arbitrary"),
                                             vmem_limit_bytes=VMEM_LIMIT),
        name="na_lat",
    )(zb, zb, zb, ck, cv, *([bias] * NA_ROWS))


def _na_bias_table(tab):
    q = np.arange(GRID_W)
    kc = np.arange(GRID_W)
    cs = np.clip(q - NA_WIN_C // 2, 0, GRID_W - NA_WIN_C)
    valid = (kc[None, :] >= cs[:, None]) & (kc[None, :] < cs[:, None] + NA_WIN_C)
    w = GRID_W - 1
    padded = jnp.pad(tab.astype(F32), ((0, 0), (0, 0), (0, 0), (w, w)))
    off = w + NA_WIN_C - 1
    cols = jnp.stack([padded[..., off - i:off - i + GRID_W] for i in range(GRID_W)], axis=-2)
    cols = jnp.where(valid, cols, -jnp.inf)
    b = jnp.stack([cols[:, :, NA_WIN_R - 1 - c:2 * NA_WIN_R - 1 - c] for c in range(NA_WIN_R)], axis=1)
    return b.transpose(0, 1, 3, 5, 2, 4).reshape(DEPTH, NA_WIN_R, NA_KEYS, NA_WIDTH)


GQA_TK = 256
GQA_KEYS = DEC_SEQ + PAST_LEN


def _gqa_kernel(q_ref, ktn_ref, ktc_ref, vn_ref, vc_ref, o_ref, s_s, m_s, *, tq):
    q = q_ref[0]
    n_new = DEC_SEQ // GQA_TK
    half = GQA_REP * tq // 2
    qh = [jnp.concatenate([q[:, r * HEAD_DIM:(r + 1) * HEAD_DIM] for r in (2 * i, 2 * i + 1)], axis=0)
          for i in range(2)]
    rows = [slice(i * half, (i + 1) * half) for i in range(2)]

    def fold(s):
        return jnp.maximum(s[:, :LANES], s[:, LANES:])

    for i in range(2):
        s = _dot(qh[i], ktc_ref[0])
        s_s[rows[i], DEC_SEQ:GQA_KEYS] = s
        m_s[rows[i], :] = fold(s)
    for c in range(0, n_new, 2):
        for i in range(2):
            s0 = _dot(qh[i], ktn_ref[0, :, c * GQA_TK:(c + 1) * GQA_TK])
            s1 = _dot(qh[i], ktn_ref[0, :, (c + 1) * GQA_TK:(c + 2) * GQA_TK])
            s_s[rows[i], c * GQA_TK:(c + 1) * GQA_TK] = s0
            s_s[rows[i], (c + 1) * GQA_TK:(c + 2) * GQA_TK] = s1
            m_s[rows[i], :] = jnp.maximum(m_s[rows[i], :], jnp.maximum(fold(s0), fold(s1)))

    m = jnp.max(m_s[...], axis=-1, keepdims=True)
    m_s[...] = jnp.broadcast_to(m, m_s.shape)

    def probs(i, c0):
        mb = m_s[rows[i], :]
        return jnp.exp2(s_s[rows[i], c0:c0 + GQA_TK] - jnp.concatenate([mb, mb], axis=1)).astype(BF16)

    acc = [_dot(probs(i, DEC_SEQ), vc_ref[0, 0]) for i in range(2)]
    for c in range(n_new):
        for i in range(2):
            acc[i] = acc[i] + _dot(probs(i, c * GQA_TK), vn_ref[0, 0, c * GQA_TK:(c + 1) * GQA_TK, :])
    outs = []
    for i in range(2):
        out = acc[i][:, :HEAD_DIM] / acc[i][:, HEAD_DIM:HEAD_DIM + 1]
        outs += [out[:tq], out[tq:]]
    o_ref[0] = jnp.concatenate(outs, axis=1).astype(BF16)


def _gqa_latent(zb, kt_new, kt_cache, v1_new, v1_cache):
    tq = 256
    rows = GQA_REP * tq
    return pl.pallas_call(
        functools.partial(_gqa_kernel, tq=tq),
        out_shape=jax.ShapeDtypeStruct((DEC_BATCH, TOKENS, GQA_WIDTH), BF16),
        grid=(DEC_BATCH, GQA_KV_HEADS, TOKENS // tq),
        in_specs=[
            pl.BlockSpec((1, tq, 256), lambda b, g, i: (b, i, Z_QG // 256 + g)),
            pl.BlockSpec((1, HEAD_DIM, DEC_SEQ), lambda b, g, i: (b, g, 0)),
            pl.BlockSpec((1, HEAD_DIM, PAST_LEN), lambda b, g, i: (b, g, 0)),
            pl.BlockSpec((1, 1, DEC_SEQ, LANES), lambda b, g, i: (b, g, 0, 0)),
            pl.BlockSpec((1, 1, PAST_LEN, LANES), lambda b, g, i: (b, g, 0, 0)),
        ],
        out_specs=pl.BlockSpec((1, tq, 256), lambda b, g, i: (b, i, g)),
        scratch_shapes=[pltpu.VMEM((rows, GQA_KEYS), F32), pltpu.VMEM((rows, LANES), F32)],
        compiler_params=pltpu.CompilerParams(dimension_semantics=("parallel", "parallel", "arbitrary"),
                                             vmem_limit_bytes=VMEM_LIMIT),
        name="gqa_lat",
    )(zb, kt_new, kt_cache, v1_new, v1_cache)


FF_CHUNK = 256


def _ffn_kernel(x_ref, ssm_ref, na_ref, gqa_ref, mod_ref, wout_ref, ln1g_ref, ln1b_ref, win_ref, wo_ref,
                ln2g_ref, ln2b_ref, o_ref, h2_s):
    ssm = jnp.concatenate([ssm_ref[0, 0], ssm_ref[0, 1]], axis=1)
    o = (_dot(ssm.astype(BF16), wout_ref[0, 0:SSM_WIDTH, :])
         + _dot(na_ref[0], wout_ref[0, SSM_WIDTH:SSM_WIDTH + NA_WIDTH, :])
         + _dot(gqa_ref[0], wout_ref[0, SSM_WIDTH + NA_WIDTH:, :]))
    y = DEEPNORM_ALPHA * x_ref[0] + mod_ref[0, 2:3, :] * o
    x1 = _ln(y) * ln1g_ref[0] + ln1b_ref[0]
    h2 = _ln(x1) * (1.0 + mod_ref[0, 4:5, :]) + mod_ref[0, 3:4, :]
    h2_s[...] = h2.astype(BF16)
    acc = None
    for j in range(D_FF // FF_CHUNK):
        h2b = h2_s[...]
        a = _dot(h2b, win_ref[0, :, j * FF_CHUNK:(j + 1) * FF_CHUNK])
        gt = _dot(h2b, win_ref[0, :, D_FF + j * FF_CHUNK:D_FF + (j + 1) * FF_CHUNK])
        f = (a * _sigmoid(a) * gt).astype(BF16)
        c = _dot(f, wo_ref[0, j * FF_CHUNK:(j + 1) * FF_CHUNK, :])
        acc = c if acc is None else acc + c
    y = DEEPNORM_ALPHA * x1 + mod_ref[0, 5:6, :] * acc
    o_ref[0] = _ln(y) * ln2g_ref[0] + ln2b_ref[0]


def _out_ffn(l, x, ssm, na, gqa, mod, w_out, ln1g, ln1b, w_ffn_in, w_ffn_out, ln2g, ln2b):
    bg = x.shape[0]
    tm = 512
    vec = pl.BlockSpec((1, 1, D_MODEL), lambda b, i: (l, 0, 0))
    resident = lambda shape: pl.BlockSpec((1,) + shape, lambda b, i: (l, 0, 0), pipeline_mode=pl.Buffered(1))
    return pl.pallas_call(
        _ffn_kernel,
        out_shape=jax.ShapeDtypeStruct((bg, TOKENS, D_MODEL), F32),
        grid=(bg, TOKENS // tm),
        in_specs=[
            pl.BlockSpec((1, tm, D_MODEL), lambda b, i: (b, i, 0)),
            pl.BlockSpec((1, 2, tm, LANES), lambda b, i: (b, 0, i, 0)),
            pl.BlockSpec((1, tm, NA_WIDTH), lambda b, i: (b, i, 0)),
            pl.BlockSpec((1, tm, GQA_WIDTH), lambda b, i: (b, i, 0)),
            pl.BlockSpec((1, 6, D_MODEL), lambda b, i: (b, 0, 0)),
            resident((D_MODEL, D_MODEL)),
            vec,
            vec,
            resident((D_MODEL, 2 * D_FF)),
            resident((D_FF, D_MODEL)),
            vec,
            vec,
        ],
        out_specs=pl.BlockSpec((1, tm, D_MODEL), lambda b, i: (b, i, 0)),
        scratch_shapes=[pltpu.VMEM((tm, D_MODEL), BF16)],
        compiler_params=pltpu.CompilerParams(dimension_semantics=("parallel", "parallel"),
                                             vmem_limit_bytes=VMEM_LIMIT),
        name="ffn",
    )(x, ssm, na, gqa, mod, w_out, ln1g, ln1b, w_ffn_in, w_ffn_out, ln2g, ln2b)


def _rope_tables():
    nf = HEAD_DIM // 4
    t = jnp.arange(DEC_SEQ)
    inv = ROPE_THETA ** (-jnp.arange(nf, dtype=F32) / nf)
    rows = (t // GRID_W).astype(F32)
    cols = (t % GRID_W).astype(F32)
    ang = jnp.stack([rows[:, None] * inv, cols[:, None] * inv], axis=1)
    cos = jnp.repeat(jnp.cos(ang)[:, :, None, :], 2, axis=2).reshape(DEC_SEQ, HEAD_DIM)
    sin = jnp.sin(ang)[:, :, None, :] * jnp.asarray([-1.0, 1.0], F32)[None, None, :, None]
    sin = sin.reshape(DEC_SEQ, HEAD_DIM)
    return jnp.tile(cos, (1, 2)), jnp.tile(sin, (1, 2))


def _head_mean_matrix():
    h = np.arange(GQA_WIDTH) // HEAD_DIM
    return jnp.asarray((h[:, None] == h[None, :]).astype(np.float32) / HEAD_DIM, BF16)


def kernel(x_prompt, x_sample, c, cache_na_k, cache_na_v, cache_gqa_k, cache_gqa_v, state_ssm_re, state_ssm_im,
           c_ctx, w_ada, b_ada, w_in, w_out, q_norm_g, k_norm_g, na_bias, ssm_lam_re, ssm_lam_im, ssm_log_dt,
           ssm_b_re, ssm_b_im, ssm_c_re, ssm_c_im, ssm_d, w_ssm_glu, ln1_g, ln1_b, ln2_g, ln2_b,
           w_ffn_in, w_ffn_out):
    cond8 = jnp.concatenate([c_ctx[None, :], c, jnp.zeros((8 - 1 - DEC_BATCH, D_MODEL), F32)], axis=0)
    mod = _modulation(cond8, w_ada, b_ada).reshape(DEPTH, 8, 6, D_MODEL)
    cos, sin = _rope_tables()
    hm = _head_mean_matrix()
    w_in_b = w_in.astype(BF16)
    w_out_b = w_out.astype(BF16)
    w_ffn_in_b = w_ffn_in.astype(BF16)
    w_ffn_out_b = w_ffn_out.astype(BF16)
    w_glu_b = w_ssm_glu.astype(BF16)
    qg = jnp.tile(q_norm_g, (1, GQA_HEADS))[:, None, :]
    kg = jnp.tile(k_norm_g, (1, GQA_KV_HEADS))[:, None, :]
    dskip = ssm_d[:, None, :]
    ln = (ln1_g[:, None, :], ln1_b[:, None, :], ln2_g[:, None, :], ln2_b[:, None, :])
    *ssm_tabs, a1, a2 = _ssm_tables(ssm_lam_re, ssm_lam_im, ssm_log_dt, ssm_b_re, ssm_b_im, ssm_c_re, ssm_c_im)
    ops = _ssm_operators(ssm_tabs)
    a1 = a1.reshape(DEPTH, 2, STATE_LANES)
    a2 = a2.reshape(DEPTH, 2, STATE_LANES)
    na_tab = _na_bias_table(na_bias)
    ck_na = cache_na_k.reshape(DEC_BATCH, DEPTH, PAST_LEN, NA_WIDTH).astype(BF16)
    cv_na = cache_na_v.reshape(DEC_BATCH, DEPTH, PAST_LEN, NA_WIDTH).astype(BF16)
    kt_cache = cache_gqa_k.transpose(1, 0, 3, 4, 2).reshape(DEPTH, DEC_BATCH, GQA_KV_WIDTH, PAST_LEN).astype(BF16)
    cv_g = cache_gqa_v.transpose(1, 0, 3, 2, 4)
    v1_cache = jnp.concatenate([cv_g, jnp.ones(cv_g.shape[:-1] + (1,), F32),
                                jnp.zeros(cv_g.shape[:-1] + (LANES - HEAD_DIM - 1,), F32)], axis=-1).astype(BF16)
    s_re = state_ssm_re.astype(F32).transpose(1, 0, 2, 3, 4)
    s_im = state_ssm_im.astype(F32).transpose(1, 0, 2, 3, 4)
    h0s_lat = jnp.concatenate([s_re, s_im], axis=-1).reshape(DEPTH, DEC_BATCH, 2, STATE_LANES)
    h0p_lat = jnp.concatenate([s_im, s_re], axis=-1).reshape(DEPTH, DEC_BATCH, 2, STATE_LANES)
    h0_ctx = jnp.zeros((1, 2, STATE_LANES), F32)

    y_ctx = x_prompt.reshape(1, TOKENS, D_MODEL)
    y_lat = x_sample
    kv_na_l, kv_g_l, fin_l = [], [], []
    for l in range(DEPTH):
        mod_c = mod[l, 0:1]
        u, zb, kv_na, kv_g = _in_projection(l, y_ctx, mod_c, w_in_b, qg, kg, hm, cos, sin, latent=False)
        ssm_o, fin = _ssm(l, u, ops, a1, a2, h0_ctx, h0_ctx, dskip, w_glu_b, n_seq=BATCH, emit_final=True)
        na_o, gqa_o = _ctx_attention(zb)
        y_ctx = _out_ffn(l, y_ctx, ssm_o, na_o, gqa_o, mod_c, w_out_b, ln[0], ln[1], w_ffn_in_b, w_ffn_out_b,
                         ln[2], ln[3])
        kv_na_l.append(kv_na); kv_g_l.append(kv_g); fin_l.append(fin)

        mod_s = mod[l, 1:1 + DEC_BATCH]
        u, zb, kt_new, v1_new = _in_projection(l, y_lat, mod_s, w_in_b, qg, kg, hm, cos, sin, latent=True)
        ssm_o = _ssm(l, u, ops, a1, a2, h0s_lat[l], h0p_lat[l], dskip, w_glu_b, n_seq=1,
                     emit_final=False)[0]
        na_o = _na_latent(l, zb, ck_na, cv_na, na_tab)
        gqa_o = _gqa_latent(zb, kt_new, kt_cache[l], v1_new, v1_cache[l])
        y_lat = _out_ffn(l, y_lat, ssm_o, na_o, gqa_o, mod_s, w_out_b, ln[0], ln[1], w_ffn_in_b, w_ffn_out_b,
                         ln[2], ln[3])

    kv_na = jnp.stack(kv_na_l, axis=0).reshape(DEPTH, BATCH, SEQ, 2, NA_HEADS, HEAD_DIM)
    kv_g = jnp.stack(kv_g_l, axis=0).reshape(DEPTH, BATCH, SEQ, 2, GQA_KV_HEADS, HEAD_DIM)
    kv_na = kv_na.transpose(3, 1, 0, 2, 4, 5)
    kv_g = kv_g.transpose(3, 1, 0, 2, 4, 5)
    fin = jnp.stack(fin_l, axis=0).reshape(DEPTH, 2, BATCH, SSM_GROUPS, 2, SSM_STATE)
    fin = fin.transpose(4, 2, 0, 1, 3, 5)
    return (y_ctx.reshape(BATCH, SEQ, D_MODEL), y_lat, kv_na[0], kv_na[1], kv_g[0], kv_g[1], fin[0], fin[1])
```

```python
import functools
import math

import numpy as np
import jax
import jax.numpy as jnp
from jax import lax
from jax.experimental import pallas as pl
from jax.experimental.pallas import tpu as pltpu

F32 = jnp.float32
BF16 = jnp.bfloat16

D_MODEL = 1024
BATCH = 16
SEQ = 256
DEPTH = 2
DEC_BATCH = 2
DEC_SEQ = 4096
PAST_LEN = 256
GRID_W = 64
GRID_H = DEC_SEQ // GRID_W
HEAD_DIM = 64
SSM_WIDTH = 256
SSM_GROUP_CH = 16
SSM_GROUPS = 16
SSM_STATE = 64
NA_HEADS = 4
NA_WIDTH = 256
NA_WIN_R = 8
NA_WIN_C = 16
GQA_HEADS = 8
GQA_KV_HEADS = 2
GQA_REP = 4
GQA_WIDTH = 512
GQA_KV_WIDTH = 128
IN_WIDTH = 1792
D_FF = 2816
ROPE_THETA = 10000.0
LN_EPS = 1e-6
RMS_EPS = 1e-6
DEEPNORM_ALPHA = (2 * DEPTH) ** 0.25
QK_SCALE = HEAD_DIM ** -0.5
LOG2E = math.log2(math.e)

TOKENS = 4096
SSM_CHUNK = 16
N_CHUNK = TOKENS // SSM_CHUNK
Z_WIDTH = IN_WIDTH - SSM_WIDTH
Z_QNA, Z_KNA, Z_VNA, Z_QG, Z_KG, Z_VG = 0, 256, 512, 768, 1280, 1408

VMEM_LIMIT = 56 * 1024 * 1024
LANES = 128


def _sigmoid(x):
    return 1.0 / (1.0 + jnp.exp(-x))


def _ln(x):
    mu = jnp.mean(x, axis=-1, keepdims=True)
    xc = x - mu
    var = jnp.mean(xc * xc, axis=-1, keepdims=True)
    return xc * lax.rsqrt(var + LN_EPS)


def _dot(a, b):
    return jnp.dot(a, b, preferred_element_type=F32)


def _dot_nt(a, b):
    return lax.dot_general(a, b, (((1,), (1,)), ((), ())), preferred_element_type=F32)


def _mod_kernel(c_ref, w_ref, b_ref, o_ref):
    c = c_ref[...]
    s = c * _sigmoid(c)
    o_ref[0] = jnp.dot(s, w_ref[0], preferred_element_type=F32, precision=lax.Precision.HIGHEST) + b_ref[0]


def _modulation(cond8, w_ada, b_ada):
    nb = 1536
    return pl.pallas_call(
        _mod_kernel,
        out_shape=jax.ShapeDtypeStruct((DEPTH, 8, 6 * D_MODEL), F32),
        grid=(DEPTH, 6 * D_MODEL // nb),
        in_specs=[
            pl.BlockSpec((8, D_MODEL), lambda l, j: (0, 0)),
            pl.BlockSpec((1, D_MODEL, nb), lambda l, j: (l, 0, j)),
            pl.BlockSpec((1, 1, nb), lambda l, j: (l, 0, j)),
        ],
        out_specs=pl.BlockSpec((1, 8, nb), lambda l, j: (l, 0, j)),
        compiler_params=pltpu.CompilerParams(dimension_semantics=("parallel", "parallel"),
                                             vmem_limit_bytes=VMEM_LIMIT),
        name="mod",
    )(cond8, w_ada, b_ada.reshape(DEPTH, 1, 6 * D_MODEL))


def _swap16(x):
    w = x.shape[-1]
    lane = lax.broadcasted_iota(jnp.int32, x.shape, x.ndim - 1)
    return jnp.where((lane & 16) != 0, pltpu.roll(x, 16, x.ndim - 1), pltpu.roll(x, w - 16, x.ndim - 1))


def _win_kernel(x_ref, mod_ref, w_ref, qg_ref, kg_ref, hm_ref, cos_ref, sin_ref, *out_refs, latent):
    u_ref, z_ref = out_refs[0], out_refs[1]
    x = x_ref[0]
    h = _ln(x) * (1.0 + mod_ref[0, 1:2, :]) + mod_ref[0, 0:1, :]
    z = _dot(h.astype(BF16), w_ref[0])
    u_ref[0, 0] = z[:, 0:LANES]
    u_ref[0, 1] = z[:, LANES:SSM_WIDTH]
    q_na = z[:, 256:512] * QK_SCALE
    kv_na = z[:, 512:1024]
    q_g = z[:, 1024:1536]
    k_g = z[:, 1536:1664]
    v_g = z[:, 1664:1792]
    q_ms = _dot((q_g * q_g).astype(BF16), hm_ref[...])
    k_ms = _dot((k_g * k_g).astype(BF16), hm_ref[0:GQA_KV_WIDTH, 0:GQA_KV_WIDTH])
    q_n = q_g * lax.rsqrt(q_ms + RMS_EPS) * qg_ref[0]
    k_n = k_g * lax.rsqrt(k_ms + RMS_EPS) * kg_ref[0]
    if latent:
        kt_ref, v1_ref = out_refs[2], out_refs[3]
        cos = cos_ref[...]
        sin = sin_ref[...]
        cos4 = jnp.concatenate([cos] * 4, axis=1)
        sin4 = jnp.concatenate([sin] * 4, axis=1)
        q_n = (q_n * cos4 + _swap16(q_n) * sin4) * (QK_SCALE * LOG2E)
        k_n = k_n * cos + _swap16(k_n) * sin
        kt_ref[0] = k_n.T.astype(BF16)
        lane = lax.broadcasted_iota(jnp.int32, v_g.shape, 1)
        ones_col = jnp.where(lane == HEAD_DIM, 1.0, 0.0)
        v1_ref[0, 0] = jnp.where(lane < HEAD_DIM, v_g, ones_col).astype(BF16)
        v1_ref[0, 1] = jnp.where(lane < HEAD_DIM, pltpu.roll(v_g, HEAD_DIM, 1), ones_col).astype(BF16)
    else:
        out_refs[2][0] = kv_na
        out_refs[3][0] = jnp.concatenate([k_n, v_g], axis=1)
        q_n = q_n * QK_SCALE
    zb = jnp.concatenate([q_na, kv_na, q_n, k_n, v_g], axis=1)
    z_ref[0] = zb.astype(BF16)


def _in_projection(l, x, mod, w_in, qg, kg, hm, cos, sin, *, latent):
    bg = x.shape[0]
    tm = 512
    out_shape = [jax.ShapeDtypeStruct((bg, 2, TOKENS, LANES), F32),
                 jax.ShapeDtypeStruct((bg, TOKENS, Z_WIDTH), BF16)]
    out_specs = [pl.BlockSpec((1, 2, tm, LANES), lambda b, i: (b, 0, i, 0)),
                 pl.BlockSpec((1, tm, Z_WIDTH), lambda b, i: (b, i, 0))]
    if latent:
        out_shape += [jax.ShapeDtypeStruct((bg, GQA_KV_WIDTH, TOKENS), BF16),
                      jax.ShapeDtypeStruct((bg, GQA_KV_HEADS, TOKENS, LANES), BF16)]
        out_specs += [pl.BlockSpec((1, GQA_KV_WIDTH, tm), lambda b, i: (b, 0, i)),
                      pl.BlockSpec((1, GQA_KV_HEADS, tm, LANES), lambda b, i: (b, 0, i, 0))]
    else:
        out_shape += [jax.ShapeDtypeStruct((bg, TOKENS, 2 * NA_WIDTH), F32),
                      jax.ShapeDtypeStruct((bg, TOKENS, 2 * GQA_KV_WIDTH), F32)]
        out_specs += [pl.BlockSpec((1, tm, 2 * NA_WIDTH), lambda b, i: (b, i, 0)),
                      pl.BlockSpec((1, tm, 2 * GQA_KV_WIDTH), lambda b, i: (b, i, 0))]
    return pl.pallas_call(
        functools.partial(_win_kernel, latent=latent),
        out_shape=out_shape,
        grid=(bg, TOKENS // tm),
        in_specs=[
            pl.BlockSpec((1, tm, D_MODEL), lambda b, i: (b, i, 0)),
            pl.BlockSpec((1, 6, D_MODEL), lambda b, i: (b, 0, 0)),
            pl.BlockSpec((1, D_MODEL, IN_WIDTH), lambda b, i: (l, 0, 0)),
            pl.BlockSpec((1, 1, GQA_WIDTH), lambda b, i: (l, 0, 0)),
            pl.BlockSpec((1, 1, GQA_KV_WIDTH), lambda b, i: (l, 0, 0)),
            pl.BlockSpec((GQA_WIDTH, GQA_WIDTH), lambda b, i: (0, 0)),
            pl.BlockSpec((tm, LANES), lambda b, i: (i, 0)),
            pl.BlockSpec((tm, LANES), lambda b, i: (i, 0)),
        ],
        out_specs=out_specs,
        compiler_params=pltpu.CompilerParams(dimension_semantics=("parallel", "parallel"),
                                             vmem_limit_bytes=VMEM_LIMIT),
        name="win_lat" if latent else "win_ctx",
    )(x, mod, w_in, qg, kg, hm, cos, sin)


STATE_LANES = SSM_GROUPS * LANES


def _ssm_ops_kernel(crt_ref, cit_ref, pkr_ref, pki_ref, pcr_ref, pci_ref, bb1_ref, bb2_ref, pwr_ref, pwi_ref,
                    m_ref, bend_ref, cpow_ref):
    d = pl.program_id(1)

    def build_ops(g, carry):
        cr = crt_ref[0, 0, g]
        ci = cit_ref[0, 0, g]

        def re_proj(pr, pi):
            return jnp.concatenate([cr * pr - ci * pi, -(cr * pi + ci * pr)], axis=0)
        cpow_ref[0, 0, g] = re_proj(pcr_ref[0, 0, g], pci_ref[0, 0, g]).astype(BF16)
        bb1 = bb1_ref[0, 0, g]
        bb2 = bb2_ref[0, 0, g]
        kcat = jnp.dot(bb1[:, :LANES], re_proj(pkr_ref[0, 0, g], pki_ref[0, 0, g]),
                       preferred_element_type=F32, precision=lax.Precision.HIGHEST)
        lane = lax.broadcasted_iota(jnp.int32, kcat.shape, 1)
        for t in range(SSM_CHUNK):
            rows = slice(t * SSM_GROUP_CH, (t + 1) * SSM_GROUP_CH)
            lo, hi = t * SSM_GROUP_CH, (t + 1) * SSM_GROUP_CH
            fwd = kcat if t == 0 else jnp.where(lane >= lo, pltpu.roll(kcat, lo, 1), 0.0)
            bwd = kcat if hi == 256 else jnp.where(lane < hi, pltpu.roll(kcat, hi, 1), 0.0)
            m_ref[0, 0, g, rows, :] = jnp.where(d == 0, fwd, bwd).astype(BF16)
            bend_ref[0, 0, g, rows, :] = (pwr_ref[0, 0, g, t:t + 1, :] * bb1
                                          + pwi_ref[0, 0, g, t:t + 1, :] * bb2).astype(BF16)
        return carry
    lax.fori_loop(0, SSM_GROUPS, build_ops, 0)


def _ssm_operators(tables):
    per_dir = lambda rows: pl.BlockSpec((1, 1, SSM_GROUPS, rows, 256), lambda l, d: (l, d, 0, 0, 0))
    shape = lambda rows: jax.ShapeDtypeStruct((DEPTH, 2, SSM_GROUPS, rows, 256), BF16)
    return pl.pallas_call(
        _ssm_ops_kernel,
        out_shape=[shape(256), shape(256), shape(LANES)],
        grid=(DEPTH, 2),
        in_specs=[per_dir(SSM_STATE)] * 6 + [per_dir(SSM_GROUP_CH)] * 4,
        out_specs=[per_dir(256), per_dir(256), per_dir(LANES)],
        compiler_params=pltpu.CompilerParams(dimension_semantics=("parallel", "parallel"),
                                             vmem_limit_bytes=VMEM_LIMIT),
        name="ssm_ops",
    )(*tables)


def _ssm_kernel(u_ref, m_ref, bend_ref, cpow_ref, a1_ref, a2_ref, h0s_ref, h0p_ref, d_ref, wglu_ref, *refs,
                n_seq, emit_final):
    if emit_final:
        out_ref, fin_ref, y_s, yt_s, u_s, inja_s, injb_s, hp_s = refs
    else:
        out_ref, y_s, yt_s, u_s, inja_s, injb_s, hp_s = refs
        fin_ref = None
    chunks_per_seq = N_CHUNK // n_seq

    def chunk_rows(t):
        sl = pl.ds(t, N_CHUNK, stride=SSM_CHUNK)
        return jnp.concatenate([u_ref[0, 0, sl, :], u_ref[0, 1, sl, :]], axis=1)

    for t in range(SSM_CHUNK):
        a_t = chunk_rows(t)
        yt_s[:, t * SSM_GROUP_CH:(t + 1) * SSM_GROUP_CH, :] = a_t.T.reshape(SSM_GROUPS, SSM_GROUP_CH, N_CHUNK)

    def tr_in(g, carry):
        u_s[g] = yt_s[g].T.astype(BF16)
        return carry
    lax.fori_loop(0, SSM_GROUPS, tr_in, 0)

    for d in range(2):
        def proj(g, carry, d=d):
            ug = u_s[g]
            y = _dot(ug, m_ref[0, d, g])
            y_s[g] = y if d == 0 else y_s[g] + y
            inj = _dot(ug, bend_ref[0, d, g])
            lanes = pl.ds(pl.multiple_of(g * LANES, LANES), LANES)
            inja_s[d, :, lanes] = inj[:, :LANES]
            injb_s[d, :, lanes] = inj[:, LANES:]
            return carry
        lax.fori_loop(0, SSM_GROUPS, proj, 0)

    a1 = a1_ref[0]
    a2 = a2_ref[0]

    def step(j, carry):
        s, sp = carry
        nf = j
        nb = N_CHUNK - 1 - j
        if n_seq > 1:
            keep = jnp.where(j % chunks_per_seq != 0, 1.0, 0.0).astype(F32)
            s = s * keep
            sp = sp * keep
        hp_s[0, pl.ds(nf, 1), :] = s[0:1]
        hp_s[1, pl.ds(nb, 1), :] = s[1:2]
        ia = jnp.concatenate([inja_s[0, pl.ds(nf, 1), :], inja_s[1, pl.ds(nb, 1), :]], axis=0)
        ib = jnp.concatenate([injb_s[0, pl.ds(nf, 1), :], injb_s[1, pl.ds(nb, 1), :]], axis=0)
        s_new = a1 * s + a2 * sp + ia
        sp_new = a1 * sp - a2 * s + ib
        if emit_final:
            @pl.when(j % chunks_per_seq == chunks_per_seq - 1)
            def _():
                sq = j // chunks_per_seq
                fin_ref[0, 0, pl.ds(sq, 1), :] = s_new[0:1]
                fin_ref[0, 1, pl.ds(n_seq - 1 - sq, 1), :] = s_new[1:2]
        return s_new, sp_new
    lax.fori_loop(0, N_CHUNK, step, (h0s_ref[0], h0p_ref[0]))

    for d in range(2):
        def carry_in(g, carry, d=d):
            hp = hp_s[d, :, pl.ds(pl.multiple_of(g * LANES, LANES), LANES)]
            y_s[g] = y_s[g] + _dot(hp.astype(BF16), cpow_ref[0, d, g])
            return carry
        lax.fori_loop(0, SSM_GROUPS, carry_in, 0)

    def tr_out(g, carry):
        yt_s[g] = y_s[g].T
        return carry
    lax.fori_loop(0, SSM_GROUPS, tr_out, 0)
    dskip = d_ref[0]
    wglu = wglu_ref[0]
    for t in range(SSM_CHUNK):
        blk = yt_s[:, t * SSM_GROUP_CH:(t + 1) * SSM_GROUP_CH, :].reshape(SSM_WIDTH, N_CHUNK)
        y = blk.T + chunk_rows(t) * dskip
        gl = jax.nn.gelu(y, approximate=True)
        gate = _dot(gl.astype(BF16), wglu)
        o = gl * _sigmoid(gate)
        out_ref[0, 0, pl.ds(t, N_CHUNK, stride=SSM_CHUNK), :] = o[:, :LANES]
        out_ref[0, 1, pl.ds(t, N_CHUNK, stride=SSM_CHUNK), :] = o[:, LANES:]


def _ssm(l, u, ops, a1, a2, h0s, h0p, dskip, wglu, *, n_seq, emit_final):
    bg = u.shape[0]
    m, bend, cpow = ops
    out_shape = [jax.ShapeDtypeStruct((bg, 2, TOKENS, LANES), F32)]
    out_specs = [pl.BlockSpec((1, 2, TOKENS, LANES), lambda b: (b, 0, 0, 0))]
    if emit_final:
        out_shape.append(jax.ShapeDtypeStruct((bg, 2, n_seq, STATE_LANES), F32))
        out_specs.append(pl.BlockSpec((1, 2, n_seq, STATE_LANES), lambda b: (b, 0, 0, 0)))
    resident = lambda rows: pl.BlockSpec((1, 2, SSM_GROUPS, rows, 256), lambda b: (l, 0, 0, 0, 0),
                                         pipeline_mode=pl.Buffered(1))
    return pl.pallas_call(
        functools.partial(_ssm_kernel, n_seq=n_seq, emit_final=emit_final),
        out_shape=out_shape,
        grid=(bg,),
        in_specs=[
            pl.BlockSpec((1, 2, TOKENS, LANES), lambda b: (b, 0, 0, 0)),
            resident(256), resident(256), resident(LANES),
            pl.BlockSpec((1, 2, STATE_LANES), lambda b: (l, 0, 0)),
            pl.BlockSpec((1, 2, STATE_LANES), lambda b: (l, 0, 0)),
            pl.BlockSpec((1, 2, STATE_LANES), lambda b: (b, 0, 0)),
            pl.BlockSpec((1, 2, STATE_LANES), lambda b: (b, 0, 0)),
            pl.BlockSpec((1, 1, SSM_WIDTH), lambda b: (l, 0, 0)),
            pl.BlockSpec((1, SSM_WIDTH, SSM_WIDTH), lambda b: (l, 0, 0)),
        ],
        out_specs=out_specs,
        scratch_shapes=[pltpu.VMEM((SSM_GROUPS, N_CHUNK, 256), F32),
                        pltpu.VMEM((SSM_GROUPS, 256, N_CHUNK), F32),
                        pltpu.VMEM((SSM_GROUPS, N_CHUNK, 256), BF16),
                        pltpu.VMEM((2, N_CHUNK, STATE_LANES), F32),
                        pltpu.VMEM((2, N_CHUNK, STATE_LANES), F32),
                        pltpu.VMEM((2, N_CHUNK, STATE_LANES), F32)],
        compiler_params=pltpu.CompilerParams(dimension_semantics=("parallel",), vmem_limit_bytes=VMEM_LIMIT),
        name="ssm_ctx" if emit_final else "ssm_lat",
    )(u, m, bend, cpow, a1, a2, h0s, h0p, dskip, wglu)


def _ssm_tables(lam_re, lam_im, log_dt, b_re, b_im, c_re, c_im):
    t_n = SSM_CHUNK
    lr = jnp.minimum(lam_re.astype(F32), -1e-4)
    li = lam_im.astype(F32)
    dt = jnp.exp(log_dt.astype(F32))[..., None]
    lr_dt, li_dt = lr * dt, li * dt

    def power(k, expand=lambda x: x):
        mag = jnp.exp(expand(lr_dt) * k)
        return mag * jnp.cos(expand(li_dt) * k), mag * jnp.sin(expand(li_dt) * k)
    l_r, l_i = power(1.0)
    a_r, a_i = power(float(t_n))
    n_r, n_i = l_r - 1.0, l_i
    den = lr * lr + li * li
    q_r, q_i = (n_r * lr + n_i * li) / den, (n_i * lr - n_r * li) / den
    bb_r = q_r[..., None] * b_re - q_i[..., None] * b_im
    bb_i = q_r[..., None] * b_im + q_i[..., None] * b_re

    up = np.arange(t_n, dtype=np.float32)
    down = up[::-1].copy()
    per_dir = lambda fwd, bwd: jnp.asarray(np.stack([fwd, bwd]))[None, :, None, None, :]
    on_lanes = lambda x: x[..., None]
    spread = lambda k: jnp.repeat(k, SSM_GROUP_CH, axis=-1)
    pk_r, pk_i = power(spread(per_dir(up, down)), on_lanes)
    pc_r, pc_i = power(spread(per_dir(up + 1.0, down + 1.0)), on_lanes)
    pw_r, pw_i = power(jnp.swapaxes(per_dir(down, up), -1, -2), lambda x: x[..., None, :])
    pw_r, pw_i = jnp.tile(pw_r, (1, 1, 1, 1, 4)), jnp.tile(pw_i, (1, 1, 1, 1, 4))
    cr_t = jnp.tile(jnp.swapaxes(c_re.astype(F32), -1, -2), (1, 1, 1, 1, t_n))
    ci_t = jnp.tile(jnp.swapaxes(c_im.astype(F32), -1, -2), (1, 1, 1, 1, t_n))
    br_t, bi_t = jnp.swapaxes(bb_r, -1, -2), jnp.swapaxes(bb_i, -1, -2)
    bb1 = jnp.concatenate([br_t, bi_t, bi_t, br_t], axis=-1)
    bb2 = jnp.concatenate([-bi_t, br_t, br_t, -bi_t], axis=-1)
    a1 = jnp.concatenate([a_r, a_r], axis=-1)
    a2 = jnp.concatenate([-a_i, a_i], axis=-1)
    return cr_t, ci_t, pk_r, pk_i, pc_r, pc_i, bb1, bb2, pw_r, pw_i, a1, a2


def _attend_transposed(k, v, q, n_kv, rep):
    nq = q.shape[0]
    qt = q.astype(F32).T
    zero = jnp.zeros((HEAD_DIM, nq), F32)
    bd = jnp.concatenate(
        [jnp.concatenate([qt[(g * rep + r) * HEAD_DIM:(g * rep + r + 1) * HEAD_DIM] if g == gg else zero
                          for gg in range(n_kv) for r in range(rep)], axis=1)
         for g in range(n_kv)], axis=0).astype(BF16)
    s = _dot(k, bd)
    m = jnp.max(s, axis=0, keepdims=True)
    p = jnp.exp(s - m)
    p = p * (1.0 / jnp.sum(p, axis=0, keepdims=True))
    ot = _dot(v.astype(F32).T.astype(BF16), p.astype(BF16))
    heads = [ot[g * HEAD_DIM:(g + 1) * HEAD_DIM, (g * rep + r) * nq:(g * rep + r + 1) * nq]
             for g in range(n_kv) for r in range(rep)]
    return jnp.concatenate(heads, axis=0).T


def _ctx_attn_kernel(z_ref, na_ref, gqa_ref):
    z = z_ref[0]
    na_ref[0] = _attend_transposed(z[:, Z_KNA:Z_KNA + NA_WIDTH], z[:, Z_VNA:Z_VNA + NA_WIDTH],
                                   z[:, Z_QNA:Z_QNA + NA_WIDTH], NA_HEADS, 1).astype(BF16)
    gqa_ref[0] = _attend_transposed(z[:, Z_KG:Z_KG + GQA_KV_WIDTH], z[:, Z_VG:Z_VG + GQA_KV_WIDTH],
                                    z[:, Z_QG:Z_QG + GQA_WIDTH], GQA_KV_HEADS, GQA_REP).astype(BF16)


def _ctx_attention(zb):
    z3 = zb.reshape(BATCH, SEQ, Z_WIDTH)
    na, gqa = pl.pallas_call(
        _ctx_attn_kernel,
        out_shape=[jax.ShapeDtypeStruct((BATCH, SEQ, NA_WIDTH), BF16),
                   jax.ShapeDtypeStruct((BATCH, SEQ, GQA_WIDTH), BF16)],
        grid=(BATCH,),
        in_specs=[pl.BlockSpec((1, SEQ, Z_WIDTH), lambda b: (b, 0, 0))],
        out_specs=[pl.BlockSpec((1, SEQ, NA_WIDTH), lambda b: (b, 0, 0)),
                   pl.BlockSpec((1, SEQ, GQA_WIDTH), lambda b: (b, 0, 0))],
        compiler_params=pltpu.CompilerParams(dimension_semantics=("parallel",), vmem_limit_bytes=VMEM_LIMIT),
        name="attn_ctx",
    )(z3)
    return na.reshape(1, TOKENS, NA_WIDTH), gqa.reshape(1, TOKENS, GQA_WIDTH)


NA_KEYS = NA_WIN_R * GRID_W


NA_ROWS = 8


def _na_lat_kernel(q_ref, k_ref, v_ref, ck_ref, cv_ref, *refs):
    bias_refs, o_ref = refs[:NA_ROWS], refs[NA_ROWS]
    i = pl.program_id(1)
    ck = ck_ref[0, 0]
    cv = cv_ref[0, 0]
    qt = q_ref[0].astype(F32).T
    hd = lax.broadcasted_iota(jnp.int32, (NA_WIDTH, NA_WIDTH), 0) // HEAD_DIM
    hq = lax.broadcasted_iota(jnp.int32, (NA_WIDTH, NA_WIDTH), 1) // HEAD_DIM
    for j in range(NA_ROWS):
        r = NA_ROWS * i + j
        rs = jnp.clip(r - NA_WIN_R // 2, 0, GRID_H - NA_WIN_R)
        start = pl.multiple_of(rs * GRID_W, GRID_W)
        kw = k_ref[0, pl.ds(start, NA_KEYS), :]
        vw = v_ref[0, pl.ds(start, NA_KEYS), :]
        qj = qt[:, j * GRID_W:(j + 1) * GRID_W]
        bd = jnp.where(hd == hq, jnp.concatenate([qj] * NA_HEADS, axis=1), 0.0).astype(BF16)
        s_loc = _dot(kw, bd) + bias_refs[j][0, 0]
        s_ctx = _dot(ck, bd)
        m = jnp.maximum(jnp.max(s_loc, axis=0, keepdims=True), jnp.max(s_ctx, axis=0, keepdims=True))
        p_loc = jnp.exp(s_loc - m)
        p_ctx = jnp.exp(s_ctx - m)
        inv = 1.0 / (jnp.sum(p_loc, axis=0, keepdims=True) + jnp.sum(p_ctx, axis=0, keepdims=True))
        full = (_dot((p_loc * inv).T.astype(BF16), vw)
                + _dot((p_ctx * inv).T.astype(BF16), cv))
        out = jnp.concatenate([full[h * HEAD_DIM:(h + 1) * HEAD_DIM, h * HEAD_DIM:(h + 1) * HEAD_DIM]
                               for h in range(NA_HEADS)], axis=1)
        o_ref[0, j * GRID_W:(j + 1) * GRID_W, :] = out.astype(BF16)


def _na_row_class(r):
    return r - jnp.clip(r - NA_WIN_R // 2, 0, GRID_H - NA_WIN_R)


def _na_latent(l, zb, ck, cv, bias):
    tq = NA_ROWS * GRID_W
    bias_specs = [pl.BlockSpec((1, 1, NA_KEYS, NA_WIDTH),
                               functools.partial(lambda b, i, j: (l, _na_row_class(NA_ROWS * i + j), 0, 0), j=j))
                  for j in range(NA_ROWS)]
    return pl.pallas_call(
        _na_lat_kernel,
        out_shape=jax.ShapeDtypeStruct((DEC_BATCH, TOKENS, NA_WIDTH), BF16),
        grid=(DEC_BATCH, GRID_H // NA_ROWS),
        in_specs=[
            pl.BlockSpec((1, tq, NA_WIDTH), lambda b, i: (b, i, Z_QNA // NA_WIDTH)),
            pl.BlockSpec((1, TOKENS, NA_WIDTH), lambda b, i: (b, 0, Z_KNA // NA_WIDTH)),
            pl.BlockSpec((1, TOKENS, NA_WIDTH), lambda b, i: (b, 0, Z_VNA // NA_WIDTH)),
            pl.BlockSpec((1, 1, PAST_LEN, NA_WIDTH), lambda b, i: (b, l, 0, 0)),
            pl.BlockSpec((1, 1, PAST_LEN, NA_WIDTH), lambda b, i: (b, l, 0, 0)),
        ] + bias_specs,
        out_specs=pl.BlockSpec((1, tq, NA_WIDTH), lambda b, i: (b, i, 0)),
        compiler_params=pltpu.CompilerParams(dimension_semantics=("parallel", "arbitrary"),
                                             vmem_limit_bytes=VMEM_LIMIT),
        name="na_lat",
    )(zb, zb, zb, ck, cv, *([bias] * NA_ROWS))


def _na_bias_table(tab):
    q = np.arange(GRID_W)
    kc = np.arange(GRID_W)
    cs = np.clip(q - NA_WIN_C // 2, 0, GRID_W - NA_WIN_C)
    valid = (kc[:, None] >= cs[None, :]) & (kc[:, None] < cs[None, :] + NA_WIN_C)
    w = GRID_W - 1
    rev = jnp.flip(tab.astype(F32).transpose(0, 2, 1, 3), axis=-1)
    padded = jnp.pad(rev, ((0, 0), (0, 0), (0, 0), (w, w)))
    off = w + NA_WIN_C - 1
    cols = jnp.stack([padded[..., off - i:off - i + GRID_W] for i in range(GRID_W)], axis=2)
    cols = jnp.where(valid[:, None, :], cols, -jnp.inf)
    b = jnp.stack([cols[:, NA_WIN_R - 1 - c:2 * NA_WIN_R - 1 - c] for c in range(NA_WIN_R)], axis=1)
    return b.reshape(DEPTH, NA_WIN_R, NA_KEYS, NA_WIDTH)


GQA_TK = 256
GQA_KEYS = DEC_SEQ + PAST_LEN


def _gqa_kernel(q_ref, ktn_ref, ktc_ref, vn_ref, vc_ref, o_ref, s_s, m_s, *, tq):
    q = q_ref[0]
    n_new = DEC_SEQ // GQA_TK
    half = GQA_REP * tq // 2
    qh = [jnp.concatenate([q[:, r * HEAD_DIM:(r + 1) * HEAD_DIM] for r in (2 * i, 2 * i + 1)], axis=0)
          for i in range(2)]
    rows = [slice(i * half, (i + 1) * half) for i in range(2)]

    def fold(s):
        return jnp.maximum(s[:, :LANES], s[:, LANES:])

    for i in range(2):
        s = _dot(qh[i], ktc_ref[0])
        s_s[rows[i], DEC_SEQ:GQA_KEYS] = s
        m_s[rows[i], :] = fold(s)
    for c in range(0, n_new, 2):
        for i in range(2):
            s0 = _dot(qh[i], ktn_ref[0, :, c * GQA_TK:(c + 1) * GQA_TK])
            s1 = _dot(qh[i], ktn_ref[0, :, (c + 1) * GQA_TK:(c + 2) * GQA_TK])
            s_s[rows[i], c * GQA_TK:(c + 1) * GQA_TK] = s0
            s_s[rows[i], (c + 1) * GQA_TK:(c + 2) * GQA_TK] = s1
            m_s[rows[i], :] = jnp.maximum(m_s[rows[i], :], jnp.maximum(fold(s0), fold(s1)))

    m = jnp.max(m_s[...], axis=-1, keepdims=True)
    m_s[...] = jnp.broadcast_to(m, m_s.shape)

    def probs(i, c0):
        mb = m_s[rows[i], :]
        return jnp.exp2(s_s[rows[i], c0:c0 + GQA_TK] - jnp.concatenate([mb, mb], axis=1)).astype(BF16)

    acc = [_dot(probs(i, DEC_SEQ), vc_ref[0, 0]) for i in range(2)]
    for c in range(n_new):
        for i in range(2):
            acc[i] = acc[i] + _dot(probs(i, c * GQA_TK), vn_ref[0, 0, c * GQA_TK:(c + 1) * GQA_TK, :])
    outs = []
    for i in range(2):
        out = acc[i][:, :HEAD_DIM] / acc[i][:, HEAD_DIM:HEAD_DIM + 1]
        outs += [out[:tq], out[tq:]]
    o_ref[0] = jnp.concatenate(outs, axis=1).astype(BF16)


def _gqa_latent(zb, kt_new, kt_cache, v1_new, v1_cache):
    tq = 256
    rows = GQA_REP * tq
    return pl.pallas_call(
        functools.partial(_gqa_kernel, tq=tq),
        out_shape=jax.ShapeDtypeStruct((DEC_BATCH, TOKENS, GQA_WIDTH), BF16),
        grid=(DEC_BATCH, GQA_KV_HEADS, TOKENS // tq),
        in_specs=[
            pl.BlockSpec((1, tq, 256), lambda b, g, i: (b, i, Z_QG // 256 + g)),
            pl.BlockSpec((1, HEAD_DIM, DEC_SEQ), lambda b, g, i: (b, g, 0)),
            pl.BlockSpec((1, HEAD_DIM, PAST_LEN), lambda b, g, i: (b, g, 0)),
            pl.BlockSpec((1, 1, DEC_SEQ, LANES), lambda b, g, i: (b, g, 0, 0)),
            pl.BlockSpec((1, 1, PAST_LEN, LANES), lambda b, g, i: (b, g, 0, 0)),
        ],
        out_specs=pl.BlockSpec((1, tq, 256), lambda b, g, i: (b, i, g)),
        scratch_shapes=[pltpu.VMEM((rows, GQA_KEYS), F32), pltpu.VMEM((rows, LANES), F32)],
        compiler_params=pltpu.CompilerParams(dimension_semantics=("parallel", "parallel", "arbitrary"),
                                             vmem_limit_bytes=VMEM_LIMIT),
        name="gqa_lat",
    )(zb, kt_new, kt_cache, v1_new, v1_cache)


FF_CHUNK = 256


def _ffn_kernel(x_ref, ssm_ref, na_ref, gqa_ref, mod_ref, wout_ref, ln1g_ref, ln1b_ref, win_ref, wo_ref,
                ln2g_ref, ln2b_ref, o_ref, h2_s):
    ssm = jnp.concatenate([ssm_ref[0, 0], ssm_ref[0, 1]], axis=1)
    o = (_dot(ssm.astype(BF16), wout_ref[0, 0:SSM_WIDTH, :])
         + _dot(na_ref[0], wout_ref[0, SSM_WIDTH:SSM_WIDTH + NA_WIDTH, :])
         + _dot(gqa_ref[0], wout_ref[0, SSM_WIDTH + NA_WIDTH:, :]))
    y = DEEPNORM_ALPHA * x_ref[0] + mod_ref[0, 2:3, :] * o
    x1 = _ln(y) * ln1g_ref[0] + ln1b_ref[0]
    h2 = _ln(x1) * (1.0 + mod_ref[0, 4:5, :]) + mod_ref[0, 3:4, :]
    h2_s[...] = h2.astype(BF16)
    acc = None
    for j in range(D_FF // FF_CHUNK):
        h2b = h2_s[...]
        a = _dot(h2b, win_ref[0, :, j * FF_CHUNK:(j + 1) * FF_CHUNK])
        gt = _dot(h2b, win_ref[0, :, D_FF + j * FF_CHUNK:D_FF + (j + 1) * FF_CHUNK])
        f = (a * _sigmoid(a) * gt).astype(BF16)
        c = _dot(f, wo_ref[0, j * FF_CHUNK:(j + 1) * FF_CHUNK, :])
        acc = c if acc is None else acc + c
    y = DEEPNORM_ALPHA * x1 + mod_ref[0, 5:6, :] * acc
    o_ref[0] = _ln(y) * ln2g_ref[0] + ln2b_ref[0]


def _out_ffn(l, x, ssm, na, gqa, mod, w_out, ln1g, ln1b, w_ffn_in, w_ffn_out, ln2g, ln2b):
    bg = x.shape[0]
    tm = 512
    vec = pl.BlockSpec((1, 1, D_MODEL), lambda b, i: (l, 0, 0))
    resident = lambda shape: pl.BlockSpec((1,) + shape, lambda b, i: (l, 0, 0), pipeline_mode=pl.Buffered(1))
    return pl.pallas_call(
        _ffn_kernel,
        out_shape=jax.ShapeDtypeStruct((bg, TOKENS, D_MODEL), F32),
        grid=(bg, TOKENS // tm),
        in_specs=[
            pl.BlockSpec((1, tm, D_MODEL), lambda b, i: (b, i, 0)),
            pl.BlockSpec((1, 2, tm, LANES), lambda b, i: (b, 0, i, 0)),
            pl.BlockSpec((1, tm, NA_WIDTH), lambda b, i: (b, i, 0)),
            pl.BlockSpec((1, tm, GQA_WIDTH), lambda b, i: (b, i, 0)),
            pl.BlockSpec((1, 6, D_MODEL), lambda b, i: (b, 0, 0)),
            resident((D_MODEL, D_MODEL)),
            vec,
            vec,
            resident((D_MODEL, 2 * D_FF)),
            resident((D_FF, D_MODEL)),
            vec,
            vec,
        ],
        out_specs=pl.BlockSpec((1, tm, D_MODEL), lambda b, i: (b, i, 0)),
        scratch_shapes=[pltpu.VMEM((tm, D_MODEL), BF16)],
        compiler_params=pltpu.CompilerParams(dimension_semantics=("parallel", "parallel"),
                                             vmem_limit_bytes=VMEM_LIMIT),
        name="ffn",
    )(x, ssm, na, gqa, mod, w_out, ln1g, ln1b, w_ffn_in, w_ffn_out, ln2g, ln2b)


def _rope_tables():
    nf = HEAD_DIM // 4
    t = jnp.arange(DEC_SEQ)
    inv = ROPE_THETA ** (-jnp.arange(nf, dtype=F32) / nf)
    rows = (t // GRID_W).astype(F32)
    cols = (t % GRID_W).astype(F32)
    ang = jnp.stack([rows[:, None] * inv, cols[:, None] * inv], axis=1)
    cos = jnp.repeat(jnp.cos(ang)[:, :, None, :], 2, axis=2).reshape(DEC_SEQ, HEAD_DIM)
    sin = jnp.sin(ang)[:, :, None, :] * jnp.asarray([-1.0, 1.0], F32)[None, None, :, None]
    sin = sin.reshape(DEC_SEQ, HEAD_DIM)
    return jnp.tile(cos, (1, 2)), jnp.tile(sin, (1, 2))


def _head_mean_matrix():
    h = np.arange(GQA_WIDTH) // HEAD_DIM
    return jnp.asarray((h[:, None] == h[None, :]).astype(np.float32) / HEAD_DIM, BF16)


def kernel(x_prompt, x_sample, c, cache_na_k, cache_na_v, cache_gqa_k, cache_gqa_v, state_ssm_re, state_ssm_im,
           c_ctx, w_ada, b_ada, w_in, w_out, q_norm_g, k_norm_g, na_bias, ssm_lam_re, ssm_lam_im, ssm_log_dt,
           ssm_b_re, ssm_b_im, ssm_c_re, ssm_c_im, ssm_d, w_ssm_glu, ln1_g, ln1_b, ln2_g, ln2_b,
           w_ffn_in, w_ffn_out):
    cond8 = jnp.concatenate([c_ctx[None, :], c, jnp.zeros((8 - 1 - DEC_BATCH, D_MODEL), F32)], axis=0)
    mod = _modulation(cond8, w_ada, b_ada).reshape(DEPTH, 8, 6, D_MODEL)
    cos, sin = _rope_tables()
    hm = _head_mean_matrix()
    w_in_b = w_in.astype(BF16)
    w_out_b = w_out.astype(BF16)
    w_ffn_in_b = w_ffn_in.astype(BF16)
    w_ffn_out_b = w_ffn_out.astype(BF16)
    w_glu_b = w_ssm_glu.astype(BF16)
    qg = jnp.tile(q_norm_g, (1, GQA_HEADS))[:, None, :]
    kg = jnp.tile(k_norm_g, (1, GQA_KV_HEADS))[:, None, :]
    dskip = ssm_d[:, None, :]
    ln = (ln1_g[:, None, :], ln1_b[:, None, :], ln2_g[:, None, :], ln2_b[:, None, :])
    *ssm_tabs, a1, a2 = _ssm_tables(ssm_lam_re, ssm_lam_im, ssm_log_dt, ssm_b_re, ssm_b_im, ssm_c_re, ssm_c_im)
    ops = _ssm_operators(ssm_tabs)
    a1 = a1.reshape(DEPTH, 2, STATE_LANES)
    a2 = a2.reshape(DEPTH, 2, STATE_LANES)
    na_tab = _na_bias_table(na_bias)
    ck_na = cache_na_k.reshape(DEC_BATCH, DEPTH, PAST_LEN, NA_WIDTH).astype(BF16)
    cv_na = cache_na_v.reshape(DEC_BATCH, DEPTH, PAST_LEN, NA_WIDTH).astype(BF16)
    kt_cache = cache_gqa_k.transpose(1, 0, 3, 4, 2).reshape(DEPTH, DEC_BATCH, GQA_KV_WIDTH, PAST_LEN).astype(BF16)
    cv_g = cache_gqa_v.transpose(1, 0, 3, 2, 4)
    v1_cache = jnp.concatenate([cv_g, jnp.ones(cv_g.shape[:-1] + (1,), F32),
                                jnp.zeros(cv_g.shape[:-1] + (LANES - HEAD_DIM - 1,), F32)], axis=-1).astype(BF16)
    s_re = state_ssm_re.astype(F32).transpose(1, 0, 2, 3, 4)
    s_im = state_ssm_im.astype(F32).transpose(1, 0, 2, 3, 4)
    h0s_lat = jnp.concatenate([s_re, s_im], axis=-1).reshape(DEPTH, DEC_BATCH, 2, STATE_LANES)
    h0p_lat = jnp.concatenate([s_im, s_re], axis=-1).reshape(DEPTH, DEC_BATCH, 2, STATE_LANES)
    h0_ctx = jnp.zeros((1, 2, STATE_LANES), F32)

    y_ctx = x_prompt.reshape(1, TOKENS, D_MODEL)
    y_lat = x_sample
    kv_na_l, kv_g_l, fin_l = [], [], []
    for l in range(DEPTH):
        mod_c = mod[l, 0:1]
        u, zb, kv_na, kv_g = _in_projection(l, y_ctx, mod_c, w_in_b, qg, kg, hm, cos, sin, latent=False)
        ssm_o, fin = _ssm(l, u, ops, a1, a2, h0_ctx, h0_ctx, dskip, w_glu_b, n_seq=BATCH, emit_final=True)
        na_o, gqa_o = _ctx_attention(zb)
        y_ctx = _out_ffn(l, y_ctx, ssm_o, na_o, gqa_o, mod_c, w_out_b, ln[0], ln[1], w_ffn_in_b, w_ffn_out_b,
                         ln[2], ln[3])
        kv_na_l.append(kv_na); kv_g_l.append(kv_g); fin_l.append(fin)

        mod_s = mod[l, 1:1 + DEC_BATCH]
        u, zb, kt_new, v1_new = _in_projection(l, y_lat, mod_s, w_in_b, qg, kg, hm, cos, sin, latent=True)
        ssm_o = _ssm(l, u, ops, a1, a2, h0s_lat[l], h0p_lat[l], dskip, w_glu_b, n_seq=1,
                     emit_final=False)[0]
        na_o = _na_latent(l, zb, ck_na, cv_na, na_tab)
        gqa_o = _gqa_latent(zb, kt_new, kt_cache[l], v1_new, v1_cache[l])
        y_lat = _out_ffn(l, y_lat, ssm_o, na_o, gqa_o, mod_s, w_out_b, ln[0], ln[1], w_ffn_in_b, w_ffn_out_b,
                         ln[2], ln[3])

    kv_na = jnp.stack(kv_na_l, axis=0).reshape(DEPTH, BATCH, SEQ, 2, NA_HEADS, HEAD_DIM)
    kv_g = jnp.stack(kv_g_l, axis=0).reshape(DEPTH, BATCH, SEQ, 2, GQA_KV_HEADS, HEAD_DIM)
    kv_na = kv_na.transpose(3, 1, 0, 2, 4, 5)
    kv_g = kv_g.transpose(3, 1, 0, 2, 4, 5)
    fin = jnp.stack(fin_l, axis=0).reshape(DEPTH, 2, BATCH, SSM_GROUPS, 2, SSM_STATE)
    fin = fin.transpose(4, 2, 0, 1, 3, 5)
    return (y_ctx.reshape(BATCH, SEQ, D_MODEL), y_lat, kv_na[0], kv_na[1], kv_g[0], kv_g[1], fin[0], fin[1])
```

```python
import functools
import math

import numpy as np
import jax
import jax.numpy as jnp
from jax import lax
from jax.experimental import pallas as pl
from jax.experimental.pallas import tpu as pltpu

F32 = jnp.float32
BF16 = jnp.bfloat16

D_MODEL = 1024
BATCH = 16
SEQ = 256
DEPTH = 2
DEC_BATCH = 2
DEC_SEQ = 4096
PAST_LEN = 256
GRID_W = 64
GRID_H = DEC_SEQ // GRID_W
HEAD_DIM = 64
SSM_WIDTH = 256
SSM_GROUP_CH = 16
SSM_GROUPS = 16
SSM_STATE = 64
NA_HEADS = 4
NA_WIDTH = 256
NA_WIN_R = 8
NA_WIN_C = 16
GQA_HEADS = 8
GQA_KV_HEADS = 2
GQA_REP = 4
GQA_WIDTH = 512
GQA_KV_WIDTH = 128
IN_WIDTH = 1792
D_FF = 2816
ROPE_THETA = 10000.0
LN_EPS = 1e-6
RMS_EPS = 1e-6
DEEPNORM_ALPHA = (2 * DEPTH) ** 0.25
QK_SCALE = HEAD_DIM ** -0.5
LOG2E = math.log2(math.e)

TOKENS = 4096
SSM_CHUNK = 16
N_CHUNK = TOKENS // SSM_CHUNK
Z_WIDTH = IN_WIDTH - SSM_WIDTH
Z_QNA, Z_KNA, Z_VNA, Z_QG, Z_KG, Z_VG = 0, 256, 512, 768, 1280, 1408

VMEM_LIMIT = 56 * 1024 * 1024
LANES = 128


def _sigmoid(x):
    return 1.0 / (1.0 + jnp.exp(-x))


def _ln(x):
    mu = jnp.mean(x, axis=-1, keepdims=True)
    xc = x - mu
    var = jnp.mean(xc * xc, axis=-1, keepdims=True)
    return xc * lax.rsqrt(var + LN_EPS)


def _dot(a, b):
    return jnp.dot(a, b, preferred_element_type=F32)


def _dot_nt(a, b):
    return lax.dot_general(a, b, (((1,), (1,)), ((), ())), preferred_element_type=F32)


def _mod_kernel(c_ref, w_ref, b_ref, o_ref):
    c = c_ref[...]
    s = c * _sigmoid(c)
    o_ref[0] = jnp.dot(s, w_ref[0], preferred_element_type=F32, precision=lax.Precision.HIGHEST) + b_ref[0]


def _modulation(cond8, w_ada, b_ada):
    nb = 1536
    return pl.pallas_call(
        _mod_kernel,
        out_shape=jax.ShapeDtypeStruct((DEPTH, 8, 6 * D_MODEL), F32),
        grid=(DEPTH, 6 * D_MODEL // nb),
        in_specs=[
            pl.BlockSpec((8, D_MODEL), lambda l, j: (0, 0)),
            pl.BlockSpec((1, D_MODEL, nb), lambda l, j: (l, 0, j)),
            pl.BlockSpec((1, 1, nb), lambda l, j: (l, 0, j)),
        ],
        out_specs=pl.BlockSpec((1, 8, nb), lambda l, j: (l, 0, j)),
        compiler_params=pltpu.CompilerParams(dimension_semantics=("parallel", "parallel"),
                                             vmem_limit_bytes=VMEM_LIMIT),
        name="mod",
    )(cond8, w_ada, b_ada.reshape(DEPTH, 1, 6 * D_MODEL))


def _swap16(x):
    w = x.shape[-1]
    lane = lax.broadcasted_iota(jnp.int32, x.shape, x.ndim - 1)
    return jnp.where((lane & 16) != 0, pltpu.roll(x, 16, x.ndim - 1), pltpu.roll(x, w - 16, x.ndim - 1))


def _win_kernel(x_ref, mod_ref, w_ref, qg_ref, kg_ref, hm_ref, cos_ref, sin_ref, *out_refs, latent):
    u_ref, z_ref = out_refs[0], out_refs[1]
    x = x_ref[0]
    h = _ln(x) * (1.0 + mod_ref[0, 1:2, :]) + mod_ref[0, 0:1, :]
    z = _dot(h.astype(BF16), w_ref[0])
    u_ref[0, 0] = z[:, 0:LANES]
    u_ref[0, 1] = z[:, LANES:SSM_WIDTH]
    q_na = z[:, 256:512] * QK_SCALE
    kv_na = z[:, 512:1024]
    q_g = z[:, 1024:1536]
    k_g = z[:, 1536:1664]
    v_g = z[:, 1664:1792]
    q_ms = _dot((q_g * q_g).astype(BF16), hm_ref[...])
    k_ms = _dot((k_g * k_g).astype(BF16), hm_ref[0:GQA_KV_WIDTH, 0:GQA_KV_WIDTH])
    q_n = q_g * lax.rsqrt(q_ms + RMS_EPS) * qg_ref[0]
    k_n = k_g * lax.rsqrt(k_ms + RMS_EPS) * kg_ref[0]
    if latent:
        kt_ref, v1_ref = out_refs[2], out_refs[3]
        cos = cos_ref[...]
        sin = sin_ref[...]
        cos4 = jnp.concatenate([cos] * 4, axis=1)
        sin4 = jnp.concatenate([sin] * 4, axis=1)
        q_n = (q_n * cos4 + _swap16(q_n) * sin4) * (QK_SCALE * LOG2E)
        k_n = k_n * cos + _swap16(k_n) * sin
        kt_ref[0] = k_n.T.astype(BF16)
        lane = lax.broadcasted_iota(jnp.int32, v_g.shape, 1)
        ones_col = jnp.where(lane == HEAD_DIM, 1.0, 0.0)
        v1_ref[0, 0] = jnp.where(lane < HEAD_DIM, v_g, ones_col).astype(BF16)
        v1_ref[0, 1] = jnp.where(lane < HEAD_DIM, pltpu.roll(v_g, HEAD_DIM, 1), ones_col).astype(BF16)
    else:
        out_refs[2][0] = kv_na
        out_refs[3][0] = jnp.concatenate([k_n, v_g], axis=1)
        q_n = q_n * QK_SCALE
    zb = jnp.concatenate([q_na, kv_na, q_n, k_n, v_g], axis=1)
    z_ref[0] = zb.astype(BF16)


def _in_projection(l, x, mod, w_in, qg, kg, hm, cos, sin, *, latent):
    bg = x.shape[0]
    tm = 512
    out_shape = [jax.ShapeDtypeStruct((bg, 2, TOKENS, LANES), F32),
                 jax.ShapeDtypeStruct((bg, TOKENS, Z_WIDTH), BF16)]
    out_specs = [pl.BlockSpec((1, 2, tm, LANES), lambda b, i: (b, 0, i, 0)),
                 pl.BlockSpec((1, tm, Z_WIDTH), lambda b, i: (b, i, 0))]
    if latent:
        out_shape += [jax.ShapeDtypeStruct((bg, GQA_KV_WIDTH, TOKENS), BF16),
                      jax.ShapeDtypeStruct((bg, GQA_KV_HEADS, TOKENS, LANES), BF16)]
        out_specs += [pl.BlockSpec((1, GQA_KV_WIDTH, tm), lambda b, i: (b, 0, i)),
                      pl.BlockSpec((1, GQA_KV_HEADS, tm, LANES), lambda b, i: (b, 0, i, 0))]
    else:
        out_shape += [jax.ShapeDtypeStruct((bg, TOKENS, 2 * NA_WIDTH), F32),
                      jax.ShapeDtypeStruct((bg, TOKENS, 2 * GQA_KV_WIDTH), F32)]
        out_specs += [pl.BlockSpec((1, tm, 2 * NA_WIDTH), lambda b, i: (b, i, 0)),
                      pl.BlockSpec((1, tm, 2 * GQA_KV_WIDTH), lambda b, i: (b, i, 0))]
    return pl.pallas_call(
        functools.partial(_win_kernel, latent=latent),
        out_shape=out_shape,
        grid=(bg, TOKENS // tm),
        in_specs=[
            pl.BlockSpec((1, tm, D_MODEL), lambda b, i: (b, i, 0)),
            pl.BlockSpec((1, 6, D_MODEL), lambda b, i: (b, 0, 0)),
            pl.BlockSpec((1, D_MODEL, IN_WIDTH), lambda b, i: (l, 0, 0)),
            pl.BlockSpec((1, 1, GQA_WIDTH), lambda b, i: (l, 0, 0)),
            pl.BlockSpec((1, 1, GQA_KV_WIDTH), lambda b, i: (l, 0, 0)),
            pl.BlockSpec((GQA_WIDTH, GQA_WIDTH), lambda b, i: (0, 0)),
            pl.BlockSpec((tm, LANES), lambda b, i: (i, 0)),
            pl.BlockSpec((tm, LANES), lambda b, i: (i, 0)),
        ],
        out_specs=out_specs,
        compiler_params=pltpu.CompilerParams(dimension_semantics=("parallel", "parallel"),
                                             vmem_limit_bytes=VMEM_LIMIT),
        name="win_lat" if latent else "win_ctx",
    )(x, mod, w_in, qg, kg, hm, cos, sin)


STATE_LANES = SSM_GROUPS * LANES
SSM_UNROLL = 4


def _ssm_ops_kernel(crt_ref, cit_ref, pkr_ref, pki_ref, pcr_ref, pci_ref, bb1_ref, bb2_ref, pwr_ref, pwi_ref,
                    m_ref, bend_ref, cpow_ref):
    d = pl.program_id(1)

    def build_ops(g, carry):
        cr = crt_ref[0, 0, g]
        ci = cit_ref[0, 0, g]

        def re_proj(pr, pi):
            return jnp.concatenate([cr * pr - ci * pi, -(cr * pi + ci * pr)], axis=0)
        cpow_ref[0, 0, g] = re_proj(pcr_ref[0, 0, g], pci_ref[0, 0, g]).astype(BF16)
        bb1 = bb1_ref[0, 0, g]
        bb2 = bb2_ref[0, 0, g]
        kcat = jnp.dot(bb1[:, :LANES], re_proj(pkr_ref[0, 0, g], pki_ref[0, 0, g]),
                       preferred_element_type=F32, precision=lax.Precision.HIGHEST)
        lane = lax.broadcasted_iota(jnp.int32, kcat.shape, 1)
        for t in range(SSM_CHUNK):
            rows = slice(t * SSM_GROUP_CH, (t + 1) * SSM_GROUP_CH)
            lo, hi = t * SSM_GROUP_CH, (t + 1) * SSM_GROUP_CH
            fwd = kcat if t == 0 else jnp.where(lane >= lo, pltpu.roll(kcat, lo, 1), 0.0)
            bwd = kcat if hi == 256 else jnp.where(lane < hi, pltpu.roll(kcat, hi, 1), 0.0)
            m_ref[0, 0, g, rows, :] = jnp.where(d == 0, fwd, bwd).astype(BF16)
            bend_ref[0, 0, g, rows, :] = (pwr_ref[0, 0, g, t:t + 1, :] * bb1
                                          + pwi_ref[0, 0, g, t:t + 1, :] * bb2).astype(BF16)
        return carry
    lax.fori_loop(0, SSM_GROUPS, build_ops, 0)


def _ssm_operators(tables):
    per_dir = lambda rows: pl.BlockSpec((1, 1, SSM_GROUPS, rows, 256), lambda l, d: (l, d, 0, 0, 0))
    shape = lambda rows: jax.ShapeDtypeStruct((DEPTH, 2, SSM_GROUPS, rows, 256), BF16)
    return pl.pallas_call(
        _ssm_ops_kernel,
        out_shape=[shape(256), shape(256), shape(LANES)],
        grid=(DEPTH, 2),
        in_specs=[per_dir(SSM_STATE)] * 6 + [per_dir(SSM_GROUP_CH)] * 4,
        out_specs=[per_dir(256), per_dir(256), per_dir(LANES)],
        compiler_params=pltpu.CompilerParams(dimension_semantics=("parallel", "parallel"),
                                             vmem_limit_bytes=VMEM_LIMIT),
        name="ssm_ops",
    )(*tables)


def _ssm_kernel(u_ref, m_ref, bend_ref, cpow_ref, a1_ref, a2_ref, h0s_ref, h0p_ref, d_ref, wglu_ref, *refs,
                n_seq, emit_final):
    if emit_final:
        out_ref, fin_ref, y_s, yt_s, u_s, inja_s, injb_s, hp_s = refs
    else:
        out_ref, y_s, yt_s, u_s, inja_s, injb_s, hp_s = refs
        fin_ref = None
    chunks_per_seq = N_CHUNK // n_seq

    def chunk_rows(t):
        sl = pl.ds(t, N_CHUNK, stride=SSM_CHUNK)
        return jnp.concatenate([u_ref[0, 0, sl, :], u_ref[0, 1, sl, :]], axis=1)

    for t in range(SSM_CHUNK):
        a_t = chunk_rows(t)
        yt_s[:, t * SSM_GROUP_CH:(t + 1) * SSM_GROUP_CH, :] = a_t.T.reshape(SSM_GROUPS, SSM_GROUP_CH, N_CHUNK)

    def tr_in(g, carry):
        u_s[g] = yt_s[g].T.astype(BF16)
        return carry
    lax.fori_loop(0, SSM_GROUPS, tr_in, 0, unroll=SSM_UNROLL)

    for d in range(2):
        def proj(g, carry, d=d):
            ug = u_s[g]
            y = _dot(ug, m_ref[0, d, g])
            y_s[g] = y if d == 0 else y_s[g] + y
            inj = _dot(ug, bend_ref[0, d, g])
            lanes = pl.ds(pl.multiple_of(g * LANES, LANES), LANES)
            inja_s[d, :, lanes] = inj[:, :LANES]
            injb_s[d, :, lanes] = inj[:, LANES:]
            return carry
        lax.fori_loop(0, SSM_GROUPS, proj, 0, unroll=SSM_UNROLL)

    a1 = a1_ref[0]
    a2 = a2_ref[0]

    def step(j, carry):
        s, sp = carry
        nf = j
        nb = N_CHUNK - 1 - j
        if n_seq > 1:
            keep = jnp.where(j % chunks_per_seq != 0, 1.0, 0.0).astype(F32)
            s = s * keep
            sp = sp * keep
        hp_s[0, pl.ds(nf, 1), :] = s[0:1]
        hp_s[1, pl.ds(nb, 1), :] = s[1:2]
        ia = jnp.concatenate([inja_s[0, pl.ds(nf, 1), :], inja_s[1, pl.ds(nb, 1), :]], axis=0)
        ib = jnp.concatenate([injb_s[0, pl.ds(nf, 1), :], injb_s[1, pl.ds(nb, 1), :]], axis=0)
        s_new = a1 * s + a2 * sp + ia
        sp_new = a1 * sp - a2 * s + ib
        if emit_final:
            @pl.when(j % chunks_per_seq == chunks_per_seq - 1)
            def _():
                sq = j // chunks_per_seq
                fin_ref[0, 0, pl.ds(sq, 1), :] = s_new[0:1]
                fin_ref[0, 1, pl.ds(n_seq - 1 - sq, 1), :] = s_new[1:2]
        return s_new, sp_new
    lax.fori_loop(0, N_CHUNK, step, (h0s_ref[0], h0p_ref[0]))

    for d in range(2):
        def carry_in(g, carry, d=d):
            hp = hp_s[d, :, pl.ds(pl.multiple_of(g * LANES, LANES), LANES)]
            y_s[g] = y_s[g] + _dot(hp.astype(BF16), cpow_ref[0, d, g])
            return carry
        lax.fori_loop(0, SSM_GROUPS, carry_in, 0, unroll=SSM_UNROLL)

    def tr_out(g, carry):
        yt_s[g] = y_s[g].T
        return carry
    lax.fori_loop(0, SSM_GROUPS, tr_out, 0, unroll=SSM_UNROLL)
    dskip = d_ref[0]
    wglu = wglu_ref[0]
    for t in range(SSM_CHUNK):
        blk = yt_s[:, t * SSM_GROUP_CH:(t + 1) * SSM_GROUP_CH, :].reshape(SSM_WIDTH, N_CHUNK)
        y = blk.T + chunk_rows(t) * dskip
        gl = jax.nn.gelu(y, approximate=True)
        gate = _dot(gl.astype(BF16), wglu)
        o = gl * _sigmoid(gate)
        out_ref[0, 0, pl.ds(t, N_CHUNK, stride=SSM_CHUNK), :] = o[:, :LANES]
        out_ref[0, 1, pl.ds(t, N_CHUNK, stride=SSM_CHUNK), :] = o[:, LANES:]


def _ssm(l, u, ops, a1, a2, h0s, h0p, dskip, wglu, *, n_seq, emit_final):
    bg = u.shape[0]
    m, bend, cpow = ops
    out_shape = [jax.ShapeDtypeStruct((bg, 2, TOKENS, LANES), F32)]
    out_specs = [pl.BlockSpec((1, 2, TOKENS, LANES), lambda b: (b, 0, 0, 0))]
    if emit_final:
        out_shape.append(jax.ShapeDtypeStruct((bg, 2, n_seq, STATE_LANES), F32))
        out_specs.append(pl.BlockSpec((1, 2, n_seq, STATE_LANES), lambda b: (b, 0, 0, 0)))
    resident = lambda rows: pl.BlockSpec((1, 2, SSM_GROUPS, rows, 256), lambda b: (l, 0, 0, 0, 0),
                                         pipeline_mode=pl.Buffered(1))
    return pl.pallas_call(
        functools.partial(_ssm_kernel, n_seq=n_seq, emit_final=emit_final),
        out_shape=out_shape,
        grid=(bg,),
        in_specs=[
            pl.BlockSpec((1, 2, TOKENS, LANES), lambda b: (b, 0, 0, 0)),
            resident(256), resident(256), resident(LANES),
            pl.BlockSpec((1, 2, STATE_LANES), lambda b: (l, 0, 0)),
            pl.BlockSpec((1, 2, STATE_LANES), lambda b: (l, 0, 0)),
            pl.BlockSpec((1, 2, STATE_LANES), lambda b: (b, 0, 0)),
            pl.BlockSpec((1, 2, STATE_LANES), lambda b: (b, 0, 0)),
            pl.BlockSpec((1, 1, SSM_WIDTH), lambda b: (l, 0, 0)),
            pl.BlockSpec((1, SSM_WIDTH, SSM_WIDTH), lambda b: (l, 0, 0)),
        ],
        out_specs=out_specs,
        scratch_shapes=[pltpu.VMEM((SSM_GROUPS, N_CHUNK, 256), F32),
                        pltpu.VMEM((SSM_GROUPS, 256, N_CHUNK), F32),
                        pltpu.VMEM((SSM_GROUPS, N_CHUNK, 256), BF16),
                        pltpu.VMEM((2, N_CHUNK, STATE_LANES), F32),
                        pltpu.VMEM((2, N_CHUNK, STATE_LANES), F32),
                        pltpu.VMEM((2, N_CHUNK, STATE_LANES), F32)],
        compiler_params=pltpu.CompilerParams(dimension_semantics=("parallel",), vmem_limit_bytes=VMEM_LIMIT),
        name="ssm_ctx" if emit_final else "ssm_lat",
    )(u, m, bend, cpow, a1, a2, h0s, h0p, dskip, wglu)


def _ssm_tables(lam_re, lam_im, log_dt, b_re, b_im, c_re, c_im):
    t_n = SSM_CHUNK
    lr = jnp.minimum(lam_re.astype(F32), -1e-4)
    li = lam_im.astype(F32)
    dt = jnp.exp(log_dt.astype(F32))[..., None]
    lr_dt, li_dt = lr * dt, li * dt

    def power(k, expand=lambda x: x):
        mag = jnp.exp(expand(lr_dt) * k)
        return mag * jnp.cos(expand(li_dt) * k), mag * jnp.sin(expand(li_dt) * k)
    l_r, l_i = power(1.0)
    a_r, a_i = power(float(t_n))
    n_r, n_i = l_r - 1.0, l_i
    den = lr * lr + li * li
    q_r, q_i = (n_r * lr + n_i * li) / den, (n_i * lr - n_r * li) / den
    bb_r = q_r[..., None] * b_re - q_i[..., None] * b_im
    bb_i = q_r[..., None] * b_im + q_i[..., None] * b_re

    up = np.arange(t_n, dtype=np.float32)
    down = up[::-1].copy()
    per_dir = lambda fwd, bwd: jnp.asarray(np.stack([fwd, bwd]))[None, :, None, None, :]
    on_lanes = lambda x: x[..., None]
    spread = lambda k: jnp.repeat(k, SSM_GROUP_CH, axis=-1)
    pk_r, pk_i = power(spread(per_dir(up, down)), on_lanes)
    pc_r, pc_i = power(spread(per_dir(up + 1.0, down + 1.0)), on_lanes)
    pw_r, pw_i = power(jnp.swapaxes(per_dir(down, up), -1, -2), lambda x: x[..., None, :])
    pw_r, pw_i = jnp.tile(pw_r, (1, 1, 1, 1, 4)), jnp.tile(pw_i, (1, 1, 1, 1, 4))
    cr_t = jnp.tile(jnp.swapaxes(c_re.astype(F32), -1, -2), (1, 1, 1, 1, t_n))
    ci_t = jnp.tile(jnp.swapaxes(c_im.astype(F32), -1, -2), (1, 1, 1, 1, t_n))
    br_t, bi_t = jnp.swapaxes(bb_r, -1, -2), jnp.swapaxes(bb_i, -1, -2)
    bb1 = jnp.concatenate([br_t, bi_t, bi_t, br_t], axis=-1)
    bb2 = jnp.concatenate([-bi_t, br_t, br_t, -bi_t], axis=-1)
    a1 = jnp.concatenate([a_r, a_r], axis=-1)
    a2 = jnp.concatenate([-a_i, a_i], axis=-1)
    return cr_t, ci_t, pk_r, pk_i, pc_r, pc_i, bb1, bb2, pw_r, pw_i, a1, a2


def _attend_transposed(k, v, q, n_kv, rep):
    nq = q.shape[0]
    qt = q.astype(F32).T
    zero = jnp.zeros((HEAD_DIM, nq), F32)
    bd = jnp.concatenate(
        [jnp.concatenate([qt[(g * rep + r) * HEAD_DIM:(g * rep + r + 1) * HEAD_DIM] if g == gg else zero
                          for gg in range(n_kv) for r in range(rep)], axis=1)
         for g in range(n_kv)], axis=0).astype(BF16)
    s = _dot(k, bd)
    m = jnp.max(s, axis=0, keepdims=True)
    p = jnp.exp(s - m)
    p = p * (1.0 / jnp.sum(p, axis=0, keepdims=True))
    ot = _dot(v.astype(F32).T.astype(BF16), p.astype(BF16))
    heads = [ot[g * HEAD_DIM:(g + 1) * HEAD_DIM, (g * rep + r) * nq:(g * rep + r + 1) * nq]
             for g in range(n_kv) for r in range(rep)]
    return jnp.concatenate(heads, axis=0).T


def _ctx_attn_kernel(z_ref, na_ref, gqa_ref):
    z = z_ref[0]
    na_ref[0] = _attend_transposed(z[:, Z_KNA:Z_KNA + NA_WIDTH], z[:, Z_VNA:Z_VNA + NA_WIDTH],
                                   z[:, Z_QNA:Z_QNA + NA_WIDTH], NA_HEADS, 1).astype(BF16)
    gqa_ref[0] = _attend_transposed(z[:, Z_KG:Z_KG + GQA_KV_WIDTH], z[:, Z_VG:Z_VG + GQA_KV_WIDTH],
                                    z[:, Z_QG:Z_QG + GQA_WIDTH], GQA_KV_HEADS, GQA_REP).astype(BF16)


def _ctx_attention(zb):
    z3 = zb.reshape(BATCH, SEQ, Z_WIDTH)
    na, gqa = pl.pallas_call(
        _ctx_attn_kernel,
        out_shape=[jax.ShapeDtypeStruct((BATCH, SEQ, NA_WIDTH), BF16),
                   jax.ShapeDtypeStruct((BATCH, SEQ, GQA_WIDTH), BF16)],
        grid=(BATCH,),
        in_specs=[pl.BlockSpec((1, SEQ, Z_WIDTH), lambda b: (b, 0, 0))],
        out_specs=[pl.BlockSpec((1, SEQ, NA_WIDTH), lambda b: (b, 0, 0)),
                   pl.BlockSpec((1, SEQ, GQA_WIDTH), lambda b: (b, 0, 0))],
        compiler_params=pltpu.CompilerParams(dimension_semantics=("parallel",), vmem_limit_bytes=VMEM_LIMIT),
        name="attn_ctx",
    )(z3)
    return na.reshape(1, TOKENS, NA_WIDTH), gqa.reshape(1, TOKENS, GQA_WIDTH)


NA_KEYS = NA_WIN_R * GRID_W


NA_ROWS = 8


def _na_lat_kernel(q_ref, k_ref, v_ref, ck_ref, cv_ref, *refs):
    bias_refs, o_ref = refs[:NA_ROWS], refs[NA_ROWS]
    i = pl.program_id(1)
    ck = ck_ref[0, 0]
    cv = cv_ref[0, 0]
    qt = q_ref[0].astype(F32).T
    hd = lax.broadcasted_iota(jnp.int32, (NA_WIDTH, NA_WIDTH), 0) // HEAD_DIM
    hq = lax.broadcasted_iota(jnp.int32, (NA_WIDTH, NA_WIDTH), 1) // HEAD_DIM
    for j in range(NA_ROWS):
        r = NA_ROWS * i + j
        rs = jnp.clip(r - NA_WIN_R // 2, 0, GRID_H - NA_WIN_R)
        start = pl.multiple_of(rs * GRID_W, GRID_W)
        kw = k_ref[0, pl.ds(start, NA_KEYS), :]
        vw = v_ref[0, pl.ds(start, NA_KEYS), :]
        qj = qt[:, j * GRID_W:(j + 1) * GRID_W]
        bd = jnp.where(hd == hq, jnp.concatenate([qj] * NA_HEADS, axis=1), 0.0).astype(BF16)
        s_loc = _dot(kw, bd) + bias_refs[j][0, 0]
        s_ctx = _dot(ck, bd)
        m = jnp.maximum(jnp.max(s_loc, axis=0, keepdims=True), jnp.max(s_ctx, axis=0, keepdims=True))
        p_loc = jnp.exp(s_loc - m)
        p_ctx = jnp.exp(s_ctx - m)
        inv = 1.0 / (jnp.sum(p_loc, axis=0, keepdims=True) + jnp.sum(p_ctx, axis=0, keepdims=True))
        full = (_dot((p_loc * inv).T.astype(BF16), vw)
                + _dot((p_ctx * inv).T.astype(BF16), cv))
        out = jnp.concatenate([full[h * HEAD_DIM:(h + 1) * HEAD_DIM, h * HEAD_DIM:(h + 1) * HEAD_DIM]
                               for h in range(NA_HEADS)], axis=1)
        o_ref[0, j * GRID_W:(j + 1) * GRID_W, :] = out.astype(BF16)


def _na_row_class(r):
    return r - jnp.clip(r - NA_WIN_R // 2, 0, GRID_H - NA_WIN_R)


def _na_latent(l, zb, ck, cv, bias):
    tq = NA_ROWS * GRID_W
    bias_specs = [pl.BlockSpec((1, 1, NA_KEYS, NA_WIDTH),
                               functools.partial(lambda b, i, j: (l, _na_row_class(NA_ROWS * i + j), 0, 0), j=j))
                  for j in range(NA_ROWS)]
    return pl.pallas_call(
        _na_lat_kernel,
        out_shape=jax.ShapeDtypeStruct((DEC_BATCH, TOKENS, NA_WIDTH), BF16),
        grid=(DEC_BATCH, GRID_H // NA_ROWS),
        in_specs=[
            pl.BlockSpec((1, tq, NA_WIDTH), lambda b, i: (b, i, Z_QNA // NA_WIDTH)),
            pl.BlockSpec((1, TOKENS, NA_WIDTH), lambda b, i: (b, 0, Z_KNA // NA_WIDTH)),
            pl.BlockSpec((1, TOKENS, NA_WIDTH), lambda b, i: (b, 0, Z_VNA // NA_WIDTH)),
            pl.BlockSpec((1, 1, PAST_LEN, NA_WIDTH), lambda b, i: (b, l, 0, 0)),
            pl.BlockSpec((1, 1, PAST_LEN, NA_WIDTH), lambda b, i: (b, l, 0, 0)),
        ] + bias_specs,
        out_specs=pl.BlockSpec((1, tq, NA_WIDTH), lambda b, i: (b, i, 0)),
        compiler_params=pltpu.CompilerParams(dimension_semantics=("parallel", "arbitrary"),
                                             vmem_limit_bytes=VMEM_LIMIT),
        name="na_lat",
    )(zb, zb, zb, ck, cv, *([bias] * NA_ROWS))


def _na_bias_table(tab):
    q = np.arange(GRID_W)
    kc = np.arange(GRID_W)
    cs = np.clip(q - NA_WIN_C // 2, 0, GRID_W - NA_WIN_C)
    valid = (kc[None, :] >= cs[:, None]) & (kc[None, :] < cs[:, None] + NA_WIN_C)
    w = GRID_W - 1
    padded = jnp.pad(tab.astype(F32), ((0, 0), (0, 0), (0, 0), (w, w)))
    off = w + NA_WIN_C - 1
    cols = jnp.stack([padded[..., off - i:off - i + GRID_W] for i in range(GRID_W)], axis=-2)
    cols = jnp.where(valid, cols, -jnp.inf)
    b = jnp.stack([cols[:, :, NA_WIN_R - 1 - c:2 * NA_WIN_R - 1 - c] for c in range(NA_WIN_R)], axis=1)
    return b.transpose(0, 1, 3, 5, 2, 4).reshape(DEPTH, NA_WIN_R, NA_KEYS, NA_WIDTH)


GQA_TK = 256
GQA_KEYS = DEC_SEQ + PAST_LEN


def _gqa_kernel(q_ref, ktn_ref, ktc_ref, vn_ref, vc_ref, o_ref, s_s, m_s, *, tq):
    q = q_ref[0]
    n_new = DEC_SEQ // GQA_TK
    half = GQA_REP * tq // 2
    qh = [jnp.concatenate([q[:, r * HEAD_DIM:(r + 1) * HEAD_DIM] for r in (2 * i, 2 * i + 1)], axis=0)
          for i in range(2)]
    rows = [slice(i * half, (i + 1) * half) for i in range(2)]

    def fold(s):
        return jnp.maximum(s[:, :LANES], s[:, LANES:])

    for i in range(2):
        s = _dot(qh[i], ktc_ref[0])
        s_s[rows[i], DEC_SEQ:GQA_KEYS] = s
        m_s[rows[i], :] = fold(s)
    for c in range(0, n_new, 2):
        for i in range(2):
            s0 = _dot(qh[i], ktn_ref[0, :, c * GQA_TK:(c + 1) * GQA_TK])
            s1 = _dot(qh[i], ktn_ref[0, :, (c + 1) * GQA_TK:(c + 2) * GQA_TK])
            s_s[rows[i], c * GQA_TK:(c + 1) * GQA_TK] = s0
            s_s[rows[i], (c + 1) * GQA_TK:(c + 2) * GQA_TK] = s1
            m_s[rows[i], :] = jnp.maximum(m_s[rows[i], :], jnp.maximum(fold(s0), fold(s1)))

    m = jnp.max(m_s[...], axis=-1, keepdims=True)
    m_s[...] = jnp.broadcast_to(m, m_s.shape)

    def probs(i, c0):
        mb = m_s[rows[i], :]
        return jnp.exp2(s_s[rows[i], c0:c0 + GQA_TK] - jnp.concatenate([mb, mb], axis=1)).astype(BF16)

    acc = [_dot(probs(i, DEC_SEQ), vc_ref[0, 0]) for i in range(2)]
    for c in range(n_new):
        for i in range(2):
            acc[i] = acc[i] + _dot(probs(i, c * GQA_TK), vn_ref[0, 0, c * GQA_TK:(c + 1) * GQA_TK, :])
    outs = []
    for i in range(2):
        out = acc[i][:, :HEAD_DIM] / acc[i][:, HEAD_DIM:HEAD_DIM + 1]
        outs += [out[:tq], out[tq:]]
    o_ref[0] = jnp.concatenate(outs, axis=1).astype(BF16)


def _gqa_latent(zb, kt_new, kt_cache, v1_new, v1_cache):
    tq = 256
    rows = GQA_REP * tq
    return pl.pallas_call(
        functools.partial(_gqa_kernel, tq=tq),
        out_shape=jax.ShapeDtypeStruct((DEC_BATCH, TOKENS, GQA_WIDTH), BF16),
        grid=(DEC_BATCH, GQA_KV_HEADS, TOKENS // tq),
        in_specs=[
            pl.BlockSpec((1, tq, 256), lambda b, g, i: (b, i, Z_QG // 256 + g)),
            pl.BlockSpec((1, HEAD_DIM, DEC_SEQ), lambda b, g, i: (b, g, 0)),
            pl.BlockSpec((1, HEAD_DIM, PAST_LEN), lambda b, g, i: (b, g, 0)),
            pl.BlockSpec((1, 1, DEC_SEQ, LANES), lambda b, g, i: (b, g, 0, 0)),
            pl.BlockSpec((1, 1, PAST_LEN, LANES), lambda b, g, i: (b, g, 0, 0)),
        ],
        out_specs=pl.BlockSpec((1, tq, 256), lambda b, g, i: (b, i, g)),
        scratch_shapes=[pltpu.VMEM((rows, GQA_KEYS), F32), pltpu.VMEM((rows, LANES), F32)],
        compiler_params=pltpu.CompilerParams(dimension_semantics=("parallel", "parallel", "arbitrary"),
                                             vmem_limit_bytes=VMEM_LIMIT),
        name="gqa_lat",
    )(zb, kt_new, kt_cache, v1_new, v1_cache)


FF_CHUNK = 256


def _ffn_kernel(x_ref, ssm_ref, na_ref, gqa_ref, mod_ref, wout_ref, ln1g_ref, ln1b_ref, win_ref, wo_ref,
                ln2g_ref, ln2b_ref, o_ref, h2_s):
    ssm = jnp.concatenate([ssm_ref[0, 0], ssm_ref[0, 1]], axis=1)
    o = (_dot(ssm.astype(BF16), wout_ref[0, 0:SSM_WIDTH, :])
         + _dot(na_ref[0], wout_ref[0, SSM_WIDTH:SSM_WIDTH + NA_WIDTH, :])
         + _dot(gqa_ref[0], wout_ref[0, SSM_WIDTH + NA_WIDTH:, :]))
    y = DEEPNORM_ALPHA * x_ref[0] + mod_ref[0, 2:3, :] * o
    x1 = _ln(y) * ln1g_ref[0] + ln1b_ref[0]
    h2 = _ln(x1) * (1.0 + mod_ref[0, 4:5, :]) + mod_ref[0, 3:4, :]
    h2_s[...] = h2.astype(BF16)
    acc = None
    for j in range(D_FF // FF_CHUNK):
        h2b = h2_s[...]
        a = _dot(h2b, win_ref[0, :, j * FF_CHUNK:(j + 1) * FF_CHUNK])
        gt = _dot(h2b, win_ref[0, :, D_FF + j * FF_CHUNK:D_FF + (j + 1) * FF_CHUNK])
        f = (a * _sigmoid(a) * gt).astype(BF16)
        c = _dot(f, wo_ref[0, j * FF_CHUNK:(j + 1) * FF_CHUNK, :])
        acc = c if acc is None else acc + c
    y = DEEPNORM_ALPHA * x1 + mod_ref[0, 5:6, :] * acc
    o_ref[0] = _ln(y) * ln2g_ref[0] + ln2b_ref[0]


def _out_ffn(l, x, ssm, na, gqa, mod, w_out, ln1g, ln1b, w_ffn_in, w_ffn_out, ln2g, ln2b):
    bg = x.shape[0]
    tm = 512
    vec = pl.BlockSpec((1, 1, D_MODEL), lambda b, i: (l, 0, 0))
    resident = lambda shape: pl.BlockSpec((1,) + shape, lambda b, i: (l, 0, 0), pipeline_mode=pl.Buffered(1))
    return pl.pallas_call(
        _ffn_kernel,
        out_shape=jax.ShapeDtypeStruct((bg, TOKENS, D_MODEL), F32),
        grid=(bg, TOKENS // tm),
        in_specs=[
            pl.BlockSpec((1, tm, D_MODEL), lambda b, i: (b, i, 0)),
            pl.BlockSpec((1, 2, tm, LANES), lambda b, i: (b, 0, i, 0)),
            pl.BlockSpec((1, tm, NA_WIDTH), lambda b, i: (b, i, 0)),
            pl.BlockSpec((1, tm, GQA_WIDTH), lambda b, i: (b, i, 0)),
            pl.BlockSpec((1, 6, D_MODEL), lambda b, i: (b, 0, 0)),
            resident((D_MODEL, D_MODEL)),
            vec,
            vec,
            resident((D_MODEL, 2 * D_FF)),
            resident((D_FF, D_MODEL)),
            vec,
            vec,
        ],
        out_specs=pl.BlockSpec((1, tm, D_MODEL), lambda b, i: (b, i, 0)),
        scratch_shapes=[pltpu.VMEM((tm, D_MODEL), BF16)],
        compiler_params=pltpu.CompilerParams(dimension_semantics=("parallel", "parallel"),
                                             vmem_limit_bytes=VMEM_LIMIT),
        name="ffn",
    )(x, ssm, na, gqa, mod, w_out, ln1g, ln1b, w_ffn_in, w_ffn_out, ln2g, ln2b)


def _rope_tables():
    nf = HEAD_DIM // 4
    t = np.arange(DEC_SEQ)
    inv = ROPE_THETA ** (-np.arange(nf, dtype=np.float64) / nf)
    pos = np.stack([t // GRID_W, t % GRID_W], axis=1).astype(np.float64)
    ang = pos[:, :, None] * inv
    cos = np.repeat(np.cos(ang)[:, :, None, :], 2, axis=2).reshape(DEC_SEQ, HEAD_DIM)
    sin = (np.sin(ang)[:, :, None, :] * np.array([-1.0, 1.0])[None, None, :, None]).reshape(DEC_SEQ, HEAD_DIM)
    return jnp.asarray(np.tile(cos, (1, 2)), F32), jnp.asarray(np.tile(sin, (1, 2)), F32)


def _head_mean_matrix():
    h = np.arange(GQA_WIDTH) // HEAD_DIM
    return jnp.asarray((h[:, None] == h[None, :]).astype(np.float32) / HEAD_DIM, BF16)


def kernel(x_prompt, x_sample, c, cache_na_k, cache_na_v, cache_gqa_k, cache_gqa_v, state_ssm_re, state_ssm_im,
           c_ctx, w_ada, b_ada, w_in, w_out, q_norm_g, k_norm_g, na_bias, ssm_lam_re, ssm_lam_im, ssm_log_dt,
           ssm_b_re, ssm_b_im, ssm_c_re, ssm_c_im, ssm_d, w_ssm_glu, ln1_g, ln1_b, ln2_g, ln2_b,
           w_ffn_in, w_ffn_out):
    cond8 = jnp.concatenate([c_ctx[None, :], c, jnp.zeros((8 - 1 - DEC_BATCH, D_MODEL), F32)], axis=0)
    mod = _modulation(cond8, w_ada, b_ada).reshape(DEPTH, 8, 6, D_MODEL)
    cos, sin = _rope_tables()
    hm = _head_mean_matrix()
    w_in_b = w_in.astype(BF16)
    w_out_b = w_out.astype(BF16)
    w_ffn_in_b = w_ffn_in.astype(BF16)
    w_ffn_out_b = w_ffn_out.astype(BF16)
    w_glu_b = w_ssm_glu.astype(BF16)
    qg = jnp.tile(q_norm_g, (1, GQA_HEADS))[:, None, :]
    kg = jnp.tile(k_norm_g, (1, GQA_KV_HEADS))[:, None, :]
    dskip = ssm_d[:, None, :]
    ln = (ln1_g[:, None, :], ln1_b[:, None, :], ln2_g[:, None, :], ln2_b[:, None, :])
    *ssm_tabs, a1, a2 = _ssm_tables(ssm_lam_re, ssm_lam_im, ssm_log_dt, ssm_b_re, ssm_b_im, ssm_c_re, ssm_c_im)
    ops = _ssm_operators(ssm_tabs)
    a1 = a1.reshape(DEPTH, 2, STATE_LANES)
    a2 = a2.reshape(DEPTH, 2, STATE_LANES)
    na_tab = _na_bias_table(na_bias)
    ck_na = cache_na_k.reshape(DEC_BATCH, DEPTH, PAST_LEN, NA_WIDTH).astype(BF16)
    cv_na = cache_na_v.reshape(DEC_BATCH, DEPTH, PAST_LEN, NA_WIDTH).astype(BF16)
    kt_cache = cache_gqa_k.transpose(1, 0, 3, 4, 2).reshape(DEPTH, DEC_BATCH, GQA_KV_WIDTH, PAST_LEN).astype(BF16)
    cv_g = cache_gqa_v.transpose(1, 0, 3, 2, 4)
    v1_cache = jnp.concatenate([cv_g, jnp.ones(cv_g.shape[:-1] + (1,), F32),
                                jnp.zeros(cv_g.shape[:-1] + (LANES - HEAD_DIM - 1,), F32)], axis=-1).astype(BF16)
    s_re = state_ssm_re.astype(F32).transpose(1, 0, 2, 3, 4)
    s_im = state_ssm_im.astype(F32).transpose(1, 0, 2, 3, 4)
    h0s_lat = jnp.concatenate([s_re, s_im], axis=-1).reshape(DEPTH, DEC_BATCH, 2, STATE_LANES)
    h0p_lat = jnp.concatenate([s_im, s_re], axis=-1).reshape(DEPTH, DEC_BATCH, 2, STATE_LANES)
    h0_ctx = jnp.zeros((1, 2, STATE_LANES), F32)

    y_ctx = x_prompt.reshape(1, TOKENS, D_MODEL)
    y_lat = x_sample
    kv_na_l, kv_g_l, fin_l = [], [], []
    for l in range(DEPTH):
        mod_c = mod[l, 0:1]
        u, zb, kv_na, kv_g = _in_projection(l, y_ctx, mod_c, w_in_b, qg, kg, hm, cos, sin, latent=False)
        ssm_o, fin = _ssm(l, u, ops, a1, a2, h0_ctx, h0_ctx, dskip, w_glu_b, n_seq=BATCH, emit_final=True)
        na_o, gqa_o = _ctx_attention(zb)
        y_ctx = _out_ffn(l, y_ctx, ssm_o, na_o, gqa_o, mod_c, w_out_b, ln[0], ln[1], w_ffn_in_b, w_ffn_out_b,
                         ln[2], ln[3])
        kv_na_l.append(kv_na); kv_g_l.append(kv_g); fin_l.append(fin)

        mod_s = mod[l, 1:1 + DEC_BATCH]
        u, zb, kt_new, v1_new = _in_projection(l, y_lat, mod_s, w_in_b, qg, kg, hm, cos, sin, latent=True)
        ssm_o = _ssm(l, u, ops, a1, a2, h0s_lat[l], h0p_lat[l], dskip, w_glu_b, n_seq=1,
                     emit_final=False)[0]
        na_o = _na_latent(l, zb, ck_na, cv_na, na_tab)
        gqa_o = _gqa_latent(zb, kt_new, kt_cache[l], v1_new, v1_cache[l])
        y_lat = _out_ffn(l, y_lat, ssm_o, na_o, gqa_o, mod_s, w_out_b, ln[0], ln[1], w_ffn_in_b, w_ffn_out_b,
                         ln[2], ln[3])

    kv_na = jnp.stack(kv_na_l, axis=0).reshape(DEPTH, BATCH, SEQ, 2, NA_HEADS, HEAD_DIM)
    kv_g = jnp.stack(kv_g_l, axis=0).reshape(DEPTH, BATCH, SEQ, 2, GQA_KV_HEADS, HEAD_DIM)
    kv_na = kv_na.transpose(3, 1, 0, 2, 4, 5)
    kv_g = kv_g.transpose(3, 1, 0, 2, 4, 5)
    fin = jnp.stack(fin_l, axis=0).reshape(DEPTH, 2, BATCH, SSM_GROUPS, 2, SSM_STATE)
    fin = fin.transpose(4, 2, 0, 1, 3, 5)
    return (y_ctx.reshape(BATCH, SEQ, D_MODEL), y_lat, kv_na[0], kv_na[1], kv_g[0], kv_g[1], fin[0], fin[1])
```

```python
import functools
import math

import numpy as np
import jax
import jax.numpy as jnp
from jax import lax
from jax.experimental import pallas as pl
from jax.experimental.pallas import tpu as pltpu

F32 = jnp.float32
BF16 = jnp.bfloat16

D_MODEL = 1024
BATCH = 16
SEQ = 256
DEPTH = 2
DEC_BATCH = 2
DEC_SEQ = 4096
PAST_LEN = 256
GRID_W = 64
GRID_H = DEC_SEQ // GRID_W
HEAD_DIM = 64
SSM_WIDTH = 256
SSM_GROUP_CH = 16
SSM_GROUPS = 16
SSM_STATE = 64
NA_HEADS = 4
NA_WIDTH = 256
NA_WIN_R = 8
NA_WIN_C = 16
GQA_HEADS = 8
GQA_KV_HEADS = 2
GQA_REP = 4
GQA_WIDTH = 512
GQA_KV_WIDTH = 128
IN_WIDTH = 1792
D_FF = 2816
ROPE_THETA = 10000.0
LN_EPS = 1e-6
RMS_EPS = 1e-6
DEEPNORM_ALPHA = (2 * DEPTH) ** 0.25
QK_SCALE = HEAD_DIM ** -0.5
LOG2E = math.log2(math.e)

TOKENS = 4096
SSM_CHUNK = 16
N_CHUNK = TOKENS // SSM_CHUNK
Z_WIDTH = IN_WIDTH - SSM_WIDTH
Z_QNA, Z_KNA, Z_VNA, Z_QG, Z_KG, Z_VG = 0, 256, 512, 768, 1280, 1408

VMEM_LIMIT = 56 * 1024 * 1024
LANES = 128


def _sigmoid(x):
    return 1.0 / (1.0 + jnp.exp(-x))


def _ln(x):
    mu = jnp.mean(x, axis=-1, keepdims=True)
    xc = x - mu
    var = jnp.mean(xc * xc, axis=-1, keepdims=True)
    return xc * lax.rsqrt(var + LN_EPS)


def _dot(a, b):
    return jnp.dot(a, b, preferred_element_type=F32)


def _dot_nt(a, b):
    return lax.dot_general(a, b, (((1,), (1,)), ((), ())), preferred_element_type=F32)


def _mod_kernel(c_ref, w_ref, b_ref, o_ref):
    c = c_ref[...]
    s = c * _sigmoid(c)
    o_ref[0] = jnp.dot(s, w_ref[0], preferred_element_type=F32, precision=lax.Precision.HIGHEST) + b_ref[0]


def _modulation(cond8, w_ada, b_ada):
    nb = 1536
    return pl.pallas_call(
        _mod_kernel,
        out_shape=jax.ShapeDtypeStruct((DEPTH, 8, 6 * D_MODEL), F32),
        grid=(DEPTH, 6 * D_MODEL // nb),
        in_specs=[
            pl.BlockSpec((8, D_MODEL), lambda l, j: (0, 0)),
            pl.BlockSpec((1, D_MODEL, nb), lambda l, j: (l, 0, j)),
            pl.BlockSpec((1, 1, nb), lambda l, j: (l, 0, j)),
        ],
        out_specs=pl.BlockSpec((1, 8, nb), lambda l, j: (l, 0, j)),
        compiler_params=pltpu.CompilerParams(dimension_semantics=("parallel", "parallel"),
                                             vmem_limit_bytes=VMEM_LIMIT),
        name="mod",
    )(cond8, w_ada, b_ada.reshape(DEPTH, 1, 6 * D_MODEL))


def _swap16(x):
    w = x.shape[-1]
    lane = lax.broadcasted_iota(jnp.int32, x.shape, x.ndim - 1)
    return jnp.where((lane & 16) != 0, pltpu.roll(x, 16, x.ndim - 1), pltpu.roll(x, w - 16, x.ndim - 1))


def _win_kernel(x_ref, mod_ref, w_ref, qg_ref, kg_ref, hm_ref, cos_ref, sin_ref, *out_refs, latent):
    u_ref, z_ref = out_refs[0], out_refs[1]
    x = x_ref[0]
    h = _ln(x) * (1.0 + mod_ref[0, 1:2, :]) + mod_ref[0, 0:1, :]
    z = _dot(h.astype(BF16), w_ref[0])
    u_ref[0, 0] = z[:, 0:LANES]
    u_ref[0, 1] = z[:, LANES:SSM_WIDTH]
    q_na = z[:, 256:512] * QK_SCALE
    kv_na = z[:, 512:1024]
    q_g = z[:, 1024:1536]
    k_g = z[:, 1536:1664]
    v_g = z[:, 1664:1792]
    q_ms = _dot((q_g * q_g).astype(BF16), hm_ref[...])
    k_ms = _dot((k_g * k_g).astype(BF16), hm_ref[0:GQA_KV_WIDTH, 0:GQA_KV_WIDTH])
    q_n = q_g * lax.rsqrt(q_ms + RMS_EPS) * qg_ref[0]
    k_n = k_g * lax.rsqrt(k_ms + RMS_EPS) * kg_ref[0]
    if latent:
        kt_ref, v1_ref = out_refs[2], out_refs[3]
        cos = cos_ref[...]
        sin = sin_ref[...]
        cos4 = jnp.concatenate([cos] * 4, axis=1)
        sin4 = jnp.concatenate([sin] * 4, axis=1)
        q_n = (q_n * cos4 + _swap16(q_n) * sin4) * (QK_SCALE * LOG2E)
        k_n = k_n * cos + _swap16(k_n) * sin
        kt_ref[0] = k_n.T.astype(BF16)
        lane = lax.broadcasted_iota(jnp.int32, v_g.shape, 1)
        ones_col = jnp.where(lane == HEAD_DIM, 1.0, 0.0)
        v1_ref[0, 0] = jnp.where(lane < HEAD_DIM, v_g, ones_col).astype(BF16)
        v1_ref[0, 1] = jnp.where(lane < HEAD_DIM, pltpu.roll(v_g, HEAD_DIM, 1), ones_col).astype(BF16)
    else:
        out_refs[2][0] = kv_na
        out_refs[3][0] = jnp.concatenate([k_n, v_g], axis=1)
        q_n = q_n * QK_SCALE
    zb = jnp.concatenate([q_na, kv_na, q_n, k_n, v_g], axis=1)
    z_ref[0] = zb.astype(BF16)


def _in_projection(l, x, mod, w_in, qg, kg, hm, cos, sin, *, latent):
    bg = x.shape[0]
    tm = 1024
    out_shape = [jax.ShapeDtypeStruct((bg, 2, TOKENS, LANES), F32),
                 jax.ShapeDtypeStruct((bg, TOKENS, Z_WIDTH), BF16)]
    out_specs = [pl.BlockSpec((1, 2, tm, LANES), lambda b, i: (b, 0, i, 0)),
                 pl.BlockSpec((1, tm, Z_WIDTH), lambda b, i: (b, i, 0))]
    if latent:
        out_shape += [jax.ShapeDtypeStruct((bg, GQA_KV_WIDTH, TOKENS), BF16),
                      jax.ShapeDtypeStruct((bg, GQA_KV_HEADS, TOKENS, LANES), BF16)]
        out_specs += [pl.BlockSpec((1, GQA_KV_WIDTH, tm), lambda b, i: (b, 0, i)),
                      pl.BlockSpec((1, GQA_KV_HEADS, tm, LANES), lambda b, i: (b, 0, i, 0))]
    else:
        out_shape += [jax.ShapeDtypeStruct((bg, TOKENS, 2 * NA_WIDTH), F32),
                      jax.ShapeDtypeStruct((bg, TOKENS, 2 * GQA_KV_WIDTH), F32)]
        out_specs += [pl.BlockSpec((1, tm, 2 * NA_WIDTH), lambda b, i: (b, i, 0)),
                      pl.BlockSpec((1, tm, 2 * GQA_KV_WIDTH), lambda b, i: (b, i, 0))]
    return pl.pallas_call(
        functools.partial(_win_kernel, latent=latent),
        out_shape=out_shape,
        grid=(bg, TOKENS // tm),
        in_specs=[
            pl.BlockSpec((1, tm, D_MODEL), lambda b, i: (b, i, 0)),
            pl.BlockSpec((1, 6, D_MODEL), lambda b, i: (b, 0, 0)),
            pl.BlockSpec((1, D_MODEL, IN_WIDTH), lambda b, i: (l, 0, 0)),
            pl.BlockSpec((1, 1, GQA_WIDTH), lambda b, i: (l, 0, 0)),
            pl.BlockSpec((1, 1, GQA_KV_WIDTH), lambda b, i: (l, 0, 0)),
            pl.BlockSpec((GQA_WIDTH, GQA_WIDTH), lambda b, i: (0, 0)),
            pl.BlockSpec((tm, LANES), lambda b, i: (i, 0)),
            pl.BlockSpec((tm, LANES), lambda b, i: (i, 0)),
        ],
        out_specs=out_specs,
        compiler_params=pltpu.CompilerParams(dimension_semantics=("parallel", "parallel"),
                                             vmem_limit_bytes=VMEM_LIMIT),
        name="win_lat" if latent else "win_ctx",
    )(x, mod, w_in, qg, kg, hm, cos, sin)


STATE_LANES = SSM_GROUPS * LANES
SSM_UNROLL = 8


def _ssm_ops_kernel(crt_ref, cit_ref, pkr_ref, pki_ref, pcr_ref, pci_ref, bb1_ref, bb2_ref, pwr_ref, pwi_ref,
                    m_ref, bend_ref, cpow_ref):
    d = pl.program_id(1)

    def build_ops(g, carry):
        cr = crt_ref[0, 0, g]
        ci = cit_ref[0, 0, g]

        def re_proj(pr, pi):
            return jnp.concatenate([cr * pr - ci * pi, -(cr * pi + ci * pr)], axis=0)
        cpow_ref[0, 0, g] = re_proj(pcr_ref[0, 0, g], pci_ref[0, 0, g]).astype(BF16)
        bb1 = bb1_ref[0, 0, g]
        bb2 = bb2_ref[0, 0, g]
        kcat = jnp.dot(bb1[:, :LANES], re_proj(pkr_ref[0, 0, g], pki_ref[0, 0, g]),
                       preferred_element_type=F32, precision=lax.Precision.HIGHEST)
        lane = lax.broadcasted_iota(jnp.int32, kcat.shape, 1)
        for t in range(SSM_CHUNK):
            rows = slice(t * SSM_GROUP_CH, (t + 1) * SSM_GROUP_CH)
            lo, hi = t * SSM_GROUP_CH, (t + 1) * SSM_GROUP_CH
            fwd = kcat if t == 0 else jnp.where(lane >= lo, pltpu.roll(kcat, lo, 1), 0.0)
            bwd = kcat if hi == 256 else jnp.where(lane < hi, pltpu.roll(kcat, hi, 1), 0.0)
            m_ref[0, 0, g, rows, :] = jnp.where(d == 0, fwd, bwd).astype(BF16)
            bend_ref[0, 0, g, rows, :] = (pwr_ref[0, 0, g, t:t + 1, :] * bb1
                                          + pwi_ref[0, 0, g, t:t + 1, :] * bb2).astype(BF16)
        return carry
    lax.fori_loop(0, SSM_GROUPS, build_ops, 0, unroll=2)


def _ssm_operators(tables):
    per_dir = lambda rows: pl.BlockSpec((1, 1, SSM_GROUPS, rows, 256), lambda l, d: (l, d, 0, 0, 0))
    shape = lambda rows: jax.ShapeDtypeStruct((DEPTH, 2, SSM_GROUPS, rows, 256), BF16)
    return pl.pallas_call(
        _ssm_ops_kernel,
        out_shape=[shape(256), shape(256), shape(LANES)],
        grid=(DEPTH, 2),
        in_specs=[per_dir(SSM_STATE)] * 6 + [per_dir(SSM_GROUP_CH)] * 4,
        out_specs=[per_dir(256), per_dir(256), per_dir(LANES)],
        compiler_params=pltpu.CompilerParams(dimension_semantics=("parallel", "parallel"),
                                             vmem_limit_bytes=VMEM_LIMIT),
        name="ssm_ops",
    )(*tables)


def _ssm_kernel(u_ref, m_ref, bend_ref, cpow_ref, a1_ref, a2_ref, h0s_ref, h0p_ref, d_ref, wglu_ref, *refs,
                n_seq, emit_final):
    if emit_final:
        out_ref, fin_ref, y_s, yt_s, u_s, inja_s, injb_s, hp_s = refs
    else:
        out_ref, y_s, yt_s, u_s, inja_s, injb_s, hp_s = refs
        fin_ref = None
    chunks_per_seq = N_CHUNK // n_seq

    def chunk_rows(t):
        sl = pl.ds(t, N_CHUNK, stride=SSM_CHUNK)
        return jnp.concatenate([u_ref[0, 0, sl, :], u_ref[0, 1, sl, :]], axis=1)

    for t in range(SSM_CHUNK):
        a_t = chunk_rows(t)
        yt_s[:, t * SSM_GROUP_CH:(t + 1) * SSM_GROUP_CH, :] = a_t.T.reshape(SSM_GROUPS, SSM_GROUP_CH, N_CHUNK)

    def tr_in(g, carry):
        u_s[g] = yt_s[g].T.astype(BF16)
        return carry
    lax.fori_loop(0, SSM_GROUPS, tr_in, 0, unroll=SSM_UNROLL)

    for d in range(2):
        def proj(g, carry, d=d):
            ug = u_s[g]
            y = _dot(ug, m_ref[0, d, g])
            y_s[g] = y if d == 0 else y_s[g] + y
            inj = _dot(ug, bend_ref[0, d, g])
            lanes = pl.ds(pl.multiple_of(g * LANES, LANES), LANES)
            inja_s[d, :, lanes] = inj[:, :LANES]
            injb_s[d, :, lanes] = inj[:, LANES:]
            return carry
        lax.fori_loop(0, SSM_GROUPS, proj, 0, unroll=SSM_UNROLL)

    a1 = a1_ref[0]
    a2 = a2_ref[0]

    def step(j, carry):
        s, sp = carry
        nf = j
        nb = N_CHUNK - 1 - j
        if n_seq > 1:
            keep = jnp.where(j % chunks_per_seq != 0, 1.0, 0.0).astype(F32)
            s = s * keep
            sp = sp * keep
        hp_s[0, pl.ds(nf, 1), :] = s[0:1]
        hp_s[1, pl.ds(nb, 1), :] = s[1:2]
        ia = jnp.concatenate([inja_s[0, pl.ds(nf, 1), :], inja_s[1, pl.ds(nb, 1), :]], axis=0)
        ib = jnp.concatenate([injb_s[0, pl.ds(nf, 1), :], injb_s[1, pl.ds(nb, 1), :]], axis=0)
        s_new = a1 * s + a2 * sp + ia
        sp_new = a1 * sp - a2 * s + ib
        if emit_final:
            @pl.when(j % chunks_per_seq == chunks_per_seq - 1)
            def _():
                sq = j // chunks_per_seq
                fin_ref[0, 0, pl.ds(sq, 1), :] = s_new[0:1]
                fin_ref[0, 1, pl.ds(n_seq - 1 - sq, 1), :] = s_new[1:2]
        return s_new, sp_new
    lax.fori_loop(0, N_CHUNK, step, (h0s_ref[0], h0p_ref[0]))

    for d in range(2):
        def carry_in(g, carry, d=d):
            hp = hp_s[d, :, pl.ds(pl.multiple_of(g * LANES, LANES), LANES)]
            y_s[g] = y_s[g] + _dot(hp.astype(BF16), cpow_ref[0, d, g])
            return carry
        lax.fori_loop(0, SSM_GROUPS, carry_in, 0, unroll=SSM_UNROLL)

    def tr_out(g, carry):
        yt_s[g] = y_s[g].T
        return carry
    lax.fori_loop(0, SSM_GROUPS, tr_out, 0, unroll=SSM_UNROLL)
    dskip = d_ref[0]
    wglu = wglu_ref[0]
    for t in range(SSM_CHUNK):
        blk = yt_s[:, t * SSM_GROUP_CH:(t + 1) * SSM_GROUP_CH, :].reshape(SSM_WIDTH, N_CHUNK)
        y = blk.T + chunk_rows(t) * dskip
        gl = jax.nn.gelu(y, approximate=True)
        gate = _dot(gl.astype(BF16), wglu)
        o = gl * _sigmoid(gate)
        out_ref[0, 0, pl.ds(t, N_CHUNK, stride=SSM_CHUNK), :] = o[:, :LANES]
        out_ref[0, 1, pl.ds(t, N_CHUNK, stride=SSM_CHUNK), :] = o[:, LANES:]


def _ssm(l, u, ops, a1, a2, h0s, h0p, dskip, wglu, *, n_seq, emit_final):
    bg = u.shape[0]
    m, bend, cpow = ops
    out_shape = [jax.ShapeDtypeStruct((bg, 2, TOKENS, LANES), F32)]
    out_specs = [pl.BlockSpec((1, 2, TOKENS, LANES), lambda b: (b, 0, 0, 0))]
    if emit_final:
        out_shape.append(jax.ShapeDtypeStruct((bg, 2, n_seq, STATE_LANES), F32))
        out_specs.append(pl.BlockSpec((1, 2, n_seq, STATE_LANES), lambda b: (b, 0, 0, 0)))
    resident = lambda rows: pl.BlockSpec((1, 2, SSM_GROUPS, rows, 256), lambda b: (l, 0, 0, 0, 0),
                                         pipeline_mode=pl.Buffered(1))
    return pl.pallas_call(
        functools.partial(_ssm_kernel, n_seq=n_seq, emit_final=emit_final),
        out_shape=out_shape,
        grid=(bg,),
        in_specs=[
            pl.BlockSpec((1, 2, TOKENS, LANES), lambda b: (b, 0, 0, 0)),
            resident(256), resident(256), resident(LANES),
            pl.BlockSpec((1, 2, STATE_LANES), lambda b: (l, 0, 0)),
            pl.BlockSpec((1, 2, STATE_LANES), lambda b: (l, 0, 0)),
            pl.BlockSpec((1, 2, STATE_LANES), lambda b: (b, 0, 0)),
            pl.BlockSpec((1, 2, STATE_LANES), lambda b: (b, 0, 0)),
            pl.BlockSpec((1, 1, SSM_WIDTH), lambda b: (l, 0, 0)),
            pl.BlockSpec((1, SSM_WIDTH, SSM_WIDTH), lambda b: (l, 0, 0)),
        ],
        out_specs=out_specs,
        scratch_shapes=[pltpu.VMEM((SSM_GROUPS, N_CHUNK, 256), F32),
                        pltpu.VMEM((SSM_GROUPS, 256, N_CHUNK), F32),
                        pltpu.VMEM((SSM_GROUPS, N_CHUNK, 256), BF16),
                        pltpu.VMEM((2, N_CHUNK, STATE_LANES), F32),
                        pltpu.VMEM((2, N_CHUNK, STATE_LANES), F32),
                        pltpu.VMEM((2, N_CHUNK, STATE_LANES), F32)],
        compiler_params=pltpu.CompilerParams(dimension_semantics=("parallel",), vmem_limit_bytes=VMEM_LIMIT),
        name="ssm_ctx" if emit_final else "ssm_lat",
    )(u, m, bend, cpow, a1, a2, h0s, h0p, dskip, wglu)


def _ssm_tables(lam_re, lam_im, log_dt, b_re, b_im, c_re, c_im):
    t_n = SSM_CHUNK
    lr = jnp.minimum(lam_re.astype(F32), -1e-4)
    li = lam_im.astype(F32)
    dt = jnp.exp(log_dt.astype(F32))[..., None]
    lr_dt, li_dt = lr * dt, li * dt

    def power(k, expand=lambda x: x):
        mag = jnp.exp(expand(lr_dt) * k)
        return mag * jnp.cos(expand(li_dt) * k), mag * jnp.sin(expand(li_dt) * k)
    l_r, l_i = power(1.0)
    a_r, a_i = power(float(t_n))
    n_r, n_i = l_r - 1.0, l_i
    den = lr * lr + li * li
    q_r, q_i = (n_r * lr + n_i * li) / den, (n_i * lr - n_r * li) / den
    bb_r = q_r[..., None] * b_re - q_i[..., None] * b_im
    bb_i = q_r[..., None] * b_im + q_i[..., None] * b_re

    up = np.arange(t_n, dtype=np.float32)
    down = up[::-1].copy()
    per_dir = lambda fwd, bwd: jnp.asarray(np.stack([fwd, bwd]))[None, :, None, None, :]
    on_lanes = lambda x: x[..., None]
    spread = lambda k: jnp.repeat(k, SSM_GROUP_CH, axis=-1)
    pk_r, pk_i = power(spread(per_dir(up, down)), on_lanes)
    pc_r, pc_i = power(spread(per_dir(up + 1.0, down + 1.0)), on_lanes)
    pw_r, pw_i = power(jnp.swapaxes(per_dir(down, up), -1, -2), lambda x: x[..., None, :])
    pw_r, pw_i = jnp.tile(pw_r, (1, 1, 1, 1, 4)), jnp.tile(pw_i, (1, 1, 1, 1, 4))
    cr_t = jnp.tile(jnp.swapaxes(c_re.astype(F32), -1, -2), (1, 1, 1, 1, t_n))
    ci_t = jnp.tile(jnp.swapaxes(c_im.astype(F32), -1, -2), (1, 1, 1, 1, t_n))
    br_t, bi_t = jnp.swapaxes(bb_r, -1, -2), jnp.swapaxes(bb_i, -1, -2)
    bb1 = jnp.concatenate([br_t, bi_t, bi_t, br_t], axis=-1)
    bb2 = jnp.concatenate([-bi_t, br_t, br_t, -bi_t], axis=-1)
    a1 = jnp.concatenate([a_r, a_r], axis=-1)
    a2 = jnp.concatenate([-a_i, a_i], axis=-1)
    return cr_t, ci_t, pk_r, pk_i, pc_r, pc_i, bb1, bb2, pw_r, pw_i, a1, a2


def _attend_transposed(k, v, q, n_kv, rep):
    nq = q.shape[0]
    qt = q.astype(F32).T
    zero = jnp.zeros((HEAD_DIM, nq), F32)
    bd = jnp.concatenate(
        [jnp.concatenate([qt[(g * rep + r) * HEAD_DIM:(g * rep + r + 1) * HEAD_DIM] if g == gg else zero
                          for gg in range(n_kv) for r in range(rep)], axis=1)
         for g in range(n_kv)], axis=0).astype(BF16)
    s = _dot(k, bd)
    m = jnp.max(s, axis=0, keepdims=True)
    p = jnp.exp(s - m)
    p = p * (1.0 / jnp.sum(p, axis=0, keepdims=True))
    ot = _dot(v.astype(F32).T.astype(BF16), p.astype(BF16))
    heads = [ot[g * HEAD_DIM:(g + 1) * HEAD_DIM, (g * rep + r) * nq:(g * rep + r + 1) * nq]
             for g in range(n_kv) for r in range(rep)]
    return jnp.concatenate(heads, axis=0).T


CTX_SEQS = 2


def _ctx_attn_kernel(z_ref, na_ref, gqa_ref):
    for i in range(CTX_SEQS):
        z = z_ref[i]
        na_ref[i] = _attend_transposed(z[:, Z_KNA:Z_KNA + NA_WIDTH], z[:, Z_VNA:Z_VNA + NA_WIDTH],
                                       z[:, Z_QNA:Z_QNA + NA_WIDTH], NA_HEADS, 1).astype(BF16)
        gqa_ref[i] = _attend_transposed(z[:, Z_KG:Z_KG + GQA_KV_WIDTH], z[:, Z_VG:Z_VG + GQA_KV_WIDTH],
                                        z[:, Z_QG:Z_QG + GQA_WIDTH], GQA_KV_HEADS, GQA_REP).astype(BF16)


def _ctx_attention(zb):
    z3 = zb.reshape(BATCH, SEQ, Z_WIDTH)
    na, gqa = pl.pallas_call(
        _ctx_attn_kernel,
        out_shape=[jax.ShapeDtypeStruct((BATCH, SEQ, NA_WIDTH), BF16),
                   jax.ShapeDtypeStruct((BATCH, SEQ, GQA_WIDTH), BF16)],
        grid=(BATCH // CTX_SEQS,),
        in_specs=[pl.BlockSpec((CTX_SEQS, SEQ, Z_WIDTH), lambda b: (b, 0, 0))],
        out_specs=[pl.BlockSpec((CTX_SEQS, SEQ, NA_WIDTH), lambda b: (b, 0, 0)),
                   pl.BlockSpec((CTX_SEQS, SEQ, GQA_WIDTH), lambda b: (b, 0, 0))],
        compiler_params=pltpu.CompilerParams(dimension_semantics=("parallel",), vmem_limit_bytes=VMEM_LIMIT),
        name="attn_ctx",
    )(z3)
    return na.reshape(1, TOKENS, NA_WIDTH), gqa.reshape(1, TOKENS, GQA_WIDTH)


NA_KEYS = NA_WIN_R * GRID_W


NA_ROWS = 16


def _na_lat_kernel(q_ref, k_ref, v_ref, ck_ref, cv_ref, *refs):
    bias_refs, o_ref = refs[:NA_ROWS], refs[NA_ROWS]
    i = pl.program_id(1)
    ck = ck_ref[0, 0]
    cv = cv_ref[0, 0]
    qt = q_ref[0].astype(F32).T
    hd = lax.broadcasted_iota(jnp.int32, (NA_WIDTH, NA_WIDTH), 0) // HEAD_DIM
    hq = lax.broadcasted_iota(jnp.int32, (NA_WIDTH, NA_WIDTH), 1) // HEAD_DIM
    for j in range(NA_ROWS):
        r = NA_ROWS * i + j
        rs = jnp.clip(r - NA_WIN_R // 2, 0, GRID_H - NA_WIN_R)
        start = pl.multiple_of(rs * GRID_W, GRID_W)
        kw = k_ref[0, pl.ds(start, NA_KEYS), :]
        vw = v_ref[0, pl.ds(start, NA_KEYS), :]
        qj = qt[:, j * GRID_W:(j + 1) * GRID_W]
        bd = jnp.where(hd == hq, jnp.concatenate([qj] * NA_HEADS, axis=1), 0.0).astype(BF16)
        s_loc = _dot(kw, bd) + bias_refs[j][0, 0]
        s_ctx = _dot(ck, bd)
        m = jnp.maximum(jnp.max(s_loc, axis=0, keepdims=True), jnp.max(s_ctx, axis=0, keepdims=True))
        p_loc = jnp.exp(s_loc - m)
        p_ctx = jnp.exp(s_ctx - m)
        inv = 1.0 / (jnp.sum(p_loc, axis=0, keepdims=True) + jnp.sum(p_ctx, axis=0, keepdims=True))
        full = (_dot((p_loc * inv).T.astype(BF16), vw)
                + _dot((p_ctx * inv).T.astype(BF16), cv))
        out = jnp.concatenate([full[h * HEAD_DIM:(h + 1) * HEAD_DIM, h * HEAD_DIM:(h + 1) * HEAD_DIM]
                               for h in range(NA_HEADS)], axis=1)
        o_ref[0, j * GRID_W:(j + 1) * GRID_W, :] = out.astype(BF16)


def _na_row_class(r):
    return r - jnp.clip(r - NA_WIN_R // 2, 0, GRID_H - NA_WIN_R)


def _na_latent(l, zb, ck, cv, bias):
    tq = NA_ROWS * GRID_W
    bias_specs = [pl.BlockSpec((1, 1, NA_KEYS, NA_WIDTH),
                               functools.partial(lambda b, i, j: (l, _na_row_class(NA_ROWS * i + j), 0, 0), j=j))
                  for j in range(NA_ROWS)]
    return pl.pallas_call(
        _na_lat_kernel,
        out_shape=jax.ShapeDtypeStruct((DEC_BATCH, TOKENS, NA_WIDTH), BF16),
        grid=(DEC_BATCH, GRID_H // NA_ROWS),
        in_specs=[
            pl.BlockSpec((1, tq, NA_WIDTH), lambda b, i: (b, i, Z_QNA // NA_WIDTH)),
            pl.BlockSpec((1, TOKENS, NA_WIDTH), lambda b, i: (b, 0, Z_KNA // NA_WIDTH)),
            pl.BlockSpec((1, TOKENS, NA_WIDTH), lambda b, i: (b, 0, Z_VNA // NA_WIDTH)),
            pl.BlockSpec((1, 1, PAST_LEN, NA_WIDTH), lambda b, i: (b, l, 0, 0)),
            pl.BlockSpec((1, 1, PAST_LEN, NA_WIDTH), lambda b, i: (b, l, 0, 0)),
        ] + bias_specs,
        out_specs=pl.BlockSpec((1, tq, NA_WIDTH), lambda b, i: (b, i, 0)),
        compiler_params=pltpu.CompilerParams(dimension_semantics=("parallel", "arbitrary"),
                                             vmem_limit_bytes=VMEM_LIMIT),
        name="na_lat",
    )(zb, zb, zb, ck, cv, *([bias] * NA_ROWS))


def _na_bias_table(tab):
    q = np.arange(GRID_W)
    kc = np.arange(GRID_W)
    cs = np.clip(q - NA_WIN_C // 2, 0, GRID_W - NA_WIN_C)
    valid = (kc[None, :] >= cs[:, None]) & (kc[None, :] < cs[:, None] + NA_WIN_C)
    w = GRID_W - 1
    padded = jnp.pad(tab.astype(F32), ((0, 0), (0, 0), (0, 0), (w, w)))
    off = w + NA_WIN_C - 1
    cols = jnp.stack([padded[..., off - i:off - i + GRID_W] for i in range(GRID_W)], axis=-2)
    cols = jnp.where(valid, cols, -jnp.inf)
    b = jnp.stack([cols[:, :, NA_WIN_R - 1 - c:2 * NA_WIN_R - 1 - c] for c in range(NA_WIN_R)], axis=1)
    return b.transpose(0, 1, 3, 5, 2, 4).reshape(DEPTH, NA_WIN_R, NA_KEYS, NA_WIDTH)


GQA_TK = 256
GQA_KEYS = DEC_SEQ + PAST_LEN


def _gqa_kernel(q_ref, ktn_ref, ktc_ref, vn_ref, vc_ref, o_ref, s_s, m_s, *, tq):
    q = q_ref[0]
    n_new = DEC_SEQ // GQA_TK
    half = GQA_REP * tq // 2
    qh = [jnp.concatenate([q[:, r * HEAD_DIM:(r + 1) * HEAD_DIM] for r in (2 * i, 2 * i + 1)], axis=0)
          for i in range(2)]
    rows = [slice(i * half, (i + 1) * half) for i in range(2)]

    def fold(s):
        return jnp.maximum(s[:, :LANES], s[:, LANES:])

    for i in range(2):
        s = _dot(qh[i], ktc_ref[0])
        s_s[rows[i], DEC_SEQ:GQA_KEYS] = s
        m_s[rows[i], :] = fold(s)
    for c in range(0, n_new, 2):
        for i in range(2):
            s0 = _dot(qh[i], ktn_ref[0, :, c * GQA_TK:(c + 1) * GQA_TK])
            s1 = _dot(qh[i], ktn_ref[0, :, (c + 1) * GQA_TK:(c + 2) * GQA_TK])
            s_s[rows[i], c * GQA_TK:(c + 1) * GQA_TK] = s0
            s_s[rows[i], (c + 1) * GQA_TK:(c + 2) * GQA_TK] = s1
            m_s[rows[i], :] = jnp.maximum(m_s[rows[i], :], jnp.maximum(fold(s0), fold(s1)))

    m = jnp.max(m_s[...], axis=-1, keepdims=True)
    m_s[...] = jnp.broadcast_to(m, m_s.shape)

    def probs(i, c0):
        mb = m_s[rows[i], :]
        return jnp.exp2(s_s[rows[i], c0:c0 + GQA_TK] - jnp.concatenate([mb, mb], axis=1)).astype(BF16)

    acc = [_dot(probs(i, DEC_SEQ), vc_ref[0, 0]) for i in range(2)]
    for c in range(n_new):
        for i in range(2):
            acc[i] = acc[i] + _dot(probs(i, c * GQA_TK), vn_ref[0, 0, c * GQA_TK:(c + 1) * GQA_TK, :])
    outs = []
    for i in range(2):
        out = acc[i][:, :HEAD_DIM] / acc[i][:, HEAD_DIM:HEAD_DIM + 1]
        outs += [out[:tq], out[tq:]]
    o_ref[0] = jnp.concatenate(outs, axis=1).astype(BF16)


def _gqa_latent(zb, kt_new, kt_cache, v1_new, v1_cache):
    tq = 256
    rows = GQA_REP * tq
    return pl.pallas_call(
        functools.partial(_gqa_kernel, tq=tq),
        out_shape=jax.ShapeDtypeStruct((DEC_BATCH, TOKENS, GQA_WIDTH), BF16),
        grid=(DEC_BATCH, GQA_KV_HEADS, TOKENS // tq),
        in_specs=[
            pl.BlockSpec((1, tq, 256), lambda b, g, i: (b, i, Z_QG // 256 + g)),
            pl.BlockSpec((1, HEAD_DIM, DEC_SEQ), lambda b, g, i: (b, g, 0)),
            pl.BlockSpec((1, HEAD_DIM, PAST_LEN), lambda b, g, i: (b, g, 0)),
            pl.BlockSpec((1, 1, DEC_SEQ, LANES), lambda b, g, i: (b, g, 0, 0)),
            pl.BlockSpec((1, 1, PAST_LEN, LANES), lambda b, g, i: (b, g, 0, 0)),
        ],
        out_specs=pl.BlockSpec((1, tq, 256), lambda b, g, i: (b, i, g)),
        scratch_shapes=[pltpu.VMEM((rows, GQA_KEYS), F32), pltpu.VMEM((rows, LANES), F32)],
        compiler_params=pltpu.CompilerParams(dimension_semantics=("parallel", "parallel", "arbitrary"),
                                             vmem_limit_bytes=VMEM_LIMIT),
        name="gqa_lat",
    )(zb, kt_new, kt_cache, v1_new, v1_cache)


FF_CHUNK = 256


def _ffn_kernel(x_ref, ssm_ref, na_ref, gqa_ref, mod_ref, wout_ref, ln1g_ref, ln1b_ref, win_ref, wo_ref,
                ln2g_ref, ln2b_ref, o_ref, h2_s):
    ssm = jnp.concatenate([ssm_ref[0, 0], ssm_ref[0, 1]], axis=1)
    o = (_dot(ssm.astype(BF16), wout_ref[0, 0:SSM_WIDTH, :])
         + _dot(na_ref[0], wout_ref[0, SSM_WIDTH:SSM_WIDTH + NA_WIDTH, :])
         + _dot(gqa_ref[0], wout_ref[0, SSM_WIDTH + NA_WIDTH:, :]))
    y = DEEPNORM_ALPHA * x_ref[0] + mod_ref[0, 2:3, :] * o
    x1 = _ln(y) * ln1g_ref[0] + ln1b_ref[0]
    h2 = _ln(x1) * (1.0 + mod_ref[0, 4:5, :]) + mod_ref[0, 3:4, :]
    h2_s[...] = h2.astype(BF16)
    acc = None
    for j in range(D_FF // FF_CHUNK):
        h2b = h2_s[...]
        a = _dot(h2b, win_ref[0, :, j * FF_CHUNK:(j + 1) * FF_CHUNK])
        gt = _dot(h2b, win_ref[0, :, D_FF + j * FF_CHUNK:D_FF + (j + 1) * FF_CHUNK])
        f = (a * _sigmoid(a) * gt).astype(BF16)
        c = _dot(f, wo_ref[0, j * FF_CHUNK:(j + 1) * FF_CHUNK, :])
        acc = c if acc is None else acc + c
    y = DEEPNORM_ALPHA * x1 + mod_ref[0, 5:6, :] * acc
    o_ref[0] = _ln(y) * ln2g_ref[0] + ln2b_ref[0]


def _out_ffn(l, x, ssm, na, gqa, mod, w_out, ln1g, ln1b, w_ffn_in, w_ffn_out, ln2g, ln2b):
    bg = x.shape[0]
    tm = 512
    vec = pl.BlockSpec((1, 1, D_MODEL), lambda b, i: (l, 0, 0))
    resident = lambda shape: pl.BlockSpec((1,) + shape, lambda b, i: (l, 0, 0), pipeline_mode=pl.Buffered(1))
    return pl.pallas_call(
        _ffn_kernel,
        out_shape=jax.ShapeDtypeStruct((bg, TOKENS, D_MODEL), F32),
        grid=(bg, TOKENS // tm),
        in_specs=[
            pl.BlockSpec((1, tm, D_MODEL), lambda b, i: (b, i, 0)),
            pl.BlockSpec((1, 2, tm, LANES), lambda b, i: (b, 0, i, 0)),
            pl.BlockSpec((1, tm, NA_WIDTH), lambda b, i: (b, i, 0)),
            pl.BlockSpec((1, tm, GQA_WIDTH), lambda b, i: (b, i, 0)),
            pl.BlockSpec((1, 6, D_MODEL), lambda b, i: (b, 0, 0)),
            resident((D_MODEL, D_MODEL)),
            vec,
            vec,
            resident((D_MODEL, 2 * D_FF)),
            resident((D_FF, D_MODEL)),
            vec,
            vec,
        ],
        out_specs=pl.BlockSpec((1, tm, D_MODEL), lambda b, i: (b, i, 0)),
        scratch_shapes=[pltpu.VMEM((tm, D_MODEL), BF16)],
        compiler_params=pltpu.CompilerParams(dimension_semantics=("parallel", "parallel"),
                                             vmem_limit_bytes=VMEM_LIMIT),
        name="ffn",
    )(x, ssm, na, gqa, mod, w_out, ln1g, ln1b, w_ffn_in, w_ffn_out, ln2g, ln2b)


def _rope_tables():
    nf = HEAD_DIM // 4
    t = np.arange(DEC_SEQ)
    inv = ROPE_THETA ** (-np.arange(nf, dtype=np.float64) / nf)
    pos = np.stack([t // GRID_W, t % GRID_W], axis=1).astype(np.float64)
    ang = pos[:, :, None] * inv
    cos = np.repeat(np.cos(ang)[:, :, None, :], 2, axis=2).reshape(DEC_SEQ, HEAD_DIM)
    sin = (np.sin(ang)[:, :, None, :] * np.array([-1.0, 1.0])[None, None, :, None]).reshape(DEC_SEQ, HEAD_DIM)
    return jnp.asarray(np.tile(cos, (1, 2)), F32), jnp.asarray(np.tile(sin, (1, 2)), F32)


def _head_mean_matrix():
    h = np.arange(GQA_WIDTH) // HEAD_DIM
    return jnp.asarray((h[:, None] == h[None, :]).astype(np.float32) / HEAD_DIM, BF16)


def kernel(x_prompt, x_sample, c, cache_na_k, cache_na_v, cache_gqa_k, cache_gqa_v, state_ssm_re, state_ssm_im,
           c_ctx, w_ada, b_ada, w_in, w_out, q_norm_g, k_norm_g, na_bias, ssm_lam_re, ssm_lam_im, ssm_log_dt,
           ssm_b_re, ssm_b_im, ssm_c_re, ssm_c_im, ssm_d, w_ssm_glu, ln1_g, ln1_b, ln2_g, ln2_b,
           w_ffn_in, w_ffn_out):
    cond8 = jnp.concatenate([c_ctx[None, :], c, jnp.zeros((8 - 1 - DEC_BATCH, D_MODEL), F32)], axis=0)
    mod = _modulation(cond8, w_ada, b_ada).reshape(DEPTH, 8, 6, D_MODEL)
    cos, sin = _rope_tables()
    hm = _head_mean_matrix()
    w_in_b = w_in.astype(BF16)
    w_out_b = w_out.astype(BF16)
    w_ffn_in_b = w_ffn_in.astype(BF16)
    w_ffn_out_b = w_ffn_out.astype(BF16)
    w_glu_b = w_ssm_glu.astype(BF16)
    qg = jnp.tile(q_norm_g, (1, GQA_HEADS))[:, None, :]
    kg = jnp.tile(k_norm_g, (1, GQA_KV_HEADS))[:, None, :]
    dskip = ssm_d[:, None, :]
    ln = (ln1_g[:, None, :], ln1_b[:, None, :], ln2_g[:, None, :], ln2_b[:, None, :])
    *ssm_tabs, a1, a2 = _ssm_tables(ssm_lam_re, ssm_lam_im, ssm_log_dt, ssm_b_re, ssm_b_im, ssm_c_re, ssm_c_im)
    ops = _ssm_operators(ssm_tabs)
    a1 = a1.reshape(DEPTH, 2, STATE_LANES)
    a2 = a2.reshape(DEPTH, 2, STATE_LANES)
    na_tab = _na_bias_table(na_bias)
    ck_na = cache_na_k.reshape(DEC_BATCH, DEPTH, PAST_LEN, NA_WIDTH).astype(BF16)
    cv_na = cache_na_v.reshape(DEC_BATCH, DEPTH, PAST_LEN, NA_WIDTH).astype(BF16)
    kt_cache = cache_gqa_k.transpose(1, 0, 3, 4, 2).reshape(DEPTH, DEC_BATCH, GQA_KV_WIDTH, PAST_LEN).astype(BF16)
    cv_g = cache_gqa_v.transpose(1, 0, 3, 2, 4)
    v1_cache = jnp.concatenate([cv_g, jnp.ones(cv_g.shape[:-1] + (1,), F32),
                                jnp.zeros(cv_g.shape[:-1] + (LANES - HEAD_DIM - 1,), F32)], axis=-1).astype(BF16)
    s_re = state_ssm_re.astype(F32).transpose(1, 0, 2, 3, 4)
    s_im = state_ssm_im.astype(F32).transpose(1, 0, 2, 3, 4)
    h0s_lat = jnp.concatenate([s_re, s_im], axis=-1).reshape(DEPTH, DEC_BATCH, 2, STATE_LANES)
    h0p_lat = jnp.concatenate([s_im, s_re], axis=-1).reshape(DEPTH, DEC_BATCH, 2, STATE_LANES)
    h0_ctx = jnp.zeros((1, 2, STATE_LANES), F32)

    y_ctx = x_prompt.reshape(1, TOKENS, D_MODEL)
    y_lat = x_sample
    kv_na_l, kv_g_l, fin_l = [], [], []
    for l in range(DEPTH):
        mod_c = mod[l, 0:1]
        u, zb, kv_na, kv_g = _in_projection(l, y_ctx, mod_c, w_in_b, qg, kg, hm, cos, sin, latent=False)
        ssm_o, fin = _ssm(l, u, ops, a1, a2, h0_ctx, h0_ctx, dskip, w_glu_b, n_seq=BATCH, emit_final=True)
        na_o, gqa_o = _ctx_attention(zb)
        y_ctx = _out_ffn(l, y_ctx, ssm_o, na_o, gqa_o, mod_c, w_out_b, ln[0], ln[1], w_ffn_in_b, w_ffn_out_b,
                         ln[2], ln[3])
        kv_na_l.append(kv_na); kv_g_l.append(kv_g); fin_l.append(fin)

        mod_s = mod[l, 1:1 + DEC_BATCH]
        u, zb, kt_new, v1_new = _in_projection(l, y_lat, mod_s, w_in_b, qg, kg, hm, cos, sin, latent=True)
        ssm_o = _ssm(l, u, ops, a1, a2, h0s_lat[l], h0p_lat[l], dskip, w_glu_b, n_seq=1,
                     emit_final=False)[0]
        na_o = _na_latent(l, zb, ck_na, cv_na, na_tab)
        gqa_o = _gqa_latent(zb, kt_new, kt_cache[l], v1_new, v1_cache[l])
        y_lat = _out_ffn(l, y_lat, ssm_o, na_o, gqa_o, mod_s, w_out_b, ln[0], ln[1], w_ffn_in_b, w_ffn_out_b,
                         ln[2], ln[3])

    kv_na = jnp.stack(kv_na_l, axis=0).reshape(DEPTH, BATCH, SEQ, 2, NA_HEADS, HEAD_DIM)
    kv_g = jnp.stack(kv_g_l, axis=0).reshape(DEPTH, BATCH, SEQ, 2, GQA_KV_HEADS, HEAD_DIM)
    kv_na = kv_na.transpose(3, 1, 0, 2, 4, 5)
    kv_g = kv_g.transpose(3, 1, 0, 2, 4, 5)
    fin = jnp.stack(fin_l, axis=0).reshape(DEPTH, 2, BATCH, SSM_GROUPS, 2, SSM_STATE)
    fin = fin.transpose(4, 2, 0, 1, 3, 5)
    return (y_ctx.reshape(BATCH, SEQ, D_MODEL), y_lat, kv_na[0], kv_na[1], kv_g[0], kv_g[1], fin[0], fin[1])
```

```python
import functools
import math

import numpy as np
import jax
import jax.numpy as jnp
from jax import lax
from jax.experimental import pallas as pl
from jax.experimental.pallas import tpu as pltpu

F32 = jnp.float32
BF16 = jnp.bfloat16

D_MODEL = 1024
BATCH = 16
SEQ = 256
DEPTH = 2
DEC_BATCH = 2
DEC_SEQ = 4096
PAST_LEN = 256
GRID_W = 64
GRID_H = DEC_SEQ // GRID_W
HEAD_DIM = 64
SSM_WIDTH = 256
SSM_GROUP_CH = 16
SSM_GROUPS = 16
SSM_STATE = 64
NA_HEADS = 4
NA_WIDTH = 256
NA_WIN_R = 8
NA_WIN_C = 16
GQA_HEADS = 8
GQA_KV_HEADS = 2
GQA_REP = 4
GQA_WIDTH = 512
GQA_KV_WIDTH = 128
IN_WIDTH = 1792
D_FF = 2816
ROPE_THETA = 10000.0
LN_EPS = 1e-6
RMS_EPS = 1e-6
DEEPNORM_ALPHA = (2 * DEPTH) ** 0.25
QK_SCALE = HEAD_DIM ** -0.5
LOG2E = math.log2(math.e)

TOKENS = 4096
SSM_CHUNK = 16
N_CHUNK = TOKENS // SSM_CHUNK
Z_WIDTH = IN_WIDTH - SSM_WIDTH
Z_QNA, Z_KNA, Z_VNA, Z_QG, Z_KG, Z_VG = 0, 256, 512, 768, 1280, 1408

VMEM_LIMIT = 56 * 1024 * 1024
LANES = 128


def _sigmoid(x):
    return 1.0 / (1.0 + jnp.exp(-x))


def _ln(x):
    mu = jnp.mean(x, axis=-1, keepdims=True)
    xc = x - mu
    var = jnp.mean(xc * xc, axis=-1, keepdims=True)
    return xc * lax.rsqrt(var + LN_EPS)


def _dot(a, b):
    return jnp.dot(a, b, preferred_element_type=F32)


def _dot_nt(a, b):
    return lax.dot_general(a, b, (((1,), (1,)), ((), ())), preferred_element_type=F32)


def _mod_kernel(c_ref, w_ref, b_ref, o_ref):
    c = c_ref[...]
    s = c * _sigmoid(c)
    o_ref[0] = jnp.dot(s, w_ref[0], preferred_element_type=F32, precision=lax.Precision.HIGHEST) + b_ref[0]


def _modulation(cond8, w_ada, b_ada):
    nb = 1536
    return pl.pallas_call(
        _mod_kernel,
        out_shape=jax.ShapeDtypeStruct((DEPTH, 8, 6 * D_MODEL), F32),
        grid=(DEPTH, 6 * D_MODEL // nb),
        in_specs=[
            pl.BlockSpec((8, D_MODEL), lambda l, j: (0, 0)),
            pl.BlockSpec((1, D_MODEL, nb), lambda l, j: (l, 0, j)),
            pl.BlockSpec((1, 1, nb), lambda l, j: (l, 0, j)),
        ],
        out_specs=pl.BlockSpec((1, 8, nb), lambda l, j: (l, 0, j)),
        compiler_params=pltpu.CompilerParams(dimension_semantics=("parallel", "parallel"),
                                             vmem_limit_bytes=VMEM_LIMIT),
        name="mod",
    )(cond8, w_ada, b_ada.reshape(DEPTH, 1, 6 * D_MODEL))


def _swap16(x):
    w = x.shape[-1]
    lane = lax.broadcasted_iota(jnp.int32, x.shape, x.ndim - 1)
    return jnp.where((lane & 16) != 0, pltpu.roll(x, 16, x.ndim - 1), pltpu.roll(x, w - 16, x.ndim - 1))


def _win_kernel(x_ref, mod_ref, w_ref, qg_ref, kg_ref, hm_ref, cos_ref, sin_ref, *out_refs, latent):
    u_ref, z_ref = out_refs[0], out_refs[1]
    x = x_ref[0]
    h = _ln(x) * (1.0 + mod_ref[0, 1:2, :]) + mod_ref[0, 0:1, :]
    z = _dot(h.astype(BF16), w_ref[0])
    u_ref[0, 0] = z[:, 0:LANES]
    u_ref[0, 1] = z[:, LANES:SSM_WIDTH]
    q_na = z[:, 256:512] * QK_SCALE
    kv_na = z[:, 512:1024]
    q_g = z[:, 1024:1536]
    k_g = z[:, 1536:1664]
    v_g = z[:, 1664:1792]
    q_ms = _dot((q_g * q_g).astype(BF16), hm_ref[...])
    k_ms = _dot((k_g * k_g).astype(BF16), hm_ref[0:GQA_KV_WIDTH, 0:GQA_KV_WIDTH])
    q_n = q_g * lax.rsqrt(q_ms + RMS_EPS) * qg_ref[0]
    k_n = k_g * lax.rsqrt(k_ms + RMS_EPS) * kg_ref[0]
    if latent:
        kt_ref, v1_ref = out_refs[2], out_refs[3]
        cos = cos_ref[...]
        sin = sin_ref[...]
        cos4 = jnp.concatenate([cos] * 4, axis=1)
        sin4 = jnp.concatenate([sin] * 4, axis=1)
        q_n = (q_n * cos4 + _swap16(q_n) * sin4) * (QK_SCALE * LOG2E)
        k_n = k_n * cos + _swap16(k_n) * sin
        kt_ref[0] = k_n.T.astype(BF16)
        lane = lax.broadcasted_iota(jnp.int32, v_g.shape, 1)
        ones_col = jnp.where(lane == HEAD_DIM, 1.0, 0.0)
        v1_ref[0, 0] = jnp.where(lane < HEAD_DIM, v_g, ones_col).astype(BF16)
        v1_ref[0, 1] = jnp.where(lane < HEAD_DIM, pltpu.roll(v_g, HEAD_DIM, 1), ones_col).astype(BF16)
    else:
        out_refs[2][0] = kv_na
        out_refs[3][0] = jnp.concatenate([k_n, v_g], axis=1)
        q_n = q_n * QK_SCALE
    zb = jnp.concatenate([q_na, kv_na, q_n, k_n, v_g], axis=1)
    z_ref[0] = zb.astype(BF16)


def _in_projection(l, x, mod, w_in, qg, kg, hm, cos, sin, *, latent):
    bg = x.shape[0]
    tm = 1024
    out_shape = [jax.ShapeDtypeStruct((bg, 2, TOKENS, LANES), F32),
                 jax.ShapeDtypeStruct((bg, TOKENS, Z_WIDTH), BF16)]
    out_specs = [pl.BlockSpec((1, 2, tm, LANES), lambda b, i: (b, 0, i, 0)),
                 pl.BlockSpec((1, tm, Z_WIDTH), lambda b, i: (b, i, 0))]
    if latent:
        out_shape += [jax.ShapeDtypeStruct((bg, GQA_KV_WIDTH, TOKENS), BF16),
                      jax.ShapeDtypeStruct((bg, GQA_KV_HEADS, TOKENS, LANES), BF16)]
        out_specs += [pl.BlockSpec((1, GQA_KV_WIDTH, tm), lambda b, i: (b, 0, i)),
                      pl.BlockSpec((1, GQA_KV_HEADS, tm, LANES), lambda b, i: (b, 0, i, 0))]
    else:
        out_shape += [jax.ShapeDtypeStruct((bg, TOKENS, 2 * NA_WIDTH), F32),
                      jax.ShapeDtypeStruct((bg, TOKENS, 2 * GQA_KV_WIDTH), F32)]
        out_specs += [pl.BlockSpec((1, tm, 2 * NA_WIDTH), lambda b, i: (b, i, 0)),
                      pl.BlockSpec((1, tm, 2 * GQA_KV_WIDTH), lambda b, i: (b, i, 0))]
    return pl.pallas_call(
        functools.partial(_win_kernel, latent=latent),
        out_shape=out_shape,
        grid=(bg, TOKENS // tm),
        in_specs=[
            pl.BlockSpec((1, tm, D_MODEL), lambda b, i: (b, i, 0)),
            pl.BlockSpec((1, 6, D_MODEL), lambda b, i: (b, 0, 0)),
            pl.BlockSpec((1, D_MODEL, IN_WIDTH), lambda b, i: (l, 0, 0)),
            pl.BlockSpec((1, 1, GQA_WIDTH), lambda b, i: (l, 0, 0)),
            pl.BlockSpec((1, 1, GQA_KV_WIDTH), lambda b, i: (l, 0, 0)),
            pl.BlockSpec((GQA_WIDTH, GQA_WIDTH), lambda b, i: (0, 0)),
            pl.BlockSpec((tm, LANES), lambda b, i: (i, 0)),
            pl.BlockSpec((tm, LANES), lambda b, i: (i, 0)),
        ],
        out_specs=out_specs,
        compiler_params=pltpu.CompilerParams(dimension_semantics=("parallel", "parallel"),
                                             vmem_limit_bytes=VMEM_LIMIT),
        name="win_lat" if latent else "win_ctx",
    )(x, mod, w_in, qg, kg, hm, cos, sin)


STATE_LANES = SSM_GROUPS * LANES
SSM_UNROLL = 8


def _ssm_ops_kernel(ct_r_ref, ct_i_ref, pk_r_ref, pk_i_ref, pc_r_ref, pc_i_ref, bb1_ref, bb2_ref, pw_r_ref, pw_i_ref,
                    tile_ref, rep_ref, tile4_ref, m_ref, bend_ref, cpow_ref):
    d = pl.program_id(1)

    def onto_lanes(x, sel):
        return jnp.dot(x, sel, preferred_element_type=F32, precision=lax.Precision.HIGHEST)

    def build_ops(g, carry):
        cr = onto_lanes(ct_r_ref[0, 0, g], tile_ref[...])
        ci = onto_lanes(ct_i_ref[0, 0, g], tile_ref[...])

        def re_proj(pr_ref, pi_ref):
            pr = onto_lanes(pr_ref[0, 0, g], rep_ref[...])
            pi = onto_lanes(pi_ref[0, 0, g], rep_ref[...])
            return jnp.concatenate([cr * pr - ci * pi, -(cr * pi + ci * pr)], axis=0)
        cpow_ref[0, 0, g] = re_proj(pc_r_ref, pc_i_ref).astype(BF16)
        bb1 = bb1_ref[0, 0, g]
        bb2 = bb2_ref[0, 0, g]
        kcat = jnp.dot(bb1[:, :LANES], re_proj(pk_r_ref, pk_i_ref),
                       preferred_element_type=F32, precision=lax.Precision.HIGHEST)
        lane = lax.broadcasted_iota(jnp.int32, kcat.shape, 1)
        pw_r = onto_lanes(pw_r_ref[0, 0, g], tile4_ref[...])
        pw_i = onto_lanes(pw_i_ref[0, 0, g], tile4_ref[...])
        for t in range(SSM_CHUNK):
            rows = slice(t * SSM_GROUP_CH, (t + 1) * SSM_GROUP_CH)
            lo, hi = t * SSM_GROUP_CH, (t + 1) * SSM_GROUP_CH
            fwd = kcat if t == 0 else jnp.where(lane >= lo, pltpu.roll(kcat, lo, 1), 0.0)
            bwd = kcat if hi == 256 else jnp.where(lane < hi, pltpu.roll(kcat, hi, 1), 0.0)
            m_ref[0, 0, g, rows, :] = jnp.where(d == 0, fwd, bwd).astype(BF16)
            bend_ref[0, 0, g, rows, :] = (pw_r[t:t + 1, :] * bb1 + pw_i[t:t + 1, :] * bb2).astype(BF16)
        return carry
    lax.fori_loop(0, SSM_GROUPS, build_ops, 0, unroll=2)


def _ssm_operators(tables):
    per_dir = lambda rows, cols: pl.BlockSpec((1, 1, SSM_GROUPS, rows, cols), lambda l, d: (l, d, 0, 0, 0))
    const = lambda rows: pl.BlockSpec((rows, 256), lambda l, d: (0, 0))
    shape = lambda rows: jax.ShapeDtypeStruct((DEPTH, 2, SSM_GROUPS, rows, 256), BF16)
    c = np.arange(256)
    tile = (c[None, :] % SSM_GROUP_CH == np.arange(SSM_GROUP_CH)[:, None]).astype(np.float32)
    rep = (c[None, :] // SSM_GROUP_CH == np.arange(SSM_CHUNK)[:, None]).astype(np.float32)
    tile4 = (c[None, :] % SSM_STATE == np.arange(SSM_STATE)[:, None]).astype(np.float32)
    return pl.pallas_call(
        _ssm_ops_kernel,
        out_shape=[shape(256), shape(256), shape(LANES)],
        grid=(DEPTH, 2),
        in_specs=([per_dir(SSM_STATE, SSM_GROUP_CH)] * 6 + [per_dir(SSM_GROUP_CH, 256)] * 2
                  + [per_dir(SSM_CHUNK, SSM_STATE)] * 2 + [const(SSM_GROUP_CH), const(SSM_CHUNK), const(SSM_STATE)]),
        out_specs=[per_dir(256, 256), per_dir(256, 256), per_dir(LANES, 256)],
        compiler_params=pltpu.CompilerParams(dimension_semantics=("parallel", "parallel"),
                                             vmem_limit_bytes=VMEM_LIMIT),
        name="ssm_ops",
    )(*tables, jnp.asarray(tile), jnp.asarray(rep), jnp.asarray(tile4))


def _ssm_kernel(u_ref, m_ref, bend_ref, cpow_ref, a1_ref, a2_ref, h0s_ref, h0p_ref, d_ref, wglu_ref, *refs,
                n_seq, emit_final):
    if emit_final:
        out_ref, fin_ref, y_s, yt_s, u_s, inja_s, injb_s, hp_s = refs
    else:
        out_ref, y_s, yt_s, u_s, inja_s, injb_s, hp_s = refs
        fin_ref = None
    chunks_per_seq = N_CHUNK // n_seq

    def chunk_rows(t):
        sl = pl.ds(t, N_CHUNK, stride=SSM_CHUNK)
        return jnp.concatenate([u_ref[0, 0, sl, :], u_ref[0, 1, sl, :]], axis=1)

    for t in range(SSM_CHUNK):
        a_t = chunk_rows(t)
        yt_s[:, t * SSM_GROUP_CH:(t + 1) * SSM_GROUP_CH, :] = a_t.T.reshape(SSM_GROUPS, SSM_GROUP_CH, N_CHUNK)

    def tr_in(g, carry):
        u_s[g] = yt_s[g].T.astype(BF16)
        return carry
    lax.fori_loop(0, SSM_GROUPS, tr_in, 0, unroll=SSM_UNROLL)

    for d in range(2):
        def proj(g, carry, d=d):
            ug = u_s[g]
            y = _dot(ug, m_ref[0, d, g])
            y_s[g] = y if d == 0 else y_s[g] + y
            inj = _dot(ug, bend_ref[0, d, g])
            lanes = pl.ds(pl.multiple_of(g * LANES, LANES), LANES)
            inja_s[d, :, lanes] = inj[:, :LANES]
            injb_s[d, :, lanes] = inj[:, LANES:]
            return carry
        lax.fori_loop(0, SSM_GROUPS, proj, 0, unroll=SSM_UNROLL)

    a1 = a1_ref[0]
    a2 = a2_ref[0]

    def step(j, carry):
        s, sp = carry
        nf = j
        nb = N_CHUNK - 1 - j
        if n_seq > 1:
            keep = jnp.where(j % chunks_per_seq != 0, 1.0, 0.0).astype(F32)
            s = s * keep
            sp = sp * keep
        hp_s[0, pl.ds(nf, 1), :] = s[0:1]
        hp_s[1, pl.ds(nb, 1), :] = s[1:2]
        ia = jnp.concatenate([inja_s[0, pl.ds(nf, 1), :], inja_s[1, pl.ds(nb, 1), :]], axis=0)
        ib = jnp.concatenate([injb_s[0, pl.ds(nf, 1), :], injb_s[1, pl.ds(nb, 1), :]], axis=0)
        s_new = a1 * s + a2 * sp + ia
        sp_new = a1 * sp - a2 * s + ib
        if emit_final:
            @pl.when(j % chunks_per_seq == chunks_per_seq - 1)
            def _():
                sq = j // chunks_per_seq
                fin_ref[0, 0, pl.ds(sq, 1), :] = s_new[0:1]
                fin_ref[0, 1, pl.ds(n_seq - 1 - sq, 1), :] = s_new[1:2]
        return s_new, sp_new
    lax.fori_loop(0, N_CHUNK, step, (h0s_ref[0], h0p_ref[0]))

    for d in range(2):
        def carry_in(g, carry, d=d):
            hp = hp_s[d, :, pl.ds(pl.multiple_of(g * LANES, LANES), LANES)]
            y_s[g] = y_s[g] + _dot(hp.astype(BF16), cpow_ref[0, d, g])
            return carry
        lax.fori_loop(0, SSM_GROUPS, carry_in, 0, unroll=SSM_UNROLL)

    def tr_out(g, carry):
        yt_s[g] = y_s[g].T
        return carry
    lax.fori_loop(0, SSM_GROUPS, tr_out, 0, unroll=SSM_UNROLL)
    dskip = d_ref[0]
    wglu = wglu_ref[0]
    for t in range(SSM_CHUNK):
        blk = yt_s[:, t * SSM_GROUP_CH:(t + 1) * SSM_GROUP_CH, :].reshape(SSM_WIDTH, N_CHUNK)
        y = blk.T + chunk_rows(t) * dskip
        gl = jax.nn.gelu(y, approximate=True)
        gate = _dot(gl.astype(BF16), wglu)
        o = gl * _sigmoid(gate)
        out_ref[0, 0, pl.ds(t, N_CHUNK, stride=SSM_CHUNK), :] = o[:, :LANES]
        out_ref[0, 1, pl.ds(t, N_CHUNK, stride=SSM_CHUNK), :] = o[:, LANES:]


def _ssm(l, u, ops, a1, a2, h0s, h0p, dskip, wglu, *, n_seq, emit_final):
    bg = u.shape[0]
    m, bend, cpow = ops
    out_shape = [jax.ShapeDtypeStruct((bg, 2, TOKENS, LANES), F32)]
    out_specs = [pl.BlockSpec((1, 2, TOKENS, LANES), lambda b: (b, 0, 0, 0))]
    if emit_final:
        out_shape.append(jax.ShapeDtypeStruct((bg, 2, n_seq, STATE_LANES), F32))
        out_specs.append(pl.BlockSpec((1, 2, n_seq, STATE_LANES), lambda b: (b, 0, 0, 0)))
    resident = lambda rows: pl.BlockSpec((1, 2, SSM_GROUPS, rows, 256), lambda b: (l, 0, 0, 0, 0),
                                         pipeline_mode=pl.Buffered(1))
    return pl.pallas_call(
        functools.partial(_ssm_kernel, n_seq=n_seq, emit_final=emit_final),
        out_shape=out_shape,
        grid=(bg,),
        in_specs=[
            pl.BlockSpec((1, 2, TOKENS, LANES), lambda b: (b, 0, 0, 0)),
            resident(256), resident(256), resident(LANES),
            pl.BlockSpec((1, 2, STATE_LANES), lambda b: (l, 0, 0)),
            pl.BlockSpec((1, 2, STATE_LANES), lambda b: (l, 0, 0)),
            pl.BlockSpec((1, 2, STATE_LANES), lambda b: (b, 0, 0)),
            pl.BlockSpec((1, 2, STATE_LANES), lambda b: (b, 0, 0)),
            pl.BlockSpec((1, 1, SSM_WIDTH), lambda b: (l, 0, 0)),
            pl.BlockSpec((1, SSM_WIDTH, SSM_WIDTH), lambda b: (l, 0, 0)),
        ],
        out_specs=out_specs,
        scratch_shapes=[pltpu.VMEM((SSM_GROUPS, N_CHUNK, 256), F32),
                        pltpu.VMEM((SSM_GROUPS, 256, N_CHUNK), F32),
                        pltpu.VMEM((SSM_GROUPS, N_CHUNK, 256), BF16),
                        pltpu.VMEM((2, N_CHUNK, STATE_LANES), F32),
                        pltpu.VMEM((2, N_CHUNK, STATE_LANES), F32),
                        pltpu.VMEM((2, N_CHUNK, STATE_LANES), F32)],
        compiler_params=pltpu.CompilerParams(dimension_semantics=("parallel",), vmem_limit_bytes=VMEM_LIMIT),
        name="ssm_ctx" if emit_final else "ssm_lat",
    )(u, m, bend, cpow, a1, a2, h0s, h0p, dskip, wglu)


def _ssm_tables(lam_re, lam_im, log_dt, b_re, b_im, c_re, c_im):
    t_n = SSM_CHUNK
    lr = jnp.minimum(lam_re.astype(F32), -1e-4)
    li = lam_im.astype(F32)
    dt = jnp.exp(log_dt.astype(F32))[..., None]
    lr_dt, li_dt = lr * dt, li * dt

    def power(k, expand=lambda x: x):
        mag = jnp.exp(expand(lr_dt) * k)
        return mag * jnp.cos(expand(li_dt) * k), mag * jnp.sin(expand(li_dt) * k)
    l_r, l_i = power(1.0)
    a_r, a_i = power(float(t_n))
    n_r, n_i = l_r - 1.0, l_i
    den = lr * lr + li * li
    q_r, q_i = (n_r * lr + n_i * li) / den, (n_i * lr - n_r * li) / den
    bb_r = q_r[..., None] * b_re - q_i[..., None] * b_im
    bb_i = q_r[..., None] * b_im + q_i[..., None] * b_re

    up = np.arange(t_n, dtype=np.float32)
    down = up[::-1].copy()
    per_dir = lambda fwd, bwd: jnp.asarray(np.stack([fwd, bwd]))[None, :, None, None, :]
    on_lanes = lambda x: x[..., None]
    pk_r, pk_i = power(per_dir(up, down), on_lanes)
    pc_r, pc_i = power(per_dir(up + 1.0, down + 1.0), on_lanes)
    pw_r, pw_i = power(jnp.swapaxes(per_dir(down, up), -1, -2), lambda x: x[..., None, :])
    cr_t = jnp.swapaxes(c_re.astype(F32), -1, -2)
    ci_t = jnp.swapaxes(c_im.astype(F32), -1, -2)
    br_t, bi_t = jnp.swapaxes(bb_r, -1, -2), jnp.swapaxes(bb_i, -1, -2)
    bb1 = jnp.concatenate([br_t, bi_t, bi_t, br_t], axis=-1)
    bb2 = jnp.concatenate([-bi_t, br_t, br_t, -bi_t], axis=-1)
    a1 = jnp.concatenate([a_r, a_r], axis=-1)
    a2 = jnp.concatenate([-a_i, a_i], axis=-1)
    return cr_t, ci_t, pk_r, pk_i, pc_r, pc_i, bb1, bb2, pw_r, pw_i, a1, a2


def _attend_transposed(k, v, q, n_kv, rep):
    nq = q.shape[0]
    qt = q.astype(F32).T
    zero = jnp.zeros((HEAD_DIM, nq), F32)
    bd = jnp.concatenate(
        [jnp.concatenate([qt[(g * rep + r) * HEAD_DIM:(g * rep + r + 1) * HEAD_DIM] if g == gg else zero
                          for gg in range(n_kv) for r in range(rep)], axis=1)
         for g in range(n_kv)], axis=0).astype(BF16)
    s = _dot(k, bd)
    m = jnp.max(s, axis=0, keepdims=True)
    p = jnp.exp(s - m)
    p = p * (1.0 / jnp.sum(p, axis=0, keepdims=True))
    ot = _dot(v.astype(F32).T.astype(BF16), p.astype(BF16))
    heads = [ot[g * HEAD_DIM:(g + 1) * HEAD_DIM, (g * rep + r) * nq:(g * rep + r + 1) * nq]
             for g in range(n_kv) for r in range(rep)]
    return jnp.concatenate(heads, axis=0).T


CTX_SEQS = 2


def _ctx_attn_kernel(z_ref, na_ref, gqa_ref):
    for i in range(CTX_SEQS):
        z = z_ref[i]
        na_ref[i] = _attend_transposed(z[:, Z_KNA:Z_KNA + NA_WIDTH], z[:, Z_VNA:Z_VNA + NA_WIDTH],
                                       z[:, Z_QNA:Z_QNA + NA_WIDTH], NA_HEADS, 1).astype(BF16)
        gqa_ref[i] = _attend_transposed(z[:, Z_KG:Z_KG + GQA_KV_WIDTH], z[:, Z_VG:Z_VG + GQA_KV_WIDTH],
                                        z[:, Z_QG:Z_QG + GQA_WIDTH], GQA_KV_HEADS, GQA_REP).astype(BF16)


def _ctx_attention(zb):
    z3 = zb.reshape(BATCH, SEQ, Z_WIDTH)
    na, gqa = pl.pallas_call(
        _ctx_attn_kernel,
        out_shape=[jax.ShapeDtypeStruct((BATCH, SEQ, NA_WIDTH), BF16),
                   jax.ShapeDtypeStruct((BATCH, SEQ, GQA_WIDTH), BF16)],
        grid=(BATCH // CTX_SEQS,),
        in_specs=[pl.BlockSpec((CTX_SEQS, SEQ, Z_WIDTH), lambda b: (b, 0, 0))],
        out_specs=[pl.BlockSpec((CTX_SEQS, SEQ, NA_WIDTH), lambda b: (b, 0, 0)),
                   pl.BlockSpec((CTX_SEQS, SEQ, GQA_WIDTH), lambda b: (b, 0, 0))],
        compiler_params=pltpu.CompilerParams(dimension_semantics=("parallel",), vmem_limit_bytes=VMEM_LIMIT),
        name="attn_ctx",
    )(z3)
    return na.reshape(1, TOKENS, NA_WIDTH), gqa.reshape(1, TOKENS, GQA_WIDTH)


NA_KEYS = NA_WIN_R * GRID_W


NA_ROWS = 16


def _na_lat_kernel(q_ref, k_ref, v_ref, ck_ref, cv_ref, bias_ref, o_ref):
    i = pl.program_id(1)
    ck = ck_ref[0, 0]
    cv = cv_ref[0, 0]
    qt = q_ref[0].astype(F32).T
    hd = lax.broadcasted_iota(jnp.int32, (NA_WIDTH, NA_WIDTH), 0) // HEAD_DIM
    hq = lax.broadcasted_iota(jnp.int32, (NA_WIDTH, NA_WIDTH), 1) // HEAD_DIM
    for j in range(NA_ROWS):
        r = NA_ROWS * i + j
        rs = jnp.clip(r - NA_WIN_R // 2, 0, GRID_H - NA_WIN_R)
        start = pl.multiple_of(rs * GRID_W, GRID_W)
        kw = k_ref[0, pl.ds(start, NA_KEYS), :]
        vw = v_ref[0, pl.ds(start, NA_KEYS), :]
        qj = qt[:, j * GRID_W:(j + 1) * GRID_W]
        bd = jnp.where(hd == hq, jnp.concatenate([qj] * NA_HEADS, axis=1), 0.0).astype(BF16)
        bias = bias_ref[0, pl.ds(NA_WIN_R - 1 - (r - rs), NA_WIN_R), :, :].reshape(NA_KEYS, NA_WIDTH)
        s_loc = _dot(kw, bd) + bias
        s_ctx = _dot(ck, bd)
        m = jnp.maximum(jnp.max(s_loc, axis=0, keepdims=True), jnp.max(s_ctx, axis=0, keepdims=True))
        p_loc = jnp.exp(s_loc - m)
        p_ctx = jnp.exp(s_ctx - m)
        inv = 1.0 / (jnp.sum(p_loc, axis=0, keepdims=True) + jnp.sum(p_ctx, axis=0, keepdims=True))
        full = (_dot((p_loc * inv).T.astype(BF16), vw)
                + _dot((p_ctx * inv).T.astype(BF16), cv))
        out = jnp.concatenate([full[h * HEAD_DIM:(h + 1) * HEAD_DIM, h * HEAD_DIM:(h + 1) * HEAD_DIM]
                               for h in range(NA_HEADS)], axis=1)
        o_ref[0, j * GRID_W:(j + 1) * GRID_W, :] = out.astype(BF16)


def _na_latent(l, zb, ck, cv, bias):
    tq = NA_ROWS * GRID_W
    return pl.pallas_call(
        _na_lat_kernel,
        out_shape=jax.ShapeDtypeStruct((DEC_BATCH, TOKENS, NA_WIDTH), BF16),
        grid=(DEC_BATCH, GRID_H // NA_ROWS),
        in_specs=[
            pl.BlockSpec((1, tq, NA_WIDTH), lambda b, i: (b, i, Z_QNA // NA_WIDTH)),
            pl.BlockSpec((1, TOKENS, NA_WIDTH), lambda b, i: (b, 0, Z_KNA // NA_WIDTH)),
            pl.BlockSpec((1, TOKENS, NA_WIDTH), lambda b, i: (b, 0, Z_VNA // NA_WIDTH)),
            pl.BlockSpec((1, 1, PAST_LEN, NA_WIDTH), lambda b, i: (b, l, 0, 0)),
            pl.BlockSpec((1, 1, PAST_LEN, NA_WIDTH), lambda b, i: (b, l, 0, 0)),
            pl.BlockSpec((1, 2 * NA_WIN_R - 1, GRID_W, NA_WIDTH), lambda b, i: (l, 0, 0, 0)),
        ],
        out_specs=pl.BlockSpec((1, tq, NA_WIDTH), lambda b, i: (b, i, 0)),
        compiler_params=pltpu.CompilerParams(dimension_semantics=("parallel", "arbitrary"),
                                             vmem_limit_bytes=VMEM_LIMIT),
        name="na_lat",
    )(zb, zb, zb, ck, cv, bias)


def _na_bias_table(tab):
    q = np.arange(GRID_W)
    kc = np.arange(GRID_W)
    cs = np.clip(q - NA_WIN_C // 2, 0, GRID_W - NA_WIN_C)
    valid = (kc[:, None] >= cs[None, :]) & (kc[:, None] < cs[None, :] + NA_WIN_C)
    w = GRID_W - 1
    rev = jnp.flip(tab.astype(F32).transpose(0, 2, 1, 3), axis=-1)
    padded = jnp.pad(rev, ((0, 0), (0, 0), (0, 0), (w, w)))
    off = w + NA_WIN_C - 1
    cols = jnp.stack([padded[..., off - i:off - i + GRID_W] for i in range(GRID_W)], axis=2)
    cols = jnp.where(valid[:, None, :], cols, -jnp.inf)
    return cols.reshape(DEPTH, 2 * NA_WIN_R - 1, GRID_W, NA_WIDTH)


GQA_TK = 256
GQA_KEYS = DEC_SEQ + PAST_LEN


def _gqa_kernel(q_ref, ktn_ref, ktc_ref, vn_ref, vc_ref, o_ref, s_s, m_s, *, tq):
    q = q_ref[0]
    n_new = DEC_SEQ // GQA_TK
    half = GQA_REP * tq // 2
    qh = [jnp.concatenate([q[:, r * HEAD_DIM:(r + 1) * HEAD_DIM] for r in (2 * i, 2 * i + 1)], axis=0)
          for i in range(2)]
    rows = [slice(i * half, (i + 1) * half) for i in range(2)]

    def fold(s):
        return jnp.maximum(s[:, :LANES], s[:, LANES:])

    for i in range(2):
        s = _dot(qh[i], ktc_ref[0])
        s_s[rows[i], DEC_SEQ:GQA_KEYS] = s
        m_s[rows[i], :] = fold(s)
    for c in range(0, n_new, 2):
        for i in range(2):
            s0 = _dot(qh[i], ktn_ref[0, :, c * GQA_TK:(c + 1) * GQA_TK])
            s1 = _dot(qh[i], ktn_ref[0, :, (c + 1) * GQA_TK:(c + 2) * GQA_TK])
            s_s[rows[i], c * GQA_TK:(c + 1) * GQA_TK] = s0
            s_s[rows[i], (c + 1) * GQA_TK:(c + 2) * GQA_TK] = s1
            m_s[rows[i], :] = jnp.maximum(m_s[rows[i], :], jnp.maximum(fold(s0), fold(s1)))

    m = jnp.max(m_s[...], axis=-1, keepdims=True)
    m_s[...] = jnp.broadcast_to(m, m_s.shape)

    def probs(i, c0):
        mb = m_s[rows[i], :]
        return jnp.exp2(s_s[rows[i], c0:c0 + GQA_TK] - jnp.concatenate([mb, mb], axis=1)).astype(BF16)

    acc = [_dot(probs(i, DEC_SEQ), vc_ref[0, 0]) for i in range(2)]
    for c in range(n_new):
        for i in range(2):
            acc[i] = acc[i] + _dot(probs(i, c * GQA_TK), vn_ref[0, 0, c * GQA_TK:(c + 1) * GQA_TK, :])
    outs = []
    for i in range(2):
        out = acc[i][:, :HEAD_DIM] / acc[i][:, HEAD_DIM:HEAD_DIM + 1]
        outs += [out[:tq], out[tq:]]
    o_ref[0] = jnp.concatenate(outs, axis=1).astype(BF16)


def _gqa_latent(zb, kt_new, kt_cache, v1_new, v1_cache):
    tq = 256
    rows = GQA_REP * tq
    return pl.pallas_call(
        functools.partial(_gqa_kernel, tq=tq),
        out_shape=jax.ShapeDtypeStruct((DEC_BATCH, TOKENS, GQA_WIDTH), BF16),
        grid=(DEC_BATCH, GQA_KV_HEADS, TOKENS // tq),
        in_specs=[
            pl.BlockSpec((1, tq, 256), lambda b, g, i: (b, i, Z_QG // 256 + g)),
            pl.BlockSpec((1, HEAD_DIM, DEC_SEQ), lambda b, g, i: (b, g, 0)),
            pl.BlockSpec((1, HEAD_DIM, PAST_LEN), lambda b, g, i: (b, g, 0)),
            pl.BlockSpec((1, 1, DEC_SEQ, LANES), lambda b, g, i: (b, g, 0, 0)),
            pl.BlockSpec((1, 1, PAST_LEN, LANES), lambda b, g, i: (b, g, 0, 0)),
        ],
        out_specs=pl.BlockSpec((1, tq, 256), lambda b, g, i: (b, i, g)),
        scratch_shapes=[pltpu.VMEM((rows, GQA_KEYS), F32), pltpu.VMEM((rows, LANES), F32)],
        compiler_params=pltpu.CompilerParams(dimension_semantics=("parallel", "parallel", "arbitrary"),
                                             vmem_limit_bytes=VMEM_LIMIT),
        name="gqa_lat",
    )(zb, kt_new, kt_cache, v1_new, v1_cache)


FF_CHUNK = 256


def _ffn_kernel(x_ref, ssm_ref, na_ref, gqa_ref, mod_ref, wout_ref, ln1g_ref, ln1b_ref, win_ref, wo_ref,
                ln2g_ref, ln2b_ref, o_ref, h2_s):
    ssm = jnp.concatenate([ssm_ref[0, 0], ssm_ref[0, 1]], axis=1)
    o = (_dot(ssm.astype(BF16), wout_ref[0, 0:SSM_WIDTH, :])
         + _dot(na_ref[0], wout_ref[0, SSM_WIDTH:SSM_WIDTH + NA_WIDTH, :])
         + _dot(gqa_ref[0], wout_ref[0, SSM_WIDTH + NA_WIDTH:, :]))
    y = DEEPNORM_ALPHA * x_ref[0] + mod_ref[0, 2:3, :] * o
    x1 = _ln(y) * ln1g_ref[0] + ln1b_ref[0]
    h2 = _ln(x1) * (1.0 + mod_ref[0, 4:5, :]) + mod_ref[0, 3:4, :]
    h2_s[...] = h2.astype(BF16)
    acc = None
    for j in range(D_FF // FF_CHUNK):
        h2b = h2_s[...]
        a = _dot(h2b, win_ref[0, :, j * FF_CHUNK:(j + 1) * FF_CHUNK])
        gt = _dot(h2b, win_ref[0, :, D_FF + j * FF_CHUNK:D_FF + (j + 1) * FF_CHUNK])
        f = (a * _sigmoid(a) * gt).astype(BF16)
        c = _dot(f, wo_ref[0, j * FF_CHUNK:(j + 1) * FF_CHUNK, :])
        acc = c if acc is None else acc + c
    y = DEEPNORM_ALPHA * x1 + mod_ref[0, 5:6, :] * acc
    o_ref[0] = _ln(y) * ln2g_ref[0] + ln2b_ref[0]


def _out_ffn(l, x, ssm, na, gqa, mod, w_out, ln1g, ln1b, w_ffn_in, w_ffn_out, ln2g, ln2b):
    bg = x.shape[0]
    tm = 512
    vec = pl.BlockSpec((1, 1, D_MODEL), lambda b, i: (l, 0, 0))
    resident = lambda shape: pl.BlockSpec((1,) + shape, lambda b, i: (l, 0, 0), pipeline_mode=pl.Buffered(1))
    return pl.pallas_call(
        _ffn_kernel,
        out_shape=jax.ShapeDtypeStruct((bg, TOKENS, D_MODEL), F32),
        grid=(bg, TOKENS // tm),
        in_specs=[
            pl.BlockSpec((1, tm, D_MODEL), lambda b, i: (b, i, 0)),
            pl.BlockSpec((1, 2, tm, LANES), lambda b, i: (b, 0, i, 0)),
            pl.BlockSpec((1, tm, NA_WIDTH), lambda b, i: (b, i, 0)),
            pl.BlockSpec((1, tm, GQA_WIDTH), lambda b, i: (b, i, 0)),
            pl.BlockSpec((1, 6, D_MODEL), lambda b, i: (b, 0, 0)),
            resident((D_MODEL, D_MODEL)),
            vec,
            vec,
            resident((D_MODEL, 2 * D_FF)),
            resident((D_FF, D_MODEL)),
            vec,
            vec,
        ],
        out_specs=pl.BlockSpec((1, tm, D_MODEL), lambda b, i: (b, i, 0)),
        scratch_shapes=[pltpu.VMEM((tm, D_MODEL), BF16)],
        compiler_params=pltpu.CompilerParams(dimension_semantics=("parallel", "parallel"),
                                             vmem_limit_bytes=VMEM_LIMIT),
        name="ffn",
    )(x, ssm, na, gqa, mod, w_out, ln1g, ln1b, w_ffn_in, w_ffn_out, ln2g, ln2b)


def _rope_tables():
    nf = HEAD_DIM // 4
    t = np.arange(DEC_SEQ)
    inv = ROPE_THETA ** (-np.arange(nf, dtype=np.float64) / nf)
    pos = np.stack([t // GRID_W, t % GRID_W], axis=1).astype(np.float64)
    ang = pos[:, :, None] * inv
    cos = np.repeat(np.cos(ang)[:, :, None, :], 2, axis=2).reshape(DEC_SEQ, HEAD_DIM)
    sin = (np.sin(ang)[:, :, None, :] * np.array([-1.0, 1.0])[None, None, :, None]).reshape(DEC_SEQ, HEAD_DIM)
    return jnp.asarray(np.tile(cos, (1, 2)), F32), jnp.asarray(np.tile(sin, (1, 2)), F32)


def _head_mean_matrix():
    h = np.arange(GQA_WIDTH) // HEAD_DIM
    return jnp.asarray((h[:, None] == h[None, :]).astype(np.float32) / HEAD_DIM, BF16)


def kernel(x_prompt, x_sample, c, cache_na_k, cache_na_v, cache_gqa_k, cache_gqa_v, state_ssm_re, state_ssm_im,
           c_ctx, w_ada, b_ada, w_in, w_out, q_norm_g, k_norm_g, na_bias, ssm_lam_re, ssm_lam_im, ssm_log_dt,
           ssm_b_re, ssm_b_im, ssm_c_re, ssm_c_im, ssm_d, w_ssm_glu, ln1_g, ln1_b, ln2_g, ln2_b,
           w_ffn_in, w_ffn_out):
    cond8 = jnp.concatenate([c_ctx[None, :], c, jnp.zeros((8 - 1 - DEC_BATCH, D_MODEL), F32)], axis=0)
    mod = _modulation(cond8, w_ada, b_ada).reshape(DEPTH, 8, 6, D_MODEL)
    cos, sin = _rope_tables()
    hm = _head_mean_matrix()
    w_in_b = w_in.astype(BF16)
    w_out_b = w_out.astype(BF16)
    w_ffn_in_b = w_ffn_in.astype(BF16)
    w_ffn_out_b = w_ffn_out.astype(BF16)
    w_glu_b = w_ssm_glu.astype(BF16)
    qg = jnp.tile(q_norm_g, (1, GQA_HEADS))[:, None, :]
    kg = jnp.tile(k_norm_g, (1, GQA_KV_HEADS))[:, None, :]
    dskip = ssm_d[:, None, :]
    ln = (ln1_g[:, None, :], ln1_b[:, None, :], ln2_g[:, None, :], ln2_b[:, None, :])
    *ssm_tabs, a1, a2 = _ssm_tables(ssm_lam_re, ssm_lam_im, ssm_log_dt, ssm_b_re, ssm_b_im, ssm_c_re, ssm_c_im)
    ops = _ssm_operators(ssm_tabs)
    a1 = a1.reshape(DEPTH, 2, STATE_LANES)
    a2 = a2.reshape(DEPTH, 2, STATE_LANES)
    na_tab = _na_bias_table(na_bias)
    ck_na = cache_na_k.reshape(DEC_BATCH, DEPTH, PAST_LEN, NA_WIDTH).astype(BF16)
    cv_na = cache_na_v.reshape(DEC_BATCH, DEPTH, PAST_LEN, NA_WIDTH).astype(BF16)
    kt_cache = cache_gqa_k.transpose(1, 0, 3, 4, 2).reshape(DEPTH, DEC_BATCH, GQA_KV_WIDTH, PAST_LEN).astype(BF16)
    cv_g = cache_gqa_v.transpose(1, 0, 3, 2, 4)
    v1_cache = jnp.concatenate([cv_g, jnp.ones(cv_g.shape[:-1] + (1,), F32),
                                jnp.zeros(cv_g.shape[:-1] + (LANES - HEAD_DIM - 1,), F32)], axis=-1).astype(BF16)
    s_re = state_ssm_re.astype(F32).transpose(1, 0, 2, 3, 4)
    s_im = state_ssm_im.astype(F32).transpose(1, 0, 2, 3, 4)
    h0s_lat = jnp.concatenate([s_re, s_im], axis=-1).reshape(DEPTH, DEC_BATCH, 2, STATE_LANES)
    h0p_lat = jnp.concatenate([s_im, s_re], axis=-1).reshape(DEPTH, DEC_BATCH, 2, STATE_LANES)
    h0_ctx = jnp.zeros((1, 2, STATE_LANES), F32)

    y_ctx = x_prompt.reshape(1, TOKENS, D_MODEL)
    y_lat = x_sample
    kv_na_l, kv_g_l, fin_l = [], [], []
    for l in range(DEPTH):
        mod_c = mod[l, 0:1]
        u, zb, kv_na, kv_g = _in_projection(l, y_ctx, mod_c, w_in_b, qg, kg, hm, cos, sin, latent=False)
        ssm_o, fin = _ssm(l, u, ops, a1, a2, h0_ctx, h0_ctx, dskip, w_glu_b, n_seq=BATCH, emit_final=True)
        na_o, gqa_o = _ctx_attention(zb)
        y_ctx = _out_ffn(l, y_ctx, ssm_o, na_o, gqa_o, mod_c, w_out_b, ln[0], ln[1], w_ffn_in_b, w_ffn_out_b,
                         ln[2], ln[3])
        kv_na_l.append(kv_na); kv_g_l.append(kv_g); fin_l.append(fin)

        mod_s = mod[l, 1:1 + DEC_BATCH]
        u, zb, kt_new, v1_new = _in_projection(l, y_lat, mod_s, w_in_b, qg, kg, hm, cos, sin, latent=True)
        ssm_o = _ssm(l, u, ops, a1, a2, h0s_lat[l], h0p_lat[l], dskip, w_glu_b, n_seq=1,
                     emit_final=False)[0]
        na_o = _na_latent(l, zb, ck_na, cv_na, na_tab)
        gqa_o = _gqa_latent(zb, kt_new, kt_cache[l], v1_new, v1_cache[l])
        y_lat = _out_ffn(l, y_lat, ssm_o, na_o, gqa_o, mod_s, w_out_b, ln[0], ln[1], w_ffn_in_b, w_ffn_out_b,
                         ln[2], ln[3])

    kv_na = jnp.stack(kv_na_l, axis=0).reshape(DEPTH, BATCH, SEQ, 2, NA_HEADS, HEAD_DIM)
    kv_g = jnp.stack(kv_g_l, axis=0).reshape(DEPTH, BATCH, SEQ, 2, GQA_KV_HEADS, HEAD_DIM)
    kv_na = kv_na.transpose(3, 1, 0, 2, 4, 5)
    kv_g = kv_g.transpose(3, 1, 0, 2, 4, 5)
    fin = jnp.stack(fin_l, axis=0).reshape(DEPTH, 2, BATCH, SSM_GROUPS, 2, SSM_STATE)
    fin = fin.transpose(4, 2, 0, 1, 3, 5)
    return (y_ctx.reshape(BATCH, SEQ, D_MODEL), y_lat, kv_na[0], kv_na[1], kv_g[0], kv_g[1], fin[0], fin[1])
```

```python
import functools
import math

import numpy as np
import jax
import jax.numpy as jnp
from jax import lax
from jax.experimental import pallas as pl
from jax.experimental.pallas import tpu as pltpu

F32 = jnp.float32
BF16 = jnp.bfloat16

D_MODEL = 1024
BATCH = 16
SEQ = 256
DEPTH = 2
DEC_BATCH = 2
DEC_SEQ = 4096
PAST_LEN = 256
GRID_W = 64
GRID_H = DEC_SEQ // GRID_W
HEAD_DIM = 64
SSM_WIDTH = 256
SSM_GROUP_CH = 16
SSM_GROUPS = 16
SSM_STATE = 64
NA_HEADS = 4
NA_WIDTH = 256
NA_WIN_R = 8
NA_WIN_C = 16
GQA_HEADS = 8
GQA_KV_HEADS = 2
GQA_REP = 4
GQA_WIDTH = 512
GQA_KV_WIDTH = 128
IN_WIDTH = 1792
D_FF = 2816
ROPE_THETA = 10000.0
LN_EPS = 1e-6
RMS_EPS = 1e-6
DEEPNORM_ALPHA = (2 * DEPTH) ** 0.25
QK_SCALE = HEAD_DIM ** -0.5
LOG2E = math.log2(math.e)

TOKENS = 4096
SSM_CHUNK = 16
N_CHUNK = TOKENS // SSM_CHUNK
Z_WIDTH = IN_WIDTH - SSM_WIDTH
Z_QNA, Z_KNA, Z_VNA, Z_QG, Z_KG, Z_VG = 0, 256, 512, 768, 1280, 1408

VMEM_LIMIT = 56 * 1024 * 1024
LANES = 128
MXU_EDGE = 256
SSM_FLAT = SSM_CHUNK * SSM_GROUP_CH
assert SSM_FLAT == MXU_EDGE

MOD_NB = 1536
WIN_TM = 1024
FFN_TM = 512
GQA_TQ = 256


def _sigmoid(x):
    return 1.0 / (1.0 + jnp.exp(-x))


def _ln(x):
    mu = jnp.mean(x, axis=-1, keepdims=True)
    xc = x - mu
    var = jnp.mean(xc * xc, axis=-1, keepdims=True)
    return xc * lax.rsqrt(var + LN_EPS)


def _dot(a, b):
    return jnp.dot(a, b, preferred_element_type=F32)


def _mod_kernel(c_ref, w_ref, b_ref, o_ref):
    c = c_ref[...]
    s = c * _sigmoid(c)
    o_ref[0] = jnp.dot(s, w_ref[0], preferred_element_type=F32, precision=lax.Precision.HIGHEST) + b_ref[0]


def _modulation(cond8, w_ada, b_ada):
    nb = MOD_NB
    return pl.pallas_call(
        _mod_kernel,
        out_shape=jax.ShapeDtypeStruct((DEPTH, 8, 6 * D_MODEL), F32),
        grid=(DEPTH, 6 * D_MODEL // nb),
        in_specs=[
            pl.BlockSpec((8, D_MODEL), lambda l, j: (0, 0)),
            pl.BlockSpec((1, D_MODEL, nb), lambda l, j: (l, 0, j)),
            pl.BlockSpec((1, 1, nb), lambda l, j: (l, 0, j)),
        ],
        out_specs=pl.BlockSpec((1, 8, nb), lambda l, j: (l, 0, j)),
        compiler_params=pltpu.CompilerParams(dimension_semantics=("parallel", "parallel"),
                                             vmem_limit_bytes=VMEM_LIMIT),
        name="mod",
    )(cond8, w_ada, b_ada.reshape(DEPTH, 1, 6 * D_MODEL))


def _swap16(x):
    w = x.shape[-1]
    lane = lax.broadcasted_iota(jnp.int32, x.shape, x.ndim - 1)
    return jnp.where((lane & 16) != 0, pltpu.roll(x, 16, x.ndim - 1), pltpu.roll(x, w - 16, x.ndim - 1))


def _win_kernel(x_ref, mod_ref, w_ref, qg_ref, kg_ref, hm_ref, cos_ref, sin_ref, *out_refs, latent):
    u_ref, z_ref = out_refs[0], out_refs[1]
    x = x_ref[0]
    h = _ln(x) * (1.0 + mod_ref[0, 1:2, :]) + mod_ref[0, 0:1, :]
    z = _dot(h.astype(BF16), w_ref[0])
    u_ref[0, 0] = z[:, 0:LANES]
    u_ref[0, 1] = z[:, LANES:SSM_WIDTH]
    c_qna, c_kna, c_qg, c_kg, c_vg = np.cumsum([SSM_WIDTH, NA_WIDTH, 2 * NA_WIDTH, GQA_WIDTH, GQA_KV_WIDTH])
    q_na = z[:, c_qna:c_kna] * QK_SCALE
    kv_na = z[:, c_kna:c_qg]
    q_g = z[:, c_qg:c_kg]
    k_g = z[:, c_kg:c_vg]
    v_g = z[:, c_vg:IN_WIDTH]
    q_ms = _dot((q_g * q_g).astype(BF16), hm_ref[...])
    k_ms = _dot((k_g * k_g).astype(BF16), hm_ref[0:GQA_KV_WIDTH, 0:GQA_KV_WIDTH])
    q_n = q_g * lax.rsqrt(q_ms + RMS_EPS) * qg_ref[0]
    k_n = k_g * lax.rsqrt(k_ms + RMS_EPS) * kg_ref[0]
    if latent:
        kt_ref, v1_ref = out_refs[2], out_refs[3]
        cos = cos_ref[...]
        sin = sin_ref[...]
        cos4 = jnp.concatenate([cos] * 4, axis=1)
        sin4 = jnp.concatenate([sin] * 4, axis=1)
        q_n = (q_n * cos4 + _swap16(q_n) * sin4) * (QK_SCALE * LOG2E)
        k_n = k_n * cos + _swap16(k_n) * sin
        kt_ref[0] = k_n.T.astype(BF16)
        lane = lax.broadcasted_iota(jnp.int32, v_g.shape, 1)
        ones_col = jnp.where(lane == HEAD_DIM, 1.0, 0.0)
        v1_ref[0, 0] = jnp.where(lane < HEAD_DIM, v_g, ones_col).astype(BF16)
        v1_ref[0, 1] = jnp.where(lane < HEAD_DIM, pltpu.roll(v_g, HEAD_DIM, 1), ones_col).astype(BF16)
    else:
        out_refs[2][0] = kv_na
        out_refs[3][0] = jnp.concatenate([k_n, v_g], axis=1)
        q_n = q_n * QK_SCALE
    zb = jnp.concatenate([q_na, kv_na, q_n, k_n, v_g], axis=1)
    z_ref[0] = zb.astype(BF16)


def _in_projection(l, x, mod, w_in, qg, kg, hm, cos, sin, *, latent):
    bg = x.shape[0]
    tm = WIN_TM
    out_shape = [jax.ShapeDtypeStruct((bg, 2, TOKENS, LANES), F32),
                 jax.ShapeDtypeStruct((bg, TOKENS, Z_WIDTH), BF16)]
    out_specs = [pl.BlockSpec((1, 2, tm, LANES), lambda b, i: (b, 0, i, 0)),
                 pl.BlockSpec((1, tm, Z_WIDTH), lambda b, i: (b, i, 0))]
    if latent:
        out_shape += [jax.ShapeDtypeStruct((bg, GQA_KV_WIDTH, TOKENS), BF16),
                      jax.ShapeDtypeStruct((bg, GQA_KV_HEADS, TOKENS, LANES), BF16)]
        out_specs += [pl.BlockSpec((1, GQA_KV_WIDTH, tm), lambda b, i: (b, 0, i)),
                      pl.BlockSpec((1, GQA_KV_HEADS, tm, LANES), lambda b, i: (b, 0, i, 0))]
    else:
        out_shape += [jax.ShapeDtypeStruct((bg, TOKENS, 2 * NA_WIDTH), F32),
                      jax.ShapeDtypeStruct((bg, TOKENS, 2 * GQA_KV_WIDTH), F32)]
        out_specs += [pl.BlockSpec((1, tm, 2 * NA_WIDTH), lambda b, i: (b, i, 0)),
                      pl.BlockSpec((1, tm, 2 * GQA_KV_WIDTH), lambda b, i: (b, i, 0))]
    return pl.pallas_call(
        functools.partial(_win_kernel, latent=latent),
        out_shape=out_shape,
        grid=(bg, TOKENS // tm),
        in_specs=[
            pl.BlockSpec((1, tm, D_MODEL), lambda b, i: (b, i, 0)),
            pl.BlockSpec((1, 6, D_MODEL), lambda b, i: (b, 0, 0)),
            pl.BlockSpec((1, D_MODEL, IN_WIDTH), lambda b, i: (l, 0, 0)),
            pl.BlockSpec((1, 1, GQA_WIDTH), lambda b, i: (l, 0, 0)),
            pl.BlockSpec((1, 1, GQA_KV_WIDTH), lambda b, i: (l, 0, 0)),
            pl.BlockSpec((GQA_WIDTH, GQA_WIDTH), lambda b, i: (0, 0)),
            pl.BlockSpec((tm, LANES), lambda b, i: (i, 0)),
            pl.BlockSpec((tm, LANES), lambda b, i: (i, 0)),
        ],
        out_specs=out_specs,
        compiler_params=pltpu.CompilerParams(dimension_semantics=("parallel", "parallel"),
                                             vmem_limit_bytes=VMEM_LIMIT),
        name="win_lat" if latent else "win_ctx",
    )(x, mod, w_in, qg, kg, hm, cos, sin)


STATE_LANES = SSM_GROUPS * LANES
SSM_UNROLL = 8


def _ssm_ops_kernel(ct_r_ref, ct_i_ref, pk_r_ref, pk_i_ref, pc_r_ref, pc_i_ref, bb1_ref, bb2_ref, pw_r_ref, pw_i_ref,
                    tile_ref, rep_ref, tile4_ref, m_ref, bend_ref, cpow_ref):
    d = pl.program_id(1)

    def onto_lanes(x, sel):
        hi = x.astype(BF16)
        lo = (x - hi.astype(F32)).astype(BF16)
        return _dot(hi, sel) + _dot(lo, sel)

    def build_ops(g, carry):
        cr = onto_lanes(ct_r_ref[0, 0, g], tile_ref[...])
        ci = onto_lanes(ct_i_ref[0, 0, g], tile_ref[...])

        def re_proj(pr_ref, pi_ref):
            pr = onto_lanes(pr_ref[0, 0, g], rep_ref[...])
            pi = onto_lanes(pi_ref[0, 0, g], rep_ref[...])
            return jnp.concatenate([cr * pr - ci * pi, -(cr * pi + ci * pr)], axis=0)
        cpow_ref[0, 0, g] = re_proj(pc_r_ref, pc_i_ref).astype(BF16)
        bb1 = bb1_ref[0, 0, g]
        bb2 = bb2_ref[0, 0, g]
        kcat = jnp.dot(bb1[:, :LANES], re_proj(pk_r_ref, pk_i_ref),
                       preferred_element_type=F32, precision=lax.Precision.HIGHEST)
        lane = lax.broadcasted_iota(jnp.int32, kcat.shape, 1)
        pw_r = onto_lanes(pw_r_ref[0, 0, g], tile4_ref[...])
        pw_i = onto_lanes(pw_i_ref[0, 0, g], tile4_ref[...])
        for t in range(SSM_CHUNK):
            rows = slice(t * SSM_GROUP_CH, (t + 1) * SSM_GROUP_CH)
            lo, hi = t * SSM_GROUP_CH, (t + 1) * SSM_GROUP_CH
            fwd = kcat if t == 0 else jnp.where(lane >= lo, pltpu.roll(kcat, lo, 1), 0.0)
            bwd = kcat if hi == SSM_FLAT else jnp.where(lane < hi, pltpu.roll(kcat, hi, 1), 0.0)
            m_ref[0, 0, g, rows, :] = jnp.where(d == 0, fwd, bwd).astype(BF16)
            bend_ref[0, 0, g, rows, :] = (pw_r[t:t + 1, :] * bb1 + pw_i[t:t + 1, :] * bb2).astype(BF16)
        return carry
    lax.fori_loop(0, SSM_GROUPS, build_ops, 0, unroll=2)


def _ssm_operators(tables):
    per_dir = lambda rows, cols: pl.BlockSpec((1, 1, SSM_GROUPS, rows, cols), lambda l, d: (l, d, 0, 0, 0))
    const = lambda rows: pl.BlockSpec((rows, SSM_FLAT), lambda l, d: (0, 0))
    shape = lambda rows: jax.ShapeDtypeStruct((DEPTH, 2, SSM_GROUPS, rows, SSM_FLAT), BF16)
    c = np.arange(SSM_FLAT)
    tile = (c[None, :] % SSM_GROUP_CH == np.arange(SSM_GROUP_CH)[:, None]).astype(np.float32)
    rep = (c[None, :] // SSM_GROUP_CH == np.arange(SSM_CHUNK)[:, None]).astype(np.float32)
    tile4 = (c[None, :] % SSM_STATE == np.arange(SSM_STATE)[:, None]).astype(np.float32)
    return pl.pallas_call(
        _ssm_ops_kernel,
        out_shape=[shape(SSM_FLAT), shape(SSM_FLAT), shape(LANES)],
        grid=(DEPTH, 2),
        in_specs=([per_dir(SSM_STATE, SSM_GROUP_CH)] * 6 + [per_dir(SSM_GROUP_CH, SSM_FLAT)] * 2
                  + [per_dir(SSM_CHUNK, SSM_STATE)] * 2 + [const(SSM_GROUP_CH), const(SSM_CHUNK), const(SSM_STATE)]),
        out_specs=[per_dir(SSM_FLAT, SSM_FLAT), per_dir(SSM_FLAT, SSM_FLAT), per_dir(LANES, SSM_FLAT)],
        compiler_params=pltpu.CompilerParams(dimension_semantics=("parallel", "parallel"),
                                             vmem_limit_bytes=VMEM_LIMIT),
        name="ssm_ops",
    )(*tables, jnp.asarray(tile, BF16), jnp.asarray(rep, BF16), jnp.asarray(tile4, BF16))


def _ssm_kernel(u_ref, m_ref, bend_ref, cpow_ref, a1_ref, a2_ref, h0s_ref, h0p_ref, d_ref, wglu_ref, *refs,
                n_seq, emit_final):
    if emit_final:
        out_ref, fin_ref, y_s, yt_s, u_s, inja_s, injb_s, hp_s = refs
    else:
        out_ref, y_s, yt_s, u_s, inja_s, injb_s, hp_s = refs
        fin_ref = None
    chunks_per_seq = N_CHUNK // n_seq

    def chunk_rows(t):
        sl = pl.ds(t, N_CHUNK, stride=SSM_CHUNK)
        return jnp.concatenate([u_ref[0, 0, sl, :], u_ref[0, 1, sl, :]], axis=1)

    for t in range(SSM_CHUNK):
        a_t = chunk_rows(t)
        yt_s[:, t * SSM_GROUP_CH:(t + 1) * SSM_GROUP_CH, :] = a_t.T.reshape(SSM_GROUPS, SSM_GROUP_CH, N_CHUNK)

    def tr_in(g, carry):
        u_s[g] = yt_s[g].T.astype(BF16)
        return carry
    lax.fori_loop(0, SSM_GROUPS, tr_in, 0, unroll=SSM_UNROLL)

    for d in range(2):
        def proj(g, carry, d=d):
            ug = u_s[g]
            y = _dot(ug, m_ref[0, d, g])
            y_s[g] = y if d == 0 else y_s[g] + y
            inj = _dot(ug, bend_ref[0, d, g])
            lanes = pl.ds(pl.multiple_of(g * LANES, LANES), LANES)
            inja_s[d, :, lanes] = inj[:, :LANES]
            injb_s[d, :, lanes] = inj[:, LANES:]
            return carry
        lax.fori_loop(0, SSM_GROUPS, proj, 0, unroll=SSM_UNROLL)

    a1 = a1_ref[0]
    a2 = a2_ref[0]

    def step(j, carry):
        s, sp = carry
        nf = j
        nb = N_CHUNK - 1 - j
        if n_seq > 1:
            keep = jnp.where(j % chunks_per_seq != 0, 1.0, 0.0).astype(F32)
            s = s * keep
            sp = sp * keep
        hp_s[0, pl.ds(nf, 1), :] = s[0:1]
        hp_s[1, pl.ds(nb, 1), :] = s[1:2]
        ia = jnp.concatenate([inja_s[0, pl.ds(nf, 1), :], inja_s[1, pl.ds(nb, 1), :]], axis=0)
        ib = jnp.concatenate([injb_s[0, pl.ds(nf, 1), :], injb_s[1, pl.ds(nb, 1), :]], axis=0)
        s_new = a1 * s + a2 * sp + ia
        sp_new = a1 * sp - a2 * s + ib
        if emit_final:
            @pl.when(j % chunks_per_seq == chunks_per_seq - 1)
            def _():
                sq = j // chunks_per_seq
                fin_ref[0, 0, pl.ds(sq, 1), :] = s_new[0:1]
                fin_ref[0, 1, pl.ds(n_seq - 1 - sq, 1), :] = s_new[1:2]
        return s_new, sp_new
    lax.fori_loop(0, N_CHUNK, step, (h0s_ref[0], h0p_ref[0]))

    for d in range(2):
        def carry_in(g, carry, d=d):
            hp = hp_s[d, :, pl.ds(pl.multiple_of(g * LANES, LANES), LANES)]
            y_s[g] = y_s[g] + _dot(hp.astype(BF16), cpow_ref[0, d, g])
            return carry
        lax.fori_loop(0, SSM_GROUPS, carry_in, 0, unroll=SSM_UNROLL)

    def tr_out(g, carry):
        yt_s[g] = y_s[g].T
        return carry
    lax.fori_loop(0, SSM_GROUPS, tr_out, 0, unroll=SSM_UNROLL)
    dskip = d_ref[0]
    wglu = wglu_ref[0]
    for t in range(SSM_CHUNK):
        blk = yt_s[:, t * SSM_GROUP_CH:(t + 1) * SSM_GROUP_CH, :].reshape(SSM_WIDTH, N_CHUNK)
        y = blk.T + chunk_rows(t) * dskip
        gl = jax.nn.gelu(y, approximate=True)
        gate = _dot(gl.astype(BF16), wglu)
        o = gl * _sigmoid(gate)
        out_ref[0, 0, pl.ds(t, N_CHUNK, stride=SSM_CHUNK), :] = o[:, :LANES]
        out_ref[0, 1, pl.ds(t, N_CHUNK, stride=SSM_CHUNK), :] = o[:, LANES:]


def _ssm(l, u, ops, a1, a2, h0s, h0p, dskip, wglu, *, n_seq, emit_final):
    bg = u.shape[0]
    m, bend, cpow = ops
    out_shape = [jax.ShapeDtypeStruct((bg, 2, TOKENS, LANES), F32)]
    out_specs = [pl.BlockSpec((1, 2, TOKENS, LANES), lambda b: (b, 0, 0, 0))]
    if emit_final:
        out_shape.append(jax.ShapeDtypeStruct((bg, 2, n_seq, STATE_LANES), F32))
        out_specs.append(pl.BlockSpec((1, 2, n_seq, STATE_LANES), lambda b: (b, 0, 0, 0)))
    resident = lambda rows: pl.BlockSpec((1, 2, SSM_GROUPS, rows, SSM_FLAT), lambda b: (l, 0, 0, 0, 0),
                                         pipeline_mode=pl.Buffered(1))
    return pl.pallas_call(
        functools.partial(_ssm_kernel, n_seq=n_seq, emit_final=emit_final),
        out_shape=out_shape,
        grid=(bg,),
        in_specs=[
            pl.BlockSpec((1, 2, TOKENS, LANES), lambda b: (b, 0, 0, 0)),
            resident(SSM_FLAT), resident(SSM_FLAT), resident(LANES),
            pl.BlockSpec((1, 2, STATE_LANES), lambda b: (l, 0, 0)),
            pl.BlockSpec((1, 2, STATE_LANES), lambda b: (l, 0, 0)),
            pl.BlockSpec((1, 2, STATE_LANES), lambda b: (b, 0, 0)),
            pl.BlockSpec((1, 2, STATE_LANES), lambda b: (b, 0, 0)),
            pl.BlockSpec((1, 1, SSM_WIDTH), lambda b: (l, 0, 0)),
            pl.BlockSpec((1, SSM_WIDTH, SSM_WIDTH), lambda b: (l, 0, 0)),
        ],
        out_specs=out_specs,
        scratch_shapes=[pltpu.VMEM((SSM_GROUPS, N_CHUNK, SSM_FLAT), F32),
                        pltpu.VMEM((SSM_GROUPS, SSM_FLAT, N_CHUNK), F32),
                        pltpu.VMEM((SSM_GROUPS, N_CHUNK, SSM_FLAT), BF16),
                        pltpu.VMEM((2, N_CHUNK, STATE_LANES), F32),
                        pltpu.VMEM((2, N_CHUNK, STATE_LANES), F32),
                        pltpu.VMEM((2, N_CHUNK, STATE_LANES), F32)],
        compiler_params=pltpu.CompilerParams(dimension_semantics=("parallel",), vmem_limit_bytes=VMEM_LIMIT),
        name="ssm_ctx" if emit_final else "ssm_lat",
    )(u, m, bend, cpow, a1, a2, h0s, h0p, dskip, wglu)


def _ssm_tables(lam_re, lam_im, log_dt, b_re, b_im, c_re, c_im):
    t_n = SSM_CHUNK
    lr = jnp.minimum(lam_re.astype(F32), -1e-4)
    li = lam_im.astype(F32)
    dt = jnp.exp(log_dt.astype(F32))[..., None]
    lr_dt, li_dt = lr * dt, li * dt

    def power(k, expand=lambda x: x):
        mag = jnp.exp(expand(lr_dt) * k)
        return mag * jnp.cos(expand(li_dt) * k), mag * jnp.sin(expand(li_dt) * k)
    l_r, l_i = power(1.0)
    a_r, a_i = power(float(t_n))
    n_r, n_i = l_r - 1.0, l_i
    den = lr * lr + li * li
    q_r, q_i = (n_r * lr + n_i * li) / den, (n_i * lr - n_r * li) / den
    bb_r = q_r[..., None] * b_re - q_i[..., None] * b_im
    bb_i = q_r[..., None] * b_im + q_i[..., None] * b_re

    up = np.arange(t_n, dtype=np.float32)
    down = up[::-1].copy()
    per_dir = lambda fwd, bwd: jnp.asarray(np.stack([fwd, bwd]))[None, :, None, None, :]
    on_lanes = lambda x: x[..., None]
    pk_r, pk_i = power(per_dir(up, down), on_lanes)
    pc_r, pc_i = power(per_dir(up + 1.0, down + 1.0), on_lanes)
    pw_r, pw_i = power(jnp.swapaxes(per_dir(down, up), -1, -2), lambda x: x[..., None, :])
    cr_t = jnp.swapaxes(c_re.astype(F32), -1, -2)
    ci_t = jnp.swapaxes(c_im.astype(F32), -1, -2)
    br_t, bi_t = jnp.swapaxes(bb_r, -1, -2), jnp.swapaxes(bb_i, -1, -2)
    bb1 = jnp.concatenate([br_t, bi_t, bi_t, br_t], axis=-1)
    bb2 = jnp.concatenate([-bi_t, br_t, br_t, -bi_t], axis=-1)
    a1 = jnp.concatenate([a_r, a_r], axis=-1)
    a2 = jnp.concatenate([-a_i, a_i], axis=-1)
    return cr_t, ci_t, pk_r, pk_i, pc_r, pc_i, bb1, bb2, pw_r, pw_i, a1, a2


def _attend_transposed(k, v, q, n_kv, rep):
    nq = q.shape[0]
    qt = q.astype(F32).T
    zero = jnp.zeros((HEAD_DIM, nq), F32)
    bd = jnp.concatenate(
        [jnp.concatenate([qt[(g * rep + r) * HEAD_DIM:(g * rep + r + 1) * HEAD_DIM] if g == gg else zero
                          for gg in range(n_kv) for r in range(rep)], axis=1)
         for g in range(n_kv)], axis=0).astype(BF16)
    s = _dot(k, bd)
    m = jnp.max(s, axis=0, keepdims=True)
    p = jnp.exp(s - m)
    p = p * (1.0 / jnp.sum(p, axis=0, keepdims=True))
    ot = _dot(v.astype(F32).T.astype(BF16), p.astype(BF16))
    heads = [ot[g * HEAD_DIM:(g + 1) * HEAD_DIM, (g * rep + r) * nq:(g * rep + r + 1) * nq]
             for g in range(n_kv) for r in range(rep)]
    return jnp.concatenate(heads, axis=0).T


CTX_SEQS = 2


def _ctx_attn_kernel(z_ref, na_ref, gqa_ref):
    for i in range(CTX_SEQS):
        z = z_ref[i]
        na_ref[i] = _attend_transposed(z[:, Z_KNA:Z_KNA + NA_WIDTH], z[:, Z_VNA:Z_VNA + NA_WIDTH],
                                       z[:, Z_QNA:Z_QNA + NA_WIDTH], NA_HEADS, 1).astype(BF16)
        gqa_ref[i] = _attend_transposed(z[:, Z_KG:Z_KG + GQA_KV_WIDTH], z[:, Z_VG:Z_VG + GQA_KV_WIDTH],
                                        z[:, Z_QG:Z_QG + GQA_WIDTH], GQA_KV_HEADS, GQA_REP).astype(BF16)


def _ctx_attention(zb):
    z3 = zb.reshape(BATCH, SEQ, Z_WIDTH)
    na, gqa = pl.pallas_call(
        _ctx_attn_kernel,
        out_shape=[jax.ShapeDtypeStruct((BATCH, SEQ, NA_WIDTH), BF16),
                   jax.ShapeDtypeStruct((BATCH, SEQ, GQA_WIDTH), BF16)],
        grid=(BATCH // CTX_SEQS,),
        in_specs=[pl.BlockSpec((CTX_SEQS, SEQ, Z_WIDTH), lambda b: (b, 0, 0))],
        out_specs=[pl.BlockSpec((CTX_SEQS, SEQ, NA_WIDTH), lambda b: (b, 0, 0)),
                   pl.BlockSpec((CTX_SEQS, SEQ, GQA_WIDTH), lambda b: (b, 0, 0))],
        compiler_params=pltpu.CompilerParams(dimension_semantics=("parallel",), vmem_limit_bytes=VMEM_LIMIT),
        name="attn_ctx",
    )(z3)
    return na.reshape(1, TOKENS, NA_WIDTH), gqa.reshape(1, TOKENS, GQA_WIDTH)


NA_KEYS = NA_WIN_R * GRID_W


NA_ROWS = 16


def _na_lat_kernel(q_ref, k_ref, v_ref, ck_ref, cv_ref, bias_ref, o_ref):
    i = pl.program_id(1)
    ck = ck_ref[0, 0]
    cv = cv_ref[0, 0]
    qt = q_ref[0].astype(F32).T
    hd = lax.broadcasted_iota(jnp.int32, (NA_WIDTH, NA_WIDTH), 0) // HEAD_DIM
    hq = lax.broadcasted_iota(jnp.int32, (NA_WIDTH, NA_WIDTH), 1) // HEAD_DIM
    for j in range(NA_ROWS):
        r = NA_ROWS * i + j
        rs = jnp.clip(r - NA_WIN_R // 2, 0, GRID_H - NA_WIN_R)
        start = pl.multiple_of(rs * GRID_W, GRID_W)
        kw = k_ref[0, pl.ds(start, NA_KEYS), :]
        vw = v_ref[0, pl.ds(start, NA_KEYS), :]
        qj = qt[:, j * GRID_W:(j + 1) * GRID_W]
        bd = jnp.where(hd == hq, jnp.concatenate([qj] * NA_HEADS, axis=1), 0.0).astype(BF16)
        bias = bias_ref[0, pl.ds(NA_WIN_R - 1 - (r - rs), NA_WIN_R), :, :].reshape(NA_KEYS, NA_WIDTH)
        s_loc = _dot(kw, bd) + bias
        s_ctx = _dot(ck, bd)
        m = jnp.maximum(jnp.max(s_loc, axis=0, keepdims=True), jnp.max(s_ctx, axis=0, keepdims=True))
        p_loc = jnp.exp(s_loc - m)
        p_ctx = jnp.exp(s_ctx - m)
        inv = 1.0 / (jnp.sum(p_loc, axis=0, keepdims=True) + jnp.sum(p_ctx, axis=0, keepdims=True))
        full = (_dot((p_loc * inv).T.astype(BF16), vw)
                + _dot((p_ctx * inv).T.astype(BF16), cv))
        out = jnp.concatenate([full[h * HEAD_DIM:(h + 1) * HEAD_DIM, h * HEAD_DIM:(h + 1) * HEAD_DIM]
                               for h in range(NA_HEADS)], axis=1)
        o_ref[0, j * GRID_W:(j + 1) * GRID_W, :] = out.astype(BF16)


def _na_latent(l, zb, ck, cv, bias):
    tq = NA_ROWS * GRID_W
    return pl.pallas_call(
        _na_lat_kernel,
        out_shape=jax.ShapeDtypeStruct((DEC_BATCH, TOKENS, NA_WIDTH), BF16),
        grid=(DEC_BATCH, GRID_H // NA_ROWS),
        in_specs=[
            pl.BlockSpec((1, tq, NA_WIDTH), lambda b, i: (b, i, Z_QNA // NA_WIDTH)),
            pl.BlockSpec((1, TOKENS, NA_WIDTH), lambda b, i: (b, 0, Z_KNA // NA_WIDTH)),
            pl.BlockSpec((1, TOKENS, NA_WIDTH), lambda b, i: (b, 0, Z_VNA // NA_WIDTH)),
            pl.BlockSpec((1, 1, PAST_LEN, NA_WIDTH), lambda b, i: (b, l, 0, 0)),
            pl.BlockSpec((1, 1, PAST_LEN, NA_WIDTH), lambda b, i: (b, l, 0, 0)),
            pl.BlockSpec((1, 2 * NA_WIN_R - 1, GRID_W, NA_WIDTH), lambda b, i: (l, 0, 0, 0)),
        ],
        out_specs=pl.BlockSpec((1, tq, NA_WIDTH), lambda b, i: (b, i, 0)),
        compiler_params=pltpu.CompilerParams(dimension_semantics=("parallel", "arbitrary"),
                                             vmem_limit_bytes=VMEM_LIMIT),
        name="na_lat",
    )(zb, zb, zb, ck, cv, bias)


def _na_bias_table(tab):
    q = np.arange(GRID_W)
    kc = np.arange(GRID_W)
    cs = np.clip(q - NA_WIN_C // 2, 0, GRID_W - NA_WIN_C)
    valid = (kc[:, None] >= cs[None, :]) & (kc[:, None] < cs[None, :] + NA_WIN_C)
    w = GRID_W - 1
    rev = tab.astype(F32)[..., ::-1]
    padded = jnp.pad(rev, ((0, 0), (0, 0), (0, 0), (w, w)))
    off = w + NA_WIN_C - 1
    cols = jnp.stack([jnp.stack([padded[:, h, :, off - i:off - i + GRID_W] for i in range(GRID_W)], axis=2)
                      for h in range(NA_HEADS)], axis=3)
    cols = jnp.where(valid[:, None, :], cols, -jnp.inf)
    return cols.reshape(DEPTH, 2 * NA_WIN_R - 1, GRID_W, NA_WIDTH)


GQA_TK = 256
GQA_KEYS = DEC_SEQ + PAST_LEN


def _gqa_kernel(q_ref, ktn_ref, ktc_ref, vn_ref, vc_ref, o_ref, s_s, m_s, *, tq):
    q = q_ref[0]
    n_new = DEC_SEQ // GQA_TK
    half = GQA_REP * tq // 2
    qh = [jnp.concatenate([q[:, r * HEAD_DIM:(r + 1) * HEAD_DIM] for r in (2 * i, 2 * i + 1)], axis=0)
          for i in range(2)]
    rows = [slice(i * half, (i + 1) * half) for i in range(2)]

    def fold(s):
        return jnp.maximum(s[:, :LANES], s[:, LANES:])

    for i in range(2):
        s = _dot(qh[i], ktc_ref[0])
        s_s[rows[i], DEC_SEQ:GQA_KEYS] = s
        m_s[rows[i], :] = fold(s)
    for c in range(0, n_new, 2):
        for i in range(2):
            s0 = _dot(qh[i], ktn_ref[0, :, c * GQA_TK:(c + 1) * GQA_TK])
            s1 = _dot(qh[i], ktn_ref[0, :, (c + 1) * GQA_TK:(c + 2) * GQA_TK])
            s_s[rows[i], c * GQA_TK:(c + 1) * GQA_TK] = s0
            s_s[rows[i], (c + 1) * GQA_TK:(c + 2) * GQA_TK] = s1
            m_s[rows[i], :] = jnp.maximum(m_s[rows[i], :], jnp.maximum(fold(s0), fold(s1)))

    m = jnp.max(m_s[...], axis=-1, keepdims=True)
    m_s[...] = jnp.broadcast_to(m, m_s.shape)

    def probs(i, c0):
        mb = m_s[rows[i], :]
        return jnp.exp2(s_s[rows[i], c0:c0 + GQA_TK] - jnp.concatenate([mb, mb], axis=1)).astype(BF16)

    acc = [_dot(probs(i, DEC_SEQ), vc_ref[0, 0]) for i in range(2)]
    for c in range(n_new):
        for i in range(2):
            acc[i] = acc[i] + _dot(probs(i, c * GQA_TK), vn_ref[0, 0, c * GQA_TK:(c + 1) * GQA_TK, :])
    outs = []
    for i in range(2):
        out = acc[i][:, :HEAD_DIM] / acc[i][:, HEAD_DIM:HEAD_DIM + 1]
        outs += [out[:tq], out[tq:]]
    o_ref[0] = jnp.concatenate(outs, axis=1).astype(BF16)


def _gqa_latent(zb, kt_new, kt_cache, v1_new, v1_cache):
    tq = GQA_TQ
    rows = GQA_REP * tq
    return pl.pallas_call(
        functools.partial(_gqa_kernel, tq=tq),
        out_shape=jax.ShapeDtypeStruct((DEC_BATCH, TOKENS, GQA_WIDTH), BF16),
        grid=(DEC_BATCH, GQA_KV_HEADS, TOKENS // tq),
        in_specs=[
            pl.BlockSpec((1, tq, GQA_REP * HEAD_DIM), lambda b, g, i: (b, i, Z_QG // (GQA_REP * HEAD_DIM) + g)),
            pl.BlockSpec((1, HEAD_DIM, DEC_SEQ), lambda b, g, i: (b, g, 0)),
            pl.BlockSpec((1, HEAD_DIM, PAST_LEN), lambda b, g, i: (b, g, 0)),
            pl.BlockSpec((1, 1, DEC_SEQ, LANES), lambda b, g, i: (b, g, 0, 0)),
            pl.BlockSpec((1, 1, PAST_LEN, LANES), lambda b, g, i: (b, g, 0, 0)),
        ],
        out_specs=pl.BlockSpec((1, tq, GQA_REP * HEAD_DIM), lambda b, g, i: (b, i, g)),
        scratch_shapes=[pltpu.VMEM((rows, GQA_KEYS), F32), pltpu.VMEM((rows, LANES), F32)],
        compiler_params=pltpu.CompilerParams(dimension_semantics=("parallel", "parallel", "arbitrary"),
                                             vmem_limit_bytes=VMEM_LIMIT),
        name="gqa_lat",
    )(zb, kt_new, kt_cache, v1_new, v1_cache)


FF_CHUNK = 256


def _ffn_kernel(x_ref, ssm_ref, na_ref, gqa_ref, mod_ref, wout_ref, ln1g_ref, ln1b_ref, win_ref, wo_ref,
                ln2g_ref, ln2b_ref, o_ref, h2_s):
    ssm = jnp.concatenate([ssm_ref[0, 0], ssm_ref[0, 1]], axis=1)
    o = (_dot(ssm.astype(BF16), wout_ref[0, 0:SSM_WIDTH, :])
         + _dot(na_ref[0], wout_ref[0, SSM_WIDTH:SSM_WIDTH + NA_WIDTH, :])
         + _dot(gqa_ref[0], wout_ref[0, SSM_WIDTH + NA_WIDTH:, :]))
    y = DEEPNORM_ALPHA * x_ref[0] + mod_ref[0, 2:3, :] * o
    x1 = _ln(y) * ln1g_ref[0] + ln1b_ref[0]
    h2 = _ln(x1) * (1.0 + mod_ref[0, 4:5, :]) + mod_ref[0, 3:4, :]
    h2_s[...] = h2.astype(BF16)
    acc = None
    for j in range(D_FF // FF_CHUNK):
        h2b = h2_s[...]
        a = _dot(h2b, win_ref[0, :, j * FF_CHUNK:(j + 1) * FF_CHUNK])
        gt = _dot(h2b, win_ref[0, :, D_FF + j * FF_CHUNK:D_FF + (j + 1) * FF_CHUNK])
        f = (a * _sigmoid(a) * gt).astype(BF16)
        c = _dot(f, wo_ref[0, j * FF_CHUNK:(j + 1) * FF_CHUNK, :])
        acc = c if acc is None else acc + c
    y = DEEPNORM_ALPHA * x1 + mod_ref[0, 5:6, :] * acc
    o_ref[0] = _ln(y) * ln2g_ref[0] + ln2b_ref[0]


def _out_ffn(l, x, ssm, na, gqa, mod, w_out, ln1g, ln1b, w_ffn_in, w_ffn_out, ln2g, ln2b):
    bg = x.shape[0]
    tm = FFN_TM
    vec = pl.BlockSpec((1, 1, D_MODEL), lambda b, i: (l, 0, 0))
    resident = lambda shape: pl.BlockSpec((1,) + shape, lambda b, i: (l, 0, 0), pipeline_mode=pl.Buffered(1))
    return pl.pallas_call(
        _ffn_kernel,
        out_shape=jax.ShapeDtypeStruct((bg, TOKENS, D_MODEL), F32),
        grid=(bg, TOKENS // tm),
        in_specs=[
            pl.BlockSpec((1, tm, D_MODEL), lambda b, i: (b, i, 0)),
            pl.BlockSpec((1, 2, tm, LANES), lambda b, i: (b, 0, i, 0)),
            pl.BlockSpec((1, tm, NA_WIDTH), lambda b, i: (b, i, 0)),
            pl.BlockSpec((1, tm, GQA_WIDTH), lambda b, i: (b, i, 0)),
            pl.BlockSpec((1, 6, D_MODEL), lambda b, i: (b, 0, 0)),
            resident((D_MODEL, D_MODEL)),
            vec,
            vec,
            resident((D_MODEL, 2 * D_FF)),
            resident((D_FF, D_MODEL)),
            vec,
            vec,
        ],
        out_specs=pl.BlockSpec((1, tm, D_MODEL), lambda b, i: (b, i, 0)),
        scratch_shapes=[pltpu.VMEM((tm, D_MODEL), BF16)],
        compiler_params=pltpu.CompilerParams(dimension_semantics=("parallel", "parallel"),
                                             vmem_limit_bytes=VMEM_LIMIT),
        name="ffn",
    )(x, ssm, na, gqa, mod, w_out, ln1g, ln1b, w_ffn_in, w_ffn_out, ln2g, ln2b)


def _rope_tables():
    nf = HEAD_DIM // 4
    t = np.arange(DEC_SEQ)
    inv = ROPE_THETA ** (-np.arange(nf, dtype=np.float64) / nf)
    pos = np.stack([t // GRID_W, t % GRID_W], axis=1).astype(np.float64)
    ang = pos[:, :, None] * inv
    cos = np.repeat(np.cos(ang)[:, :, None, :], 2, axis=2).reshape(DEC_SEQ, HEAD_DIM)
    sin = (np.sin(ang)[:, :, None, :] * np.array([-1.0, 1.0])[None, None, :, None]).reshape(DEC_SEQ, HEAD_DIM)
    return jnp.asarray(np.tile(cos, (1, 2)), F32), jnp.asarray(np.tile(sin, (1, 2)), F32)


def _head_mean_matrix():
    h = np.arange(GQA_WIDTH) // HEAD_DIM
    return jnp.asarray((h[:, None] == h[None, :]).astype(np.float32) / HEAD_DIM, BF16)


def kernel(x_prompt, x_sample, c, cache_na_k, cache_na_v, cache_gqa_k, cache_gqa_v, state_ssm_re, state_ssm_im,
           c_ctx, w_ada, b_ada, w_in, w_out, q_norm_g, k_norm_g, na_bias, ssm_lam_re, ssm_lam_im, ssm_log_dt,
           ssm_b_re, ssm_b_im, ssm_c_re, ssm_c_im, ssm_d, w_ssm_glu, ln1_g, ln1_b, ln2_g, ln2_b,
           w_ffn_in, w_ffn_out):
    cond8 = jnp.concatenate([c_ctx[None, :], c, jnp.zeros((8 - 1 - DEC_BATCH, D_MODEL), F32)], axis=0)
    mod = _modulation(cond8, w_ada, b_ada).reshape(DEPTH, 8, 6, D_MODEL)
    cos, sin = _rope_tables()
    hm = _head_mean_matrix()
    w_in_b = w_in.astype(BF16)
    w_out_b = w_out.astype(BF16)
    w_ffn_in_b = w_ffn_in.astype(BF16)
    w_ffn_out_b = w_ffn_out.astype(BF16)
    w_glu_b = w_ssm_glu.astype(BF16)
    qg = jnp.tile(q_norm_g, (1, GQA_HEADS))[:, None, :]
    kg = jnp.tile(k_norm_g, (1, GQA_KV_HEADS))[:, None, :]
    dskip = ssm_d[:, None, :]
    ln = (ln1_g[:, None, :], ln1_b[:, None, :], ln2_g[:, None, :], ln2_b[:, None, :])
    *ssm_tabs, a1, a2 = _ssm_tables(ssm_lam_re, ssm_lam_im, ssm_log_dt, ssm_b_re, ssm_b_im, ssm_c_re, ssm_c_im)
    ops = _ssm_operators(ssm_tabs)
    a1 = a1.reshape(DEPTH, 2, STATE_LANES)
    a2 = a2.reshape(DEPTH, 2, STATE_LANES)
    na_tab = _na_bias_table(na_bias)
    ck_na = cache_na_k.reshape(DEC_BATCH, DEPTH, PAST_LEN, NA_WIDTH).astype(BF16)
    cv_na = cache_na_v.reshape(DEC_BATCH, DEPTH, PAST_LEN, NA_WIDTH).astype(BF16)
    kt_cache = cache_gqa_k.transpose(1, 0, 3, 4, 2).reshape(DEPTH, DEC_BATCH, GQA_KV_WIDTH, PAST_LEN).astype(BF16)
    cv_g = cache_gqa_v.transpose(1, 0, 3, 2, 4)
    v1_cache = jnp.concatenate([cv_g, jnp.ones(cv_g.shape[:-1] + (1,), F32),
                                jnp.zeros(cv_g.shape[:-1] + (LANES - HEAD_DIM - 1,), F32)], axis=-1).astype(BF16)
    s_re = state_ssm_re.astype(F32).transpose(1, 0, 2, 3, 4)
    s_im = state_ssm_im.astype(F32).transpose(1, 0, 2, 3, 4)
    h0s_lat = jnp.concatenate([s_re, s_im], axis=-1).reshape(DEPTH, DEC_BATCH, 2, STATE_LANES)
    h0p_lat = jnp.concatenate([s_im, s_re], axis=-1).reshape(DEPTH, DEC_BATCH, 2, STATE_LANES)
    h0_ctx = jnp.zeros((1, 2, STATE_LANES), F32)

    y_ctx = x_prompt.reshape(1, TOKENS, D_MODEL)
    y_lat = x_sample
    kv_na_l, kv_g_l, fin_l = [], [], []
    for l in range(DEPTH):
        mod_c = mod[l, 0:1]
        u, zb, kv_na, kv_g = _in_projection(l, y_ctx, mod_c, w_in_b, qg, kg, hm, cos, sin, latent=False)
        ssm_o, fin = _ssm(l, u, ops, a1, a2, h0_ctx, h0_ctx, dskip, w_glu_b, n_seq=BATCH, emit_final=True)
        na_o, gqa_o = _ctx_attention(zb)
        y_ctx = _out_ffn(l, y_ctx, ssm_o, na_o, gqa_o, mod_c, w_out_b, ln[0], ln[1], w_ffn_in_b, w_ffn_out_b,
                         ln[2], ln[3])
        kv_na_l.append(kv_na); kv_g_l.append(kv_g); fin_l.append(fin)

        mod_s = mod[l, 1:1 + DEC_BATCH]
        u, zb, kt_new, v1_new = _in_projection(l, y_lat, mod_s, w_in_b, qg, kg, hm, cos, sin, latent=True)
        ssm_o = _ssm(l, u, ops, a1, a2, h0s_lat[l], h0p_lat[l], dskip, w_glu_b, n_seq=1,
                     emit_final=False)[0]
        na_o = _na_latent(l, zb, ck_na, cv_na, na_tab)
        gqa_o = _gqa_latent(zb, kt_new, kt_cache[l], v1_new, v1_cache[l])
        y_lat = _out_ffn(l, y_lat, ssm_o, na_o, gqa_o, mod_s, w_out_b, ln[0], ln[1], w_ffn_in_b, w_ffn_out_b,
                         ln[2], ln[3])

    kv_na = jnp.stack(kv_na_l, axis=0).reshape(DEPTH, BATCH, SEQ, 2, NA_HEADS, HEAD_DIM)
    kv_g = jnp.stack(kv_g_l, axis=0).reshape(DEPTH, BATCH, SEQ, 2, GQA_KV_HEADS, HEAD_DIM)
    kv_na = kv_na.transpose(3, 1, 0, 2, 4, 5)
    kv_g = kv_g.transpose(3, 1, 0, 2, 4, 5)
    fin = jnp.stack(fin_l, axis=0).reshape(DEPTH, 2, BATCH, SSM_GROUPS, 2, SSM_STATE)
    fin = fin.transpose(4, 2, 0, 1, 3, 5)
    return (y_ctx.reshape(BATCH, SEQ, D_MODEL), y_lat, kv_na[0], kv_na[1], kv_g[0], kv_g[1], fin[0], fin[1])
```

```python
import functools
import math

import numpy as np
import jax
import jax.numpy as jnp
from jax import lax
from jax.experimental import pallas as pl
from jax.experimental.pallas import tpu as pltpu

F32 = jnp.float32
BF16 = jnp.bfloat16

D_MODEL = 1024
BATCH = 16
SEQ = 256
DEPTH = 2
DEC_BATCH = 2
DEC_SEQ = 4096
PAST_LEN = 256
GRID_W = 64
GRID_H = DEC_SEQ // GRID_W
HEAD_DIM = 64
SSM_WIDTH = 256
SSM_GROUP_CH = 16
SSM_GROUPS = 16
SSM_STATE = 64
NA_HEADS = 4
NA_WIDTH = 256
NA_WIN_R = 8
NA_WIN_C = 16
GQA_HEADS = 8
GQA_KV_HEADS = 2
GQA_REP = 4
GQA_WIDTH = 512
GQA_KV_WIDTH = 128
IN_WIDTH = 1792
D_FF = 2816
ROPE_THETA = 10000.0
LN_EPS = 1e-6
RMS_EPS = 1e-6
DEEPNORM_ALPHA = (2 * DEPTH) ** 0.25
QK_SCALE = HEAD_DIM ** -0.5
LOG2E = math.log2(math.e)

TOKENS = 4096
SSM_CHUNK = 16
N_CHUNK = TOKENS // SSM_CHUNK
Z_WIDTH = IN_WIDTH - SSM_WIDTH
Z_QNA, Z_KNA, Z_VNA, Z_QG, Z_KG, Z_VG = 0, 256, 512, 768, 1280, 1408

VMEM_LIMIT = 56 * 1024 * 1024
LANES = 128
MXU_EDGE = 256
SSM_FLAT = SSM_CHUNK * SSM_GROUP_CH
assert SSM_FLAT == MXU_EDGE

MOD_NB = 1536
WIN_TM = 1024
FFN_TM = 512
GQA_TQ = 512


def _sigmoid(x):
    return 1.0 / (1.0 + jnp.exp(-x))


def _ln(x):
    mu = jnp.mean(x, axis=-1, keepdims=True)
    xc = x - mu
    var = jnp.mean(xc * xc, axis=-1, keepdims=True)
    return xc * lax.rsqrt(var + LN_EPS)


def _dot(a, b):
    return jnp.dot(a, b, preferred_element_type=F32)


def _mod_kernel(c_ref, w_ref, b_ref, o_ref):
    c = c_ref[...]
    s = c * _sigmoid(c)
    o_ref[0] = jnp.dot(s, w_ref[0], preferred_element_type=F32, precision=lax.Precision.HIGHEST) + b_ref[0]


def _modulation(cond8, w_ada, b_ada):
    nb = MOD_NB
    return pl.pallas_call(
        _mod_kernel,
        out_shape=jax.ShapeDtypeStruct((DEPTH, 8, 6 * D_MODEL), F32),
        grid=(DEPTH, 6 * D_MODEL // nb),
        in_specs=[
            pl.BlockSpec((8, D_MODEL), lambda l, j: (0, 0)),
            pl.BlockSpec((1, D_MODEL, nb), lambda l, j: (l, 0, j)),
            pl.BlockSpec((1, 1, nb), lambda l, j: (l, 0, j)),
        ],
        out_specs=pl.BlockSpec((1, 8, nb), lambda l, j: (l, 0, j)),
        compiler_params=pltpu.CompilerParams(dimension_semantics=("parallel", "parallel"),
                                             vmem_limit_bytes=VMEM_LIMIT),
        name="mod",
    )(cond8, w_ada, b_ada.reshape(DEPTH, 1, 6 * D_MODEL))


def _swap16(x):
    w = x.shape[-1]
    lane = lax.broadcasted_iota(jnp.int32, x.shape, x.ndim - 1)
    return jnp.where((lane & 16) != 0, pltpu.roll(x, 16, x.ndim - 1), pltpu.roll(x, w - 16, x.ndim - 1))


def _win_kernel(x_ref, mod_ref, w_ref, qg_ref, kg_ref, hm_ref, cos_ref, sin_ref, *out_refs, latent):
    u_ref, z_ref = out_refs[0], out_refs[1]
    x = x_ref[0]
    h = _ln(x) * (1.0 + mod_ref[0, 1:2, :]) + mod_ref[0, 0:1, :]
    z = _dot(h.astype(BF16), w_ref[0])
    u_ref[0, 0] = z[:, 0:LANES]
    u_ref[0, 1] = z[:, LANES:SSM_WIDTH]
    c_qna, c_kna, c_qg, c_kg, c_vg = np.cumsum([SSM_WIDTH, NA_WIDTH, 2 * NA_WIDTH, GQA_WIDTH, GQA_KV_WIDTH])
    q_na = z[:, c_qna:c_kna] * QK_SCALE
    kv_na = z[:, c_kna:c_qg]
    q_g = z[:, c_qg:c_kg]
    k_g = z[:, c_kg:c_vg]
    v_g = z[:, c_vg:IN_WIDTH]
    q_ms = _dot((q_g * q_g).astype(BF16), hm_ref[...])
    k_ms = _dot((k_g * k_g).astype(BF16), hm_ref[0:GQA_KV_WIDTH, 0:GQA_KV_WIDTH])
    q_n = q_g * lax.rsqrt(q_ms + RMS_EPS) * qg_ref[0]
    k_n = k_g * lax.rsqrt(k_ms + RMS_EPS) * kg_ref[0]
    if latent:
        kt_ref, v1_ref = out_refs[2], out_refs[3]
        cos = cos_ref[...]
        sin = sin_ref[...]
        cos4 = jnp.concatenate([cos] * 4, axis=1)
        sin4 = jnp.concatenate([sin] * 4, axis=1)
        q_n = (q_n * cos4 + _swap16(q_n) * sin4) * (QK_SCALE * LOG2E)
        k_n = k_n * cos + _swap16(k_n) * sin
        kt_ref[0] = k_n.T.astype(BF16)
        lane = lax.broadcasted_iota(jnp.int32, v_g.shape, 1)
        ones_col = jnp.where(lane == HEAD_DIM, 1.0, 0.0)
        v1_ref[0, 0] = jnp.where(lane < HEAD_DIM, v_g, ones_col).astype(BF16)
        v1_ref[0, 1] = jnp.where(lane < HEAD_DIM, pltpu.roll(v_g, HEAD_DIM, 1), ones_col).astype(BF16)
    else:
        out_refs[2][0] = kv_na
        out_refs[3][0] = jnp.concatenate([k_n, v_g], axis=1)
        q_n = q_n * QK_SCALE
    zb = jnp.concatenate([q_na, kv_na, q_n, k_n, v_g], axis=1)
    z_ref[0] = zb.astype(BF16)


def _in_projection(l, x, mod, w_in, qg, kg, hm, cos, sin, *, latent):
    bg = x.shape[0]
    tm = WIN_TM
    out_shape = [jax.ShapeDtypeStruct((bg, 2, TOKENS, LANES), F32),
                 jax.ShapeDtypeStruct((bg, TOKENS, Z_WIDTH), BF16)]
    out_specs = [pl.BlockSpec((1, 2, tm, LANES), lambda b, i: (b, 0, i, 0)),
                 pl.BlockSpec((1, tm, Z_WIDTH), lambda b, i: (b, i, 0))]
    if latent:
        out_shape += [jax.ShapeDtypeStruct((bg, GQA_KV_WIDTH, TOKENS), BF16),
                      jax.ShapeDtypeStruct((bg, GQA_KV_HEADS, TOKENS, LANES), BF16)]
        out_specs += [pl.BlockSpec((1, GQA_KV_WIDTH, tm), lambda b, i: (b, 0, i)),
                      pl.BlockSpec((1, GQA_KV_HEADS, tm, LANES), lambda b, i: (b, 0, i, 0))]
    else:
        out_shape += [jax.ShapeDtypeStruct((bg, TOKENS, 2 * NA_WIDTH), F32),
                      jax.ShapeDtypeStruct((bg, TOKENS, 2 * GQA_KV_WIDTH), F32)]
        out_specs += [pl.BlockSpec((1, tm, 2 * NA_WIDTH), lambda b, i: (b, i, 0)),
                      pl.BlockSpec((1, tm, 2 * GQA_KV_WIDTH), lambda b, i: (b, i, 0))]
    return pl.pallas_call(
        functools.partial(_win_kernel, latent=latent),
        out_shape=out_shape,
        grid=(bg, TOKENS // tm),
        in_specs=[
            pl.BlockSpec((1, tm, D_MODEL), lambda b, i: (b, i, 0)),
            pl.BlockSpec((1, 6, D_MODEL), lambda b, i: (b, 0, 0)),
            pl.BlockSpec((1, D_MODEL, IN_WIDTH), lambda b, i: (l, 0, 0)),
            pl.BlockSpec((1, 1, GQA_WIDTH), lambda b, i: (l, 0, 0)),
            pl.BlockSpec((1, 1, GQA_KV_WIDTH), lambda b, i: (l, 0, 0)),
            pl.BlockSpec((GQA_WIDTH, GQA_WIDTH), lambda b, i: (0, 0)),
            pl.BlockSpec((tm, LANES), lambda b, i: (i, 0)),
            pl.BlockSpec((tm, LANES), lambda b, i: (i, 0)),
        ],
        out_specs=out_specs,
        compiler_params=pltpu.CompilerParams(dimension_semantics=("parallel", "parallel"),
                                             vmem_limit_bytes=VMEM_LIMIT),
        name="win_lat" if latent else "win_ctx",
    )(x, mod, w_in, qg, kg, hm, cos, sin)


STATE_LANES = SSM_GROUPS * LANES
SSM_UNROLL = 8


def _ssm_ops_kernel(ct_r_ref, ct_i_ref, pk_r_ref, pk_i_ref, pc_r_ref, pc_i_ref, bb1_ref, bb2_ref, pw_r_ref, pw_i_ref,
                    tile_ref, rep_ref, tile4_ref, m_ref, bend_ref, cpow_ref):
    d = pl.program_id(1)

    def onto_lanes(x, sel):
        hi = x.astype(BF16)
        lo = (x - hi.astype(F32)).astype(BF16)
        return _dot(hi, sel) + _dot(lo, sel)

    def build_ops(g, carry):
        cr = onto_lanes(ct_r_ref[0, 0, g], tile_ref[...])
        ci = onto_lanes(ct_i_ref[0, 0, g], tile_ref[...])

        def re_proj(pr_ref, pi_ref):
            pr = onto_lanes(pr_ref[0, 0, g], rep_ref[...])
            pi = onto_lanes(pi_ref[0, 0, g], rep_ref[...])
            return jnp.concatenate([cr * pr - ci * pi, -(cr * pi + ci * pr)], axis=0)
        cpow_ref[0, 0, g] = re_proj(pc_r_ref, pc_i_ref).astype(BF16)
        bb1 = bb1_ref[0, 0, g]
        bb2 = bb2_ref[0, 0, g]
        kcat = jnp.dot(bb1[:, :LANES], re_proj(pk_r_ref, pk_i_ref),
                       preferred_element_type=F32, precision=lax.Precision.HIGHEST)
        lane = lax.broadcasted_iota(jnp.int32, kcat.shape, 1)
        pw_r = onto_lanes(pw_r_ref[0, 0, g], tile4_ref[...])
        pw_i = onto_lanes(pw_i_ref[0, 0, g], tile4_ref[...])
        for t in range(SSM_CHUNK):
            rows = slice(t * SSM_GROUP_CH, (t + 1) * SSM_GROUP_CH)
            lo, hi = t * SSM_GROUP_CH, (t + 1) * SSM_GROUP_CH
            fwd = kcat if t == 0 else jnp.where(lane >= lo, pltpu.roll(kcat, lo, 1), 0.0)
            bwd = kcat if hi == SSM_FLAT else jnp.where(lane < hi, pltpu.roll(kcat, hi, 1), 0.0)
            m_ref[0, 0, g, rows, :] = jnp.where(d == 0, fwd, bwd).astype(BF16)
            bend_ref[0, 0, g, rows, :] = (pw_r[t:t + 1, :] * bb1 + pw_i[t:t + 1, :] * bb2).astype(BF16)
        return carry
    lax.fori_loop(0, SSM_GROUPS, build_ops, 0, unroll=4)


def _ssm_operators(tables):
    per_dir = lambda rows, cols: pl.BlockSpec((1, 1, SSM_GROUPS, rows, cols), lambda l, d: (l, d, 0, 0, 0))
    const = lambda rows: pl.BlockSpec((rows, SSM_FLAT), lambda l, d: (0, 0))
    shape = lambda rows: jax.ShapeDtypeStruct((DEPTH, 2, SSM_GROUPS, rows, SSM_FLAT), BF16)
    c = np.arange(SSM_FLAT)
    tile = (c[None, :] % SSM_GROUP_CH == np.arange(SSM_GROUP_CH)[:, None]).astype(np.float32)
    rep = (c[None, :] // SSM_GROUP_CH == np.arange(SSM_CHUNK)[:, None]).astype(np.float32)
    tile4 = (c[None, :] % SSM_STATE == np.arange(SSM_STATE)[:, None]).astype(np.float32)
    return pl.pallas_call(
        _ssm_ops_kernel,
        out_shape=[shape(SSM_FLAT), shape(SSM_FLAT), shape(LANES)],
        grid=(DEPTH, 2),
        in_specs=([per_dir(SSM_STATE, SSM_GROUP_CH)] * 6 + [per_dir(SSM_GROUP_CH, SSM_FLAT)] * 2
                  + [per_dir(SSM_CHUNK, SSM_STATE)] * 2 + [const(SSM_GROUP_CH), const(SSM_CHUNK), const(SSM_STATE)]),
        out_specs=[per_dir(SSM_FLAT, SSM_FLAT), per_dir(SSM_FLAT, SSM_FLAT), per_dir(LANES, SSM_FLAT)],
        compiler_params=pltpu.CompilerParams(dimension_semantics=("parallel", "parallel"),
                                             vmem_limit_bytes=VMEM_LIMIT),
        name="ssm_ops",
    )(*tables, jnp.asarray(tile, BF16), jnp.asarray(rep, BF16), jnp.asarray(tile4, BF16))


def _ssm_kernel(u_ref, m_ref, bend_ref, cpow_ref, a1_ref, a2_ref, h0s_ref, h0p_ref, d_ref, wglu_ref, *refs,
                n_seq, emit_final):
    if emit_final:
        out_ref, fin_ref, y_s, yt_s, u_s, inja_s, injb_s, hp_s = refs
    else:
        out_ref, y_s, yt_s, u_s, inja_s, injb_s, hp_s = refs
        fin_ref = None
    chunks_per_seq = N_CHUNK // n_seq

    def chunk_rows(t):
        sl = pl.ds(t, N_CHUNK, stride=SSM_CHUNK)
        return jnp.concatenate([u_ref[0, 0, sl, :], u_ref[0, 1, sl, :]], axis=1)

    for t in range(SSM_CHUNK):
        a_t = chunk_rows(t)
        yt_s[:, t * SSM_GROUP_CH:(t + 1) * SSM_GROUP_CH, :] = a_t.T.reshape(SSM_GROUPS, SSM_GROUP_CH, N_CHUNK)

    def tr_in(g, carry):
        u_s[g] = yt_s[g].T.astype(BF16)
        return carry
    lax.fori_loop(0, SSM_GROUPS, tr_in, 0, unroll=SSM_UNROLL)

    for d in range(2):
        def proj(g, carry, d=d):
            ug = u_s[g]
            y = _dot(ug, m_ref[0, d, g])
            y_s[g] = y if d == 0 else y_s[g] + y
            inj = _dot(ug, bend_ref[0, d, g])
            lanes = pl.ds(pl.multiple_of(g * LANES, LANES), LANES)
            inja_s[d, :, lanes] = inj[:, :LANES]
            injb_s[d, :, lanes] = inj[:, LANES:]
            return carry
        lax.fori_loop(0, SSM_GROUPS, proj, 0, unroll=SSM_UNROLL)

    a1 = a1_ref[0]
    a2 = a2_ref[0]

    def step(j, carry):
        s, sp = carry
        nf = j
        nb = N_CHUNK - 1 - j
        if n_seq > 1:
            keep = jnp.where(j % chunks_per_seq != 0, 1.0, 0.0).astype(F32)
            s = s * keep
            sp = sp * keep
        hp_s[0, pl.ds(nf, 1), :] = s[0:1]
        hp_s[1, pl.ds(nb, 1), :] = s[1:2]
        ia = jnp.concatenate([inja_s[0, pl.ds(nf, 1), :], inja_s[1, pl.ds(nb, 1), :]], axis=0)
        ib = jnp.concatenate([injb_s[0, pl.ds(nf, 1), :], injb_s[1, pl.ds(nb, 1), :]], axis=0)
        s_new = a1 * s + a2 * sp + ia
        sp_new = a1 * sp - a2 * s + ib
        if emit_final:
            @pl.when(j % chunks_per_seq == chunks_per_seq - 1)
            def _():
                sq = j // chunks_per_seq
                fin_ref[0, 0, pl.ds(sq, 1), :] = s_new[0:1]
                fin_ref[0, 1, pl.ds(n_seq - 1 - sq, 1), :] = s_new[1:2]
        return s_new, sp_new
    lax.fori_loop(0, N_CHUNK, step, (h0s_ref[0], h0p_ref[0]))

    for d in range(2):
        def carry_in(g, carry, d=d):
            hp = hp_s[d, :, pl.ds(pl.multiple_of(g * LANES, LANES), LANES)]
            y_s[g] = y_s[g] + _dot(hp.astype(BF16), cpow_ref[0, d, g])
            return carry
        lax.fori_loop(0, SSM_GROUPS, carry_in, 0, unroll=SSM_UNROLL)

    def tr_out(g, carry):
        yt_s[g] = y_s[g].T
        return carry
    lax.fori_loop(0, SSM_GROUPS, tr_out, 0, unroll=SSM_UNROLL)
    dskip = d_ref[0]
    wglu = wglu_ref[0]
    for t in range(SSM_CHUNK):
        blk = yt_s[:, t * SSM_GROUP_CH:(t + 1) * SSM_GROUP_CH, :].reshape(SSM_WIDTH, N_CHUNK)
        y = blk.T + chunk_rows(t) * dskip
        gl = jax.nn.gelu(y, approximate=True)
        gate = _dot(gl.astype(BF16), wglu)
        o = gl * _sigmoid(gate)
        out_ref[0, 0, pl.ds(t, N_CHUNK, stride=SSM_CHUNK), :] = o[:, :LANES]
        out_ref[0, 1, pl.ds(t, N_CHUNK, stride=SSM_CHUNK), :] = o[:, LANES:]


def _ssm(l, u, ops, a1, a2, h0s, h0p, dskip, wglu, *, n_seq, emit_final):
    bg = u.shape[0]
    m, bend, cpow = ops
    out_shape = [jax.ShapeDtypeStruct((bg, 2, TOKENS, LANES), F32)]
    out_specs = [pl.BlockSpec((1, 2, TOKENS, LANES), lambda b: (b, 0, 0, 0))]
    if emit_final:
        out_shape.append(jax.ShapeDtypeStruct((bg, 2, n_seq, STATE_LANES), F32))
        out_specs.append(pl.BlockSpec((1, 2, n_seq, STATE_LANES), lambda b: (b, 0, 0, 0)))
    resident = lambda rows: pl.BlockSpec((1, 2, SSM_GROUPS, rows, SSM_FLAT), lambda b: (l, 0, 0, 0, 0),
                                         pipeline_mode=pl.Buffered(1))
    return pl.pallas_call(
        functools.partial(_ssm_kernel, n_seq=n_seq, emit_final=emit_final),
        out_shape=out_shape,
        grid=(bg,),
        in_specs=[
            pl.BlockSpec((1, 2, TOKENS, LANES), lambda b: (b, 0, 0, 0)),
            resident(SSM_FLAT), resident(SSM_FLAT), resident(LANES),
            pl.BlockSpec((1, 2, STATE_LANES), lambda b: (l, 0, 0)),
            pl.BlockSpec((1, 2, STATE_LANES), lambda b: (l, 0, 0)),
            pl.BlockSpec((1, 2, STATE_LANES), lambda b: (b, 0, 0)),
            pl.BlockSpec((1, 2, STATE_LANES), lambda b: (b, 0, 0)),
            pl.BlockSpec((1, 1, SSM_WIDTH), lambda b: (l, 0, 0)),
            pl.BlockSpec((1, SSM_WIDTH, SSM_WIDTH), lambda b: (l, 0, 0)),
        ],
        out_specs=out_specs,
        scratch_shapes=[pltpu.VMEM((SSM_GROUPS, N_CHUNK, SSM_FLAT), F32),
                        pltpu.VMEM((SSM_GROUPS, SSM_FLAT, N_CHUNK), F32),
                        pltpu.VMEM((SSM_GROUPS, N_CHUNK, SSM_FLAT), BF16),
                        pltpu.VMEM((2, N_CHUNK, STATE_LANES), F32),
                        pltpu.VMEM((2, N_CHUNK, STATE_LANES), F32),
                        pltpu.VMEM((2, N_CHUNK, STATE_LANES), F32)],
        compiler_params=pltpu.CompilerParams(dimension_semantics=("parallel",), vmem_limit_bytes=VMEM_LIMIT),
        name="ssm_ctx" if emit_final else "ssm_lat",
    )(u, m, bend, cpow, a1, a2, h0s, h0p, dskip, wglu)


def _ssm_tables(lam_re, lam_im, log_dt, b_re, b_im, c_re, c_im):
    t_n = SSM_CHUNK
    lr = jnp.minimum(lam_re.astype(F32), -1e-4)
    li = lam_im.astype(F32)
    dt = jnp.exp(log_dt.astype(F32))[..., None]
    lr_dt, li_dt = lr * dt, li * dt

    def power(k, expand=lambda x: x):
        mag = jnp.exp(expand(lr_dt) * k)
        return mag * jnp.cos(expand(li_dt) * k), mag * jnp.sin(expand(li_dt) * k)
    l_r, l_i = power(1.0)
    a_r, a_i = power(float(t_n))
    n_r, n_i = l_r - 1.0, l_i
    den = lr * lr + li * li
    q_r, q_i = (n_r * lr + n_i * li) / den, (n_i * lr - n_r * li) / den
    bb_r = q_r[..., None] * b_re - q_i[..., None] * b_im
    bb_i = q_r[..., None] * b_im + q_i[..., None] * b_re

    up = np.arange(t_n, dtype=np.float32)
    down = up[::-1].copy()
    per_dir = lambda fwd, bwd: jnp.asarray(np.stack([fwd, bwd]))[None, :, None, None, :]
    on_lanes = lambda x: x[..., None]
    pk_r, pk_i = power(per_dir(up, down), on_lanes)
    pc_r, pc_i = power(per_dir(up + 1.0, down + 1.0), on_lanes)
    pw_r, pw_i = power(jnp.swapaxes(per_dir(down, up), -1, -2), lambda x: x[..., None, :])
    cr_t = jnp.swapaxes(c_re.astype(F32), -1, -2)
    ci_t = jnp.swapaxes(c_im.astype(F32), -1, -2)
    br_t, bi_t = jnp.swapaxes(bb_r, -1, -2), jnp.swapaxes(bb_i, -1, -2)
    bb1 = jnp.concatenate([br_t, bi_t, bi_t, br_t], axis=-1)
    bb2 = jnp.concatenate([-bi_t, br_t, br_t, -bi_t], axis=-1)
    a1 = jnp.concatenate([a_r, a_r], axis=-1)
    a2 = jnp.concatenate([-a_i, a_i], axis=-1)
    return cr_t, ci_t, pk_r, pk_i, pc_r, pc_i, bb1, bb2, pw_r, pw_i, a1, a2


def _attend_transposed(k, v, q, n_kv, rep):
    nq = q.shape[0]
    qt = q.astype(F32).T
    zero = jnp.zeros((HEAD_DIM, nq), F32)
    bd = jnp.concatenate(
        [jnp.concatenate([qt[(g * rep + r) * HEAD_DIM:(g * rep + r + 1) * HEAD_DIM] if g == gg else zero
                          for gg in range(n_kv) for r in range(rep)], axis=1)
         for g in range(n_kv)], axis=0).astype(BF16)
    s = _dot(k, bd)
    m = jnp.max(s, axis=0, keepdims=True)
    p = jnp.exp(s - m)
    p = p * (1.0 / jnp.sum(p, axis=0, keepdims=True))
    ot = _dot(v.astype(F32).T.astype(BF16), p.astype(BF16))
    heads = [ot[g * HEAD_DIM:(g + 1) * HEAD_DIM, (g * rep + r) * nq:(g * rep + r + 1) * nq]
             for g in range(n_kv) for r in range(rep)]
    return jnp.concatenate(heads, axis=0).T


CTX_SEQS = 4


def _ctx_attn_kernel(z_ref, na_ref, gqa_ref):
    for i in range(CTX_SEQS):
        z = z_ref[i]
        na_ref[i] = _attend_transposed(z[:, Z_KNA:Z_KNA + NA_WIDTH], z[:, Z_VNA:Z_VNA + NA_WIDTH],
                                       z[:, Z_QNA:Z_QNA + NA_WIDTH], NA_HEADS, 1).astype(BF16)
        gqa_ref[i] = _attend_transposed(z[:, Z_KG:Z_KG + GQA_KV_WIDTH], z[:, Z_VG:Z_VG + GQA_KV_WIDTH],
                                        z[:, Z_QG:Z_QG + GQA_WIDTH], GQA_KV_HEADS, GQA_REP).astype(BF16)


def _ctx_attention(zb):
    z3 = zb.reshape(BATCH, SEQ, Z_WIDTH)
    na, gqa = pl.pallas_call(
        _ctx_attn_kernel,
        out_shape=[jax.ShapeDtypeStruct((BATCH, SEQ, NA_WIDTH), BF16),
                   jax.ShapeDtypeStruct((BATCH, SEQ, GQA_WIDTH), BF16)],
        grid=(BATCH // CTX_SEQS,),
        in_specs=[pl.BlockSpec((CTX_SEQS, SEQ, Z_WIDTH), lambda b: (b, 0, 0))],
        out_specs=[pl.BlockSpec((CTX_SEQS, SEQ, NA_WIDTH), lambda b: (b, 0, 0)),
                   pl.BlockSpec((CTX_SEQS, SEQ, GQA_WIDTH), lambda b: (b, 0, 0))],
        compiler_params=pltpu.CompilerParams(dimension_semantics=("parallel",), vmem_limit_bytes=VMEM_LIMIT),
        name="attn_ctx",
    )(z3)
    return na.reshape(1, TOKENS, NA_WIDTH), gqa.reshape(1, TOKENS, GQA_WIDTH)


NA_KEYS = NA_WIN_R * GRID_W


NA_ROWS = 16


def _na_lat_kernel(q_ref, k_ref, v_ref, ck_ref, cv_ref, bias_ref, o_ref):
    i = pl.program_id(1)
    ck = ck_ref[0, 0]
    cv = cv_ref[0, 0]
    qt = q_ref[0].astype(F32).T
    hd = lax.broadcasted_iota(jnp.int32, (NA_WIDTH, NA_WIDTH), 0) // HEAD_DIM
    hq = lax.broadcasted_iota(jnp.int32, (NA_WIDTH, NA_WIDTH), 1) // HEAD_DIM
    for j in range(NA_ROWS):
        r = NA_ROWS * i + j
        rs = jnp.clip(r - NA_WIN_R // 2, 0, GRID_H - NA_WIN_R)
        start = pl.multiple_of(rs * GRID_W, GRID_W)
        kw = k_ref[0, pl.ds(start, NA_KEYS), :]
        vw = v_ref[0, pl.ds(start, NA_KEYS), :]
        qj = qt[:, j * GRID_W:(j + 1) * GRID_W]
        bd = jnp.where(hd == hq, jnp.concatenate([qj] * NA_HEADS, axis=1), 0.0).astype(BF16)
        bias = bias_ref[0, pl.ds(NA_WIN_R - 1 - (r - rs), NA_WIN_R), :, :].reshape(NA_KEYS, NA_WIDTH)
        s_loc = _dot(kw, bd) + bias
        s_ctx = _dot(ck, bd)
        m = jnp.maximum(jnp.max(s_loc, axis=0, keepdims=True), jnp.max(s_ctx, axis=0, keepdims=True))
        p_loc = jnp.exp(s_loc - m)
        p_ctx = jnp.exp(s_ctx - m)
        inv = 1.0 / (jnp.sum(p_loc, axis=0, keepdims=True) + jnp.sum(p_ctx, axis=0, keepdims=True))
        full = (_dot((p_loc * inv).T.astype(BF16), vw)
                + _dot((p_ctx * inv).T.astype(BF16), cv))
        out = jnp.concatenate([full[h * HEAD_DIM:(h + 1) * HEAD_DIM, h * HEAD_DIM:(h + 1) * HEAD_DIM]
                               for h in range(NA_HEADS)], axis=1)
        o_ref[0, j * GRID_W:(j + 1) * GRID_W, :] = out.astype(BF16)


def _na_latent(l, zb, ck, cv, bias):
    tq = NA_ROWS * GRID_W
    return pl.pallas_call(
        _na_lat_kernel,
        out_shape=jax.ShapeDtypeStruct((DEC_BATCH, TOKENS, NA_WIDTH), BF16),
        grid=(DEC_BATCH, GRID_H // NA_ROWS),
        in_specs=[
            pl.BlockSpec((1, tq, NA_WIDTH), lambda b, i: (b, i, Z_QNA // NA_WIDTH)),
            pl.BlockSpec((1, TOKENS, NA_WIDTH), lambda b, i: (b, 0, Z_KNA // NA_WIDTH)),
            pl.BlockSpec((1, TOKENS, NA_WIDTH), lambda b, i: (b, 0, Z_VNA // NA_WIDTH)),
            pl.BlockSpec((1, 1, PAST_LEN, NA_WIDTH), lambda b, i: (b, l, 0, 0)),
            pl.BlockSpec((1, 1, PAST_LEN, NA_WIDTH), lambda b, i: (b, l, 0, 0)),
            pl.BlockSpec((1, 2 * NA_WIN_R - 1, GRID_W, NA_WIDTH), lambda b, i: (l, 0, 0, 0)),
        ],
        out_specs=pl.BlockSpec((1, tq, NA_WIDTH), lambda b, i: (b, i, 0)),
        compiler_params=pltpu.CompilerParams(dimension_semantics=("parallel", "arbitrary"),
                                             vmem_limit_bytes=VMEM_LIMIT),
        name="na_lat",
    )(zb, zb, zb, ck, cv, bias)


def _na_bias_table(tab):
    q = np.arange(GRID_W)
    kc = np.arange(GRID_W)
    cs = np.clip(q - NA_WIN_C // 2, 0, GRID_W - NA_WIN_C)
    valid = (kc[:, None] >= cs[None, :]) & (kc[:, None] < cs[None, :] + NA_WIN_C)
    w = GRID_W - 1
    rev = tab.astype(F32)[..., ::-1]
    padded = jnp.pad(rev, ((0, 0), (0, 0), (0, 0), (w, w)))
    off = w + NA_WIN_C - 1
    cols = jnp.stack([jnp.stack([padded[:, h, :, off - i:off - i + GRID_W] for i in range(GRID_W)], axis=2)
                      for h in range(NA_HEADS)], axis=3)
    cols = jnp.where(valid[:, None, :], cols, -jnp.inf)
    return cols.reshape(DEPTH, 2 * NA_WIN_R - 1, GRID_W, NA_WIDTH)


GQA_TK = 256
GQA_KEYS = DEC_SEQ + PAST_LEN


def _gqa_kernel(q_ref, ktn_ref, ktc_ref, vn_ref, vc_ref, o_ref, s_s, m_s, *, tq):
    q = q_ref[0]
    n_new = DEC_SEQ // GQA_TK
    half = GQA_REP * tq // 2
    qh = [jnp.concatenate([q[:, r * HEAD_DIM:(r + 1) * HEAD_DIM] for r in (2 * i, 2 * i + 1)], axis=0)
          for i in range(2)]
    rows = [slice(i * half, (i + 1) * half) for i in range(2)]

    def fold(s):
        return jnp.maximum(s[:, :LANES], s[:, LANES:])

    for i in range(2):
        s = _dot(qh[i], ktc_ref[0])
        s_s[rows[i], DEC_SEQ:GQA_KEYS] = s
        m_s[rows[i], :] = fold(s)
    for c in range(0, n_new, 2):
        for i in range(2):
            s0 = _dot(qh[i], ktn_ref[0, :, c * GQA_TK:(c + 1) * GQA_TK])
            s1 = _dot(qh[i], ktn_ref[0, :, (c + 1) * GQA_TK:(c + 2) * GQA_TK])
            s_s[rows[i], c * GQA_TK:(c + 1) * GQA_TK] = s0
            s_s[rows[i], (c + 1) * GQA_TK:(c + 2) * GQA_TK] = s1
            m_s[rows[i], :] = jnp.maximum(m_s[rows[i], :], jnp.maximum(fold(s0), fold(s1)))

    m = jnp.max(m_s[...], axis=-1, keepdims=True)
    m_s[...] = jnp.broadcast_to(m, m_s.shape)

    def probs(i, c0):
        mb = m_s[rows[i], :]
        return jnp.exp2(s_s[rows[i], c0:c0 + GQA_TK] - jnp.concatenate([mb, mb], axis=1)).astype(BF16)

    acc = [_dot(probs(i, DEC_SEQ), vc_ref[0, 0]) for i in range(2)]
    for c in range(n_new):
        for i in range(2):
            acc[i] = acc[i] + _dot(probs(i, c * GQA_TK), vn_ref[0, 0, c * GQA_TK:(c + 1) * GQA_TK, :])
    outs = []
    for i in range(2):
        out = acc[i][:, :HEAD_DIM] / acc[i][:, HEAD_DIM:HEAD_DIM + 1]
        outs += [out[:tq], out[tq:]]
    o_ref[0] = jnp.concatenate(outs, axis=1).astype(BF16)


def _gqa_latent(zb, kt_new, kt_cache, v1_new, v1_cache):
    tq = GQA_TQ
    rows = GQA_REP * tq
    return pl.pallas_call(
        functools.partial(_gqa_kernel, tq=tq),
        out_shape=jax.ShapeDtypeStruct((DEC_BATCH, TOKENS, GQA_WIDTH), BF16),
        grid=(DEC_BATCH, GQA_KV_HEADS, TOKENS // tq),
        in_specs=[
            pl.BlockSpec((1, tq, GQA_REP * HEAD_DIM), lambda b, g, i: (b, i, Z_QG // (GQA_REP * HEAD_DIM) + g)),
            pl.BlockSpec((1, HEAD_DIM, DEC_SEQ), lambda b, g, i: (b, g, 0)),
            pl.BlockSpec((1, HEAD_DIM, PAST_LEN), lambda b, g, i: (b, g, 0)),
            pl.BlockSpec((1, 1, DEC_SEQ, LANES), lambda b, g, i: (b, g, 0, 0)),
            pl.BlockSpec((1, 1, PAST_LEN, LANES), lambda b, g, i: (b, g, 0, 0)),
        ],
        out_specs=pl.BlockSpec((1, tq, GQA_REP * HEAD_DIM), lambda b, g, i: (b, i, g)),
        scratch_shapes=[pltpu.VMEM((rows, GQA_KEYS), F32), pltpu.VMEM((rows, LANES), F32)],
        compiler_params=pltpu.CompilerParams(dimension_semantics=("parallel", "parallel", "arbitrary"),
                                             vmem_limit_bytes=VMEM_LIMIT),
        name="gqa_lat",
    )(zb, kt_new, kt_cache, v1_new, v1_cache)


FF_CHUNK = 256


def _ffn_kernel(x_ref, ssm_ref, na_ref, gqa_ref, mod_ref, wout_ref, ln1g_ref, ln1b_ref, win_ref, wo_ref,
                ln2g_ref, ln2b_ref, o_ref, h2_s):
    ssm = jnp.concatenate([ssm_ref[0, 0], ssm_ref[0, 1]], axis=1)
    o = (_dot(ssm.astype(BF16), wout_ref[0, 0:SSM_WIDTH, :])
         + _dot(na_ref[0], wout_ref[0, SSM_WIDTH:SSM_WIDTH + NA_WIDTH, :])
         + _dot(gqa_ref[0], wout_ref[0, SSM_WIDTH + NA_WIDTH:, :]))
    y = DEEPNORM_ALPHA * x_ref[0] + mod_ref[0, 2:3, :] * o
    x1 = _ln(y) * ln1g_ref[0] + ln1b_ref[0]
    h2 = _ln(x1) * (1.0 + mod_ref[0, 4:5, :]) + mod_ref[0, 3:4, :]
    h2_s[...] = h2.astype(BF16)
    acc = None
    for j in range(D_FF // FF_CHUNK):
        h2b = h2_s[...]
        a = _dot(h2b, win_ref[0, :, j * FF_CHUNK:(j + 1) * FF_CHUNK])
        gt = _dot(h2b, win_ref[0, :, D_FF + j * FF_CHUNK:D_FF + (j + 1) * FF_CHUNK])
        f = (a * _sigmoid(a) * gt).astype(BF16)
        c = _dot(f, wo_ref[0, j * FF_CHUNK:(j + 1) * FF_CHUNK, :])
        acc = c if acc is None else acc + c
    y = DEEPNORM_ALPHA * x1 + mod_ref[0, 5:6, :] * acc
    o_ref[0] = _ln(y) * ln2g_ref[0] + ln2b_ref[0]


def _out_ffn(l, x, ssm, na, gqa, mod, w_out, ln1g, ln1b, w_ffn_in, w_ffn_out, ln2g, ln2b):
    bg = x.shape[0]
    tm = FFN_TM
    vec = pl.BlockSpec((1, 1, D_MODEL), lambda b, i: (l, 0, 0))
    resident = lambda shape: pl.BlockSpec((1,) + shape, lambda b, i: (l, 0, 0), pipeline_mode=pl.Buffered(1))
    return pl.pallas_call(
        _ffn_kernel,
        out_shape=jax.ShapeDtypeStruct((bg, TOKENS, D_MODEL), F32),
        grid=(bg, TOKENS // tm),
        in_specs=[
            pl.BlockSpec((1, tm, D_MODEL), lambda b, i: (b, i, 0)),
            pl.BlockSpec((1, 2, tm, LANES), lambda b, i: (b, 0, i, 0)),
            pl.BlockSpec((1, tm, NA_WIDTH), lambda b, i: (b, i, 0)),
            pl.BlockSpec((1, tm, GQA_WIDTH), lambda b, i: (b, i, 0)),
            pl.BlockSpec((1, 6, D_MODEL), lambda b, i: (b, 0, 0)),
            resident((D_MODEL, D_MODEL)),
            vec,
            vec,
            resident((D_MODEL, 2 * D_FF)),
            resident((D_FF, D_MODEL)),
            vec,
            vec,
        ],
        out_specs=pl.BlockSpec((1, tm, D_MODEL), lambda b, i: (b, i, 0)),
        scratch_shapes=[pltpu.VMEM((tm, D_MODEL), BF16)],
        compiler_params=pltpu.CompilerParams(dimension_semantics=("parallel", "parallel"),
                                             vmem_limit_bytes=VMEM_LIMIT),
        name="ffn",
    )(x, ssm, na, gqa, mod, w_out, ln1g, ln1b, w_ffn_in, w_ffn_out, ln2g, ln2b)


def _rope_tables():
    nf = HEAD_DIM // 4
    t = np.arange(DEC_SEQ)
    inv = ROPE_THETA ** (-np.arange(nf, dtype=np.float64) / nf)
    pos = np.stack([t // GRID_W, t % GRID_W], axis=1).astype(np.float64)
    ang = pos[:, :, None] * inv
    cos = np.repeat(np.cos(ang)[:, :, None, :], 2, axis=2).reshape(DEC_SEQ, HEAD_DIM)
    sin = (np.sin(ang)[:, :, None, :] * np.array([-1.0, 1.0])[None, None, :, None]).reshape(DEC_SEQ, HEAD_DIM)
    return jnp.asarray(np.tile(cos, (1, 2)), F32), jnp.asarray(np.tile(sin, (1, 2)), F32)


def _head_mean_matrix():
    h = np.arange(GQA_WIDTH) // HEAD_DIM
    return jnp.asarray((h[:, None] == h[None, :]).astype(np.float32) / HEAD_DIM, BF16)


def kernel(x_prompt, x_sample, c, cache_na_k, cache_na_v, cache_gqa_k, cache_gqa_v, state_ssm_re, state_ssm_im,
           c_ctx, w_ada, b_ada, w_in, w_out, q_norm_g, k_norm_g, na_bias, ssm_lam_re, ssm_lam_im, ssm_log_dt,
           ssm_b_re, ssm_b_im, ssm_c_re, ssm_c_im, ssm_d, w_ssm_glu, ln1_g, ln1_b, ln2_g, ln2_b,
           w_ffn_in, w_ffn_out):
    cond8 = jnp.concatenate([c_ctx[None, :], c, jnp.zeros((8 - 1 - DEC_BATCH, D_MODEL), F32)], axis=0)
    mod = _modulation(cond8, w_ada, b_ada).reshape(DEPTH, 8, 6, D_MODEL)
    cos, sin = _rope_tables()
    hm = _head_mean_matrix()
    w_in_b = w_in.astype(BF16)
    w_out_b = w_out.astype(BF16)
    w_ffn_in_b = w_ffn_in.astype(BF16)
    w_ffn_out_b = w_ffn_out.astype(BF16)
    w_glu_b = w_ssm_glu.astype(BF16)
    qg = jnp.tile(q_norm_g, (1, GQA_HEADS))[:, None, :]
    kg = jnp.tile(k_norm_g, (1, GQA_KV_HEADS))[:, None, :]
    dskip = ssm_d[:, None, :]
    ln = (ln1_g[:, None, :], ln1_b[:, None, :], ln2_g[:, None, :], ln2_b[:, None, :])
    *ssm_tabs, a1, a2 = _ssm_tables(ssm_lam_re, ssm_lam_im, ssm_log_dt, ssm_b_re, ssm_b_im, ssm_c_re, ssm_c_im)
    ops = _ssm_operators(ssm_tabs)
    a1 = a1.reshape(DEPTH, 2, STATE_LANES)
    a2 = a2.reshape(DEPTH, 2, STATE_LANES)
    na_tab = _na_bias_table(na_bias)
    ck_na = cache_na_k.reshape(DEC_BATCH, DEPTH, PAST_LEN, NA_WIDTH).astype(BF16)
    cv_na = cache_na_v.reshape(DEC_BATCH, DEPTH, PAST_LEN, NA_WIDTH).astype(BF16)
    kt_cache = cache_gqa_k.transpose(1, 0, 3, 4, 2).reshape(DEPTH, DEC_BATCH, GQA_KV_WIDTH, PAST_LEN).astype(BF16)
    cv_g = cache_gqa_v.transpose(1, 0, 3, 2, 4)
    v1_cache = jnp.concatenate([cv_g, jnp.ones(cv_g.shape[:-1] + (1,), F32),
                                jnp.zeros(cv_g.shape[:-1] + (LANES - HEAD_DIM - 1,), F32)], axis=-1).astype(BF16)
    s_re = state_ssm_re.astype(F32).transpose(1, 0, 2, 3, 4)
    s_im = state_ssm_im.astype(F32).transpose(1, 0, 2, 3, 4)
    h0s_lat = jnp.concatenate([s_re, s_im], axis=-1).reshape(DEPTH, DEC_BATCH, 2, STATE_LANES)
    h0p_lat = jnp.concatenate([s_im, s_re], axis=-1).reshape(DEPTH, DEC_BATCH, 2, STATE_LANES)
    h0_ctx = jnp.zeros((1, 2, STATE_LANES), F32)

    y_ctx = x_prompt.reshape(1, TOKENS, D_MODEL)
    y_lat = x_sample
    kv_na_l, kv_g_l, fin_l = [], [], []
    for l in range(DEPTH):
        mod_c = mod[l, 0:1]
        u, zb, kv_na, kv_g = _in_projection(l, y_ctx, mod_c, w_in_b, qg, kg, hm, cos, sin, latent=False)
        ssm_o, fin = _ssm(l, u, ops, a1, a2, h0_ctx, h0_ctx, dskip, w_glu_b, n_seq=BATCH, emit_final=True)
        na_o, gqa_o = _ctx_attention(zb)
        y_ctx = _out_ffn(l, y_ctx, ssm_o, na_o, gqa_o, mod_c, w_out_b, ln[0], ln[1], w_ffn_in_b, w_ffn_out_b,
                         ln[2], ln[3])
        kv_na_l.append(kv_na); kv_g_l.append(kv_g); fin_l.append(fin)

        mod_s = mod[l, 1:1 + DEC_BATCH]
        u, zb, kt_new, v1_new = _in_projection(l, y_lat, mod_s, w_in_b, qg, kg, hm, cos, sin, latent=True)
        ssm_o = _ssm(l, u, ops, a1, a2, h0s_lat[l], h0p_lat[l], dskip, w_glu_b, n_seq=1,
                     emit_final=False)[0]
        na_o = _na_latent(l, zb, ck_na, cv_na, na_tab)
        gqa_o = _gqa_latent(zb, kt_new, kt_cache[l], v1_new, v1_cache[l])
        y_lat = _out_ffn(l, y_lat, ssm_o, na_o, gqa_o, mod_s, w_out_b, ln[0], ln[1], w_ffn_in_b, w_ffn_out_b,
                         ln[2], ln[3])

    kv_na = jnp.stack(kv_na_l, axis=0).reshape(DEPTH, BATCH, SEQ, 2, NA_HEADS, HEAD_DIM)
    kv_g = jnp.stack(kv_g_l, axis=0).reshape(DEPTH, BATCH, SEQ, 2, GQA_KV_HEADS, HEAD_DIM)
    kv_na = kv_na.transpose(3, 1, 0, 2, 4, 5)
    kv_g = kv_g.transpose(3, 1, 0, 2, 4, 5)
    fin = jnp.stack(fin_l, axis=0).reshape(DEPTH, 2, BATCH, SSM_GROUPS, 2, SSM_STATE)
    fin = fin.transpose(4, 2, 0, 1, 3, 5)
    return (y_ctx.reshape(BATCH, SEQ, D_MODEL), y_lat, kv_na[0], kv_na[1], kv_g[0], kv_g[1], fin[0], fin[1])
```

```python
import functools
import math

import numpy as np
import jax
import jax.numpy as jnp
from jax import lax
from jax.experimental import pallas as pl
from jax.experimental.pallas import tpu as pltpu

F32 = jnp.float32
BF16 = jnp.bfloat16

D_MODEL = 1024
BATCH = 16
SEQ = 256
DEPTH = 2
DEC_BATCH = 2
DEC_SEQ = 4096
PAST_LEN = 256
GRID_W = 64
GRID_H = DEC_SEQ // GRID_W
HEAD_DIM = 64
SSM_WIDTH = 256
SSM_GROUP_CH = 16
SSM_GROUPS = 16
SSM_STATE = 64
NA_HEADS = 4
NA_WIDTH = 256
NA_WIN_R = 8
NA_WIN_C = 16
GQA_HEADS = 8
GQA_KV_HEADS = 2
GQA_REP = 4
GQA_WIDTH = 512
GQA_KV_WIDTH = 128
IN_WIDTH = 1792
D_FF = 2816
ROPE_THETA = 10000.0
LN_EPS = 1e-6
RMS_EPS = 1e-6
DEEPNORM_ALPHA = (2 * DEPTH) ** 0.25
LOG2E = math.log2(math.e)
Q_SCALE = HEAD_DIM ** -0.5 * LOG2E

TOKENS = 4096
SSM_CHUNK = 16
N_CHUNK = TOKENS // SSM_CHUNK
Z_WIDTH = IN_WIDTH - SSM_WIDTH
Z_QNA, Z_KNA, Z_VNA, Z_QG, Z_KG, Z_VG = 0, 256, 512, 768, 1280, 1408

VMEM_LIMIT = 56 * 1024 * 1024
LANES = 128
MXU_EDGE = 256
SSM_FLAT = SSM_CHUNK * SSM_GROUP_CH
assert SSM_FLAT == MXU_EDGE

MOD_NB = 1536
WIN_TM = 1024
FFN_TM = 512
GQA_TQ = 512


def _sigmoid(x):
    return 1.0 / (1.0 + jnp.exp(-x))


def _ln(x):
    mu = jnp.mean(x, axis=-1, keepdims=True)
    xc = x - mu
    var = jnp.mean(xc * xc, axis=-1, keepdims=True)
    return xc * lax.rsqrt(var + LN_EPS)


def _dot(a, b):
    return jnp.dot(a, b, preferred_element_type=F32)


def _mod_kernel(c_ref, w_ref, b_ref, o_ref):
    c = c_ref[...]
    s = c * _sigmoid(c)
    o_ref[0] = jnp.dot(s, w_ref[0], preferred_element_type=F32, precision=lax.Precision.HIGHEST) + b_ref[0]


def _modulation(cond8, w_ada, b_ada):
    nb = MOD_NB
    return pl.pallas_call(
        _mod_kernel,
        out_shape=jax.ShapeDtypeStruct((DEPTH, 8, 6 * D_MODEL), F32),
        grid=(DEPTH, 6 * D_MODEL // nb),
        in_specs=[
            pl.BlockSpec((8, D_MODEL), lambda l, j: (0, 0)),
            pl.BlockSpec((1, D_MODEL, nb), lambda l, j: (l, 0, j)),
            pl.BlockSpec((1, 1, nb), lambda l, j: (l, 0, j)),
        ],
        out_specs=pl.BlockSpec((1, 8, nb), lambda l, j: (l, 0, j)),
        compiler_params=pltpu.CompilerParams(dimension_semantics=("parallel", "parallel"),
                                             vmem_limit_bytes=VMEM_LIMIT),
        name="mod",
    )(cond8, w_ada, b_ada.reshape(DEPTH, 1, 6 * D_MODEL))


def _swap16(x):
    w = x.shape[-1]
    lane = lax.broadcasted_iota(jnp.int32, x.shape, x.ndim - 1)
    return jnp.where((lane & 16) != 0, pltpu.roll(x, 16, x.ndim - 1), pltpu.roll(x, w - 16, x.ndim - 1))


def _win_kernel(x_ref, mod_ref, w_ref, qg_ref, kg_ref, hm_ref, cos_ref, sin_ref, *out_refs, latent):
    u_ref, z_ref = out_refs[0], out_refs[1]
    x = x_ref[0]
    h = _ln(x) * (1.0 + mod_ref[0, 1:2, :]) + mod_ref[0, 0:1, :]
    z = _dot(h.astype(BF16), w_ref[0])
    u_ref[0, 0] = z[:, 0:LANES]
    u_ref[0, 1] = z[:, LANES:SSM_WIDTH]
    c_qna, c_kna, c_qg, c_kg, c_vg = np.cumsum([SSM_WIDTH, NA_WIDTH, 2 * NA_WIDTH, GQA_WIDTH, GQA_KV_WIDTH])
    q_na = z[:, c_qna:c_kna] * Q_SCALE
    kv_na = z[:, c_kna:c_qg]
    q_g = z[:, c_qg:c_kg]
    k_g = z[:, c_kg:c_vg]
    v_g = z[:, c_vg:IN_WIDTH]
    q_ms = _dot((q_g * q_g).astype(BF16), hm_ref[...])
    k_ms = _dot((k_g * k_g).astype(BF16), hm_ref[0:GQA_KV_WIDTH, 0:GQA_KV_WIDTH])
    q_n = q_g * lax.rsqrt(q_ms + RMS_EPS) * qg_ref[0]
    k_n = k_g * lax.rsqrt(k_ms + RMS_EPS) * kg_ref[0]
    if latent:
        kt_ref, v1_ref = out_refs[2], out_refs[3]
        cos = cos_ref[...]
        sin = sin_ref[...]
        cos4 = jnp.concatenate([cos] * 4, axis=1)
        sin4 = jnp.concatenate([sin] * 4, axis=1)
        q_n = (q_n * cos4 + _swap16(q_n) * sin4) * Q_SCALE
        k_n = k_n * cos + _swap16(k_n) * sin
        kt_ref[0] = k_n.T.astype(BF16)
        lane = lax.broadcasted_iota(jnp.int32, v_g.shape, 1)
        ones_col = jnp.where(lane == HEAD_DIM, 1.0, 0.0)
        v1_ref[0, 0] = jnp.where(lane < HEAD_DIM, v_g, ones_col).astype(BF16)
        v1_ref[0, 1] = jnp.where(lane < HEAD_DIM, pltpu.roll(v_g, HEAD_DIM, 1), ones_col).astype(BF16)
    else:
        out_refs[2][0] = kv_na
        out_refs[3][0] = jnp.concatenate([k_n, v_g], axis=1)
        q_n = q_n * Q_SCALE
    zb = jnp.concatenate([q_na, kv_na, q_n, k_n, v_g], axis=1)
    z_ref[0] = zb.astype(BF16)


def _in_projection(l, x, mod, w_in, qg, kg, hm, cos, sin, *, latent):
    bg = x.shape[0]
    tm = WIN_TM
    out_shape = [jax.ShapeDtypeStruct((bg, 2, TOKENS, LANES), F32),
                 jax.ShapeDtypeStruct((bg, TOKENS, Z_WIDTH), BF16)]
    out_specs = [pl.BlockSpec((1, 2, tm, LANES), lambda b, i: (b, 0, i, 0)),
                 pl.BlockSpec((1, tm, Z_WIDTH), lambda b, i: (b, i, 0))]
    if latent:
        out_shape += [jax.ShapeDtypeStruct((bg, GQA_KV_WIDTH, TOKENS), BF16),
                      jax.ShapeDtypeStruct((bg, GQA_KV_HEADS, TOKENS, LANES), BF16)]
        out_specs += [pl.BlockSpec((1, GQA_KV_WIDTH, tm), lambda b, i: (b, 0, i)),
                      pl.BlockSpec((1, GQA_KV_HEADS, tm, LANES), lambda b, i: (b, 0, i, 0))]
    else:
        out_shape += [jax.ShapeDtypeStruct((bg, TOKENS, 2 * NA_WIDTH), F32),
                      jax.ShapeDtypeStruct((bg, TOKENS, 2 * GQA_KV_WIDTH), F32)]
        out_specs += [pl.BlockSpec((1, tm, 2 * NA_WIDTH), lambda b, i: (b, i, 0)),
                      pl.BlockSpec((1, tm, 2 * GQA_KV_WIDTH), lambda b, i: (b, i, 0))]
    return pl.pallas_call(
        functools.partial(_win_kernel, latent=latent),
        out_shape=out_shape,
        grid=(bg, TOKENS // tm),
        in_specs=[
            pl.BlockSpec((1, tm, D_MODEL), lambda b, i: (b, i, 0)),
            pl.BlockSpec((1, 6, D_MODEL), lambda b, i: (b, 0, 0)),
            pl.BlockSpec((1, D_MODEL, IN_WIDTH), lambda b, i: (l, 0, 0)),
            pl.BlockSpec((1, 1, GQA_WIDTH), lambda b, i: (l, 0, 0)),
            pl.BlockSpec((1, 1, GQA_KV_WIDTH), lambda b, i: (l, 0, 0)),
            pl.BlockSpec((GQA_WIDTH, GQA_WIDTH), lambda b, i: (0, 0)),
            pl.BlockSpec((tm, LANES), lambda b, i: (i, 0)),
            pl.BlockSpec((tm, LANES), lambda b, i: (i, 0)),
        ],
        out_specs=out_specs,
        compiler_params=pltpu.CompilerParams(dimension_semantics=("parallel", "parallel"),
                                             vmem_limit_bytes=VMEM_LIMIT),
        name="win_lat" if latent else "win_ctx",
    )(x, mod, w_in, qg, kg, hm, cos, sin)


STATE_LANES = SSM_GROUPS * LANES
SSM_UNROLL = 8


def _ssm_ops_kernel(ct_r_ref, ct_i_ref, pk_r_ref, pk_i_ref, pc_r_ref, pc_i_ref, bb1_ref, bb2_ref, pw_r_ref, pw_i_ref,
                    tile_ref, rep_ref, tile4_ref, m_ref, bend_ref, cpow_ref):
    d = pl.program_id(1)

    def onto_lanes(x, sel):
        hi = x.astype(BF16)
        lo = (x - hi.astype(F32)).astype(BF16)
        return _dot(hi, sel) + _dot(lo, sel)

    def build_ops(g, carry):
        cr = onto_lanes(ct_r_ref[0, 0, g], tile_ref[...])
        ci = onto_lanes(ct_i_ref[0, 0, g], tile_ref[...])

        def re_proj(pr_ref, pi_ref):
            pr = onto_lanes(pr_ref[0, 0, g], rep_ref[...])
            pi = onto_lanes(pi_ref[0, 0, g], rep_ref[...])
            return jnp.concatenate([cr * pr - ci * pi, -(cr * pi + ci * pr)], axis=0)
        cpow_ref[0, 0, g] = re_proj(pc_r_ref, pc_i_ref).astype(BF16)
        bb1 = bb1_ref[0, 0, g]
        bb2 = bb2_ref[0, 0, g]
        kcat = jnp.dot(bb1[:, :LANES], re_proj(pk_r_ref, pk_i_ref),
                       preferred_element_type=F32, precision=lax.Precision.HIGHEST)
        lane = lax.broadcasted_iota(jnp.int32, kcat.shape, 1)
        pw_r = onto_lanes(pw_r_ref[0, 0, g], tile4_ref[...])
        pw_i = onto_lanes(pw_i_ref[0, 0, g], tile4_ref[...])
        for t in range(SSM_CHUNK):
            rows = slice(t * SSM_GROUP_CH, (t + 1) * SSM_GROUP_CH)
            lo, hi = t * SSM_GROUP_CH, (t + 1) * SSM_GROUP_CH
            fwd = kcat if t == 0 else jnp.where(lane >= lo, pltpu.roll(kcat, lo, 1), 0.0)
            bwd = kcat if hi == SSM_FLAT else jnp.where(lane < hi, pltpu.roll(kcat, hi, 1), 0.0)
            m_ref[0, 0, g, rows, :] = jnp.where(d == 0, fwd, bwd).astype(BF16)
            bend_ref[0, 0, g, rows, :] = (pw_r[t:t + 1, :] * bb1 + pw_i[t:t + 1, :] * bb2).astype(BF16)
        return carry
    lax.fori_loop(0, SSM_GROUPS, build_ops, 0, unroll=4)


def _ssm_operators(tables):
    per_dir = lambda rows, cols: pl.BlockSpec((1, 1, SSM_GROUPS, rows, cols), lambda l, d: (l, d, 0, 0, 0))
    const = lambda rows: pl.BlockSpec((rows, SSM_FLAT), lambda l, d: (0, 0))
    shape = lambda rows: jax.ShapeDtypeStruct((DEPTH, 2, SSM_GROUPS, rows, SSM_FLAT), BF16)
    c = np.arange(SSM_FLAT)
    tile = (c[None, :] % SSM_GROUP_CH == np.arange(SSM_GROUP_CH)[:, None]).astype(np.float32)
    rep = (c[None, :] // SSM_GROUP_CH == np.arange(SSM_CHUNK)[:, None]).astype(np.float32)
    tile4 = (c[None, :] % SSM_STATE == np.arange(SSM_STATE)[:, None]).astype(np.float32)
    return pl.pallas_call(
        _ssm_ops_kernel,
        out_shape=[shape(SSM_FLAT), shape(SSM_FLAT), shape(LANES)],
        grid=(DEPTH, 2),
        in_specs=([per_dir(SSM_STATE, SSM_GROUP_CH)] * 6 + [per_dir(SSM_GROUP_CH, SSM_FLAT)] * 2
                  + [per_dir(SSM_CHUNK, SSM_STATE)] * 2 + [const(SSM_GROUP_CH), const(SSM_CHUNK), const(SSM_STATE)]),
        out_specs=[per_dir(SSM_FLAT, SSM_FLAT), per_dir(SSM_FLAT, SSM_FLAT), per_dir(LANES, SSM_FLAT)],
        compiler_params=pltpu.CompilerParams(dimension_semantics=("parallel", "parallel"),
                                             vmem_limit_bytes=VMEM_LIMIT),
        name="ssm_ops",
    )(*tables, jnp.asarray(tile, BF16), jnp.asarray(rep, BF16), jnp.asarray(tile4, BF16))


def _ssm_kernel(u_ref, m_ref, bend_ref, cpow_ref, a1_ref, a2_ref, h0s_ref, h0p_ref, d_ref, wglu_ref, *refs,
                n_seq, emit_final):
    if emit_final:
        out_ref, fin_ref, y_s, yt_s, u_s, inja_s, injb_s, hp_s = refs
    else:
        out_ref, y_s, yt_s, u_s, inja_s, injb_s, hp_s = refs
        fin_ref = None
    chunks_per_seq = N_CHUNK // n_seq

    def chunk_rows(t):
        sl = pl.ds(t, N_CHUNK, stride=SSM_CHUNK)
        return jnp.concatenate([u_ref[0, 0, sl, :], u_ref[0, 1, sl, :]], axis=1)

    for t in range(SSM_CHUNK):
        a_t = chunk_rows(t)
        yt_s[:, t * SSM_GROUP_CH:(t + 1) * SSM_GROUP_CH, :] = a_t.T.reshape(SSM_GROUPS, SSM_GROUP_CH, N_CHUNK)

    def tr_in(g, carry):
        u_s[g] = yt_s[g].T.astype(BF16)
        return carry
    lax.fori_loop(0, SSM_GROUPS, tr_in, 0, unroll=SSM_UNROLL)

    for d in range(2):
        def proj(g, carry, d=d):
            ug = u_s[g]
            y = _dot(ug, m_ref[0, d, g])
            y_s[g] = y if d == 0 else y_s[g] + y
            inj = _dot(ug, bend_ref[0, d, g])
            lanes = pl.ds(pl.multiple_of(g * LANES, LANES), LANES)
            inja_s[d, :, lanes] = inj[:, :LANES]
            injb_s[d, :, lanes] = inj[:, LANES:]
            return carry
        lax.fori_loop(0, SSM_GROUPS, proj, 0, unroll=SSM_UNROLL)

    a1 = a1_ref[0]
    a2 = a2_ref[0]

    def step(j, carry):
        s, sp = carry
        nf = j
        nb = N_CHUNK - 1 - j
        if n_seq > 1:
            keep = jnp.where(j % chunks_per_seq != 0, 1.0, 0.0).astype(F32)
            s = s * keep
            sp = sp * keep
        hp_s[0, pl.ds(nf, 1), :] = s[0:1]
        hp_s[1, pl.ds(nb, 1), :] = s[1:2]
        ia = jnp.concatenate([inja_s[0, pl.ds(nf, 1), :], inja_s[1, pl.ds(nb, 1), :]], axis=0)
        ib = jnp.concatenate([injb_s[0, pl.ds(nf, 1), :], injb_s[1, pl.ds(nb, 1), :]], axis=0)
        s_new = a1 * s + a2 * sp + ia
        sp_new = a1 * sp - a2 * s + ib
        if emit_final:
            @pl.when(j % chunks_per_seq == chunks_per_seq - 1)
            def _():
                sq = j // chunks_per_seq
                fin_ref[0, 0, pl.ds(sq, 1), :] = s_new[0:1]
                fin_ref[0, 1, pl.ds(n_seq - 1 - sq, 1), :] = s_new[1:2]
        return s_new, sp_new
    lax.fori_loop(0, N_CHUNK, step, (h0s_ref[0], h0p_ref[0]))

    for d in range(2):
        def carry_in(g, carry, d=d):
            hp = hp_s[d, :, pl.ds(pl.multiple_of(g * LANES, LANES), LANES)]
            y_s[g] = y_s[g] + _dot(hp.astype(BF16), cpow_ref[0, d, g])
            return carry
        lax.fori_loop(0, SSM_GROUPS, carry_in, 0, unroll=SSM_UNROLL)

    def tr_out(g, carry):
        yt_s[g] = y_s[g].T
        return carry
    lax.fori_loop(0, SSM_GROUPS, tr_out, 0, unroll=SSM_UNROLL)
    dskip = d_ref[0]
    wglu = wglu_ref[0]
    for t in range(SSM_CHUNK):
        blk = yt_s[:, t * SSM_GROUP_CH:(t + 1) * SSM_GROUP_CH, :].reshape(SSM_WIDTH, N_CHUNK)
        y = blk.T + chunk_rows(t) * dskip
        gl = jax.nn.gelu(y, approximate=True)
        gate = _dot(gl.astype(BF16), wglu)
        o = gl * _sigmoid(gate)
        out_ref[0, 0, pl.ds(t, N_CHUNK, stride=SSM_CHUNK), :] = o[:, :LANES]
        out_ref[0, 1, pl.ds(t, N_CHUNK, stride=SSM_CHUNK), :] = o[:, LANES:]


def _ssm(l, u, ops, a1, a2, h0s, h0p, dskip, wglu, *, n_seq, emit_final):
    bg = u.shape[0]
    m, bend, cpow = ops
    out_shape = [jax.ShapeDtypeStruct((bg, 2, TOKENS, LANES), F32)]
    out_specs = [pl.BlockSpec((1, 2, TOKENS, LANES), lambda b: (b, 0, 0, 0))]
    if emit_final:
        out_shape.append(jax.ShapeDtypeStruct((bg, 2, n_seq, STATE_LANES), F32))
        out_specs.append(pl.BlockSpec((1, 2, n_seq, STATE_LANES), lambda b: (b, 0, 0, 0)))
    resident = lambda rows: pl.BlockSpec((1, 2, SSM_GROUPS, rows, SSM_FLAT), lambda b: (l, 0, 0, 0, 0),
                                         pipeline_mode=pl.Buffered(1))
    return pl.pallas_call(
        functools.partial(_ssm_kernel, n_seq=n_seq, emit_final=emit_final),
        out_shape=out_shape,
        grid=(bg,),
        in_specs=[
            pl.BlockSpec((1, 2, TOKENS, LANES), lambda b: (b, 0, 0, 0)),
            resident(SSM_FLAT), resident(SSM_FLAT), resident(LANES),
            pl.BlockSpec((1, 2, STATE_LANES), lambda b: (l, 0, 0)),
            pl.BlockSpec((1, 2, STATE_LANES), lambda b: (l, 0, 0)),
            pl.BlockSpec((1, 2, STATE_LANES), lambda b: (b, 0, 0)),
            pl.BlockSpec((1, 2, STATE_LANES), lambda b: (b, 0, 0)),
            pl.BlockSpec((1, 1, SSM_WIDTH), lambda b: (l, 0, 0)),
            pl.BlockSpec((1, SSM_WIDTH, SSM_WIDTH), lambda b: (l, 0, 0)),
        ],
        out_specs=out_specs,
        scratch_shapes=[pltpu.VMEM((SSM_GROUPS, N_CHUNK, SSM_FLAT), F32),
                        pltpu.VMEM((SSM_GROUPS, SSM_FLAT, N_CHUNK), F32),
                        pltpu.VMEM((SSM_GROUPS, N_CHUNK, SSM_FLAT), BF16),
                        pltpu.VMEM((2, N_CHUNK, STATE_LANES), F32),
                        pltpu.VMEM((2, N_CHUNK, STATE_LANES), F32),
                        pltpu.VMEM((2, N_CHUNK, STATE_LANES), F32)],
        compiler_params=pltpu.CompilerParams(dimension_semantics=("parallel",), vmem_limit_bytes=VMEM_LIMIT),
        name="ssm_ctx" if emit_final else "ssm_lat",
    )(u, m, bend, cpow, a1, a2, h0s, h0p, dskip, wglu)


def _ssm_tables(lam_re, lam_im, log_dt, b_re, b_im, c_re, c_im):
    t_n = SSM_CHUNK
    lr = jnp.minimum(lam_re.astype(F32), -1e-4)
    li = lam_im.astype(F32)
    dt = jnp.exp(log_dt.astype(F32))[..., None]
    lr_dt, li_dt = lr * dt, li * dt

    def power(k, expand=lambda x: x):
        mag = jnp.exp(expand(lr_dt) * k)
        return mag * jnp.cos(expand(li_dt) * k), mag * jnp.sin(expand(li_dt) * k)
    l_r, l_i = power(1.0)
    a_r, a_i = power(float(t_n))
    n_r, n_i = l_r - 1.0, l_i
    den = lr * lr + li * li
    q_r, q_i = (n_r * lr + n_i * li) / den, (n_i * lr - n_r * li) / den
    bb_r = q_r[..., None] * b_re - q_i[..., None] * b_im
    bb_i = q_r[..., None] * b_im + q_i[..., None] * b_re

    up = np.arange(t_n, dtype=np.float32)
    down = up[::-1].copy()
    per_dir = lambda fwd, bwd: jnp.asarray(np.stack([fwd, bwd]))[None, :, None, None, :]
    on_lanes = lambda x: x[..., None]
    pk_r, pk_i = power(per_dir(up, down), on_lanes)
    pc_r, pc_i = power(per_dir(up + 1.0, down + 1.0), on_lanes)
    pw_r, pw_i = power(jnp.swapaxes(per_dir(down, up), -1, -2), lambda x: x[..., None, :])
    cr_t = jnp.swapaxes(c_re.astype(F32), -1, -2)
    ci_t = jnp.swapaxes(c_im.astype(F32), -1, -2)
    br_t, bi_t = jnp.swapaxes(bb_r, -1, -2), jnp.swapaxes(bb_i, -1, -2)
    bb1 = jnp.concatenate([br_t, bi_t, bi_t, br_t], axis=-1)
    bb2 = jnp.concatenate([-bi_t, br_t, br_t, -bi_t], axis=-1)
    a1 = jnp.concatenate([a_r, a_r], axis=-1)
    a2 = jnp.concatenate([-a_i, a_i], axis=-1)
    return cr_t, ci_t, pk_r, pk_i, pc_r, pc_i, bb1, bb2, pw_r, pw_i, a1, a2


def _attend_transposed(k, v, q, n_kv, rep):
    nq = q.shape[0]
    qt = q.astype(F32).T
    zero = jnp.zeros((HEAD_DIM, nq), F32)
    bd = jnp.concatenate(
        [jnp.concatenate([qt[(g * rep + r) * HEAD_DIM:(g * rep + r + 1) * HEAD_DIM] if g == gg else zero
                          for gg in range(n_kv) for r in range(rep)], axis=1)
         for g in range(n_kv)], axis=0).astype(BF16)
    s = _dot(k, bd)
    m = jnp.max(s, axis=0, keepdims=True)
    p = jnp.exp2(s - m)
    p = p * (1.0 / jnp.sum(p, axis=0, keepdims=True))
    ot = _dot(v.astype(F32).T.astype(BF16), p.astype(BF16))
    heads = [ot[g * HEAD_DIM:(g + 1) * HEAD_DIM, (g * rep + r) * nq:(g * rep + r + 1) * nq]
             for g in range(n_kv) for r in range(rep)]
    return jnp.concatenate(heads, axis=0).T


CTX_SEQS = 4


def _ctx_attn_kernel(z_ref, na_ref, gqa_ref):
    for i in range(CTX_SEQS):
        z = z_ref[i]
        na_ref[i] = _attend_transposed(z[:, Z_KNA:Z_KNA + NA_WIDTH], z[:, Z_VNA:Z_VNA + NA_WIDTH],
                                       z[:, Z_QNA:Z_QNA + NA_WIDTH], NA_HEADS, 1).astype(BF16)
        gqa_ref[i] = _attend_transposed(z[:, Z_KG:Z_KG + GQA_KV_WIDTH], z[:, Z_VG:Z_VG + GQA_KV_WIDTH],
                                        z[:, Z_QG:Z_QG + GQA_WIDTH], GQA_KV_HEADS, GQA_REP).astype(BF16)


def _ctx_attention(zb):
    z3 = zb.reshape(BATCH, SEQ, Z_WIDTH)
    na, gqa = pl.pallas_call(
        _ctx_attn_kernel,
        out_shape=[jax.ShapeDtypeStruct((BATCH, SEQ, NA_WIDTH), BF16),
                   jax.ShapeDtypeStruct((BATCH, SEQ, GQA_WIDTH), BF16)],
        grid=(BATCH // CTX_SEQS,),
        in_specs=[pl.BlockSpec((CTX_SEQS, SEQ, Z_WIDTH), lambda b: (b, 0, 0))],
        out_specs=[pl.BlockSpec((CTX_SEQS, SEQ, NA_WIDTH), lambda b: (b, 0, 0)),
                   pl.BlockSpec((CTX_SEQS, SEQ, GQA_WIDTH), lambda b: (b, 0, 0))],
        compiler_params=pltpu.CompilerParams(dimension_semantics=("parallel",), vmem_limit_bytes=VMEM_LIMIT),
        name="attn_ctx",
    )(z3)
    return na.reshape(1, TOKENS, NA_WIDTH), gqa.reshape(1, TOKENS, GQA_WIDTH)


NA_KEYS = NA_WIN_R * GRID_W


NA_ROWS = 16


def _na_lat_kernel(q_ref, k_ref, v_ref, ck_ref, cv_ref, bias_ref, o_ref):
    i = pl.program_id(1)
    ck = ck_ref[0, 0]
    cv = cv_ref[0, 0]
    qt = q_ref[0].astype(F32).T
    hd = lax.broadcasted_iota(jnp.int32, (NA_WIDTH, NA_WIDTH), 0) // HEAD_DIM
    hq = lax.broadcasted_iota(jnp.int32, (NA_WIDTH, NA_WIDTH), 1) // HEAD_DIM
    for j in range(NA_ROWS):
        r = NA_ROWS * i + j
        rs = jnp.clip(r - NA_WIN_R // 2, 0, GRID_H - NA_WIN_R)
        start = pl.multiple_of(rs * GRID_W, GRID_W)
        kw = k_ref[0, pl.ds(start, NA_KEYS), :]
        vw = v_ref[0, pl.ds(start, NA_KEYS), :]
        qj = qt[:, j * GRID_W:(j + 1) * GRID_W]
        bd = jnp.where(hd == hq, jnp.concatenate([qj] * NA_HEADS, axis=1), 0.0).astype(BF16)
        bias = bias_ref[0, pl.ds(NA_WIN_R - 1 - (r - rs), NA_WIN_R), :, :].reshape(NA_KEYS, NA_WIDTH)
        s_loc = _dot(kw, bd) + bias
        s_ctx = _dot(ck, bd)
        m = jnp.maximum(jnp.max(s_loc, axis=0, keepdims=True), jnp.max(s_ctx, axis=0, keepdims=True))
        p_loc = jnp.exp2(s_loc - m)
        p_ctx = jnp.exp2(s_ctx - m)
        inv = 1.0 / (jnp.sum(p_loc, axis=0, keepdims=True) + jnp.sum(p_ctx, axis=0, keepdims=True))
        full = (_dot((p_loc * inv).T.astype(BF16), vw)
                + _dot((p_ctx * inv).T.astype(BF16), cv))
        out = jnp.concatenate([full[h * HEAD_DIM:(h + 1) * HEAD_DIM, h * HEAD_DIM:(h + 1) * HEAD_DIM]
                               for h in range(NA_HEADS)], axis=1)
        o_ref[0, j * GRID_W:(j + 1) * GRID_W, :] = out.astype(BF16)


def _na_latent(l, zb, ck, cv, bias):
    tq = NA_ROWS * GRID_W
    return pl.pallas_call(
        _na_lat_kernel,
        out_shape=jax.ShapeDtypeStruct((DEC_BATCH, TOKENS, NA_WIDTH), BF16),
        grid=(DEC_BATCH, GRID_H // NA_ROWS),
        in_specs=[
            pl.BlockSpec((1, tq, NA_WIDTH), lambda b, i: (b, i, Z_QNA // NA_WIDTH)),
            pl.BlockSpec((1, TOKENS, NA_WIDTH), lambda b, i: (b, 0, Z_KNA // NA_WIDTH)),
            pl.BlockSpec((1, TOKENS, NA_WIDTH), lambda b, i: (b, 0, Z_VNA // NA_WIDTH)),
            pl.BlockSpec((1, 1, PAST_LEN, NA_WIDTH), lambda b, i: (b, l, 0, 0)),
            pl.BlockSpec((1, 1, PAST_LEN, NA_WIDTH), lambda b, i: (b, l, 0, 0)),
            pl.BlockSpec((1, 2 * NA_WIN_R - 1, GRID_W, NA_WIDTH), lambda b, i: (l, 0, 0, 0)),
        ],
        out_specs=pl.BlockSpec((1, tq, NA_WIDTH), lambda b, i: (b, i, 0)),
        compiler_params=pltpu.CompilerParams(dimension_semantics=("parallel", "arbitrary"),
                                             vmem_limit_bytes=VMEM_LIMIT),
        name="na_lat",
    )(zb, zb, zb, ck, cv, bias)


def _na_bias_table(tab):
    q = np.arange(GRID_W)
    kc = np.arange(GRID_W)
    cs = np.clip(q - NA_WIN_C // 2, 0, GRID_W - NA_WIN_C)
    valid = (kc[:, None] >= cs[None, :]) & (kc[:, None] < cs[None, :] + NA_WIN_C)
    w = GRID_W - 1
    rev = tab.astype(F32)[..., ::-1]
    padded = jnp.pad(rev, ((0, 0), (0, 0), (0, 0), (w, w)))
    off = w + NA_WIN_C - 1
    cols = jnp.stack([jnp.stack([padded[:, h, :, off - i:off - i + GRID_W] for i in range(GRID_W)], axis=2)
                      for h in range(NA_HEADS)], axis=3)
    cols = jnp.where(valid[:, None, :], cols * LOG2E, -jnp.inf)
    return cols.reshape(DEPTH, 2 * NA_WIN_R - 1, GRID_W, NA_WIDTH)


GQA_TK = 256
GQA_KEYS = DEC_SEQ + PAST_LEN


def _gqa_kernel(q_ref, ktn_ref, ktc_ref, vn_ref, vc_ref, o_ref, s_s, m_s, *, tq):
    q = q_ref[0]
    n_new = DEC_SEQ // GQA_TK
    half = GQA_REP * tq // 2
    qh = [jnp.concatenate([q[:, r * HEAD_DIM:(r + 1) * HEAD_DIM] for r in (2 * i, 2 * i + 1)], axis=0)
          for i in range(2)]
    rows = [slice(i * half, (i + 1) * half) for i in range(2)]

    def fold(s):
        return jnp.maximum(s[:, :LANES], s[:, LANES:])

    for i in range(2):
        s = _dot(qh[i], ktc_ref[0])
        s_s[rows[i], DEC_SEQ:GQA_KEYS] = s
        m_s[rows[i], :] = fold(s)
    for c in range(0, n_new, 2):
        for i in range(2):
            s0 = _dot(qh[i], ktn_ref[0, :, c * GQA_TK:(c + 1) * GQA_TK])
            s1 = _dot(qh[i], ktn_ref[0, :, (c + 1) * GQA_TK:(c + 2) * GQA_TK])
            s_s[rows[i], c * GQA_TK:(c + 1) * GQA_TK] = s0
            s_s[rows[i], (c + 1) * GQA_TK:(c + 2) * GQA_TK] = s1
            m_s[rows[i], :] = jnp.maximum(m_s[rows[i], :], jnp.maximum(fold(s0), fold(s1)))

    m = jnp.max(m_s[...], axis=-1, keepdims=True)
    m_s[...] = jnp.broadcast_to(m, m_s.shape)

    def probs(i, c0):
        mb = m_s[rows[i], :]
        return jnp.exp2(s_s[rows[i], c0:c0 + GQA_TK] - jnp.concatenate([mb, mb], axis=1)).astype(BF16)

    acc = [_dot(probs(i, DEC_SEQ), vc_ref[0, 0]) for i in range(2)]
    for c in range(n_new):
        for i in range(2):
            acc[i] = acc[i] + _dot(probs(i, c * GQA_TK), vn_ref[0, 0, c * GQA_TK:(c + 1) * GQA_TK, :])
    outs = []
    for i in range(2):
        out = acc[i][:, :HEAD_DIM] / acc[i][:, HEAD_DIM:HEAD_DIM + 1]
        outs += [out[:tq], out[tq:]]
    o_ref[0] = jnp.concatenate(outs, axis=1).astype(BF16)


def _gqa_latent(zb, kt_new, kt_cache, v1_new, v1_cache):
    tq = GQA_TQ
    rows = GQA_REP * tq
    return pl.pallas_call(
        functools.partial(_gqa_kernel, tq=tq),
        out_shape=jax.ShapeDtypeStruct((DEC_BATCH, TOKENS, GQA_WIDTH), BF16),
        grid=(DEC_BATCH, GQA_KV_HEADS, TOKENS // tq),
        in_specs=[
            pl.BlockSpec((1, tq, GQA_REP * HEAD_DIM), lambda b, g, i: (b, i, Z_QG // (GQA_REP * HEAD_DIM) + g)),
            pl.BlockSpec((1, HEAD_DIM, DEC_SEQ), lambda b, g, i: (b, g, 0)),
            pl.BlockSpec((1, HEAD_DIM, PAST_LEN), lambda b, g, i: (b, g, 0)),
            pl.BlockSpec((1, 1, DEC_SEQ, LANES), lambda b, g, i: (b, g, 0, 0)),
            pl.BlockSpec((1, 1, PAST_LEN, LANES), lambda b, g, i: (b, g, 0, 0)),
        ],
        out_specs=pl.BlockSpec((1, tq, GQA_REP * HEAD_DIM), lambda b, g, i: (b, i, g)),
        scratch_shapes=[pltpu.VMEM((rows, GQA_KEYS), F32), pltpu.VMEM((rows, LANES), F32)],
        compiler_params=pltpu.CompilerParams(dimension_semantics=("parallel", "parallel", "arbitrary"),
                                             vmem_limit_bytes=VMEM_LIMIT),
        name="gqa_lat",
    )(zb, kt_new, kt_cache, v1_new, v1_cache)


FF_CHUNK = 256


def _ffn_kernel(x_ref, ssm_ref, na_ref, gqa_ref, mod_ref, wout_ref, ln1g_ref, ln1b_ref, win_ref, wo_ref,
                ln2g_ref, ln2b_ref, o_ref, h2_s):
    ssm = jnp.concatenate([ssm_ref[0, 0], ssm_ref[0, 1]], axis=1)
    o = (_dot(ssm.astype(BF16), wout_ref[0, 0:SSM_WIDTH, :])
         + _dot(na_ref[0], wout_ref[0, SSM_WIDTH:SSM_WIDTH + NA_WIDTH, :])
         + _dot(gqa_ref[0], wout_ref[0, SSM_WIDTH + NA_WIDTH:, :]))
    y = DEEPNORM_ALPHA * x_ref[0] + mod_ref[0, 2:3, :] * o
    x1 = _ln(y) * ln1g_ref[0] + ln1b_ref[0]
    h2 = _ln(x1) * (1.0 + mod_ref[0, 4:5, :]) + mod_ref[0, 3:4, :]
    h2_s[...] = h2.astype(BF16)
    acc = None
    for j in range(D_FF // FF_CHUNK):
        h2b = h2_s[...]
        a = _dot(h2b, win_ref[0, :, j * FF_CHUNK:(j + 1) * FF_CHUNK])
        gt = _dot(h2b, win_ref[0, :, D_FF + j * FF_CHUNK:D_FF + (j + 1) * FF_CHUNK])
        f = (a * _sigmoid(a) * gt).astype(BF16)
        c = _dot(f, wo_ref[0, j * FF_CHUNK:(j + 1) * FF_CHUNK, :])
        acc = c if acc is None else acc + c
    y = DEEPNORM_ALPHA * x1 + mod_ref[0, 5:6, :] * acc
    o_ref[0] = _ln(y) * ln2g_ref[0] + ln2b_ref[0]


def _out_ffn(l, x, ssm, na, gqa, mod, w_out, ln1g, ln1b, w_ffn_in, w_ffn_out, ln2g, ln2b):
    bg = x.shape[0]
    tm = FFN_TM
    vec = pl.BlockSpec((1, 1, D_MODEL), lambda b, i: (l, 0, 0))
    resident = lambda shape: pl.BlockSpec((1,) + shape, lambda b, i: (l, 0, 0), pipeline_mode=pl.Buffered(1))
    return pl.pallas_call(
        _ffn_kernel,
        out_shape=jax.ShapeDtypeStruct((bg, TOKENS, D_MODEL), F32),
        grid=(bg, TOKENS // tm),
        in_specs=[
            pl.BlockSpec((1, tm, D_MODEL), lambda b, i: (b, i, 0)),
            pl.BlockSpec((1, 2, tm, LANES), lambda b, i: (b, 0, i, 0)),
            pl.BlockSpec((1, tm, NA_WIDTH), lambda b, i: (b, i, 0)),
            pl.BlockSpec((1, tm, GQA_WIDTH), lambda b, i: (b, i, 0)),
            pl.BlockSpec((1, 6, D_MODEL), lambda b, i: (b, 0, 0)),
            resident((D_MODEL, D_MODEL)),
            vec,
            vec,
            resident((D_MODEL, 2 * D_FF)),
            resident((D_FF, D_MODEL)),
            vec,
            vec,
        ],
        out_specs=pl.BlockSpec((1, tm, D_MODEL), lambda b, i: (b, i, 0)),
        scratch_shapes=[pltpu.VMEM((tm, D_MODEL), BF16)],
        compiler_params=pltpu.CompilerParams(dimension_semantics=("parallel", "parallel"),
                                             vmem_limit_bytes=VMEM_LIMIT),
        name="ffn",
    )(x, ssm, na, gqa, mod, w_out, ln1g, ln1b, w_ffn_in, w_ffn_out, ln2g, ln2b)


def _rope_tables():
    nf = HEAD_DIM // 4
    t = np.arange(DEC_SEQ)
    inv = ROPE_THETA ** (-np.arange(nf, dtype=np.float64) / nf)
    pos = np.stack([t // GRID_W, t % GRID_W], axis=1).astype(np.float64)
    ang = pos[:, :, None] * inv
    cos = np.repeat(np.cos(ang)[:, :, None, :], 2, axis=2).reshape(DEC_SEQ, HEAD_DIM)
    sin = (np.sin(ang)[:, :, None, :] * np.array([-1.0, 1.0])[None, None, :, None]).reshape(DEC_SEQ, HEAD_DIM)
    return jnp.asarray(np.tile(cos, (1, 2)), F32), jnp.asarray(np.tile(sin, (1, 2)), F32)


def _head_mean_matrix():
    h = np.arange(GQA_WIDTH) // HEAD_DIM
    return jnp.asarray((h[:, None] == h[None, :]).astype(np.float32) / HEAD_DIM, BF16)


def kernel(x_prompt, x_sample, c, cache_na_k, cache_na_v, cache_gqa_k, cache_gqa_v, state_ssm_re, state_ssm_im,
           c_ctx, w_ada, b_ada, w_in, w_out, q_norm_g, k_norm_g, na_bias, ssm_lam_re, ssm_lam_im, ssm_log_dt,
           ssm_b_re, ssm_b_im, ssm_c_re, ssm_c_im, ssm_d, w_ssm_glu, ln1_g, ln1_b, ln2_g, ln2_b,
           w_ffn_in, w_ffn_out):
    cond8 = jnp.concatenate([c_ctx[None, :], c, jnp.zeros((8 - 1 - DEC_BATCH, D_MODEL), F32)], axis=0)
    mod = _modulation(cond8, w_ada, b_ada).reshape(DEPTH, 8, 6, D_MODEL)
    cos, sin = _rope_tables()
    hm = _head_mean_matrix()
    w_in_b = w_in.astype(BF16)
    w_out_b = w_out.astype(BF16)
    w_ffn_in_b = w_ffn_in.astype(BF16)
    w_ffn_out_b = w_ffn_out.astype(BF16)
    w_glu_b = w_ssm_glu.astype(BF16)
    qg = jnp.tile(q_norm_g, (1, GQA_HEADS))[:, None, :]
    kg = jnp.tile(k_norm_g, (1, GQA_KV_HEADS))[:, None, :]
    dskip = ssm_d[:, None, :]
    ln = (ln1_g[:, None, :], ln1_b[:, None, :], ln2_g[:, None, :], ln2_b[:, None, :])
    *ssm_tabs, a1, a2 = _ssm_tables(ssm_lam_re, ssm_lam_im, ssm_log_dt, ssm_b_re, ssm_b_im, ssm_c_re, ssm_c_im)
    ops = _ssm_operators(ssm_tabs)
    a1 = a1.reshape(DEPTH, 2, STATE_LANES)
    a2 = a2.reshape(DEPTH, 2, STATE_LANES)
    na_tab = _na_bias_table(na_bias)
    ck_na = cache_na_k.reshape(DEC_BATCH, DEPTH, PAST_LEN, NA_WIDTH).astype(BF16)
    cv_na = cache_na_v.reshape(DEC_BATCH, DEPTH, PAST_LEN, NA_WIDTH).astype(BF16)
    kt_cache = cache_gqa_k.transpose(1, 0, 3, 4, 2).reshape(DEPTH, DEC_BATCH, GQA_KV_WIDTH, PAST_LEN).astype(BF16)
    cv_g = cache_gqa_v.transpose(1, 0, 3, 2, 4)
    v1_cache = jnp.concatenate([cv_g, jnp.ones(cv_g.shape[:-1] + (1,), F32),
                                jnp.zeros(cv_g.shape[:-1] + (LANES - HEAD_DIM - 1,), F32)], axis=-1).astype(BF16)
    s_re = state_ssm_re.astype(F32).transpose(1, 0, 2, 3, 4)
    s_im = state_ssm_im.astype(F32).transpose(1, 0, 2, 3, 4)
    h0s_lat = jnp.concatenate([s_re, s_im], axis=-1).reshape(DEPTH, DEC_BATCH, 2, STATE_LANES)
    h0p_lat = jnp.concatenate([s_im, s_re], axis=-1).reshape(DEPTH, DEC_BATCH, 2, STATE_LANES)
    h0_ctx = jnp.zeros((1, 2, STATE_LANES), F32)

    y_ctx = x_prompt.reshape(1, TOKENS, D_MODEL)
    y_lat = x_sample
    kv_na_l, kv_g_l, fin_l = [], [], []
    for l in range(DEPTH):
        mod_c = mod[l, 0:1]
        u, zb, kv_na, kv_g = _in_projection(l, y_ctx, mod_c, w_in_b, qg, kg, hm, cos, sin, latent=False)
        ssm_o, fin = _ssm(l, u, ops, a1, a2, h0_ctx, h0_ctx, dskip, w_glu_b, n_seq=BATCH, emit_final=True)
        na_o, gqa_o = _ctx_attention(zb)
        y_ctx = _out_ffn(l, y_ctx, ssm_o, na_o, gqa_o, mod_c, w_out_b, ln[0], ln[1], w_ffn_in_b, w_ffn_out_b,
                         ln[2], ln[3])
        kv_na_l.append(kv_na); kv_g_l.append(kv_g); fin_l.append(fin)

        mod_s = mod[l, 1:1 + DEC_BATCH]
        u, zb, kt_new, v1_new = _in_projection(l, y_lat, mod_s, w_in_b, qg, kg, hm, cos, sin, latent=True)
        ssm_o = _ssm(l, u, ops, a1, a2, h0s_lat[l], h0p_lat[l], dskip, w_glu_b, n_seq=1,
                     emit_final=False)[0]
        na_o = _na_latent(l, zb, ck_na, cv_na, na_tab)
        gqa_o = _gqa_latent(zb, kt_new, kt_cache[l], v1_new, v1_cache[l])
        y_lat = _out_ffn(l, y_lat, ssm_o, na_o, gqa_o, mod_s, w_out_b, ln[0], ln[1], w_ffn_in_b, w_ffn_out_b,
                         ln[2], ln[3])

    kv_na = jnp.stack(kv_na_l, axis=0).reshape(DEPTH, BATCH, SEQ, 2, NA_HEADS, HEAD_DIM)
    kv_g = jnp.stack(kv_g_l, axis=0).reshape(DEPTH, BATCH, SEQ, 2, GQA_KV_HEADS, HEAD_DIM)
    kv_na = kv_na.transpose(3, 1, 0, 2, 4, 5)
    kv_g = kv_g.transpose(3, 1, 0, 2, 4, 5)
    fin = jnp.stack(fin_l, axis=0).reshape(DEPTH, 2, BATCH, SSM_GROUPS, 2, SSM_STATE)
    fin = fin.transpose(4, 2, 0, 1, 3, 5)
    return (y_ctx.reshape(BATCH, SEQ, D_MODEL), y_lat, kv_na[0], kv_na[1], kv_g[0], kv_g[1], fin[0], fin[1])
```

```python
import functools
import math

import numpy as np
import jax
import jax.numpy as jnp
from jax import lax
from jax.experimental import pallas as pl
from jax.experimental.pallas import tpu as pltpu

F32 = jnp.float32
BF16 = jnp.bfloat16

D_MODEL = 1024
BATCH = 16
SEQ = 256
DEPTH = 2
DEC_BATCH = 2
DEC_SEQ = 4096
PAST_LEN = 256
GRID_W = 64
GRID_H = DEC_SEQ // GRID_W
HEAD_DIM = 64
SSM_WIDTH = 256
SSM_GROUP_CH = 16
SSM_GROUPS = 16
SSM_STATE = 64
NA_HEADS = 4
NA_WIDTH = 256
NA_WIN_R = 8
NA_WIN_C = 16
GQA_HEADS = 8
GQA_KV_HEADS = 2
GQA_REP = 4
GQA_WIDTH = 512
GQA_KV_WIDTH = 128
IN_WIDTH = 1792
D_FF = 2816
ROPE_THETA = 10000.0
LN_EPS = 1e-6
RMS_EPS = 1e-6
DEEPNORM_ALPHA = (2 * DEPTH) ** 0.25
LOG2E = math.log2(math.e)
Q_SCALE = HEAD_DIM ** -0.5 * LOG2E

TOKENS = 4096
SSM_CHUNK = 16
N_CHUNK = TOKENS // SSM_CHUNK
Z_WIDTH = IN_WIDTH - SSM_WIDTH
Z_QNA, Z_KNA, Z_VNA, Z_QG, Z_KG, Z_VG = 0, 256, 512, 768, 1280, 1408

VMEM_LIMIT = 56 * 1024 * 1024
LANES = 128
MXU_EDGE = 256
SSM_FLAT = SSM_CHUNK * SSM_GROUP_CH
assert SSM_FLAT == MXU_EDGE

MOD_NB = 1536
WIN_TM = 1024
FFN_TM = 512
GQA_TQ = 512


def _sigmoid(x):
    return 1.0 / (1.0 + jnp.exp(-x))


def _ln(x):
    mu = jnp.mean(x, axis=-1, keepdims=True)
    xc = x - mu
    var = jnp.mean(xc * xc, axis=-1, keepdims=True)
    return xc * lax.rsqrt(var + LN_EPS)


def _dot(a, b):
    return jnp.dot(a, b, preferred_element_type=F32)


def _mod_kernel(c_ref, w_ref, b_ref, o_ref):
    c = c_ref[...]
    s = c * _sigmoid(c)
    o_ref[0] = jnp.dot(s, w_ref[0], preferred_element_type=F32, precision=lax.Precision.HIGHEST) + b_ref[0]


def _modulation(cond8, w_ada, b_ada):
    nb = MOD_NB
    return pl.pallas_call(
        _mod_kernel,
        out_shape=jax.ShapeDtypeStruct((DEPTH, 8, 6 * D_MODEL), F32),
        grid=(DEPTH, 6 * D_MODEL // nb),
        in_specs=[
            pl.BlockSpec((8, D_MODEL), lambda l, j: (0, 0)),
            pl.BlockSpec((1, D_MODEL, nb), lambda l, j: (l, 0, j)),
            pl.BlockSpec((1, 1, nb), lambda l, j: (l, 0, j)),
        ],
        out_specs=pl.BlockSpec((1, 8, nb), lambda l, j: (l, 0, j)),
        compiler_params=pltpu.CompilerParams(dimension_semantics=("parallel", "parallel"),
                                             vmem_limit_bytes=VMEM_LIMIT),
        name="mod",
    )(cond8, w_ada, b_ada.reshape(DEPTH, 1, 6 * D_MODEL))


def _swap16(x):
    w = x.shape[-1]
    lane = lax.broadcasted_iota(jnp.int32, x.shape, x.ndim - 1)
    return jnp.where((lane & 16) != 0, pltpu.roll(x, 16, x.ndim - 1), pltpu.roll(x, w - 16, x.ndim - 1))


def _win_kernel(x_ref, mod_ref, w_ref, qg_ref, kg_ref, hm_ref, cos_ref, sin_ref, *out_refs, latent):
    u_ref, z_ref = out_refs[0], out_refs[1]
    x = x_ref[0]
    h = _ln(x) * (1.0 + mod_ref[0, 1:2, :]) + mod_ref[0, 0:1, :]
    z = _dot(h.astype(BF16), w_ref[0])
    u_ref[0, 0] = z[:, 0:LANES]
    u_ref[0, 1] = z[:, LANES:SSM_WIDTH]
    c_qna, c_kna, c_qg, c_kg, c_vg = np.cumsum([SSM_WIDTH, NA_WIDTH, 2 * NA_WIDTH, GQA_WIDTH, GQA_KV_WIDTH])
    q_na = z[:, c_qna:c_kna] * Q_SCALE
    kv_na = z[:, c_kna:c_qg]
    q_g = z[:, c_qg:c_kg]
    k_g = z[:, c_kg:c_vg]
    v_g = z[:, c_vg:IN_WIDTH]
    q_ms = _dot((q_g * q_g).astype(BF16), hm_ref[...])
    k_ms = _dot((k_g * k_g).astype(BF16), hm_ref[0:GQA_KV_WIDTH, 0:GQA_KV_WIDTH])
    q_n = q_g * lax.rsqrt(q_ms + RMS_EPS) * qg_ref[0]
    k_n = k_g * lax.rsqrt(k_ms + RMS_EPS) * kg_ref[0]
    if latent:
        kt_ref, v1_ref = out_refs[2], out_refs[3]
        cos = cos_ref[...]
        sin = sin_ref[...]
        cos4 = jnp.concatenate([cos] * 4, axis=1)
        sin4 = jnp.concatenate([sin] * 4, axis=1)
        q_n = (q_n * cos4 + _swap16(q_n) * sin4) * Q_SCALE
        k_n = k_n * cos + _swap16(k_n) * sin
        kt_ref[0] = k_n.T.astype(BF16)
        lane = lax.broadcasted_iota(jnp.int32, v_g.shape, 1)
        ones_col = jnp.where(lane == HEAD_DIM, 1.0, 0.0)
        v1_ref[0, 0] = jnp.where(lane < HEAD_DIM, v_g, ones_col).astype(BF16)
        v1_ref[0, 1] = jnp.where(lane < HEAD_DIM, pltpu.roll(v_g, HEAD_DIM, 1), ones_col).astype(BF16)
    else:
        out_refs[2][0] = kv_na
        out_refs[3][0] = jnp.concatenate([k_n, v_g], axis=1)
        q_n = q_n * Q_SCALE
    zb = jnp.concatenate([q_na, kv_na, q_n, k_n, v_g], axis=1)
    z_ref[0] = zb.astype(BF16)


def _in_projection(l, x, mod, w_in, qg, kg, hm, cos, sin, *, latent):
    bg = x.shape[0]
    tm = WIN_TM
    out_shape = [jax.ShapeDtypeStruct((bg, 2, TOKENS, LANES), F32),
                 jax.ShapeDtypeStruct((bg, TOKENS, Z_WIDTH), BF16)]
    out_specs = [pl.BlockSpec((1, 2, tm, LANES), lambda b, i: (b, 0, i, 0)),
                 pl.BlockSpec((1, tm, Z_WIDTH), lambda b, i: (b, i, 0))]
    if latent:
        out_shape += [jax.ShapeDtypeStruct((bg, GQA_KV_WIDTH, TOKENS), BF16),
                      jax.ShapeDtypeStruct((bg, GQA_KV_HEADS, TOKENS, LANES), BF16)]
        out_specs += [pl.BlockSpec((1, GQA_KV_WIDTH, tm), lambda b, i: (b, 0, i)),
                      pl.BlockSpec((1, GQA_KV_HEADS, tm, LANES), lambda b, i: (b, 0, i, 0))]
    else:
        out_shape += [jax.ShapeDtypeStruct((bg, TOKENS, 2 * NA_WIDTH), F32),
                      jax.ShapeDtypeStruct((bg, TOKENS, 2 * GQA_KV_WIDTH), F32)]
        out_specs += [pl.BlockSpec((1, tm, 2 * NA_WIDTH), lambda b, i: (b, i, 0)),
                      pl.BlockSpec((1, tm, 2 * GQA_KV_WIDTH), lambda b, i: (b, i, 0))]
    return pl.pallas_call(
        functools.partial(_win_kernel, latent=latent),
        out_shape=out_shape,
        grid=(bg, TOKENS // tm),
        in_specs=[
            pl.BlockSpec((1, tm, D_MODEL), lambda b, i: (b, i, 0)),
            pl.BlockSpec((1, 6, D_MODEL), lambda b, i: (b, 0, 0)),
            pl.BlockSpec((1, D_MODEL, IN_WIDTH), lambda b, i: (l, 0, 0)),
            pl.BlockSpec((1, 1, GQA_WIDTH), lambda b, i: (l, 0, 0)),
            pl.BlockSpec((1, 1, GQA_KV_WIDTH), lambda b, i: (l, 0, 0)),
            pl.BlockSpec((GQA_WIDTH, GQA_WIDTH), lambda b, i: (0, 0)),
            pl.BlockSpec((tm, LANES), lambda b, i: (i, 0)),
            pl.BlockSpec((tm, LANES), lambda b, i: (i, 0)),
        ],
        out_specs=out_specs,
        compiler_params=pltpu.CompilerParams(dimension_semantics=("parallel", "parallel"),
                                             vmem_limit_bytes=VMEM_LIMIT),
        name="win_lat" if latent else "win_ctx",
    )(x, mod, w_in, qg, kg, hm, cos, sin)


STATE_LANES = SSM_GROUPS * LANES
SSM_UNROLL = 16


def _ssm_ops_kernel(ct_r_ref, ct_i_ref, pk_r_ref, pk_i_ref, pc_r_ref, pc_i_ref, bb1_ref, bb2_ref, pw_r_ref, pw_i_ref,
                    tile_ref, rep_ref, tile4_ref, m_ref, bend_ref, cpow_ref):
    d = pl.program_id(1)

    def onto_lanes(x, sel):
        hi = x.astype(BF16)
        lo = (x - hi.astype(F32)).astype(BF16)
        return _dot(hi, sel) + _dot(lo, sel)

    def build_ops(g, carry):
        cr = onto_lanes(ct_r_ref[0, 0, g], tile_ref[...])
        ci = onto_lanes(ct_i_ref[0, 0, g], tile_ref[...])

        def re_proj(pr_ref, pi_ref):
            pr = onto_lanes(pr_ref[0, 0, g], rep_ref[...])
            pi = onto_lanes(pi_ref[0, 0, g], rep_ref[...])
            return jnp.concatenate([cr * pr - ci * pi, -(cr * pi + ci * pr)], axis=0)
        cpow_ref[0, 0, g] = re_proj(pc_r_ref, pc_i_ref).astype(BF16)
        bb1 = bb1_ref[0, 0, g]
        bb2 = bb2_ref[0, 0, g]
        kcat = jnp.dot(bb1[:, :LANES], re_proj(pk_r_ref, pk_i_ref),
                       preferred_element_type=F32, precision=lax.Precision.HIGHEST)
        lane = lax.broadcasted_iota(jnp.int32, kcat.shape, 1)
        pw_r = onto_lanes(pw_r_ref[0, 0, g], tile4_ref[...])
        pw_i = onto_lanes(pw_i_ref[0, 0, g], tile4_ref[...])
        for t in range(SSM_CHUNK):
            rows = slice(t * SSM_GROUP_CH, (t + 1) * SSM_GROUP_CH)
            lo, hi = t * SSM_GROUP_CH, (t + 1) * SSM_GROUP_CH
            fwd = kcat if t == 0 else jnp.where(lane >= lo, pltpu.roll(kcat, lo, 1), 0.0)
            bwd = kcat if hi == SSM_FLAT else jnp.where(lane < hi, pltpu.roll(kcat, hi, 1), 0.0)
            m_ref[0, 0, g, rows, :] = jnp.where(d == 0, fwd, bwd).astype(BF16)
            bend_ref[0, 0, g, rows, :] = (pw_r[t:t + 1, :] * bb1 + pw_i[t:t + 1, :] * bb2).astype(BF16)
        return carry
    lax.fori_loop(0, SSM_GROUPS, build_ops, 0, unroll=4)


def _ssm_operators(tables):
    per_dir = lambda rows, cols: pl.BlockSpec((1, 1, SSM_GROUPS, rows, cols), lambda l, d: (l, d, 0, 0, 0))
    const = lambda rows: pl.BlockSpec((rows, SSM_FLAT), lambda l, d: (0, 0))
    shape = lambda rows: jax.ShapeDtypeStruct((DEPTH, 2, SSM_GROUPS, rows, SSM_FLAT), BF16)
    c = np.arange(SSM_FLAT)
    tile = (c[None, :] % SSM_GROUP_CH == np.arange(SSM_GROUP_CH)[:, None]).astype(np.float32)
    rep = (c[None, :] // SSM_GROUP_CH == np.arange(SSM_CHUNK)[:, None]).astype(np.float32)
    tile4 = (c[None, :] % SSM_STATE == np.arange(SSM_STATE)[:, None]).astype(np.float32)
    return pl.pallas_call(
        _ssm_ops_kernel,
        out_shape=[shape(SSM_FLAT), shape(SSM_FLAT), shape(LANES)],
        grid=(DEPTH, 2),
        in_specs=([per_dir(SSM_STATE, SSM_GROUP_CH)] * 6 + [per_dir(SSM_GROUP_CH, SSM_FLAT)] * 2
                  + [per_dir(SSM_CHUNK, SSM_STATE)] * 2 + [const(SSM_GROUP_CH), const(SSM_CHUNK), const(SSM_STATE)]),
        out_specs=[per_dir(SSM_FLAT, SSM_FLAT), per_dir(SSM_FLAT, SSM_FLAT), per_dir(LANES, SSM_FLAT)],
        compiler_params=pltpu.CompilerParams(dimension_semantics=("parallel", "parallel"),
                                             vmem_limit_bytes=VMEM_LIMIT),
        name="ssm_ops",
    )(*tables, jnp.asarray(tile, BF16), jnp.asarray(rep, BF16), jnp.asarray(tile4, BF16))


def _ssm_kernel(u_ref, m_ref, bend_ref, cpow_ref, a1_ref, a2_ref, h0s_ref, h0p_ref, d_ref, wglu_ref, *refs,
                n_seq, emit_final):
    if emit_final:
        out_ref, fin_ref, y_s, yt_s, u_s, inja_s, injb_s, hp_s = refs
    else:
        out_ref, y_s, yt_s, u_s, inja_s, injb_s, hp_s = refs
        fin_ref = None
    chunks_per_seq = N_CHUNK // n_seq

    def chunk_rows(t):
        sl = pl.ds(t, N_CHUNK, stride=SSM_CHUNK)
        return jnp.concatenate([u_ref[0, 0, sl, :], u_ref[0, 1, sl, :]], axis=1)

    for t in range(SSM_CHUNK):
        a_t = chunk_rows(t)
        yt_s[:, t * SSM_GROUP_CH:(t + 1) * SSM_GROUP_CH, :] = a_t.T.reshape(SSM_GROUPS, SSM_GROUP_CH, N_CHUNK)

    def tr_in(g, carry):
        u_s[g] = yt_s[g].T.astype(BF16)
        return carry
    lax.fori_loop(0, SSM_GROUPS, tr_in, 0, unroll=SSM_UNROLL)

    for d in range(2):
        def proj(g, carry, d=d):
            ug = u_s[g]
            y = _dot(ug, m_ref[0, d, g])
            y_s[g] = y if d == 0 else y_s[g] + y
            inj = _dot(ug, bend_ref[0, d, g])
            lanes = pl.ds(pl.multiple_of(g * LANES, LANES), LANES)
            inja_s[d, :, lanes] = inj[:, :LANES]
            injb_s[d, :, lanes] = inj[:, LANES:]
            return carry
        lax.fori_loop(0, SSM_GROUPS, proj, 0, unroll=SSM_UNROLL)

    a1 = a1_ref[0]
    a2 = a2_ref[0]

    def step(j, carry):
        s, sp = carry
        nf = j
        nb = N_CHUNK - 1 - j
        if n_seq > 1:
            keep = jnp.where(j % chunks_per_seq != 0, 1.0, 0.0).astype(F32)
            s = s * keep
            sp = sp * keep
        hp_s[0, pl.ds(nf, 1), :] = s[0:1]
        hp_s[1, pl.ds(nb, 1), :] = s[1:2]
        ia = jnp.concatenate([inja_s[0, pl.ds(nf, 1), :], inja_s[1, pl.ds(nb, 1), :]], axis=0)
        ib = jnp.concatenate([injb_s[0, pl.ds(nf, 1), :], injb_s[1, pl.ds(nb, 1), :]], axis=0)
        s_new = a1 * s + a2 * sp + ia
        sp_new = a1 * sp - a2 * s + ib
        if emit_final:
            @pl.when(j % chunks_per_seq == chunks_per_seq - 1)
            def _():
                sq = j // chunks_per_seq
                fin_ref[0, 0, pl.ds(sq, 1), :] = s_new[0:1]
                fin_ref[0, 1, pl.ds(n_seq - 1 - sq, 1), :] = s_new[1:2]
        return s_new, sp_new
    lax.fori_loop(0, N_CHUNK, step, (h0s_ref[0], h0p_ref[0]))

    for d in range(2):
        def carry_in(g, carry, d=d):
            hp = hp_s[d, :, pl.ds(pl.multiple_of(g * LANES, LANES), LANES)]
            y_s[g] = y_s[g] + _dot(hp.astype(BF16), cpow_ref[0, d, g])
            return carry
        lax.fori_loop(0, SSM_GROUPS, carry_in, 0, unroll=SSM_UNROLL)

    def tr_out(g, carry):
        yt_s[g] = y_s[g].T
        return carry
    lax.fori_loop(0, SSM_GROUPS, tr_out, 0, unroll=SSM_UNROLL)
    dskip = d_ref[0]
    wglu = wglu_ref[0]
    for t in range(SSM_CHUNK):
        blk = yt_s[:, t * SSM_GROUP_CH:(t + 1) * SSM_GROUP_CH, :].reshape(SSM_WIDTH, N_CHUNK)
        y = blk.T + chunk_rows(t) * dskip
        gl = jax.nn.gelu(y, approximate=True)
        gate = _dot(gl.astype(BF16), wglu)
        o = gl * _sigmoid(gate)
        out_ref[0, 0, pl.ds(t, N_CHUNK, stride=SSM_CHUNK), :] = o[:, :LANES]
        out_ref[0, 1, pl.ds(t, N_CHUNK, stride=SSM_CHUNK), :] = o[:, LANES:]


def _ssm(l, u, ops, a1, a2, h0s, h0p, dskip, wglu, *, n_seq, emit_final):
    bg = u.shape[0]
    m, bend, cpow = ops
    out_shape = [jax.ShapeDtypeStruct((bg, 2, TOKENS, LANES), F32)]
    out_specs = [pl.BlockSpec((1, 2, TOKENS, LANES), lambda b: (b, 0, 0, 0))]
    if emit_final:
        out_shape.append(jax.ShapeDtypeStruct((bg, 2, n_seq, STATE_LANES), F32))
        out_specs.append(pl.BlockSpec((1, 2, n_seq, STATE_LANES), lambda b: (b, 0, 0, 0)))
    resident = lambda rows: pl.BlockSpec((1, 2, SSM_GROUPS, rows, SSM_FLAT), lambda b: (l, 0, 0, 0, 0),
                                         pipeline_mode=pl.Buffered(1))
    return pl.pallas_call(
        functools.partial(_ssm_kernel, n_seq=n_seq, emit_final=emit_final),
        out_shape=out_shape,
        grid=(bg,),
        in_specs=[
            pl.BlockSpec((1, 2, TOKENS, LANES), lambda b: (b, 0, 0, 0)),
            resident(SSM_FLAT), resident(SSM_FLAT), resident(LANES),
            pl.BlockSpec((1, 2, STATE_LANES), lambda b: (l, 0, 0)),
            pl.BlockSpec((1, 2, STATE_LANES), lambda b: (l, 0, 0)),
            pl.BlockSpec((1, 2, STATE_LANES), lambda b: (b, 0, 0)),
            pl.BlockSpec((1, 2, STATE_LANES), lambda b: (b, 0, 0)),
            pl.BlockSpec((1, 1, SSM_WIDTH), lambda b: (l, 0, 0)),
            pl.BlockSpec((1, SSM_WIDTH, SSM_WIDTH), lambda b: (l, 0, 0)),
        ],
        out_specs=out_specs,
        scratch_shapes=[pltpu.VMEM((SSM_GROUPS, N_CHUNK, SSM_FLAT), F32),
                        pltpu.VMEM((SSM_GROUPS, SSM_FLAT, N_CHUNK), F32),
                        pltpu.VMEM((SSM_GROUPS, N_CHUNK, SSM_FLAT), BF16),
                        pltpu.VMEM((2, N_CHUNK, STATE_LANES), F32),
                        pltpu.VMEM((2, N_CHUNK, STATE_LANES), F32),
                        pltpu.VMEM((2, N_CHUNK, STATE_LANES), F32)],
        compiler_params=pltpu.CompilerParams(dimension_semantics=("parallel",), vmem_limit_bytes=VMEM_LIMIT),
        name="ssm_ctx" if emit_final else "ssm_lat",
    )(u, m, bend, cpow, a1, a2, h0s, h0p, dskip, wglu)


def _ssm_tables(lam_re, lam_im, log_dt, b_re, b_im, c_re, c_im):
    t_n = SSM_CHUNK
    lr = jnp.minimum(lam_re.astype(F32), -1e-4)
    li = lam_im.astype(F32)
    dt = jnp.exp(log_dt.astype(F32))[..., None]
    lr_dt, li_dt = lr * dt, li * dt

    def power(k, expand=lambda x: x):
        mag = jnp.exp(expand(lr_dt) * k)
        return mag * jnp.cos(expand(li_dt) * k), mag * jnp.sin(expand(li_dt) * k)
    l_r, l_i = power(1.0)
    a_r, a_i = power(float(t_n))
    n_r, n_i = l_r - 1.0, l_i
    den = lr * lr + li * li
    q_r, q_i = (n_r * lr + n_i * li) / den, (n_i * lr - n_r * li) / den
    bb_r = q_r[..., None] * b_re - q_i[..., None] * b_im
    bb_i = q_r[..., None] * b_im + q_i[..., None] * b_re

    up = np.arange(t_n, dtype=np.float32)
    down = up[::-1].copy()
    per_dir = lambda fwd, bwd: jnp.asarray(np.stack([fwd, bwd]))[None, :, None, None, :]
    on_lanes = lambda x: x[..., None]
    pk_r, pk_i = power(per_dir(up, down), on_lanes)
    pc_r, pc_i = power(per_dir(up + 1.0, down + 1.0), on_lanes)
    pw_r, pw_i = power(jnp.swapaxes(per_dir(down, up), -1, -2), lambda x: x[..., None, :])
    cr_t = jnp.swapaxes(c_re.astype(F32), -1, -2)
    ci_t = jnp.swapaxes(c_im.astype(F32), -1, -2)
    br_t, bi_t = jnp.swapaxes(bb_r, -1, -2), jnp.swapaxes(bb_i, -1, -2)
    bb1 = jnp.concatenate([br_t, bi_t, bi_t, br_t], axis=-1)
    bb2 = jnp.concatenate([-bi_t, br_t, br_t, -bi_t], axis=-1)
    a1 = jnp.concatenate([a_r, a_r], axis=-1)
    a2 = jnp.concatenate([-a_i, a_i], axis=-1)
    return cr_t, ci_t, pk_r, pk_i, pc_r, pc_i, bb1, bb2, pw_r, pw_i, a1, a2


def _attend_transposed(k, v, q, n_kv, rep):
    nq = q.shape[0]
    qt = q.astype(F32).T
    zero = jnp.zeros((HEAD_DIM, nq), F32)
    bd = jnp.concatenate(
        [jnp.concatenate([qt[(g * rep + r) * HEAD_DIM:(g * rep + r + 1) * HEAD_DIM] if g == gg else zero
                          for gg in range(n_kv) for r in range(rep)], axis=1)
         for g in range(n_kv)], axis=0).astype(BF16)
    s = _dot(k, bd)
    m = jnp.max(s, axis=0, keepdims=True)
    p = jnp.exp2(s - m)
    p = p * (1.0 / jnp.sum(p, axis=0, keepdims=True))
    ot = _dot(v.astype(F32).T.astype(BF16), p.astype(BF16))
    heads = [ot[g * HEAD_DIM:(g + 1) * HEAD_DIM, (g * rep + r) * nq:(g * rep + r + 1) * nq]
             for g in range(n_kv) for r in range(rep)]
    return jnp.concatenate(heads, axis=0).T


CTX_SEQS = 4


def _ctx_attn_kernel(z_ref, na_ref, gqa_ref):
    for i in range(CTX_SEQS):
        z = z_ref[i]
        na_ref[i] = _attend_transposed(z[:, Z_KNA:Z_KNA + NA_WIDTH], z[:, Z_VNA:Z_VNA + NA_WIDTH],
                                       z[:, Z_QNA:Z_QNA + NA_WIDTH], NA_HEADS, 1).astype(BF16)
        gqa_ref[i] = _attend_transposed(z[:, Z_KG:Z_KG + GQA_KV_WIDTH], z[:, Z_VG:Z_VG + GQA_KV_WIDTH],
                                        z[:, Z_QG:Z_QG + GQA_WIDTH], GQA_KV_HEADS, GQA_REP).astype(BF16)


def _ctx_attention(zb):
    z3 = zb.reshape(BATCH, SEQ, Z_WIDTH)
    na, gqa = pl.pallas_call(
        _ctx_attn_kernel,
        out_shape=[jax.ShapeDtypeStruct((BATCH, SEQ, NA_WIDTH), BF16),
                   jax.ShapeDtypeStruct((BATCH, SEQ, GQA_WIDTH), BF16)],
        grid=(BATCH // CTX_SEQS,),
        in_specs=[pl.BlockSpec((CTX_SEQS, SEQ, Z_WIDTH), lambda b: (b, 0, 0))],
        out_specs=[pl.BlockSpec((CTX_SEQS, SEQ, NA_WIDTH), lambda b: (b, 0, 0)),
                   pl.BlockSpec((CTX_SEQS, SEQ, GQA_WIDTH), lambda b: (b, 0, 0))],
        compiler_params=pltpu.CompilerParams(dimension_semantics=("parallel",), vmem_limit_bytes=VMEM_LIMIT),
        name="attn_ctx",
    )(z3)
    return na.reshape(1, TOKENS, NA_WIDTH), gqa.reshape(1, TOKENS, GQA_WIDTH)


NA_KEYS = NA_WIN_R * GRID_W


NA_ROWS = 16


def _na_lat_kernel(q_ref, k_ref, v_ref, ck_ref, cv_ref, bias_ref, o_ref):
    i = pl.program_id(1)
    ck = ck_ref[0, 0]
    cv = cv_ref[0, 0]
    qt = q_ref[0].astype(F32).T
    hd = lax.broadcasted_iota(jnp.int32, (NA_WIDTH, NA_WIDTH), 0) // HEAD_DIM
    hq = lax.broadcasted_iota(jnp.int32, (NA_WIDTH, NA_WIDTH), 1) // HEAD_DIM
    for j in range(NA_ROWS):
        r = NA_ROWS * i + j
        rs = jnp.clip(r - NA_WIN_R // 2, 0, GRID_H - NA_WIN_R)
        start = pl.multiple_of(rs * GRID_W, GRID_W)
        kw = k_ref[0, pl.ds(start, NA_KEYS), :]
        vw = v_ref[0, pl.ds(start, NA_KEYS), :]
        qj = qt[:, j * GRID_W:(j + 1) * GRID_W]
        bd = jnp.where(hd == hq, jnp.concatenate([qj] * NA_HEADS, axis=1), 0.0).astype(BF16)
        bias = bias_ref[0, pl.ds(NA_WIN_R - 1 - (r - rs), NA_WIN_R), :, :].reshape(NA_KEYS, NA_WIDTH)
        s_loc = _dot(kw, bd) + bias
        s_ctx = _dot(ck, bd)
        m = jnp.maximum(jnp.max(s_loc, axis=0, keepdims=True), jnp.max(s_ctx, axis=0, keepdims=True))
        p_loc = jnp.exp2(s_loc - m)
        p_ctx = jnp.exp2(s_ctx - m)
        inv = 1.0 / (jnp.sum(p_loc, axis=0, keepdims=True) + jnp.sum(p_ctx, axis=0, keepdims=True))
        full = (_dot((p_loc * inv).T.astype(BF16), vw)
                + _dot((p_ctx * inv).T.astype(BF16), cv))
        out = jnp.concatenate([full[h * HEAD_DIM:(h + 1) * HEAD_DIM, h * HEAD_DIM:(h + 1) * HEAD_DIM]
                               for h in range(NA_HEADS)], axis=1)
        o_ref[0, j * GRID_W:(j + 1) * GRID_W, :] = out.astype(BF16)


def _na_latent(l, zb, ck, cv, bias):
    tq = NA_ROWS * GRID_W
    return pl.pallas_call(
        _na_lat_kernel,
        out_shape=jax.ShapeDtypeStruct((DEC_BATCH, TOKENS, NA_WIDTH), BF16),
        grid=(DEC_BATCH, GRID_H // NA_ROWS),
        in_specs=[
            pl.BlockSpec((1, tq, NA_WIDTH), lambda b, i: (b, i, Z_QNA // NA_WIDTH)),
            pl.BlockSpec((1, TOKENS, NA_WIDTH), lambda b, i: (b, 0, Z_KNA // NA_WIDTH)),
            pl.BlockSpec((1, TOKENS, NA_WIDTH), lambda b, i: (b, 0, Z_VNA // NA_WIDTH)),
            pl.BlockSpec((1, 1, PAST_LEN, NA_WIDTH), lambda b, i: (b, l, 0, 0)),
            pl.BlockSpec((1, 1, PAST_LEN, NA_WIDTH), lambda b, i: (b, l, 0, 0)),
            pl.BlockSpec((1, 2 * NA_WIN_R - 1, GRID_W, NA_WIDTH), lambda b, i: (l, 0, 0, 0)),
        ],
        out_specs=pl.BlockSpec((1, tq, NA_WIDTH), lambda b, i: (b, i, 0)),
        compiler_params=pltpu.CompilerParams(dimension_semantics=("parallel", "arbitrary"),
                                             vmem_limit_bytes=VMEM_LIMIT),
        name="na_lat",
    )(zb, zb, zb, ck, cv, bias)


def _na_bias_table(tab):
    q = np.arange(GRID_W)
    kc = np.arange(GRID_W)
    cs = np.clip(q - NA_WIN_C // 2, 0, GRID_W - NA_WIN_C)
    valid = (kc[:, None] >= cs[None, :]) & (kc[:, None] < cs[None, :] + NA_WIN_C)
    w = GRID_W - 1
    rev = tab.astype(F32)[..., ::-1]
    padded = jnp.pad(rev, ((0, 0), (0, 0), (0, 0), (w, w)))
    off = w + NA_WIN_C - 1
    cols = jnp.stack([jnp.stack([padded[:, h, :, off - i:off - i + GRID_W] for i in range(GRID_W)], axis=2)
                      for h in range(NA_HEADS)], axis=3)
    cols = jnp.where(valid[:, None, :], cols * LOG2E, -jnp.inf)
    return cols.reshape(DEPTH, 2 * NA_WIN_R - 1, GRID_W, NA_WIDTH)


GQA_TK = 256
GQA_KEYS = DEC_SEQ + PAST_LEN


def _gqa_kernel(q_ref, ktn_ref, ktc_ref, vn_ref, vc_ref, o_ref, s_s, m_s, *, tq):
    q = q_ref[0]
    n_new = DEC_SEQ // GQA_TK
    half = GQA_REP * tq // 2
    qh = [jnp.concatenate([q[:, r * HEAD_DIM:(r + 1) * HEAD_DIM] for r in (2 * i, 2 * i + 1)], axis=0)
          for i in range(2)]
    rows = [slice(i * half, (i + 1) * half) for i in range(2)]

    def fold(s):
        return jnp.maximum(s[:, :LANES], s[:, LANES:])

    for i in range(2):
        s = _dot(qh[i], ktc_ref[0])
        s_s[rows[i], DEC_SEQ:GQA_KEYS] = s
        m_s[rows[i], :] = fold(s)
    for c in range(0, n_new, 2):
        for i in range(2):
            s0 = _dot(qh[i], ktn_ref[0, :, c * GQA_TK:(c + 1) * GQA_TK])
            s1 = _dot(qh[i], ktn_ref[0, :, (c + 1) * GQA_TK:(c + 2) * GQA_TK])
            s_s[rows[i], c * GQA_TK:(c + 1) * GQA_TK] = s0
            s_s[rows[i], (c + 1) * GQA_TK:(c + 2) * GQA_TK] = s1
            m_s[rows[i], :] = jnp.maximum(m_s[rows[i], :], jnp.maximum(fold(s0), fold(s1)))

    m = jnp.max(m_s[...], axis=-1, keepdims=True)
    m_s[...] = jnp.broadcast_to(m, m_s.shape)

    def probs(i, c0):
        mb = m_s[rows[i], :]
        return jnp.exp2(s_s[rows[i], c0:c0 + GQA_TK] - jnp.concatenate([mb, mb], axis=1)).astype(BF16)

    acc = [_dot(probs(i, DEC_SEQ), vc_ref[0, 0]) for i in range(2)]
    for c in range(n_new):
        for i in range(2):
            acc[i] = acc[i] + _dot(probs(i, c * GQA_TK), vn_ref[0, 0, c * GQA_TK:(c + 1) * GQA_TK, :])
    outs = []
    for i in range(2):
        out = acc[i][:, :HEAD_DIM] / acc[i][:, HEAD_DIM:HEAD_DIM + 1]
        outs += [out[:tq], out[tq:]]
    o_ref[0] = jnp.concatenate(outs, axis=1).astype(BF16)


def _gqa_latent(zb, kt_new, kt_cache, v1_new, v1_cache):
    tq = GQA_TQ
    rows = GQA_REP * tq
    return pl.pallas_call(
        functools.partial(_gqa_kernel, tq=tq),
        out_shape=jax.ShapeDtypeStruct((DEC_BATCH, TOKENS, GQA_WIDTH), BF16),
        grid=(DEC_BATCH, GQA_KV_HEADS, TOKENS // tq),
        in_specs=[
            pl.BlockSpec((1, tq, GQA_REP * HEAD_DIM), lambda b, g, i: (b, i, Z_QG // (GQA_REP * HEAD_DIM) + g)),
            pl.BlockSpec((1, HEAD_DIM, DEC_SEQ), lambda b, g, i: (b, g, 0)),
            pl.BlockSpec((1, HEAD_DIM, PAST_LEN), lambda b, g, i: (b, g, 0)),
            pl.BlockSpec((1, 1, DEC_SEQ, LANES), lambda b, g, i: (b, g, 0, 0)),
            pl.BlockSpec((1, 1, PAST_LEN, LANES), lambda b, g, i: (b, g, 0, 0)),
        ],
        out_specs=pl.BlockSpec((1, tq, GQA_REP * HEAD_DIM), lambda b, g, i: (b, i, g)),
        scratch_shapes=[pltpu.VMEM((rows, GQA_KEYS), F32), pltpu.VMEM((rows, LANES), F32)],
        compiler_params=pltpu.CompilerParams(dimension_semantics=("parallel", "parallel", "arbitrary"),
                                             vmem_limit_bytes=VMEM_LIMIT),
        name="gqa_lat",
    )(zb, kt_new, kt_cache, v1_new, v1_cache)


FF_CHUNK = 256


def _ffn_kernel(x_ref, ssm_ref, na_ref, gqa_ref, mod_ref, wout_ref, ln1g_ref, ln1b_ref, win_ref, wo_ref,
                ln2g_ref, ln2b_ref, o_ref, h2_s):
    ssm = jnp.concatenate([ssm_ref[0, 0], ssm_ref[0, 1]], axis=1)
    o = (_dot(ssm.astype(BF16), wout_ref[0, 0:SSM_WIDTH, :])
         + _dot(na_ref[0], wout_ref[0, SSM_WIDTH:SSM_WIDTH + NA_WIDTH, :])
         + _dot(gqa_ref[0], wout_ref[0, SSM_WIDTH + NA_WIDTH:, :]))
    y = DEEPNORM_ALPHA * x_ref[0] + mod_ref[0, 2:3, :] * o
    x1 = _ln(y) * ln1g_ref[0] + ln1b_ref[0]
    h2 = _ln(x1) * (1.0 + mod_ref[0, 4:5, :]) + mod_ref[0, 3:4, :]
    h2_s[...] = h2.astype(BF16)
    acc = None
    for j in range(D_FF // FF_CHUNK):
        h2b = h2_s[...]
        a = _dot(h2b, win_ref[0, :, j * FF_CHUNK:(j + 1) * FF_CHUNK])
        gt = _dot(h2b, win_ref[0, :, D_FF + j * FF_CHUNK:D_FF + (j + 1) * FF_CHUNK])
        f = (a * _sigmoid(a) * gt).astype(BF16)
        c = _dot(f, wo_ref[0, j * FF_CHUNK:(j + 1) * FF_CHUNK, :])
        acc = c if acc is None else acc + c
    y = DEEPNORM_ALPHA * x1 + mod_ref[0, 5:6, :] * acc
    o_ref[0] = _ln(y) * ln2g_ref[0] + ln2b_ref[0]


def _out_ffn(l, x, ssm, na, gqa, mod, w_out, ln1g, ln1b, w_ffn_in, w_ffn_out, ln2g, ln2b):
    bg = x.shape[0]
    tm = FFN_TM
    vec = pl.BlockSpec((1, 1, D_MODEL), lambda b, i: (l, 0, 0))
    resident = lambda shape: pl.BlockSpec((1,) + shape, lambda b, i: (l, 0, 0), pipeline_mode=pl.Buffered(1))
    return pl.pallas_call(
        _ffn_kernel,
        out_shape=jax.ShapeDtypeStruct((bg, TOKENS, D_MODEL), F32),
        grid=(bg, TOKENS // tm),
        in_specs=[
            pl.BlockSpec((1, tm, D_MODEL), lambda b, i: (b, i, 0)),
            pl.BlockSpec((1, 2, tm, LANES), lambda b, i: (b, 0, i, 0)),
            pl.BlockSpec((1, tm, NA_WIDTH), lambda b, i: (b, i, 0)),
            pl.BlockSpec((1, tm, GQA_WIDTH), lambda b, i: (b, i, 0)),
            pl.BlockSpec((1, 6, D_MODEL), lambda b, i: (b, 0, 0)),
            resident((D_MODEL, D_MODEL)),
            vec,
            vec,
            resident((D_MODEL, 2 * D_FF)),
            resident((D_FF, D_MODEL)),
            vec,
            vec,
        ],
        out_specs=pl.BlockSpec((1, tm, D_MODEL), lambda b, i: (b, i, 0)),
        scratch_shapes=[pltpu.VMEM((tm, D_MODEL), BF16)],
        compiler_params=pltpu.CompilerParams(dimension_semantics=("parallel", "parallel"),
                                             vmem_limit_bytes=VMEM_LIMIT),
        name="ffn",
    )(x, ssm, na, gqa, mod, w_out, ln1g, ln1b, w_ffn_in, w_ffn_out, ln2g, ln2b)


def _rope_tables():
    nf = HEAD_DIM // 4
    t = np.arange(DEC_SEQ)
    inv = ROPE_THETA ** (-np.arange(nf, dtype=np.float64) / nf)
    pos = np.stack([t // GRID_W, t % GRID_W], axis=1).astype(np.float64)
    ang = pos[:, :, None] * inv
    cos = np.repeat(np.cos(ang)[:, :, None, :], 2, axis=2).reshape(DEC_SEQ, HEAD_DIM)
    sin = (np.sin(ang)[:, :, None, :] * np.array([-1.0, 1.0])[None, None, :, None]).reshape(DEC_SEQ, HEAD_DIM)
    return jnp.asarray(np.tile(cos, (1, 2)), F32), jnp.asarray(np.tile(sin, (1, 2)), F32)


def _head_mean_matrix():
    h = np.arange(GQA_WIDTH) // HEAD_DIM
    return jnp.asarray((h[:, None] == h[None, :]).astype(np.float32) / HEAD_DIM, BF16)


def kernel(x_prompt, x_sample, c, cache_na_k, cache_na_v, cache_gqa_k, cache_gqa_v, state_ssm_re, state_ssm_im,
           c_ctx, w_ada, b_ada, w_in, w_out, q_norm_g, k_norm_g, na_bias, ssm_lam_re, ssm_lam_im, ssm_log_dt,
           ssm_b_re, ssm_b_im, ssm_c_re, ssm_c_im, ssm_d, w_ssm_glu, ln1_g, ln1_b, ln2_g, ln2_b,
           w_ffn_in, w_ffn_out):
    cond8 = jnp.concatenate([c_ctx[None, :], c, jnp.zeros((8 - 1 - DEC_BATCH, D_MODEL), F32)], axis=0)
    mod = _modulation(cond8, w_ada, b_ada).reshape(DEPTH, 8, 6, D_MODEL)
    cos, sin = _rope_tables()
    hm = _head_mean_matrix()
    w_in_b = w_in.astype(BF16)
    w_out_b = w_out.astype(BF16)
    w_ffn_in_b = w_ffn_in.astype(BF16)
    w_ffn_out_b = w_ffn_out.astype(BF16)
    w_glu_b = w_ssm_glu.astype(BF16)
    qg = jnp.tile(q_norm_g, (1, GQA_HEADS))[:, None, :]
    kg = jnp.tile(k_norm_g, (1, GQA_KV_HEADS))[:, None, :]
    dskip = ssm_d[:, None, :]
    ln = (ln1_g[:, None, :], ln1_b[:, None, :], ln2_g[:, None, :], ln2_b[:, None, :])
    *ssm_tabs, a1, a2 = _ssm_tables(ssm_lam_re, ssm_lam_im, ssm_log_dt, ssm_b_re, ssm_b_im, ssm_c_re, ssm_c_im)
    ops = _ssm_operators(ssm_tabs)
    a1 = a1.reshape(DEPTH, 2, STATE_LANES)
    a2 = a2.reshape(DEPTH, 2, STATE_LANES)
    na_tab = _na_bias_table(na_bias)
    ck_na = cache_na_k.reshape(DEC_BATCH, DEPTH, PAST_LEN, NA_WIDTH).astype(BF16)
    cv_na = cache_na_v.reshape(DEC_BATCH, DEPTH, PAST_LEN, NA_WIDTH).astype(BF16)
    kt_cache = cache_gqa_k.transpose(1, 0, 3, 4, 2).reshape(DEPTH, DEC_BATCH, GQA_KV_WIDTH, PAST_LEN).astype(BF16)
    cv_g = cache_gqa_v.transpose(1, 0, 3, 2, 4)
    v1_cache = jnp.concatenate([cv_g, jnp.ones(cv_g.shape[:-1] + (1,), F32),
                                jnp.zeros(cv_g.shape[:-1] + (LANES - HEAD_DIM - 1,), F32)], axis=-1).astype(BF16)
    s_re = state_ssm_re.astype(F32).transpose(1, 0, 2, 3, 4)
    s_im = state_ssm_im.astype(F32).transpose(1, 0, 2, 3, 4)
    h0s_lat = jnp.concatenate([s_re, s_im], axis=-1).reshape(DEPTH, DEC_BATCH, 2, STATE_LANES)
    h0p_lat = jnp.concatenate([s_im, s_re], axis=-1).reshape(DEPTH, DEC_BATCH, 2, STATE_LANES)
    h0_ctx = jnp.zeros((1, 2, STATE_LANES), F32)

    y_ctx = x_prompt.reshape(1, TOKENS, D_MODEL)
    y_lat = x_sample
    kv_na_l, kv_g_l, fin_l = [], [], []
    for l in range(DEPTH):
        mod_c = mod[l, 0:1]
        u, zb, kv_na, kv_g = _in_projection(l, y_ctx, mod_c, w_in_b, qg, kg, hm, cos, sin, latent=False)
        ssm_o, fin = _ssm(l, u, ops, a1, a2, h0_ctx, h0_ctx, dskip, w_glu_b, n_seq=BATCH, emit_final=True)
        na_o, gqa_o = _ctx_attention(zb)
        y_ctx = _out_ffn(l, y_ctx, ssm_o, na_o, gqa_o, mod_c, w_out_b, ln[0], ln[1], w_ffn_in_b, w_ffn_out_b,
                         ln[2], ln[3])
        kv_na_l.append(kv_na); kv_g_l.append(kv_g); fin_l.append(fin)

        mod_s = mod[l, 1:1 + DEC_BATCH]
        u, zb, kt_new, v1_new = _in_projection(l, y_lat, mod_s, w_in_b, qg, kg, hm, cos, sin, latent=True)
        ssm_o = _ssm(l, u, ops, a1, a2, h0s_lat[l], h0p_lat[l], dskip, w_glu_b, n_seq=1,
                     emit_final=False)[0]
        na_o = _na_latent(l, zb, ck_na, cv_na, na_tab)
        gqa_o = _gqa_latent(zb, kt_new, kt_cache[l], v1_new, v1_cache[l])
        y_lat = _out_ffn(l, y_lat, ssm_o, na_o, gqa_o, mod_s, w_out_b, ln[0], ln[1], w_ffn_in_b, w_ffn_out_b,
                         ln[2], ln[3])

    kv_na = jnp.stack(kv_na_l, axis=0).reshape(DEPTH, BATCH, SEQ, 2, NA_HEADS, HEAD_DIM)
    kv_g = jnp.stack(kv_g_l, axis=0).reshape(DEPTH, BATCH, SEQ, 2, GQA_KV_HEADS, HEAD_DIM)
    kv_na = kv_na.transpose(3, 1, 0, 2, 4, 5)
    kv_g = kv_g.transpose(3, 1, 0, 2, 4, 5)
    fin = jnp.stack(fin_l, axis=0).reshape(DEPTH, 2, BATCH, SSM_GROUPS, 2, SSM_STATE)
    fin = fin.transpose(4, 2, 0, 1, 3, 5)
    return (y_ctx.reshape(BATCH, SEQ, D_MODEL), y_lat, kv_na[0], kv_na[1], kv_g[0], kv_g[1], fin[0], fin[1])
```

```python
import functools
import math

import numpy as np
import jax
import jax.numpy as jnp
from jax import lax
from jax.experimental import pallas as pl
from jax.experimental.pallas import tpu as pltpu

F32 = jnp.float32
BF16 = jnp.bfloat16

D_MODEL = 1024
BATCH = 16
SEQ = 256
DEPTH = 2
DEC_BATCH = 2
DEC_SEQ = 4096
PAST_LEN = 256
GRID_W = 64
GRID_H = DEC_SEQ // GRID_W
HEAD_DIM = 64
SSM_WIDTH = 256
SSM_GROUP_CH = 16
SSM_GROUPS = 16
SSM_STATE = 64
NA_HEADS = 4
NA_WIDTH = 256
NA_WIN_R = 8
NA_WIN_C = 16
GQA_HEADS = 8
GQA_KV_HEADS = 2
GQA_REP = 4
GQA_WIDTH = 512
GQA_KV_WIDTH = 128
IN_WIDTH = 1792
D_FF = 2816
ROPE_THETA = 10000.0
LN_EPS = 1e-6
RMS_EPS = 1e-6
DEEPNORM_ALPHA = (2 * DEPTH) ** 0.25
LOG2E = math.log2(math.e)
Q_SCALE = HEAD_DIM ** -0.5 * LOG2E

TOKENS = 4096
SSM_CHUNK = 16
N_CHUNK = TOKENS // SSM_CHUNK
Z_WIDTH = IN_WIDTH - SSM_WIDTH
Z_QNA, Z_KNA, Z_VNA, Z_QG, Z_KG, Z_VG = 0, 256, 512, 768, 1280, 1408

VMEM_LIMIT = 56 * 1024 * 1024
LANES = 128
MXU_EDGE = 256
SSM_FLAT = SSM_CHUNK * SSM_GROUP_CH
assert SSM_FLAT == MXU_EDGE

MOD_NB = 1536
WIN_TM = 1024
FFN_TM = 512
GQA_TQ = 512


def _sigmoid(x):
    return 1.0 / (1.0 + jnp.exp(-x))


def _ln(x):
    mu = jnp.mean(x, axis=-1, keepdims=True)
    xc = x - mu
    var = jnp.mean(xc * xc, axis=-1, keepdims=True)
    return xc * lax.rsqrt(var + LN_EPS)


def _dot(a, b):
    return jnp.dot(a, b, preferred_element_type=F32)


def _mod_kernel(c_ref, w_ref, b_ref, o_ref):
    c = c_ref[...]
    s = c * _sigmoid(c)
    o_ref[0] = jnp.dot(s, w_ref[0], preferred_element_type=F32, precision=lax.Precision.HIGHEST) + b_ref[0]


def _modulation(cond8, w_ada, b_ada):
    nb = MOD_NB
    return pl.pallas_call(
        _mod_kernel,
        out_shape=jax.ShapeDtypeStruct((DEPTH, 8, 6 * D_MODEL), F32),
        grid=(DEPTH, 6 * D_MODEL // nb),
        in_specs=[
            pl.BlockSpec((8, D_MODEL), lambda l, j: (0, 0)),
            pl.BlockSpec((1, D_MODEL, nb), lambda l, j: (l, 0, j)),
            pl.BlockSpec((1, 1, nb), lambda l, j: (l, 0, j)),
        ],
        out_specs=pl.BlockSpec((1, 8, nb), lambda l, j: (l, 0, j)),
        compiler_params=pltpu.CompilerParams(dimension_semantics=("parallel", "parallel"),
                                             vmem_limit_bytes=VMEM_LIMIT),
        name="mod",
    )(cond8, w_ada, b_ada.reshape(DEPTH, 1, 6 * D_MODEL))


def _swap16(x):
    w = x.shape[-1]
    lane = lax.broadcasted_iota(jnp.int32, x.shape, x.ndim - 1)
    return jnp.where((lane & 16) != 0, pltpu.roll(x, 16, x.ndim - 1), pltpu.roll(x, w - 16, x.ndim - 1))


def _win_kernel(x_ref, mod_ref, w_ref, qg_ref, kg_ref, hm_ref, cos_ref, sin_ref, *out_refs, latent):
    u_ref, z_ref = out_refs[0], out_refs[1]
    x = x_ref[0]
    h = _ln(x) * (1.0 + mod_ref[0, 1:2, :]) + mod_ref[0, 0:1, :]
    z = _dot(h.astype(BF16), w_ref[0])
    u_ref[0, 0] = z[:, 0:LANES]
    u_ref[0, 1] = z[:, LANES:SSM_WIDTH]
    c_qna, c_kna, c_qg, c_kg, c_vg = np.cumsum([SSM_WIDTH, NA_WIDTH, 2 * NA_WIDTH, GQA_WIDTH, GQA_KV_WIDTH])
    q_na = z[:, c_qna:c_kna] * Q_SCALE
    kv_na = z[:, c_kna:c_qg]
    q_g = z[:, c_qg:c_kg]
    k_g = z[:, c_kg:c_vg]
    v_g = z[:, c_vg:IN_WIDTH]
    q_ms = _dot((q_g * q_g).astype(BF16), hm_ref[...])
    k_ms = _dot((k_g * k_g).astype(BF16), hm_ref[0:GQA_KV_WIDTH, 0:GQA_KV_WIDTH])
    q_n = q_g * lax.rsqrt(q_ms + RMS_EPS) * qg_ref[0]
    k_n = k_g * lax.rsqrt(k_ms + RMS_EPS) * kg_ref[0]
    if latent:
        kt_ref, v1_ref = out_refs[2], out_refs[3]
        cos = cos_ref[...]
        sin = sin_ref[...]
        cos4 = jnp.concatenate([cos] * 4, axis=1)
        sin4 = jnp.concatenate([sin] * 4, axis=1)
        q_n = (q_n * cos4 + _swap16(q_n) * sin4) * Q_SCALE
        k_n = k_n * cos + _swap16(k_n) * sin
        kt_ref[0] = k_n.T.astype(BF16)
        lane = lax.broadcasted_iota(jnp.int32, v_g.shape, 1)
        ones_col = jnp.where(lane == HEAD_DIM, 1.0, 0.0)
        v1_ref[0, 0] = jnp.where(lane < HEAD_DIM, v_g, ones_col).astype(BF16)
        v1_ref[0, 1] = jnp.where(lane < HEAD_DIM, pltpu.roll(v_g, HEAD_DIM, 1), ones_col).astype(BF16)
    else:
        out_refs[2][0] = kv_na
        out_refs[3][0] = jnp.concatenate([k_n, v_g], axis=1)
        q_n = q_n * Q_SCALE
    zb = jnp.concatenate([q_na, kv_na, q_n, k_n, v_g], axis=1)
    z_ref[0] = zb.astype(BF16)


def _in_projection(l, x, mod, w_in, qg, kg, hm, cos, sin, *, latent):
    bg = x.shape[0]
    tm = WIN_TM
    out_shape = [jax.ShapeDtypeStruct((bg, 2, TOKENS, LANES), F32),
                 jax.ShapeDtypeStruct((bg, TOKENS, Z_WIDTH), BF16)]
    out_specs = [pl.BlockSpec((1, 2, tm, LANES), lambda b, i: (b, 0, i, 0)),
                 pl.BlockSpec((1, tm, Z_WIDTH), lambda b, i: (b, i, 0))]
    if latent:
        out_shape += [jax.ShapeDtypeStruct((bg, GQA_KV_WIDTH, TOKENS), BF16),
                      jax.ShapeDtypeStruct((bg, GQA_KV_HEADS, TOKENS, LANES), BF16)]
        out_specs += [pl.BlockSpec((1, GQA_KV_WIDTH, tm), lambda b, i: (b, 0, i)),
                      pl.BlockSpec((1, GQA_KV_HEADS, tm, LANES), lambda b, i: (b, 0, i, 0))]
    else:
        out_shape += [jax.ShapeDtypeStruct((bg, TOKENS, 2 * NA_WIDTH), F32),
                      jax.ShapeDtypeStruct((bg, TOKENS, 2 * GQA_KV_WIDTH), F32)]
        out_specs += [pl.BlockSpec((1, tm, 2 * NA_WIDTH), lambda b, i: (b, i, 0)),
                      pl.BlockSpec((1, tm, 2 * GQA_KV_WIDTH), lambda b, i: (b, i, 0))]
    return pl.pallas_call(
        functools.partial(_win_kernel, latent=latent),
        out_shape=out_shape,
        grid=(bg, TOKENS // tm),
        in_specs=[
            pl.BlockSpec((1, tm, D_MODEL), lambda b, i: (b, i, 0)),
            pl.BlockSpec((1, 6, D_MODEL), lambda b, i: (b, 0, 0)),
            pl.BlockSpec((1, D_MODEL, IN_WIDTH), lambda b, i: (l, 0, 0)),
            pl.BlockSpec((1, 1, GQA_WIDTH), lambda b, i: (l, 0, 0)),
            pl.BlockSpec((1, 1, GQA_KV_WIDTH), lambda b, i: (l, 0, 0)),
            pl.BlockSpec((GQA_WIDTH, GQA_WIDTH), lambda b, i: (0, 0)),
            pl.BlockSpec((tm, LANES), lambda b, i: (i, 0)),
            pl.BlockSpec((tm, LANES), lambda b, i: (i, 0)),
        ],
        out_specs=out_specs,
        compiler_params=pltpu.CompilerParams(dimension_semantics=("parallel", "parallel"),
                                             vmem_limit_bytes=VMEM_LIMIT),
        name="win_lat" if latent else "win_ctx",
    )(x, mod, w_in, qg, kg, hm, cos, sin)


STATE_LANES = SSM_GROUPS * LANES
SSM_UNROLL = 16


def _ssm_ops_kernel(ct_r_ref, ct_i_ref, pk_r_ref, pk_i_ref, pc_r_ref, pc_i_ref, bb1_ref, bb2_ref, pw_r_ref, pw_i_ref,
                    tile_ref, rep_ref, tile4_ref, m_ref, bend_ref, cpow_ref):
    d = pl.program_id(1)

    def onto_lanes(x, sel):
        hi = x.astype(BF16)
        lo = (x - hi.astype(F32)).astype(BF16)
        return _dot(hi, sel) + _dot(lo, sel)

    def build_ops(g, carry):
        cr = onto_lanes(ct_r_ref[0, 0, g], tile_ref[...])
        ci = onto_lanes(ct_i_ref[0, 0, g], tile_ref[...])

        def re_proj(pr_ref, pi_ref):
            pr = onto_lanes(pr_ref[0, 0, g], rep_ref[...])
            pi = onto_lanes(pi_ref[0, 0, g], rep_ref[...])
            return jnp.concatenate([cr * pr - ci * pi, -(cr * pi + ci * pr)], axis=0)
        cpow_ref[0, 0, g] = re_proj(pc_r_ref, pc_i_ref).astype(BF16)
        bb1 = bb1_ref[0, 0, g]
        bb2 = bb2_ref[0, 0, g]
        kcat = jnp.dot(bb1[:, :LANES], re_proj(pk_r_ref, pk_i_ref),
                       preferred_element_type=F32, precision=lax.Precision.HIGHEST)
        lane = lax.broadcasted_iota(jnp.int32, kcat.shape, 1)
        pw_r = onto_lanes(pw_r_ref[0, 0, g], tile4_ref[...])
        pw_i = onto_lanes(pw_i_ref[0, 0, g], tile4_ref[...])
        for t in range(SSM_CHUNK):
            rows = slice(t * SSM_GROUP_CH, (t + 1) * SSM_GROUP_CH)
            lo, hi = t * SSM_GROUP_CH, (t + 1) * SSM_GROUP_CH
            fwd = kcat if t == 0 else jnp.where(lane >= lo, pltpu.roll(kcat, lo, 1), 0.0)
            bwd = kcat if hi == SSM_FLAT else jnp.where(lane < hi, pltpu.roll(kcat, hi, 1), 0.0)
            m_ref[0, 0, g, rows, :] = jnp.where(d == 0, fwd, bwd).astype(BF16)
            bend_ref[0, 0, g, rows, :] = (pw_r[t:t + 1, :] * bb1 + pw_i[t:t + 1, :] * bb2).astype(BF16)
        return carry
    lax.fori_loop(0, SSM_GROUPS, build_ops, 0, unroll=4)


def _ssm_operators(tables):
    per_dir = lambda rows, cols: pl.BlockSpec((1, 1, SSM_GROUPS, rows, cols), lambda l, d: (l, d, 0, 0, 0))
    const = lambda rows: pl.BlockSpec((rows, SSM_FLAT), lambda l, d: (0, 0))
    shape = lambda rows: jax.ShapeDtypeStruct((DEPTH, 2, SSM_GROUPS, rows, SSM_FLAT), BF16)
    c = np.arange(SSM_FLAT)
    tile = (c[None, :] % SSM_GROUP_CH == np.arange(SSM_GROUP_CH)[:, None]).astype(np.float32)
    rep = (c[None, :] // SSM_GROUP_CH == np.arange(SSM_CHUNK)[:, None]).astype(np.float32)
    tile4 = (c[None, :] % SSM_STATE == np.arange(SSM_STATE)[:, None]).astype(np.float32)
    return pl.pallas_call(
        _ssm_ops_kernel,
        out_shape=[shape(SSM_FLAT), shape(SSM_FLAT), shape(LANES)],
        grid=(DEPTH, 2),
        in_specs=([per_dir(SSM_STATE, SSM_GROUP_CH)] * 6 + [per_dir(SSM_GROUP_CH, SSM_FLAT)] * 2
                  + [per_dir(SSM_CHUNK, SSM_STATE)] * 2 + [const(SSM_GROUP_CH), const(SSM_CHUNK), const(SSM_STATE)]),
        out_specs=[per_dir(SSM_FLAT, SSM_FLAT), per_dir(SSM_FLAT, SSM_FLAT), per_dir(LANES, SSM_FLAT)],
        compiler_params=pltpu.CompilerParams(dimension_semantics=("parallel", "parallel"),
                                             vmem_limit_bytes=VMEM_LIMIT),
        name="ssm_ops",
    )(*tables, jnp.asarray(tile, BF16), jnp.asarray(rep, BF16), jnp.asarray(tile4, BF16))


def _ssm_kernel(u_ref, m_ref, bend_ref, cpow_ref, a1_ref, a2_ref, h0s_ref, h0p_ref, d_ref, wglu_ref, *refs,
                n_seq, emit_final):
    if emit_final:
        out_ref, fin_ref, y_s, yt_s, u_s, inja_s, injb_s, hp_s = refs
    else:
        out_ref, y_s, yt_s, u_s, inja_s, injb_s, hp_s = refs
        fin_ref = None
    chunks_per_seq = N_CHUNK // n_seq

    def chunk_rows(t):
        sl = pl.ds(t, N_CHUNK, stride=SSM_CHUNK)
        return jnp.concatenate([u_ref[0, 0, sl, :], u_ref[0, 1, sl, :]], axis=1)

    for t in range(SSM_CHUNK):
        a_t = chunk_rows(t)
        yt_s[:, t * SSM_GROUP_CH:(t + 1) * SSM_GROUP_CH, :] = a_t.T.reshape(SSM_GROUPS, SSM_GROUP_CH, N_CHUNK)

    def tr_in(g, carry):
        u_s[g] = yt_s[g].T.astype(BF16)
        return carry
    lax.fori_loop(0, SSM_GROUPS, tr_in, 0, unroll=SSM_UNROLL)

    for d in range(2):
        def proj(g, carry, d=d):
            ug = u_s[g]
            y = _dot(ug, m_ref[0, d, g])
            y_s[g] = y if d == 0 else y_s[g] + y
            inj = _dot(ug, bend_ref[0, d, g])
            lanes = pl.ds(pl.multiple_of(g * LANES, LANES), LANES)
            inja_s[d, :, lanes] = inj[:, :LANES]
            injb_s[d, :, lanes] = inj[:, LANES:]
            return carry
        lax.fori_loop(0, SSM_GROUPS, proj, 0, unroll=SSM_UNROLL)

    a1 = a1_ref[0]
    a2 = a2_ref[0]

    def step(j, carry):
        s, sp = carry
        nf = j
        nb = N_CHUNK - 1 - j
        if n_seq > 1:
            keep = jnp.where(j % chunks_per_seq != 0, 1.0, 0.0).astype(F32)
            s = s * keep
            sp = sp * keep
        hp_s[0, pl.ds(nf, 1), :] = s[0:1]
        hp_s[1, pl.ds(nb, 1), :] = s[1:2]
        ia = jnp.concatenate([inja_s[0, pl.ds(nf, 1), :], inja_s[1, pl.ds(nb, 1), :]], axis=0)
        ib = jnp.concatenate([injb_s[0, pl.ds(nf, 1), :], injb_s[1, pl.ds(nb, 1), :]], axis=0)
        s_new = a1 * s + a2 * sp + ia
        sp_new = a1 * sp - a2 * s + ib
        if emit_final:
            @pl.when(j % chunks_per_seq == chunks_per_seq - 1)
            def _():
                sq = j // chunks_per_seq
                fin_ref[0, 0, pl.ds(sq, 1), :] = s_new[0:1]
                fin_ref[0, 1, pl.ds(n_seq - 1 - sq, 1), :] = s_new[1:2]
        return s_new, sp_new
    lax.fori_loop(0, N_CHUNK, step, (h0s_ref[0], h0p_ref[0]))

    for d in range(2):
        def carry_in(g, carry, d=d):
            hp = hp_s[d, :, pl.ds(pl.multiple_of(g * LANES, LANES), LANES)]
            y_s[g] = y_s[g] + _dot(hp.astype(BF16), cpow_ref[0, d, g])
            return carry
        lax.fori_loop(0, SSM_GROUPS, carry_in, 0, unroll=SSM_UNROLL)

    def tr_out(g, carry):
        yt_s[g] = y_s[g].T
        return carry
    lax.fori_loop(0, SSM_GROUPS, tr_out, 0, unroll=SSM_UNROLL)
    dskip = d_ref[0]
    wglu = wglu_ref[0]
    for t in range(SSM_CHUNK):
        blk = yt_s[:, t * SSM_GROUP_CH:(t + 1) * SSM_GROUP_CH, :].reshape(SSM_WIDTH, N_CHUNK)
        y = blk.T + chunk_rows(t) * dskip
        gl = jax.nn.gelu(y, approximate=True)
        gate = _dot(gl.astype(BF16), wglu)
        o = gl * _sigmoid(gate)
        out_ref[0, 0, pl.ds(t, N_CHUNK, stride=SSM_CHUNK), :] = o[:, :LANES]
        out_ref[0, 1, pl.ds(t, N_CHUNK, stride=SSM_CHUNK), :] = o[:, LANES:]


def _ssm(l, u, ops, a1, a2, h0s, h0p, dskip, wglu, *, n_seq, emit_final):
    bg = u.shape[0]
    m, bend, cpow = ops
    out_shape = [jax.ShapeDtypeStruct((bg, 2, TOKENS, LANES), F32)]
    out_specs = [pl.BlockSpec((1, 2, TOKENS, LANES), lambda b: (b, 0, 0, 0))]
    if emit_final:
        out_shape.append(jax.ShapeDtypeStruct((bg, 2, n_seq, STATE_LANES), F32))
        out_specs.append(pl.BlockSpec((1, 2, n_seq, STATE_LANES), lambda b: (b, 0, 0, 0)))
    resident = lambda rows: pl.BlockSpec((1, 2, SSM_GROUPS, rows, SSM_FLAT), lambda b: (l, 0, 0, 0, 0),
                                         pipeline_mode=pl.Buffered(1))
    return pl.pallas_call(
        functools.partial(_ssm_kernel, n_seq=n_seq, emit_final=emit_final),
        out_shape=out_shape,
        grid=(bg,),
        in_specs=[
            pl.BlockSpec((1, 2, TOKENS, LANES), lambda b: (b, 0, 0, 0)),
            resident(SSM_FLAT), resident(SSM_FLAT), resident(LANES),
            pl.BlockSpec((1, 2, STATE_LANES), lambda b: (l, 0, 0)),
            pl.BlockSpec((1, 2, STATE_LANES), lambda b: (l, 0, 0)),
            pl.BlockSpec((1, 2, STATE_LANES), lambda b: (b, 0, 0)),
            pl.BlockSpec((1, 2, STATE_LANES), lambda b: (b, 0, 0)),
            pl.BlockSpec((1, 1, SSM_WIDTH), lambda b: (l, 0, 0)),
            pl.BlockSpec((1, SSM_WIDTH, SSM_WIDTH), lambda b: (l, 0, 0)),
        ],
        out_specs=out_specs,
        scratch_shapes=[pltpu.VMEM((SSM_GROUPS, N_CHUNK, SSM_FLAT), F32),
                        pltpu.VMEM((SSM_GROUPS, SSM_FLAT, N_CHUNK), F32),
                        pltpu.VMEM((SSM_GROUPS, N_CHUNK, SSM_FLAT), BF16),
                        pltpu.VMEM((2, N_CHUNK, STATE_LANES), F32),
                        pltpu.VMEM((2, N_CHUNK, STATE_LANES), F32),
                        pltpu.VMEM((2, N_CHUNK, STATE_LANES), F32)],
        compiler_params=pltpu.CompilerParams(dimension_semantics=("parallel",), vmem_limit_bytes=VMEM_LIMIT),
        name="ssm_ctx" if emit_final else "ssm_lat",
    )(u, m, bend, cpow, a1, a2, h0s, h0p, dskip, wglu)


def _ssm_tables(lam_re, lam_im, log_dt, b_re, b_im, c_re, c_im):
    t_n = SSM_CHUNK
    lr = jnp.minimum(lam_re.astype(F32), -1e-4)
    li = lam_im.astype(F32)
    dt = jnp.exp(log_dt.astype(F32))[..., None]
    lr_dt, li_dt = lr * dt, li * dt

    def power(k, expand=lambda x: x):
        mag = jnp.exp(expand(lr_dt) * k)
        return mag * jnp.cos(expand(li_dt) * k), mag * jnp.sin(expand(li_dt) * k)
    l_r, l_i = power(1.0)
    a_r, a_i = power(float(t_n))
    n_r, n_i = l_r - 1.0, l_i
    den = lr * lr + li * li
    q_r, q_i = (n_r * lr + n_i * li) / den, (n_i * lr - n_r * li) / den
    bb_r = q_r[..., None] * b_re - q_i[..., None] * b_im
    bb_i = q_r[..., None] * b_im + q_i[..., None] * b_re

    up = np.arange(t_n, dtype=np.float32)
    down = up[::-1].copy()
    per_dir = lambda fwd, bwd: jnp.asarray(np.stack([fwd, bwd]))[None, :, None, None, :]
    on_lanes = lambda x: x[..., None]
    pk_r, pk_i = power(per_dir(up, down), on_lanes)
    pc_r, pc_i = power(per_dir(up + 1.0, down + 1.0), on_lanes)
    pw_r, pw_i = power(jnp.swapaxes(per_dir(down, up), -1, -2), lambda x: x[..., None, :])
    cr_t = jnp.swapaxes(c_re.astype(F32), -1, -2)
    ci_t = jnp.swapaxes(c_im.astype(F32), -1, -2)
    br_t, bi_t = jnp.swapaxes(bb_r, -1, -2), jnp.swapaxes(bb_i, -1, -2)
    bb1 = jnp.concatenate([br_t, bi_t, bi_t, br_t], axis=-1)
    bb2 = jnp.concatenate([-bi_t, br_t, br_t, -bi_t], axis=-1)
    a1 = jnp.concatenate([a_r, a_r], axis=-1)
    a2 = jnp.concatenate([-a_i, a_i], axis=-1)
    return cr_t, ci_t, pk_r, pk_i, pc_r, pc_i, bb1, bb2, pw_r, pw_i, a1, a2


def _attend_transposed(k, v, q, n_kv, rep):
    nq = q.shape[0]
    qt = q.astype(F32).T
    zero = jnp.zeros((HEAD_DIM, nq), F32)
    bd = jnp.concatenate(
        [jnp.concatenate([qt[(g * rep + r) * HEAD_DIM:(g * rep + r + 1) * HEAD_DIM] if g == gg else zero
                          for gg in range(n_kv) for r in range(rep)], axis=1)
         for g in range(n_kv)], axis=0).astype(BF16)
    s = _dot(k, bd)
    m = jnp.max(s, axis=0, keepdims=True)
    p = jnp.exp2(s - m)
    p = p * (1.0 / jnp.sum(p, axis=0, keepdims=True))
    ot = _dot(v.astype(F32).T.astype(BF16), p.astype(BF16))
    heads = [ot[g * HEAD_DIM:(g + 1) * HEAD_DIM, (g * rep + r) * nq:(g * rep + r + 1) * nq]
             for g in range(n_kv) for r in range(rep)]
    return jnp.concatenate(heads, axis=0).T


CTX_SEQS = 4


def _ctx_attn_kernel(z_ref, na_ref, gqa_ref):
    for i in range(CTX_SEQS):
        z = z_ref[i]
        na_ref[i] = _attend_transposed(z[:, Z_KNA:Z_KNA + NA_WIDTH], z[:, Z_VNA:Z_VNA + NA_WIDTH],
                                       z[:, Z_QNA:Z_QNA + NA_WIDTH], NA_HEADS, 1).astype(BF16)
        gqa_ref[i] = _attend_transposed(z[:, Z_KG:Z_KG + GQA_KV_WIDTH], z[:, Z_VG:Z_VG + GQA_KV_WIDTH],
                                        z[:, Z_QG:Z_QG + GQA_WIDTH], GQA_KV_HEADS, GQA_REP).astype(BF16)


def _ctx_attention(zb):
    z3 = zb.reshape(BATCH, SEQ, Z_WIDTH)
    na, gqa = pl.pallas_call(
        _ctx_attn_kernel,
        out_shape=[jax.ShapeDtypeStruct((BATCH, SEQ, NA_WIDTH), BF16),
                   jax.ShapeDtypeStruct((BATCH, SEQ, GQA_WIDTH), BF16)],
        grid=(BATCH // CTX_SEQS,),
        in_specs=[pl.BlockSpec((CTX_SEQS, SEQ, Z_WIDTH), lambda b: (b, 0, 0))],
        out_specs=[pl.BlockSpec((CTX_SEQS, SEQ, NA_WIDTH), lambda b: (b, 0, 0)),
                   pl.BlockSpec((CTX_SEQS, SEQ, GQA_WIDTH), lambda b: (b, 0, 0))],
        compiler_params=pltpu.CompilerParams(dimension_semantics=("parallel",), vmem_limit_bytes=VMEM_LIMIT),
        name="attn_ctx",
    )(z3)
    return na.reshape(1, TOKENS, NA_WIDTH), gqa.reshape(1, TOKENS, GQA_WIDTH)


NA_KEYS = NA_WIN_R * GRID_W


NA_ROWS = 16


def _na_lat_kernel(q_ref, k_ref, v_ref, ck_ref, cv_ref, bias_ref, o_ref):
    i = pl.program_id(1)
    ck = ck_ref[0, 0]
    cv = cv_ref[0, 0]
    qt = q_ref[0].astype(F32).T
    hd = lax.broadcasted_iota(jnp.int32, (NA_WIDTH, NA_WIDTH), 0) // HEAD_DIM
    hq = lax.broadcasted_iota(jnp.int32, (NA_WIDTH, NA_WIDTH), 1) // HEAD_DIM
    for j in range(NA_ROWS):
        r = NA_ROWS * i + j
        rs = jnp.clip(r - NA_WIN_R // 2, 0, GRID_H - NA_WIN_R)
        start = pl.multiple_of(rs * GRID_W, GRID_W)
        kw = k_ref[0, pl.ds(start, NA_KEYS), :]
        vw = v_ref[0, pl.ds(start, NA_KEYS), :]
        qj = qt[:, j * GRID_W:(j + 1) * GRID_W]
        bd = jnp.where(hd == hq, jnp.concatenate([qj] * NA_HEADS, axis=1), 0.0).astype(BF16)
        bias = bias_ref[0, pl.ds(NA_WIN_R - 1 - (r - rs), NA_WIN_R), :, :].reshape(NA_KEYS, NA_WIDTH)
        s_loc = _dot(kw, bd) + bias
        s_ctx = _dot(ck, bd)
        m = jnp.maximum(jnp.max(s_loc, axis=0, keepdims=True), jnp.max(s_ctx, axis=0, keepdims=True))
        p_loc = jnp.exp2(s_loc - m)
        p_ctx = jnp.exp2(s_ctx - m)
        inv = 1.0 / (jnp.sum(p_loc, axis=0, keepdims=True) + jnp.sum(p_ctx, axis=0, keepdims=True))
        full = (_dot((p_loc * inv).T.astype(BF16), vw)
                + _dot((p_ctx * inv).T.astype(BF16), cv))
        out = jnp.concatenate([full[h * HEAD_DIM:(h + 1) * HEAD_DIM, h * HEAD_DIM:(h + 1) * HEAD_DIM]
                               for h in range(NA_HEADS)], axis=1)
        o_ref[0, j * GRID_W:(j + 1) * GRID_W, :] = out.astype(BF16)


def _na_latent(l, zb, ck, cv, bias):
    tq = NA_ROWS * GRID_W
    return pl.pallas_call(
        _na_lat_kernel,
        out_shape=jax.ShapeDtypeStruct((DEC_BATCH, TOKENS, NA_WIDTH), BF16),
        grid=(DEC_BATCH, GRID_H // NA_ROWS),
        in_specs=[
            pl.BlockSpec((1, tq, NA_WIDTH), lambda b, i: (b, i, Z_QNA // NA_WIDTH)),
            pl.BlockSpec((1, TOKENS, NA_WIDTH), lambda b, i: (b, 0, Z_KNA // NA_WIDTH)),
            pl.BlockSpec((1, TOKENS, NA_WIDTH), lambda b, i: (b, 0, Z_VNA // NA_WIDTH)),
            pl.BlockSpec((1, 1, PAST_LEN, NA_WIDTH), lambda b, i: (b, l, 0, 0)),
            pl.BlockSpec((1, 1, PAST_LEN, NA_WIDTH), lambda b, i: (b, l, 0, 0)),
            pl.BlockSpec((1, 2 * NA_WIN_R - 1, GRID_W, NA_WIDTH), lambda b, i: (l, 0, 0, 0)),
        ],
        out_specs=pl.BlockSpec((1, tq, NA_WIDTH), lambda b, i: (b, i, 0)),
        compiler_params=pltpu.CompilerParams(dimension_semantics=("parallel", "arbitrary"),
                                             vmem_limit_bytes=VMEM_LIMIT),
        name="na_lat",
    )(zb, zb, zb, ck, cv, bias)


def _na_bias_table(tab):
    q = np.arange(GRID_W)
    kc = np.arange(GRID_W)
    cs = np.clip(q - NA_WIN_C // 2, 0, GRID_W - NA_WIN_C)
    valid = (kc[:, None] >= cs[None, :]) & (kc[:, None] < cs[None, :] + NA_WIN_C)
    w = GRID_W - 1
    rev = tab.astype(F32)[..., ::-1]
    padded = jnp.pad(rev, ((0, 0), (0, 0), (0, 0), (w, w)))
    off = w + NA_WIN_C - 1
    cols = jnp.stack([jnp.stack([padded[:, h, :, off - i:off - i + GRID_W] for i in range(GRID_W)], axis=2)
                      for h in range(NA_HEADS)], axis=3)
    cols = jnp.where(valid[:, None, :], cols * LOG2E, -jnp.inf)
    return cols.reshape(DEPTH, 2 * NA_WIN_R - 1, GRID_W, NA_WIDTH)


GQA_TK = 256
GQA_KEYS = DEC_SEQ + PAST_LEN


def _gqa_kernel(q_ref, ktn_ref, ktc_ref, vn_ref, vc_ref, o_ref, s_s, m_s, *, tq):
    q = q_ref[0]
    n_new = DEC_SEQ // GQA_TK
    half = GQA_REP * tq // 2
    qh = [jnp.concatenate([q[:, r * HEAD_DIM:(r + 1) * HEAD_DIM] for r in (2 * i, 2 * i + 1)], axis=0)
          for i in range(2)]
    rows = [slice(i * half, (i + 1) * half) for i in range(2)]

    def fold(s):
        return jnp.maximum(s[:, :LANES], s[:, LANES:])

    for i in range(2):
        s = _dot(qh[i], ktc_ref[0])
        s_s[rows[i], DEC_SEQ:GQA_KEYS] = s
        m_s[rows[i], :] = fold(s)
    for c in range(0, n_new, 2):
        for i in range(2):
            s0 = _dot(qh[i], ktn_ref[0, :, c * GQA_TK:(c + 1) * GQA_TK])
            s1 = _dot(qh[i], ktn_ref[0, :, (c + 1) * GQA_TK:(c + 2) * GQA_TK])
            s_s[rows[i], c * GQA_TK:(c + 1) * GQA_TK] = s0
            s_s[rows[i], (c + 1) * GQA_TK:(c + 2) * GQA_TK] = s1
            m_s[rows[i], :] = jnp.maximum(m_s[rows[i], :], jnp.maximum(fold(s0), fold(s1)))

    m = jnp.max(m_s[...], axis=-1, keepdims=True)
    m_s[...] = jnp.broadcast_to(m, m_s.shape)

    def probs(i, c0):
        mb = m_s[rows[i], :]
        return jnp.exp2(s_s[rows[i], c0:c0 + GQA_TK] - jnp.concatenate([mb, mb], axis=1)).astype(BF16)

    acc = [_dot(probs(i, DEC_SEQ), vc_ref[0, 0]) for i in range(2)]
    for c in range(n_new):
        for i in range(2):
            acc[i] = acc[i] + _dot(probs(i, c * GQA_TK), vn_ref[0, 0, c * GQA_TK:(c + 1) * GQA_TK, :])
    outs = []
    for i in range(2):
        out = acc[i][:, :HEAD_DIM] / acc[i][:, HEAD_DIM:HEAD_DIM + 1]
        outs += [out[:tq], out[tq:]]
    o_ref[0] = jnp.concatenate(outs, axis=1).astype(BF16)


def _gqa_latent(zb, kt_new, kt_cache, v1_new, v1_cache):
    tq = GQA_TQ
    rows = GQA_REP * tq
    return pl.pallas_call(
        functools.partial(_gqa_kernel, tq=tq),
        out_shape=jax.ShapeDtypeStruct((DEC_BATCH, TOKENS, GQA_WIDTH), BF16),
        grid=(DEC_BATCH, GQA_KV_HEADS, TOKENS // tq),
        in_specs=[
            pl.BlockSpec((1, tq, GQA_REP * HEAD_DIM), lambda b, g, i: (b, i, Z_QG // (GQA_REP * HEAD_DIM) + g)),
            pl.BlockSpec((1, HEAD_DIM, DEC_SEQ), lambda b, g, i: (b, g, 0)),
            pl.BlockSpec((1, HEAD_DIM, PAST_LEN), lambda b, g, i: (b, g, 0)),
            pl.BlockSpec((1, 1, DEC_SEQ, LANES), lambda b, g, i: (b, g, 0, 0)),
            pl.BlockSpec((1, 1, PAST_LEN, LANES), lambda b, g, i: (b, g, 0, 0)),
        ],
        out_specs=pl.BlockSpec((1, tq, GQA_REP * HEAD_DIM), lambda b, g, i: (b, i, g)),
        scratch_shapes=[pltpu.VMEM((rows, GQA_KEYS), F32), pltpu.VMEM((rows, LANES), F32)],
        compiler_params=pltpu.CompilerParams(dimension_semantics=("parallel", "parallel", "arbitrary"),
                                             vmem_limit_bytes=VMEM_LIMIT),
        name="gqa_lat",
    )(zb, kt_new, kt_cache, v1_new, v1_cache)


FF_CHUNK = 256


def _ffn_kernel(x_ref, ssm_ref, na_ref, gqa_ref, mod_ref, wout_ref, ln1g_ref, ln1b_ref, win_ref, wo_ref,
                ln2g_ref, ln2b_ref, o_ref, h2_s):
    tm = x_ref.shape[1]
    x1_halves = []
    for rows in (slice(0, tm // 2), slice(tm // 2, tm)):
        ssm = jnp.concatenate([ssm_ref[0, 0, rows, :], ssm_ref[0, 1, rows, :]], axis=1)
        o = (_dot(ssm.astype(BF16), wout_ref[0, 0:SSM_WIDTH, :])
             + _dot(na_ref[0, rows, :], wout_ref[0, SSM_WIDTH:SSM_WIDTH + NA_WIDTH, :])
             + _dot(gqa_ref[0, rows, :], wout_ref[0, SSM_WIDTH + NA_WIDTH:, :]))
        y = DEEPNORM_ALPHA * x_ref[0, rows, :] + mod_ref[0, 2:3, :] * o
        x1_halves.append(_ln(y) * ln1g_ref[0] + ln1b_ref[0])
        h2 = _ln(x1_halves[-1]) * (1.0 + mod_ref[0, 4:5, :]) + mod_ref[0, 3:4, :]
        h2_s[rows, :] = h2.astype(BF16)
    x1 = jnp.concatenate(x1_halves, axis=0)
    acc = None
    for j in range(D_FF // FF_CHUNK):
        h2b = h2_s[...]
        a = _dot(h2b, win_ref[0, :, j * FF_CHUNK:(j + 1) * FF_CHUNK])
        gt = _dot(h2b, win_ref[0, :, D_FF + j * FF_CHUNK:D_FF + (j + 1) * FF_CHUNK])
        f = (a * _sigmoid(a) * gt).astype(BF16)
        c = _dot(f, wo_ref[0, j * FF_CHUNK:(j + 1) * FF_CHUNK, :])
        acc = c if acc is None else acc + c
    y = DEEPNORM_ALPHA * x1 + mod_ref[0, 5:6, :] * acc
    o_ref[0] = _ln(y) * ln2g_ref[0] + ln2b_ref[0]


def _out_ffn(l, x, ssm, na, gqa, mod, w_out, ln1g, ln1b, w_ffn_in, w_ffn_out, ln2g, ln2b):
    bg = x.shape[0]
    tm = FFN_TM
    vec = pl.BlockSpec((1, 1, D_MODEL), lambda b, i: (l, 0, 0))
    resident = lambda shape: pl.BlockSpec((1,) + shape, lambda b, i: (l, 0, 0), pipeline_mode=pl.Buffered(1))
    return pl.pallas_call(
        _ffn_kernel,
        out_shape=jax.ShapeDtypeStruct((bg, TOKENS, D_MODEL), F32),
        grid=(bg, TOKENS // tm),
        in_specs=[
            pl.BlockSpec((1, tm, D_MODEL), lambda b, i: (b, i, 0)),
            pl.BlockSpec((1, 2, tm, LANES), lambda b, i: (b, 0, i, 0)),
            pl.BlockSpec((1, tm, NA_WIDTH), lambda b, i: (b, i, 0)),
            pl.BlockSpec((1, tm, GQA_WIDTH), lambda b, i: (b, i, 0)),
            pl.BlockSpec((1, 6, D_MODEL), lambda b, i: (b, 0, 0)),
            resident((D_MODEL, D_MODEL)),
            vec,
            vec,
            resident((D_MODEL, 2 * D_FF)),
            resident((D_FF, D_MODEL)),
            vec,
            vec,
        ],
        out_specs=pl.BlockSpec((1, tm, D_MODEL), lambda b, i: (b, i, 0)),
        scratch_shapes=[pltpu.VMEM((tm, D_MODEL), BF16)],
        compiler_params=pltpu.CompilerParams(dimension_semantics=("parallel", "parallel"),
                                             vmem_limit_bytes=VMEM_LIMIT),
        name="ffn",
    )(x, ssm, na, gqa, mod, w_out, ln1g, ln1b, w_ffn_in, w_ffn_out, ln2g, ln2b)


def _rope_tables():
    nf = HEAD_DIM // 4
    t = np.arange(DEC_SEQ)
    inv = ROPE_THETA ** (-np.arange(nf, dtype=np.float64) / nf)
    pos = np.stack([t // GRID_W, t % GRID_W], axis=1).astype(np.float64)
    ang = pos[:, :, None] * inv
    cos = np.repeat(np.cos(ang)[:, :, None, :], 2, axis=2).reshape(DEC_SEQ, HEAD_DIM)
    sin = (np.sin(ang)[:, :, None, :] * np.array([-1.0, 1.0])[None, None, :, None]).reshape(DEC_SEQ, HEAD_DIM)
    return jnp.asarray(np.tile(cos, (1, 2)), F32), jnp.asarray(np.tile(sin, (1, 2)), F32)


def _head_mean_matrix():
    h = np.arange(GQA_WIDTH) // HEAD_DIM
    return jnp.asarray((h[:, None] == h[None, :]).astype(np.float32) / HEAD_DIM, BF16)


def kernel(x_prompt, x_sample, c, cache_na_k, cache_na_v, cache_gqa_k, cache_gqa_v, state_ssm_re, state_ssm_im,
           c_ctx, w_ada, b_ada, w_in, w_out, q_norm_g, k_norm_g, na_bias, ssm_lam_re, ssm_lam_im, ssm_log_dt,
           ssm_b_re, ssm_b_im, ssm_c_re, ssm_c_im, ssm_d, w_ssm_glu, ln1_g, ln1_b, ln2_g, ln2_b,
           w_ffn_in, w_ffn_out):
    cond8 = jnp.concatenate([c_ctx[None, :], c, jnp.zeros((8 - 1 - DEC_BATCH, D_MODEL), F32)], axis=0)
    mod = _modulation(cond8, w_ada, b_ada).reshape(DEPTH, 8, 6, D_MODEL)
    cos, sin = _rope_tables()
    hm = _head_mean_matrix()
    w_in_b = w_in.astype(BF16)
    w_out_b = w_out.astype(BF16)
    w_ffn_in_b = w_ffn_in.astype(BF16)
    w_ffn_out_b = w_ffn_out.astype(BF16)
    w_glu_b = w_ssm_glu.astype(BF16)
    qg = jnp.tile(q_norm_g, (1, GQA_HEADS))[:, None, :]
    kg = jnp.tile(k_norm_g, (1, GQA_KV_HEADS))[:, None, :]
    dskip = ssm_d[:, None, :]
    ln = (ln1_g[:, None, :], ln1_b[:, None, :], ln2_g[:, None, :], ln2_b[:, None, :])
    *ssm_tabs, a1, a2 = _ssm_tables(ssm_lam_re, ssm_lam_im, ssm_log_dt, ssm_b_re, ssm_b_im, ssm_c_re, ssm_c_im)
    ops = _ssm_operators(ssm_tabs)
    a1 = a1.reshape(DEPTH, 2, STATE_LANES)
    a2 = a2.reshape(DEPTH, 2, STATE_LANES)
    na_tab = _na_bias_table(na_bias)
    ck_na = cache_na_k.reshape(DEC_BATCH, DEPTH, PAST_LEN, NA_WIDTH).astype(BF16)
    cv_na = cache_na_v.reshape(DEC_BATCH, DEPTH, PAST_LEN, NA_WIDTH).astype(BF16)
    kt_cache = cache_gqa_k.transpose(1, 0, 3, 4, 2).reshape(DEPTH, DEC_BATCH, GQA_KV_WIDTH, PAST_LEN).astype(BF16)
    cv_g = cache_gqa_v.transpose(1, 0, 3, 2, 4)
    v1_cache = jnp.concatenate([cv_g, jnp.ones(cv_g.shape[:-1] + (1,), F32),
                                jnp.zeros(cv_g.shape[:-1] + (LANES - HEAD_DIM - 1,), F32)], axis=-1).astype(BF16)
    s_re = state_ssm_re.astype(F32).transpose(1, 0, 2, 3, 4)
    s_im = state_ssm_im.astype(F32).transpose(1, 0, 2, 3, 4)
    h0s_lat = jnp.concatenate([s_re, s_im], axis=-1).reshape(DEPTH, DEC_BATCH, 2, STATE_LANES)
    h0p_lat = jnp.concatenate([s_im, s_re], axis=-1).reshape(DEPTH, DEC_BATCH, 2, STATE_LANES)
    h0_ctx = jnp.zeros((1, 2, STATE_LANES), F32)

    y_ctx = x_prompt.reshape(1, TOKENS, D_MODEL)
    y_lat = x_sample
    kv_na_l, kv_g_l, fin_l = [], [], []
    for l in range(DEPTH):
        mod_c = mod[l, 0:1]
        u, zb, kv_na, kv_g = _in_projection(l, y_ctx, mod_c, w_in_b, qg, kg, hm, cos, sin, latent=False)
        ssm_o, fin = _ssm(l, u, ops, a1, a2, h0_ctx, h0_ctx, dskip, w_glu_b, n_seq=BATCH, emit_final=True)
        na_o, gqa_o = _ctx_attention(zb)
        y_ctx = _out_ffn(l, y_ctx, ssm_o, na_o, gqa_o, mod_c, w_out_b, ln[0], ln[1], w_ffn_in_b, w_ffn_out_b,
                         ln[2], ln[3])
        kv_na_l.append(kv_na); kv_g_l.append(kv_g); fin_l.append(fin)

        mod_s = mod[l, 1:1 + DEC_BATCH]
        u, zb, kt_new, v1_new = _in_projection(l, y_lat, mod_s, w_in_b, qg, kg, hm, cos, sin, latent=True)
        ssm_o = _ssm(l, u, ops, a1, a2, h0s_lat[l], h0p_lat[l], dskip, w_glu_b, n_seq=1,
                     emit_final=False)[0]
        na_o = _na_latent(l, zb, ck_na, cv_na, na_tab)
        gqa_o = _gqa_latent(zb, kt_new, kt_cache[l], v1_new, v1_cache[l])
        y_lat = _out_ffn(l, y_lat, ssm_o, na_o, gqa_o, mod_s, w_out_b, ln[0], ln[1], w_ffn_in_b, w_ffn_out_b,
                         ln[2], ln[3])

    kv_na = jnp.stack(kv_na_l, axis=0).reshape(DEPTH, BATCH, SEQ, 2, NA_HEADS, HEAD_DIM)
    kv_g = jnp.stack(kv_g_l, axis=0).reshape(DEPTH, BATCH, SEQ, 2, GQA_KV_HEADS, HEAD_DIM)
    kv_na = kv_na.transpose(3, 1, 0, 2, 4, 5)
    kv_g = kv_g.transpose(3, 1, 0, 2, 4, 5)
    fin = jnp.stack(fin_l, axis=0).reshape(DEPTH, 2, BATCH, SSM_GROUPS, 2, SSM_STATE)
    fin = fin.transpose(4, 2, 0, 1, 3, 5)
    return (y_ctx.reshape(BATCH, SEQ, D_MODEL), y_lat, kv_na[0], kv_na[1], kv_g[0], kv_g[1], fin[0], fin[1])
```
